```python
import jax, jax.numpy as jnp
from jax import lax
import numpy as np

D_MODEL = 1024
BATCH = 8
SEQ = 4096
DEPTH = 2

N_META = 16
GRID_W = 64
BLOCK = 128
FRONT_PAD = BLOCK - N_META
EPS = 1e-6
D_FF = 4 * D_MODEL

A_HEADS = 4
A_DK = 128
A_DV = 128
A_CHUNK = 64
A_CONV = 5
A_QKW = A_HEADS * A_DK
A_VW = A_HEADS * A_DV

B_HEADS = 8
B_KV = 2
B_HD = 64
B_WIN = 128
B_QW = B_HEADS * B_HD
B_KVW = B_KV * B_HD

C_HEADS = 8
C_KV = 2
C_HD = 128
C_QW = C_HEADS * C_HD
C_KVW = C_KV * C_HD
ROPE_THETA = 10000.0

MIX_AB = A_VW + B_QW
AB_SIZES = (A_QKW, A_QKW, A_VW, A_VW, 2 * A_HEADS, 2 * A_HEADS, B_QW, B_KVW, B_KVW)
IN_AB = sum(AB_SIZES)
IN_C = C_QW + 2 * C_KVW

kernel_name = "hybrid_deltanet_swa_axialrope_encoder"


def rmsnorm(x, g):
    xf = x.astype(jnp.float32)
    y = xf * lax.rsqrt(jnp.mean(xf * xf, axis=-1, keepdims=True) + EPS)
    return (y * g.astype(jnp.float32)).astype(x.dtype)


def l2norm(x):
    return x * lax.rsqrt(jnp.sum(x * x, axis=-1, keepdims=True) + EPS)


def front_pad(t, n):
    return jnp.pad(t, [(0, 0), (n, 0)] + [(0, 0)] * (t.ndim - 2))


def split_cols(t, sizes):
    idx = np.cumsum(np.array(sizes))[:-1].tolist()
    return jnp.split(t, idx, axis=-1)


def centred_conv(x, w):
    k = w.shape[0]
    h = k // 2
    n = x.shape[1]
    xp = jnp.pad(x, ((0, 0), (h, h), (0, 0)))
    y = xp[:, 0:n] * w[0]
    for j in range(1, k):
        y = y + xp[:, j:j + n] * w[j]
    return y


def gated_delta_chunked(q, k, v, beta, g):
    f32 = jnp.float32
    q, k, v, beta, g = (t.astype(f32) for t in (q, k, v, beta, g))
    bsz, t_len, nh, dk = q.shape
    dv = v.shape[-1]
    c = A_CHUNK
    nc = t_len // c

    def blk(t):
        return jnp.moveaxis(t.reshape(bsz, nc, c, nh, *t.shape[3:]), 3, 1)

    q, k, v, beta, g = blk(q), blk(k), blk(v), blk(beta), blk(g)
    gc = jnp.cumsum(g, axis=-1)
    incl = jnp.tril(jnp.ones((c, c), bool))
    strict = jnp.tril(jnp.ones((c, c), bool), -1)
    decay = jnp.exp(jnp.where(incl, gc[..., :, None] - gc[..., None, :], -jnp.inf))
    kb = k * beta[..., None]
    a_mat = jnp.where(strict, jnp.einsum('bhnid,bhnjd->bhnij', kb, k) * decay, 0.0)
    rhs = jnp.concatenate([v * beta[..., None], kb * jnp.exp(gc)[..., None]], axis=-1)
    sol = lax.linalg.triangular_solve(a_mat, rhs, left_side=True, lower=True,
                                      unit_diagonal=True)
    u, w = sol[..., :dv], sol[..., dv:]
    qk = jnp.einsum('bhnid,bhnjd->bhnij', q, k) * decay
    q_dec = q * jnp.exp(gc)[..., None]
    k_dec = k * jnp.exp(gc[..., -1:] - gc)[..., None]
    g_last = jnp.exp(gc[..., -1])

    def step(s, xs):
        q_n, k_n, u_n, w_n, qk_n, gl_n = xs
        v_new = u_n - jnp.einsum('bhck,bhkv->bhcv', w_n, s)
        o = jnp.einsum('bhck,bhkv->bhcv', q_n, s) + jnp.einsum('bhij,bhjv->bhiv', qk_n, v_new)
        s = s * gl_n[..., None, None] + jnp.einsum('bhck,bhcv->bhkv', k_n, v_new)
        return s, o

    xs = tuple(jnp.moveaxis(t, 2, 0) for t in (q_dec, k_dec, u, w, qk, g_last))
    s0 = jnp.zeros((bsz, nh, dk, dv), f32)
    _, o = lax.scan(step, s0, xs)
    return jnp.transpose(o, (1, 0, 3, 2, 4)).reshape(bsz, t_len, nh, dv)


def delta_mixer(hq, hk, hv, hz, hb, ha, conv_w, a_log, dt_bias, o_gain):
    bsz, n, _ = hq.shape
    qkv = jax.nn.silu(centred_conv(jnp.concatenate([hq, hk, hv], axis=-1), conv_w))
    q, k, v = split_cols(qkv, (A_QKW, A_QKW, A_VW))
    q = l2norm(q.reshape(bsz, n, A_HEADS, A_DK).astype(jnp.float32)) * (A_DK ** -0.5)
    k = l2norm(k.reshape(bsz, n, A_HEADS, A_DK).astype(jnp.float32))
    v = v.reshape(bsz, n, A_HEADS, A_DV).astype(jnp.float32)
    beta = jax.nn.sigmoid(hb.astype(jnp.float32)).reshape(bsz, n, 2, A_HEADS)
    g = -jnp.exp(a_log.astype(jnp.float32)) * jax.nn.softplus(
        ha.astype(jnp.float32).reshape(bsz, n, 2, A_HEADS) + dt_bias.astype(jnp.float32))
    q, k, v, beta, g = (front_pad(t, FRONT_PAD) for t in (q, k, v, beta, g))
    fwd = gated_delta_chunked(q, k, v, beta[:, :, 0], g[:, :, 0])
    rev = lambda t: jnp.flip(t, axis=1)
    bwd = rev(gated_delta_chunked(rev(q), rev(k), rev(v), rev(beta[:, :, 1]), rev(g[:, :, 1])))
    o = (fwd + bwd)[:, FRONT_PAD:]
    o = rmsnorm(o, o_gain) * jax.nn.silu(hz.reshape(bsz, n, A_HEADS, A_DV).astype(jnp.float32))
    return o.reshape(bsz, n, A_VW).astype(hq.dtype)


def window_mixer(q, k, v, sink):
    bsz, n = q.shape[:2]
    grp = B_HEADS // B_KV
    lp = n + FRONT_PAD
    nb = lp // BLOCK
    qb = front_pad(q, FRONT_PAD).reshape(bsz, nb, BLOCK, B_KV, grp, B_HD)

    def ext(t):
        t = jnp.pad(t, ((0, 0), (FRONT_PAD + BLOCK, BLOCK), (0, 0), (0, 0)))
        t = t.reshape(bsz, nb + 2, BLOCK, B_KV, B_HD)
        return jnp.concatenate([t[:, :-2], t[:, 1:-1], t[:, 2:]], axis=2)

    kw, vw = ext(k), ext(v)
    mk, mv = k[:, :N_META], v[:, :N_META]
    s_win = jnp.einsum('bnqkgd,bnskd->bnkgqs', qb, kw).astype(jnp.float32)
    s_meta = jnp.einsum('bnqkgd,bmkd->bnkgqm', qb, mk).astype(jnp.float32)
    r = jnp.arange(BLOCK)[:, None]
    cpos = jnp.arange(3 * BLOCK)[None, :]
    dist = jnp.abs(BLOCK + r - cpos)
    pk = (jnp.arange(nb)[:, None] - 1) * BLOCK + cpos
    in_seq = (pk >= BLOCK) & (pk < lp)
    valid = (dist <= B_WIN)[None] & in_seq[:, None, :]
    slopes = jnp.exp2(-8.0 * (jnp.arange(B_HEADS, dtype=jnp.float32) + 1.0) / B_HEADS)
    bias = -slopes.reshape(B_KV, grp, 1, 1) * dist.astype(jnp.float32)
    s_win = jnp.where(valid[None, :, None, None], s_win + bias, -jnp.inf)
    s_sink = jnp.broadcast_to(sink.astype(jnp.float32).reshape(B_KV, grp, 1, 1),
                              (bsz, nb, B_KV, grp, BLOCK, 1))
    p = jax.nn.softmax(jnp.concatenate([s_win, s_meta, s_sink], axis=-1), axis=-1)
    p_win = p[..., :3 * BLOCK].astype(v.dtype)
    p_meta = p[..., 3 * BLOCK:3 * BLOCK + N_META].astype(v.dtype)
    o = (jnp.einsum('bnkgqs,bnskd->bnqkgd', p_win, vw)
         + jnp.einsum('bnkgqm,bmkd->bnqkgd', p_meta, mv))
    return o.reshape(bsz, lp, B_HEADS * B_HD)[:, FRONT_PAD:]


def ab_mixer(u, w_in, conv_w, a_log, dt_bias, a_out_g, bq_g, bk_g, sink, w_out):
    bsz, n, _ = u.shape
    proj = u @ w_in
    qa, ka, va, za, ba, aa, qb, kb, vb = split_cols(proj, AB_SIZES)
    ya = delta_mixer(qa, ka, va, za, ba, aa, conv_w, a_log, dt_bias, a_out_g)
    qb = rmsnorm(qb.reshape(bsz, n, B_HEADS, B_HD), bq_g) * (B_HD ** -0.5)
    kb = rmsnorm(kb.reshape(bsz, n, B_KV, B_HD), bk_g)
    vb = vb.reshape(bsz, n, B_KV, B_HD)
    yb = window_mixer(qb, kb, vb, sink)
    return jnp.concatenate([ya, yb.astype(ya.dtype)], axis=-1) @ w_out


def axial_rope(rows):
    row = jnp.repeat(jnp.arange(rows), GRID_W)
    col = jnp.tile(jnp.arange(GRID_W), rows)
    meta = jnp.arange(N_META) - N_META
    row = jnp.concatenate([meta, row]).astype(jnp.float32)
    col = jnp.concatenate([meta, col]).astype(jnp.float32)
    axis_dim = C_HD // 2
    freqs = ROPE_THETA ** (-jnp.arange(0, axis_dim, 2, dtype=jnp.float32) / axis_dim)
    ang = jnp.concatenate([row[:, None] * freqs, col[:, None] * freqs], axis=-1)
    return jnp.cos(ang), jnp.sin(ang)


def apply_rope(x, cos, sin):
    xf = x.astype(jnp.float32).reshape(*x.shape[:-1], x.shape[-1] // 2, 2)
    x0, x1 = xf[..., 0], xf[..., 1]
    c = cos[None, :, None, :]
    s = sin[None, :, None, :]
    out = jnp.stack([x0 * c - x1 * s, x0 * s + x1 * c], axis=-1)
    return out.reshape(x.shape).astype(x.dtype)


def dense_mixer(q, k, v):
    bsz, n = q.shape[:2]
    grp = C_HEADS // C_KV
    lp = n + FRONT_PAD
    nb = lp // BLOCK
    qb = jnp.moveaxis(front_pad(q, FRONT_PAD).reshape(bsz, nb, BLOCK, C_KV, grp, C_HD), 1, 0)

    def one_block(qi):
        s = jnp.einsum('bqkgd,bskd->bkgqs', qi, k).astype(jnp.float32)
        p = jax.nn.softmax(s, axis=-1).astype(v.dtype)
        return jnp.einsum('bkgqs,bskd->bqkgd', p, v)

    o = lax.map(one_block, qb)
    return jnp.moveaxis(o, 0, 1).reshape(bsz, lp, C_QW)[:, FRONT_PAD:]


def c_mixer(u, w_qkv, qg, kg, w_out, cos, sin):
    bsz, n, _ = u.shape
    q, k, v = split_cols(u @ w_qkv, (C_QW, C_KVW, C_KVW))
    q = apply_rope(rmsnorm(q.reshape(bsz, n, C_HEADS, C_HD), qg), cos, sin) * (C_HD ** -0.5)
    k = apply_rope(rmsnorm(k.reshape(bsz, n, C_KV, C_HD), kg), cos, sin)
    v = v.reshape(bsz, n, C_KV, C_HD)
    return dense_mixer(q, k, v) @ w_out


def setup_inputs(seed: int = 0) -> dict:
    key = jax.random.key(seed)
    ks = jax.random.split(key, 20)
    n_even = (DEPTH + 1) // 2
    n_odd = DEPTH // 2
    nrm = lambda k, shape, scale: jax.random.normal(k, shape, jnp.float32) * scale
    gain = lambda k, shape: 1.0 + 0.02 * jax.random.normal(k, shape, jnp.float32)
    dt = jnp.exp(jax.random.uniform(ks[6], (n_even, 2, A_HEADS), jnp.float32,
                                    np.log(1e-3), np.log(1e-1)))
    return {
        "x": nrm(ks[0], (BATCH, SEQ, D_MODEL), 1.0),
        "meta_tokens": nrm(ks[1], (N_META, D_MODEL), 1.0),
        "attn_norm_g": gain(ks[2], (DEPTH, D_MODEL)),
        "mlp_norm_g": gain(ks[3], (DEPTH, D_MODEL)),
        "w_in_ab": nrm(ks[4], (n_even, D_MODEL, IN_AB), D_MODEL ** -0.5),
        "conv_w_a": nrm(ks[5], (n_even, A_CONV, 2 * A_QKW + A_VW), A_CONV ** -0.5),
        "a_log": jnp.log(jax.random.uniform(ks[7], (n_even, 2, A_HEADS), jnp.float32, 1.0, 16.0)),
        "dt_bias": dt + jnp.log(-jnp.expm1(-dt)),
        "a_out_norm_g": gain(ks[8], (n_even, A_DV)),
        "b_q_norm_g": gain(ks[9], (n_even, B_HD)),
        "b_k_norm_g": gain(ks[10], (n_even, B_HD)),
        "b_sink": nrm(ks[11], (n_even, B_HEADS), 0.5),
        "w_out_ab": nrm(ks[12], (n_even, MIX_AB, D_MODEL), MIX_AB ** -0.5),
        "w_qkv_c": nrm(ks[13], (n_odd, D_MODEL, IN_C), D_MODEL ** -0.5),
        "c_q_norm_g": gain(ks[14], (n_odd, C_HD)),
        "c_k_norm_g": gain(ks[15], (n_odd, C_HD)),
        "w_out_c": nrm(ks[16], (n_odd, C_QW, D_MODEL), C_QW ** -0.5),
        "w_ff1": nrm(ks[17], (DEPTH, D_MODEL, D_FF), D_MODEL ** -0.5),
        "w_ff2": nrm(ks[18], (DEPTH, D_FF, D_MODEL), D_FF ** -0.5),
    }


def reference(x, meta_tokens, attn_norm_g, mlp_norm_g, w_in_ab, conv_w_a, a_log, dt_bias,
              a_out_norm_g, b_q_norm_g, b_k_norm_g, b_sink, w_out_ab, w_qkv_c, c_q_norm_g,
              c_k_norm_g, w_out_c, w_ff1, w_ff2):
    bsz, n_tok, _ = x.shape
    rows = n_tok // GRID_W
    meta = jnp.broadcast_to(meta_tokens.astype(x.dtype)[None], (bsz, N_META, x.shape[-1]))
    h = jnp.concatenate([meta, x], axis=1)
    cos, sin = axial_rope(rows)
    for layer in range(DEPTH):
        i = layer // 2
        u = rmsnorm(h, attn_norm_g[layer])
        if layer % 2 == 0:
            mix = ab_mixer(u, w_in_ab[i], conv_w_a[i], a_log[i], dt_bias[i], a_out_norm_g[i],
                           b_q_norm_g[i], b_k_norm_g[i], b_sink[i], w_out_ab[i])
        else:
            mix = c_mixer(u, w_qkv_c[i], c_q_norm_g[i], c_k_norm_g[i], w_out_c[i], cos, sin)
        h = h + mix.astype(h.dtype)
        u = rmsnorm(h, mlp_norm_g[layer])
        h = h + jnp.square(jax.nn.relu(u @ w_ff1[layer])) @ w_ff2[layer]
    return h[:, N_META:]
```

```python
import functools
import math

import jax
import jax.numpy as jnp
from jax import lax
from jax.experimental import pallas as pl
from jax.experimental.pallas import tpu as pltpu

F32 = jnp.float32
BF16 = jnp.bfloat16

EPS = 1e-6
N_META = 16
BLK = 128
FRONT = BLK - N_META
GRID_W = 64
ROPE_THETA = 10000.0
A_HEADS, A_DK, A_CONV = 4, 128, 5
B_HEADS, B_KV, B_HD, B_WIN = 8, 2, 64, 128
C_HEADS, C_KV, C_HD = 8, 2, 128
NEG = -1e30

VMEM_LIMIT = 56 * 1024 * 1024
ROW_TILE = 512
FF_CHUNK = 512
INV_SQUARINGS = 6


def _cparams(*sem):
    return pltpu.CompilerParams(dimension_semantics=sem, vmem_limit_bytes=VMEM_LIMIT)


def _sigmoid(x):
    return 1.0 / (1.0 + jnp.exp(-x))


def _silu(x):
    return x * _sigmoid(x)


def _softplus(x):
    return jnp.maximum(x, 0.0) + jnp.log1p(jnp.exp(-jnp.abs(x)))


def _rms(x, g):
    return x * lax.rsqrt(jnp.mean(x * x, axis=-1, keepdims=True) + EPS) * g


def _dot(a, b):
    return jnp.dot(a, b, preferred_element_type=F32)


def _dot_nt(a, b):
    return lax.dot_general(a, b, (((1,), (1,)), ((), ())), preferred_element_type=F32)


def _ab_proj_kernel(h_ref, g_ref, wqkv_ref, wz_ref, wg_ref, wq_ref, wkv_ref, bqg_ref, bkg_ref,
                    qkv_ref, z_ref, gate_ref, qb_ref, kvb_ref):
    u = _rms(h_ref[...], g_ref[...]).astype(BF16)
    for c in range(3):
        sl = slice(c * 512, (c + 1) * 512)
        qkv_ref[:, sl] = _dot(u, wqkv_ref[:, sl])
    z_ref[...] = _dot(u, wz_ref[...])
    gate_ref[...] = _dot(u, wg_ref[...])
    qb = _dot(u, wq_ref[...])
    scale = B_HD ** -0.5
    for hh in range(B_HEADS):
        sl = slice(hh * B_HD, (hh + 1) * B_HD)
        qb_ref[:, sl] = (_rms(qb[:, sl], bqg_ref[...]) * scale).astype(BF16)
    kv = _dot(u, wkv_ref[...])
    for hh in range(B_KV):
        sl = slice(hh * B_HD, (hh + 1) * B_HD)
        kvb_ref[:, sl] = _rms(kv[:, sl], bkg_ref[...]).astype(BF16)
    kvb_ref[:, B_KV * B_HD:] = kv[:, B_KV * B_HD:].astype(BF16)


def _ab_proj(h, g, wqkv, wz, wg, wq, wkv, bqg, bkg):
    r, d = h.shape
    row = lambda n: pl.BlockSpec((ROW_TILE, n), lambda i: (i, 0))
    full = lambda a: pl.BlockSpec(a.shape, lambda i: (0, 0))
    return pl.pallas_call(
        _ab_proj_kernel,
        grid=(r // ROW_TILE,),
        in_specs=[row(d), full(g), full(wqkv), full(wz), full(wg), full(wq), full(wkv), full(bqg), full(bkg)],
        out_specs=[row(1536), row(512), row(512), row(512), row(256)],
        out_shape=[jax.ShapeDtypeStruct((r, 1536), F32), jax.ShapeDtypeStruct((r, 512), F32),
                   jax.ShapeDtypeStruct((r, 512), F32), jax.ShapeDtypeStruct((r, 512), BF16),
                   jax.ShapeDtypeStruct((r, 256), BF16)],
        compiler_params=_cparams("parallel"),
        name="ab_proj",
    )(h, g, wqkv, wz, wg, wq, wkv, bqg, bkg)


def _split3(x):
    hi = x.astype(BF16)
    r1 = x - hi.astype(F32)
    mid = r1.astype(BF16)
    lo = (r1 - mid.astype(F32)).astype(BF16)
    return hi, mid, lo


def _gates_kernel(pre_ref, alog_ref, dtb_ref, bg_ref, gt_ref):
    nblk = pre_ref.shape[0] // BLK
    ri = lax.broadcasted_iota(jnp.int32, (BLK, BLK), 0)
    ci = lax.broadcasted_iota(jnp.int32, (BLK, BLK), 1)
    lower = (ri >= ci).astype(BF16)
    upper = (ri <= ci).astype(BF16)
    neg_a = -jnp.exp(alog_ref[...])
    dtb = dtb_ref[...]

    def body(n, carry):
        rows = pl.ds(pl.multiple_of(n * BLK, BLK), BLK)
        x = pre_ref[rows, :]
        live = (ri + n * BLK) >= FRONT
        beta = jnp.where(live, _sigmoid(x), 0.0)
        g = jnp.where(live, neg_a * _softplus(x + dtb), 0.0)
        parts = _split3(g)
        pre = sum(_dot(lower, p) for p in parts)
        suf = sum(_dot(upper, p) for p in parts)
        tot = pre + suf - g
        out = jnp.where(ci < 2, beta, jnp.where(ci == 2, pre, jnp.where(ci == 3, suf, tot)))
        bg_ref[rows, :] = out
        gt_ref[n] = out.T[0:8, :]
        return carry

    lax.fori_loop(0, nblk, body, 0)


def _gates(pre, alog_rows, dtb_rows):
    b, lp, _ = pre.shape
    nblk = lp // BLK
    return pl.pallas_call(
        _gates_kernel,
        grid=(b, A_HEADS),
        in_specs=[pl.BlockSpec((None, lp, BLK), lambda i, j: (i, 0, j)),
                  pl.BlockSpec((None, 1, BLK), lambda i, j: (j, 0, 0)),
                  pl.BlockSpec((None, 1, BLK), lambda i, j: (j, 0, 0))],
        out_specs=[pl.BlockSpec((None, lp, BLK), lambda i, j: (i, 0, j)),
                   pl.BlockSpec((None, None, nblk, 8, BLK), lambda i, j: (i, j, 0, 0, 0))],
        out_shape=[jax.ShapeDtypeStruct((b, lp, A_HEADS * BLK), F32),
                   jax.ShapeDtypeStruct((b, A_HEADS, nblk, 8, BLK), F32)],
        compiler_params=_cparams("parallel", "parallel"),
        name="delta_gates",
    )(pre, alog_rows, dtb_rows)


def _delta_kernel(q_ref, k_ref, v_ref, z_ref, bg_ref, gt_ref, cwq_ref, cwk_ref, cwv_ref, og_ref, y_ref,
                  u_s, w_s, qk_s, qd_s, kdt_s, gl_s, of_s, ob_s):
    lp = q_ref.shape[0]
    nblk = lp // BLK
    ri = lax.broadcasted_iota(jnp.int32, (BLK, BLK), 0)
    ci = lax.broadcasted_iota(jnp.int32, (BLK, BLK), 1)
    eye = (ri == ci).astype(F32)
    incl = (ri >= ci, ri <= ci)
    strict = (ri > ci, ri < ci)

    def conv_silu(ref, w_ref, n):
        base = n * BLK
        cur = ref[pl.ds(pl.multiple_of(base, BLK), BLK), :]
        prev = ref[pl.ds(pl.multiple_of(jnp.maximum(base - 8, 0), 8), 8), :]
        nxt_start = jnp.minimum(base + BLK, lp - 8)
        nxt = ref[pl.ds(pl.multiple_of(nxt_start, 8), 8), :]
        nxt = jnp.where(n < nblk - 1, nxt, 0.0)
        win = jnp.concatenate([prev, cur, nxt], axis=0)
        h = A_CONV // 2
        acc = win[8 - h:8 - h + BLK, :] * w_ref[0:1, :]
        for j in range(1, A_CONV):
            acc = acc + win[8 - h + j:8 - h + j + BLK, :] * w_ref[j:j + 1, :]
        return _silu(acc)

    def l2n(x):
        return x * lax.rsqrt(jnp.sum(x * x, axis=-1, keepdims=True) + EPS)

    def prep(n, carry):
        rows = pl.ds(pl.multiple_of(n * BLK, BLK), BLK)
        live = (ri[:, 0:1] + n * BLK) >= FRONT
        qn = jnp.where(live, l2n(conv_silu(q_ref, cwq_ref, n)) * (A_DK ** -0.5), 0.0)
        kn = jnp.where(live, l2n(conv_silu(k_ref, cwk_ref, n)), 0.0)
        vv = jnp.where(live, conv_silu(v_ref, cwv_ref, n), 0.0)
        kn16 = kn.astype(BF16)
        kq = _dot_nt(jnp.concatenate([kn16, qn.astype(BF16)], axis=0), kn16)
        kk, qk = kq[:BLK], kq[BLK:]
        bg = bg_ref[rows, :]
        gt = gt_ref[n]
        for d in range(2):
            beta = bg[:, d:d + 1]
            ccol = bg[:, 2 + d:3 + d]
            tot = bg[:, 4 + d:5 + d]
            crow = gt[2 + d:3 + d, :]
            dec = jnp.exp(jnp.where(incl[d], ccol - crow, NEG))
            a = jnp.where(strict[d], beta * kk * dec, 0.0)
            t = eye - a
            x = a.astype(BF16)
            for s in range(INV_SQUARINGS):
                x2 = _dot(x, x)
                t = t + _dot(t.astype(BF16), x2.astype(BF16))
                x = x2.astype(BF16)
            ec = jnp.exp(ccol)
            rhs = jnp.concatenate([beta * vv, beta * kn * ec], axis=1).astype(BF16)
            sol = _dot(t.astype(BF16), rhs)
            u_s[d, rows, :] = sol[:, :BLK]
            w_s[d, rows, :] = sol[:, BLK:].astype(BF16)
            qk_s[d, rows, :] = (qk * dec).astype(BF16)
            qd_s[d, rows, :] = (qn * ec).astype(BF16)
            kdt_s[d, rows, :] = (kn * jnp.exp(tot - ccol)).T.astype(BF16)
            gl_s[d * nblk + n] = jnp.broadcast_to(jnp.exp(tot), (BLK, BLK))[0:8, :]
        return carry

    lax.fori_loop(0, nblk, prep, 0)

    def scan_step(d, n, s, o_ref):
        rows = pl.ds(pl.multiple_of(n * BLK, BLK), BLK)
        s16 = s.astype(BF16)
        wq = _dot(jnp.concatenate([w_s[d, rows, :], qd_s[d, rows, :]], axis=0), s16)
        v_new = (u_s[d, rows, :] - wq[:BLK]).astype(BF16)
        o_ref[rows, :] = wq[BLK:] + _dot(qk_s[d, rows, :], v_new)
        return s * gl_s[d * nblk + n][0:1, :] + _dot(kdt_s[d, rows, :], v_new)

    def scan(i, carry):
        sf, sb = carry
        sf = scan_step(0, i, sf, of_s)
        sb = scan_step(1, nblk - 1 - i, sb, ob_s)
        return sf, sb

    s0 = jnp.zeros((BLK, BLK), F32)
    lax.fori_loop(0, nblk, scan, (s0, s0))

    def finish(n, carry):
        rows = pl.ds(pl.multiple_of(n * BLK, BLK), BLK)
        o = of_s[rows, :] + ob_s[rows, :]
        y_ref[rows, :] = (_rms(o, og_ref[...]) * _silu(z_ref[rows, :])).astype(y_ref.dtype)
        return carry

    lax.fori_loop(0, nblk, finish, 0)


def _delta(qkv, z, bg, gt, conv_w, o_gain):
    b, lp, _ = qkv.shape
    nblk = lp // BLK
    col = lambda off: pl.BlockSpec((None, lp, BLK), lambda i, j: (i, 0, j + off))
    cw = lambda off: pl.BlockSpec((A_CONV, BLK), lambda i, j: (0, j + off))
    return pl.pallas_call(
        _delta_kernel,
        grid=(b, A_HEADS),
        in_specs=[col(0), col(A_HEADS), col(2 * A_HEADS), col(0), col(0),
                  pl.BlockSpec((None, None, nblk, 8, BLK), lambda i, j: (i, j, 0, 0, 0)),
                  cw(0), cw(A_HEADS), cw(2 * A_HEADS),
                  pl.BlockSpec((1, BLK), lambda i, j: (0, 0))],
        out_specs=col(0),
        out_shape=jax.ShapeDtypeStruct((b, lp, A_HEADS * BLK), BF16),
        scratch_shapes=[pltpu.VMEM((2, lp, BLK), F32), pltpu.VMEM((2, lp, BLK), BF16),
                        pltpu.VMEM((2, lp, BLK), BF16), pltpu.VMEM((2, lp, BLK), BF16),
                        pltpu.VMEM((2, lp, BLK), BF16), pltpu.VMEM((2 * nblk, 8, BLK), F32),
                        pltpu.VMEM((lp, BLK), F32), pltpu.VMEM((lp, BLK), F32)],
        compiler_params=_cparams("parallel", "parallel"),
        name="delta_mixer",
    )(qkv, qkv, qkv, z, bg, gt, conv_w, conv_w, conv_w, o_gain)


def _window_kernel(q_ref, kp_ref, kc_ref, kn_ref, km_ref, sink_ref, y_ref):
    i = pl.program_id(1)
    nblk = pl.num_programs(1)
    grp = B_HEADS // B_KV
    nk = 4 * BLK
    r = lax.broadcasted_iota(jnp.int32, (BLK, nk), 0)
    c = lax.broadcasted_iota(jnp.int32, (BLK, nk), 1)
    dist = jnp.abs(BLK + r - c)
    kblk = i - 1 + c // BLK
    win_ok = (c < 3 * BLK) & (dist <= B_WIN) & (kblk >= 1) & (kblk < nblk)
    meta_ok = c >= 3 * BLK + FRONT
    distf = dist.astype(F32)
    q = q_ref[...]
    kvs = (kp_ref[...], kc_ref[...], kn_ref[...], km_ref[...])
    for kvh in range(B_KV):
        ks = jnp.concatenate([t[:, kvh * B_HD:(kvh + 1) * B_HD] for t in kvs], axis=0)
        vs = jnp.concatenate([t[:, (B_KV + kvh) * B_HD:(B_KV + kvh + 1) * B_HD] for t in kvs], axis=0)
        for gi in range(grp):
            hh = kvh * grp + gi
            slope = 2.0 ** (-8.0 * (hh + 1.0) / B_HEADS)
            s = _dot_nt(q[:, hh * B_HD:(hh + 1) * B_HD], ks)
            s = jnp.where(win_ok, s - slope * distf, jnp.where(meta_ok, s, NEG))
            sink = sink_ref[hh:hh + 1, 0:1]
            m = jnp.maximum(jnp.max(s, axis=-1, keepdims=True), sink)
            p = jnp.exp(s - m)
            den = jnp.sum(p, axis=-1, keepdims=True) + jnp.exp(sink - m)
            o = _dot(p.astype(BF16), vs) / den
            y_ref[:, hh * B_HD:(hh + 1) * B_HD] = o.astype(y_ref.dtype)

    @pl.when(i == 0)
    def _():
        rr = lax.broadcasted_iota(jnp.int32, y_ref.shape, 0)
        y_ref[...] = jnp.where(rr >= FRONT, y_ref[...], 0).astype(y_ref.dtype)


def _window(qb, kvb, sink_rows):
    b, lp, _ = qb.shape
    nblk = lp // BLK
    kv = lambda f: pl.BlockSpec((None, BLK, 2 * B_KV * B_HD), f)
    return pl.pallas_call(
        _window_kernel,
        grid=(b, nblk),
        in_specs=[pl.BlockSpec((None, BLK, B_HEADS * B_HD), lambda i, j: (i, j, 0)),
                  kv(lambda i, j: (i, jnp.maximum(j - 1, 0), 0)),
                  kv(lambda i, j: (i, j, 0)),
                  kv(lambda i, j: (i, jnp.minimum(j + 1, nblk - 1), 0)),
                  kv(lambda i, j: (i, 0, 0)),
                  pl.BlockSpec((B_HEADS, BLK), lambda i, j: (0, 0))],
        out_specs=pl.BlockSpec((None, BLK, B_HEADS * B_HD), lambda i, j: (i, j, 0)),
        out_shape=jax.ShapeDtypeStruct((b, lp, B_HEADS * B_HD), BF16),
        compiler_params=_cparams("parallel", "parallel"),
        name="window_mixer",
    )(qb, kvb, kvb, kvb, kvb, sink_rows)


def _out_mlp_kernel(*refs, n_mix):
    h_ref = refs[0]
    mix_refs = refs[1:1 + n_mix]
    wo_ref, g_ref, w1_ref, w2_ref, o_ref = refs[1 + n_mix:]
    mix = jnp.concatenate([m[...] for m in mix_refs], axis=1)
    h = h_ref[...] + _dot(mix, wo_ref[...])
    u = _rms(h, g_ref[...]).astype(BF16)
    dff = w1_ref.shape[1]
    acc = h
    for c in range(dff // FF_CHUNK):
        sl = slice(c * FF_CHUNK, (c + 1) * FF_CHUNK)
        a = jnp.maximum(_dot(u, w1_ref[:, sl]), 0.0)
        acc = acc + _dot((a * a).astype(BF16), w2_ref[sl, :])
    o_ref[...] = acc


def _out_mlp(h, mixes, wo, g, w1, w2):
    r, d = h.shape
    row = lambda n: pl.BlockSpec((ROW_TILE, n), lambda i: (i, 0))
    full = lambda a: pl.BlockSpec(a.shape, lambda i: (0, 0))
    return pl.pallas_call(
        functools.partial(_out_mlp_kernel, n_mix=len(mixes)),
        grid=(r // ROW_TILE,),
        in_specs=[row(d)] + [row(m.shape[1]) for m in mixes] + [full(wo), full(g), full(w1), full(w2)],
        out_specs=row(d),
        out_shape=jax.ShapeDtypeStruct((r, d), F32),
        compiler_params=_cparams("parallel"),
        name="out_mlp",
    )(h, *mixes, wo, g, w1, w2)


def _c_proj_kernel(h_ref, g_ref, wq_ref, wk_ref, wv_ref, qg_ref, kg_ref, cos_ref, sin_ref,
                   q_ref, k_ref, v_ref):
    u = _rms(h_ref[...], g_ref[...]).astype(BF16)
    cosf = cos_ref[...]
    sinf = sin_ref[...]
    half = C_HD // 2

    def norm_rope(x, gain):
        x = _rms(x, gain)
        swapped = jnp.concatenate([x[:, half:], x[:, :half]], axis=1)
        return x * cosf + swapped * sinf

    q = _dot(u, wq_ref[...])
    for hh in range(C_HEADS):
        sl = slice(hh * C_HD, (hh + 1) * C_HD)
        q_ref[:, sl] = (norm_rope(q[:, sl], qg_ref[...]) * (C_HD ** -0.5)).astype(BF16)
    k = _dot(u, wk_ref[...])
    for hh in range(C_KV):
        sl = slice(hh * C_HD, (hh + 1) * C_HD)
        k_ref[:, sl] = norm_rope(k[:, sl], kg_ref[...]).astype(BF16)
    v_ref[...] = _dot(u, wv_ref[...]).astype(BF16)


def _c_proj(h, g, wq, wk, wv, qg, kg, cosf, sinf):
    b, lp, d = h.shape
    tm = 3 * BLK
    row = lambda n: pl.BlockSpec((None, tm, n), lambda i, j: (i, j, 0))
    full = lambda a: pl.BlockSpec(a.shape, lambda i, j: (0, 0))
    pos = pl.BlockSpec((tm, C_HD), lambda i, j: (j, 0))
    return pl.pallas_call(
        _c_proj_kernel,
        grid=(b, lp // tm),
        in_specs=[row(d), full(g), full(wq), full(wk), full(wv), full(qg), full(kg), pos, pos],
        out_specs=[row(C_HEADS * C_HD), row(C_KV * C_HD), row(C_KV * C_HD)],
        out_shape=[jax.ShapeDtypeStruct((b, lp, C_HEADS * C_HD), BF16),
                   jax.ShapeDtypeStruct((b, lp, C_KV * C_HD), BF16),
                   jax.ShapeDtypeStruct((b, lp, C_KV * C_HD), BF16)],
        compiler_params=_cparams("parallel", "parallel"),
        name="c_proj",
    )(h, g, wq, wk, wv, qg, kg, cosf, sinf)


ATT_TK = 512


def _dense_kernel(q_ref, k_ref, v_ref, y_ref):
    i = pl.program_id(2)
    grp = C_HEADS // C_KV
    lp = k_ref.shape[0]
    q = jnp.concatenate([q_ref[:, g * C_HD:(g + 1) * C_HD] for g in range(grp)], axis=0)
    m_rows = q.shape[0]

    s0 = _dot_nt(q, k_ref[0:BLK, :])
    kc = lax.broadcasted_iota(jnp.int32, (m_rows, BLK), 1)
    s0 = jnp.where(kc >= FRONT, s0, NEG)
    m = jnp.max(s0, axis=-1, keepdims=True)
    p0 = jnp.exp(s0 - m)
    l = jnp.sum(p0, axis=-1, keepdims=True)
    acc = _dot(p0.astype(BF16), v_ref[0:BLK, :])

    def body(t, carry):
        m, l, acc = carry
        rows = pl.ds(pl.multiple_of(BLK + t * ATT_TK, BLK), ATT_TK)
        s = _dot_nt(q, k_ref[rows, :])
        m_new = jnp.maximum(m, jnp.max(s, axis=-1, keepdims=True))
        alpha = jnp.exp(m - m_new)
        p = jnp.exp(s - m_new)
        l = alpha * l + jnp.sum(p, axis=-1, keepdims=True)
        acc = alpha * acc + _dot(p.astype(BF16), v_ref[rows, :])
        return m_new, l, acc

    m, l, acc = lax.fori_loop(0, (lp - BLK) // ATT_TK, body, (m, l, acc))
    o = acc / l
    if_first = i == 0
    rr = lax.broadcasted_iota(jnp.int32, (BLK, C_HD), 0)
    for g in range(grp):
        og = o[g * BLK:(g + 1) * BLK, :]
        og = jnp.where(if_first & (rr < FRONT), 0.0, og)
        y_ref[:, g * C_HD:(g + 1) * C_HD] = og.astype(y_ref.dtype)


def _dense(q, k, v):
    b, lp, _ = q.shape
    grp = C_HEADS // C_KV
    return pl.pallas_call(
        _dense_kernel,
        grid=(b, C_KV, lp // BLK),
        in_specs=[pl.BlockSpec((None, BLK, grp * C_HD), lambda i, j, t: (i, t, j)),
                  pl.BlockSpec((None, lp, C_HD), lambda i, j, t: (i, 0, j)),
                  pl.BlockSpec((None, lp, C_HD), lambda i, j, t: (i, 0, j))],
        out_specs=pl.BlockSpec((None, BLK, grp * C_HD), lambda i, j, t: (i, t, j)),
        out_shape=jax.ShapeDtypeStruct((b, lp, C_HEADS * C_HD), BF16),
        compiler_params=_cparams("parallel", "parallel", "arbitrary"),
        name="dense_mixer",
    )(q, k, v)


def _rope_tables(lp, n_tok):
    rows = n_tok // GRID_W
    row = jnp.repeat(jnp.arange(rows), GRID_W)
    col = jnp.tile(jnp.arange(GRID_W), rows)
    meta = jnp.arange(N_META) - N_META
    front = jnp.zeros((lp - N_META - n_tok,), jnp.int32)
    row = jnp.concatenate([front, meta, row]).astype(F32)
    col = jnp.concatenate([front, meta, col]).astype(F32)
    axis_dim = C_HD // 2
    freqs = ROPE_THETA ** (-jnp.arange(0, axis_dim, 2, dtype=F32) / axis_dim)
    ang = jnp.concatenate([row[:, None] * freqs, col[:, None] * freqs], axis=-1)
    cos, sin = jnp.cos(ang), jnp.sin(ang)
    return jnp.concatenate([cos, cos], axis=-1), jnp.concatenate([-sin, sin], axis=-1)


def _gate_weight(w_b, w_a):
    d = w_b.shape[0]
    w_b = w_b.reshape(d, 2, A_HEADS)
    w_a = w_a.reshape(d, 2, A_HEADS)
    per_head = jnp.concatenate([w_b, w_a, w_a], axis=1)
    per_head = jnp.transpose(per_head, (0, 2, 1))
    per_head = jnp.pad(per_head, ((0, 0), (0, 0), (0, BLK - 6)))
    return per_head.reshape(d, A_HEADS * BLK)


def _gate_rows(p):
    t = jnp.transpose(p.astype(F32), (1, 0))
    rows = jnp.concatenate([jnp.zeros_like(t), t, t], axis=1)
    return jnp.pad(rows, ((0, 0), (0, BLK - 6)))[:, None, :]


def kernel(x, meta_tokens, attn_norm_g, mlp_norm_g, w_in_ab, conv_w_a, a_log, dt_bias, a_out_norm_g,
           b_q_norm_g, b_k_norm_g, b_sink, w_out_ab, w_qkv_c, c_q_norm_g, c_k_norm_g, w_out_c, w_ff1, w_ff2):
    bsz, n_tok, d = x.shape
    lp = FRONT + N_META + n_tok
    depth = attn_norm_g.shape[0]
    meta = jnp.broadcast_to(meta_tokens.astype(x.dtype)[None], (bsz, N_META, d))
    h = jnp.concatenate([jnp.zeros((bsz, FRONT, d), x.dtype), meta, x], axis=1).reshape(bsz * lp, d)
    cosf, sinf = _rope_tables(lp, n_tok)
    deint = jnp.concatenate([jnp.arange(0, C_HD, 2), jnp.arange(1, C_HD, 2)])
    row2 = lambda v: v.astype(F32).reshape(1, -1)

    for layer in range(depth):
        i = layer // 2
        g_attn = row2(attn_norm_g[layer])
        if layer % 2 == 0:
            w = w_in_ab[i]
            qkv_w = w[:, :1536].astype(BF16)
            z_w = w[:, 1536:2048].astype(BF16)
            gate_w = _gate_weight(w[:, 2048:2056], w[:, 2056:2064]).astype(BF16)
            bq_w = w[:, 2064:2576].astype(BF16)
            bkv_w = w[:, 2576:2832].astype(BF16)
            qkv, z, gate_pre, qb, kvb = _ab_proj(h, g_attn, qkv_w, z_w, gate_w, bq_w, bkv_w,
                                                 row2(b_q_norm_g[i]), row2(b_k_norm_g[i]))
            r3 = lambda t: t.reshape(bsz, lp, t.shape[-1])
            bg, gt = _gates(r3(gate_pre), _gate_rows(a_log[i]), _gate_rows(dt_bias[i]))
            ya = _delta(r3(qkv), r3(z), bg, gt, conv_w_a[i].astype(F32), row2(a_out_norm_g[i]))
            sink_rows = jnp.broadcast_to(b_sink[i].astype(F32)[:, None], (B_HEADS, BLK))
            yb = _window(r3(qb), r3(kvb), sink_rows)
            wo = w_out_ab[i].astype(BF16)
            mixes = [ya.reshape(bsz * lp, -1), yb.reshape(bsz * lp, -1)]
        else:
            w = w_qkv_c[i]
            perm = lambda wc, nh: wc.reshape(d, nh, C_HD)[:, :, deint].reshape(d, nh * C_HD)
            wq = perm(w[:, :C_HEADS * C_HD], C_HEADS).astype(BF16)
            wk = perm(w[:, C_HEADS * C_HD:(C_HEADS + C_KV) * C_HD], C_KV).astype(BF16)
            wv = w[:, (C_HEADS + C_KV) * C_HD:].astype(BF16)
            q, k, v = _c_proj(h.reshape(bsz, lp, d), g_attn, wq, wk, wv,
                              row2(c_q_norm_g[i][deint]), row2(c_k_norm_g[i][deint]), cosf, sinf)
            att = _dense(q, k, v)
            mixes = [att.reshape(bsz * lp, -1)]
            wo = w_out_c[i].astype(BF16)
        h = _out_mlp(h, mixes, wo, row2(mlp_norm_g[layer]), w_ff1[layer].astype(BF16),
                     w_ff2[layer].astype(BF16))
    return h.reshape(bsz, lp, d)[:, FRONT + N_META:]
```

```python
import functools
import math

import jax
import jax.numpy as jnp
from jax import lax
from jax.experimental import pallas as pl
from jax.experimental.pallas import tpu as pltpu

F32 = jnp.float32
BF16 = jnp.bfloat16

EPS = 1e-6
N_META = 16
BLK = 128
FRONT = BLK - N_META
GRID_W = 64
ROPE_THETA = 10000.0
A_HEADS, A_DK, A_CONV = 4, 128, 5
B_HEADS, B_KV, B_HD, B_WIN = 8, 2, 64, 128
C_HEADS, C_KV, C_HD = 8, 2, 128
NEG = -1e30
LOG2E = math.log2(math.e)

VMEM_LIMIT = 56 * 1024 * 1024
ROW_TILE = 512
FF_CHUNK = 512
INV_SQUARINGS = 6
PREP_UNROLL = 3


def _cparams(*sem):
    return pltpu.CompilerParams(dimension_semantics=sem, vmem_limit_bytes=VMEM_LIMIT)


def _sigmoid(x):
    return 1.0 / (1.0 + jnp.exp(-x))


def _silu(x):
    return x * _sigmoid(x)


def _softplus(x):
    return jnp.maximum(x, 0.0) + jnp.log1p(jnp.exp(-jnp.abs(x)))


def _rms(x, g):
    return x * lax.rsqrt(jnp.mean(x * x, axis=-1, keepdims=True) + EPS) * g


def _dot(a, b):
    return jnp.dot(a, b, preferred_element_type=F32)


def _dot_nt(a, b):
    return lax.dot_general(a, b, (((1,), (1,)), ((), ())), preferred_element_type=F32)


def _ab_proj_kernel(h_ref, g_ref, wqkv_ref, wz_ref, wg_ref, wq_ref, wkv_ref, bqg_ref, bkg_ref,
                    qkv_ref, z_ref, gate_ref, qb_ref, kvb_ref):
    u = _rms(h_ref[...], g_ref[...]).astype(BF16)
    for c in range(3):
        sl = slice(c * 512, (c + 1) * 512)
        qkv_ref[:, sl] = _dot(u, wqkv_ref[:, sl])
    z_ref[...] = _dot(u, wz_ref[...])
    gate_ref[...] = _dot(u, wg_ref[...])
    qb = _dot(u, wq_ref[...])
    scale = B_HD ** -0.5 * LOG2E
    for hh in range(B_HEADS):
        sl = slice(hh * B_HD, (hh + 1) * B_HD)
        qb_ref[:, sl] = (_rms(qb[:, sl], bqg_ref[...]) * scale).astype(BF16)
    kv = _dot(u, wkv_ref[...])
    for hh in range(B_KV):
        sl = slice(hh * B_HD, (hh + 1) * B_HD)
        kvb_ref[:, sl] = _rms(kv[:, sl], bkg_ref[...]).astype(BF16)
    kvb_ref[:, B_KV * B_HD:] = kv[:, B_KV * B_HD:].astype(BF16)


def _ab_proj(h, g, wqkv, wz, wg, wq, wkv, bqg, bkg):
    r, d = h.shape
    row = lambda n: pl.BlockSpec((ROW_TILE, n), lambda i: (i, 0))
    full = lambda a: pl.BlockSpec(a.shape, lambda i: (0, 0))
    return pl.pallas_call(
        _ab_proj_kernel,
        grid=(r // ROW_TILE,),
        in_specs=[row(d), full(g), full(wqkv), full(wz), full(wg), full(wq), full(wkv), full(bqg), full(bkg)],
        out_specs=[row(1536), row(512), row(512), row(512), row(256)],
        out_shape=[jax.ShapeDtypeStruct((r, 1536), F32), jax.ShapeDtypeStruct((r, 512), F32),
                   jax.ShapeDtypeStruct((r, 512), F32), jax.ShapeDtypeStruct((r, 512), BF16),
                   jax.ShapeDtypeStruct((r, 256), BF16)],
        compiler_params=_cparams("parallel"),
        name="ab_proj",
    )(h, g, wqkv, wz, wg, wq, wkv, bqg, bkg)


def _split3(x):
    hi = x.astype(BF16)
    r1 = x - hi.astype(F32)
    mid = r1.astype(BF16)
    lo = (r1 - mid.astype(F32)).astype(BF16)
    return hi, mid, lo


def _gates_kernel(pre_ref, alog_ref, dtb_ref, bg_ref, gt_ref):
    nblk = pre_ref.shape[0] // BLK
    ri = lax.broadcasted_iota(jnp.int32, (BLK, BLK), 0)
    ci = lax.broadcasted_iota(jnp.int32, (BLK, BLK), 1)
    lower = (ri >= ci).astype(BF16)
    upper = (ri <= ci).astype(BF16)
    neg_a = -jnp.exp(alog_ref[...])
    dtb = dtb_ref[...]

    def one(n):
        x = pre_ref[pl.ds(pl.multiple_of(n * BLK, BLK), BLK), :]
        live = (ri + n * BLK) >= FRONT
        beta = jnp.where(live, _sigmoid(x), 0.0)
        g = jnp.where(live, neg_a * _softplus(x + dtb), 0.0)
        parts = _split3(g)
        pre = sum(_dot(lower, p) for p in parts)
        suf = sum(_dot(upper, p) for p in parts)
        tot = pre + suf - g
        return jnp.where(ci < 2, beta, jnp.where(ci == 2, pre, jnp.where(ci == 3, suf, tot)))

    def body(g, carry):
        outs = [one(g * PREP_UNROLL + j) for j in range(PREP_UNROLL)]
        for j, out in enumerate(outs):
            n = g * PREP_UNROLL + j
            bg_ref[pl.ds(pl.multiple_of(n * BLK, BLK), BLK), :] = out
            gt_ref[n] = out.T[0:8, :]
        return carry

    lax.fori_loop(0, nblk // PREP_UNROLL, body, 0)


def _gates(pre, alog_rows, dtb_rows):
    b, lp, _ = pre.shape
    nblk = lp // BLK
    return pl.pallas_call(
        _gates_kernel,
        grid=(b, A_HEADS),
        in_specs=[pl.BlockSpec((None, lp, BLK), lambda i, j: (i, 0, j)),
                  pl.BlockSpec((None, 1, BLK), lambda i, j: (j, 0, 0)),
                  pl.BlockSpec((None, 1, BLK), lambda i, j: (j, 0, 0))],
        out_specs=[pl.BlockSpec((None, lp, BLK), lambda i, j: (i, 0, j)),
                   pl.BlockSpec((None, None, nblk, 8, BLK), lambda i, j: (i, j, 0, 0, 0))],
        out_shape=[jax.ShapeDtypeStruct((b, lp, A_HEADS * BLK), F32),
                   jax.ShapeDtypeStruct((b, A_HEADS, nblk, 8, BLK), F32)],
        compiler_params=_cparams("parallel", "parallel"),
        name="delta_gates",
    )(pre, alog_rows, dtb_rows)


def _delta_kernel(q_ref, k_ref, v_ref, z_ref, bg_ref, gt_ref, cwq_ref, cwk_ref, cwv_ref, og_ref, y_ref,
                  sadd_s, smul_s, o_s, omul_s, gl_s):
    lp = q_ref.shape[0]
    nblk = lp // BLK
    ri = lax.broadcasted_iota(jnp.int32, (BLK, BLK), 0)
    ci = lax.broadcasted_iota(jnp.int32, (BLK, BLK), 1)
    eye = (ri == ci).astype(F32)
    incl = (ri >= ci, ri <= ci)
    strict = (ri > ci, ri < ci)

    def conv_silu(ref, w_ref, n):
        base = n * BLK
        cur = ref[pl.ds(pl.multiple_of(base, BLK), BLK), :]
        prev = ref[pl.ds(pl.multiple_of(jnp.maximum(base - 8, 0), 8), 8), :]
        nxt_start = jnp.minimum(base + BLK, lp - 8)
        nxt = ref[pl.ds(pl.multiple_of(nxt_start, 8), 8), :]
        nxt = jnp.where(n < nblk - 1, nxt, 0.0)
        win = jnp.concatenate([prev, cur, nxt], axis=0)
        h = A_CONV // 2
        acc = win[8 - h:8 - h + BLK, :] * w_ref[0:1, :]
        for j in range(1, A_CONV):
            acc = acc + win[8 - h + j:8 - h + j + BLK, :] * w_ref[j:j + 1, :]
        return _silu(acc)

    def l2n(x):
        return x * lax.rsqrt(jnp.sum(x * x, axis=-1, keepdims=True) + EPS)

    def chunk_inputs(n):
        rows = pl.ds(pl.multiple_of(n * BLK, BLK), BLK)
        live = (ri[:, 0:1] + n * BLK) >= FRONT
        qn = jnp.where(live, l2n(conv_silu(q_ref, cwq_ref, n)) * (A_DK ** -0.5), 0.0)
        kn = jnp.where(live, l2n(conv_silu(k_ref, cwk_ref, n)), 0.0)
        vv = jnp.where(live, conv_silu(v_ref, cwv_ref, n), 0.0)
        kn16 = kn.astype(BF16)
        kq = _dot_nt(jnp.concatenate([kn16, qn.astype(BF16)], axis=0), kn16)
        return dict(n=n, rows=rows, qn=qn, kn=kn, vv=vv, kk=kq[:BLK], qk=kq[BLK:], bg=bg_ref[rows, :], gt=gt_ref[n])

    def chain_setup(c, d):
        bg, gt = c["bg"], c["gt"]
        beta, ccol, tot = bg[:, d:d + 1], bg[:, 2 + d:3 + d], bg[:, 4 + d:5 + d]
        crow = gt[2 + d:3 + d, :]
        dec = jnp.exp(jnp.where(incl[d], ccol - crow, NEG))
        a = jnp.where(strict[d], beta * c["kk"] * dec, 0.0)
        return dict(c=c, d=d, beta=beta, ccol=ccol, tot=tot, dec=dec, t=eye - a, x=a.astype(BF16))

    def prep(g, carry):
        chunks = [chunk_inputs(g * PREP_UNROLL + j) for j in range(PREP_UNROLL)]
        chains = [chain_setup(c, d) for c in chunks for d in range(2)]
        for _ in range(INV_SQUARINGS):
            x2s = [_dot(ch["x"], ch["x"]) for ch in chains]
            for ch, x2 in zip(chains, x2s):
                ch["x"] = x2.astype(BF16)
            tx = [_dot(ch["t"].astype(BF16), ch["x"]) for ch in chains]
            for ch, v in zip(chains, tx):
                ch["t"] = ch["t"] + v
        uws = []
        for ch in chains:
            c = ch["c"]
            ch["ec"] = jnp.exp(ch["ccol"])
            rhs = jnp.concatenate([ch["beta"] * c["vv"], ch["beta"] * c["kn"] * ch["ec"]], axis=1).astype(BF16)
            uws.append(_dot(ch["t"].astype(BF16), rhs).astype(BF16))
        kuws = [_dot((ch["c"]["kn"] * jnp.exp(ch["tot"] - ch["ccol"])).T.astype(BF16), uw)
                for ch, uw in zip(chains, uws)]
        quws = [_dot((ch["c"]["qk"] * ch["dec"]).astype(BF16), uw) for ch, uw in zip(chains, uws)]
        for ch, kuw, quw in zip(chains, kuws, quws):
            c, d = ch["c"], ch["d"]
            rows = c["rows"]
            sadd_s[d, rows, :] = kuw[:, :BLK]
            smul_s[d, rows, :] = (-kuw[:, BLK:]).astype(BF16)
            o_s[d, rows, :] = quw[:, :BLK]
            omul_s[d, rows, :] = (c["qn"] * ch["ec"] - quw[:, BLK:]).astype(BF16)
            gl_s[d * nblk + c["n"]] = jnp.broadcast_to(jnp.exp(ch["tot"]), (BLK, BLK))[0:8, :]
        return carry

    lax.fori_loop(0, nblk // PREP_UNROLL, prep, 0)

    def scan_step(d, n, s):
        rows = pl.ds(pl.multiple_of(n * BLK, BLK), BLK)
        both = _dot(jnp.concatenate([smul_s[d, rows, :], omul_s[d, rows, :]], axis=0), s.astype(BF16))
        o_s[d, rows, :] = o_s[d, rows, :] + both[BLK:]
        return s * gl_s[d * nblk + n][0:1, :] + both[:BLK] + sadd_s[d, rows, :]

    def scan(i, carry):
        sf, sb = carry
        sf = scan_step(0, i, sf)
        sb = scan_step(1, nblk - 1 - i, sb)
        return sf, sb

    s0 = jnp.zeros((BLK, BLK), F32)
    lax.fori_loop(0, nblk, scan, (s0, s0))

    def finish(n, carry):
        rows = pl.ds(pl.multiple_of(n * BLK, BLK), BLK)
        o = o_s[0, rows, :] + o_s[1, rows, :]
        y_ref[rows, :] = (_rms(o, og_ref[...]) * _silu(z_ref[rows, :])).astype(y_ref.dtype)
        return carry

    lax.fori_loop(0, nblk, finish, 0, unroll=PREP_UNROLL)


def _delta(qkv, z, bg, gt, conv_w, o_gain):
    b, lp, _ = qkv.shape
    nblk = lp // BLK
    col = lambda off: pl.BlockSpec((None, lp, BLK), lambda i, j: (i, 0, j + off))
    cw = lambda off: pl.BlockSpec((A_CONV, BLK), lambda i, j: (0, j + off))
    return pl.pallas_call(
        _delta_kernel,
        grid=(b, A_HEADS),
        in_specs=[col(0), col(A_HEADS), col(2 * A_HEADS), col(0), col(0),
                  pl.BlockSpec((None, None, nblk, 8, BLK), lambda i, j: (i, j, 0, 0, 0)),
                  cw(0), cw(A_HEADS), cw(2 * A_HEADS),
                  pl.BlockSpec((1, BLK), lambda i, j: (0, 0))],
        out_specs=col(0),
        out_shape=jax.ShapeDtypeStruct((b, lp, A_HEADS * BLK), BF16),
        scratch_shapes=[pltpu.VMEM((2, lp, BLK), F32), pltpu.VMEM((2, lp, BLK), BF16),
                        pltpu.VMEM((2, lp, BLK), F32), pltpu.VMEM((2, lp, BLK), BF16),
                        pltpu.VMEM((2 * nblk, 8, BLK), F32)],
        compiler_params=_cparams("parallel", "parallel"),
        name="delta_mixer",
    )(qkv, qkv, qkv, z, bg, gt, conv_w, conv_w, conv_w, o_gain)


def _window_kernel(q_ref, kp_ref, kc_ref, kn_ref, km_ref, bias_ref, sink_ref, y_ref):
    i = pl.program_id(1)
    nblk = pl.num_programs(1)
    grp = B_HEADS // B_KV
    nk = 4 * BLK
    c = lax.broadcasted_iota(jnp.int32, (1, nk), 1)
    kblk = i - 1 + (c >> 7)
    edge = jnp.where((c >= 3 * BLK) | ((kblk >= 1) & (kblk < nblk)), 0.0, NEG)
    q = q_ref[...]
    kvs = (kp_ref[...], kc_ref[...], kn_ref[...], km_ref[...])
    ones = jnp.ones((nk, B_HD), BF16)
    for kvh in range(B_KV):
        ks = jnp.concatenate([t[:, kvh * B_HD:(kvh + 1) * B_HD] for t in kvs], axis=0)
        vs = jnp.concatenate([t[:, (B_KV + kvh) * B_HD:(B_KV + kvh + 1) * B_HD] for t in kvs], axis=0)
        heads = range(kvh * grp, (kvh + 1) * grp)
        q4 = jnp.concatenate([q[:, hh * B_HD:(hh + 1) * B_HD] for hh in heads], axis=0)
        s4 = _dot_nt(q4, ks)
        ps, ms = [], []
        for gi, hh in enumerate(heads):
            s = s4[gi * BLK:(gi + 1) * BLK] + bias_ref[hh] + edge
            m = jnp.maximum(jnp.max(s, axis=-1, keepdims=True), sink_ref[hh:hh + 1, 0:1])
            ps.append(jnp.exp2(s - m).astype(BF16))
            ms.append(m)
        pv = _dot(jnp.concatenate(ps, axis=0), jnp.concatenate([vs, ones], axis=1))
        for gi, hh in enumerate(heads):
            o = pv[gi * BLK:(gi + 1) * BLK]
            den = o[:, B_HD:B_HD + 1] + jnp.exp2(sink_ref[hh:hh + 1, 0:1] - ms[gi])
            y_ref[:, hh * B_HD:(hh + 1) * B_HD] = (o[:, :B_HD] / den).astype(y_ref.dtype)

    @pl.when(i == 0)
    def _():
        rr = lax.broadcasted_iota(jnp.int32, y_ref.shape, 0)
        y_ref[...] = jnp.where(rr >= FRONT, y_ref[...], 0).astype(y_ref.dtype)


def _window_bias():
    r = jnp.arange(BLK)[:, None]
    c = jnp.arange(4 * BLK)[None, :]
    dist = jnp.abs(BLK + r - c)
    slopes = jnp.exp2(-8.0 * (jnp.arange(B_HEADS, dtype=F32) + 1.0) / B_HEADS)
    band = (c < 3 * BLK) & (dist <= B_WIN)
    alibi = -slopes[:, None, None] * dist.astype(F32)[None] * LOG2E
    rest = jnp.where(c >= 3 * BLK + FRONT, 0.0, NEG)
    return jnp.where(band[None], alibi, rest[None]).astype(F32)


def _window(qb, kvb, sink_rows):
    b, lp, _ = qb.shape
    nblk = lp // BLK
    bias = _window_bias()
    kv = lambda f: pl.BlockSpec((None, BLK, 2 * B_KV * B_HD), f)
    return pl.pallas_call(
        _window_kernel,
        grid=(b, nblk),
        in_specs=[pl.BlockSpec((None, BLK, B_HEADS * B_HD), lambda i, j: (i, j, 0)),
                  kv(lambda i, j: (i, jnp.maximum(j - 1, 0), 0)),
                  kv(lambda i, j: (i, j, 0)),
                  kv(lambda i, j: (i, jnp.minimum(j + 1, nblk - 1), 0)),
                  kv(lambda i, j: (i, 0, 0)),
                  pl.BlockSpec(bias.shape, lambda i, j: (0, 0, 0)),
                  pl.BlockSpec((B_HEADS, BLK), lambda i, j: (0, 0))],
        out_specs=pl.BlockSpec((None, BLK, B_HEADS * B_HD), lambda i, j: (i, j, 0)),
        out_shape=jax.ShapeDtypeStruct((b, lp, B_HEADS * B_HD), BF16),
        compiler_params=_cparams("parallel", "parallel"),
        name="window_mixer",
    )(qb, kvb, kvb, kvb, kvb, bias, sink_rows)


def _out_mlp_kernel(*refs, n_mix):
    h_ref = refs[0]
    mix_refs = refs[1:1 + n_mix]
    wo_ref, g_ref, w1_ref, w2_ref, o_ref = refs[1 + n_mix:]
    mix = jnp.concatenate([m[...] for m in mix_refs], axis=1)
    h = h_ref[...] + _dot(mix, wo_ref[...])
    u = _rms(h, g_ref[...]).astype(BF16)
    dff = w1_ref.shape[1]
    acc = h
    for c in range(dff // FF_CHUNK):
        sl = slice(c * FF_CHUNK, (c + 1) * FF_CHUNK)
        a = jnp.maximum(_dot(u, w1_ref[:, sl]), 0.0)
        acc = acc + _dot((a * a).astype(BF16), w2_ref[sl, :])
    o_ref[...] = acc


def _out_mlp(h, mixes, wo, g, w1, w2):
    r, d = h.shape
    row = lambda n: pl.BlockSpec((ROW_TILE, n), lambda i: (i, 0))
    full = lambda a: pl.BlockSpec(a.shape, lambda i: (0, 0))
    return pl.pallas_call(
        functools.partial(_out_mlp_kernel, n_mix=len(mixes)),
        grid=(r // ROW_TILE,),
        in_specs=[row(d)] + [row(m.shape[1]) for m in mixes] + [full(wo), full(g), full(w1), full(w2)],
        out_specs=row(d),
        out_shape=jax.ShapeDtypeStruct((r, d), F32),
        compiler_params=_cparams("parallel"),
        name="out_mlp",
    )(h, *mixes, wo, g, w1, w2)


def _c_proj_kernel(h_ref, g_ref, wq_ref, wk_ref, wv_ref, qg_ref, kg_ref, cos_ref, sin_ref,
                   q_ref, k_ref, v_ref):
    u = _rms(h_ref[...], g_ref[...]).astype(BF16)
    cosf = cos_ref[...]
    sinf = sin_ref[...]
    half = C_HD // 2

    def norm_rope(x, gain):
        x = _rms(x, gain)
        swapped = jnp.concatenate([x[:, half:], x[:, :half]], axis=1)
        return x * cosf + swapped * sinf

    q = _dot(u, wq_ref[...])
    for hh in range(C_HEADS):
        sl = slice(hh * C_HD, (hh + 1) * C_HD)
        q_ref[:, sl] = (norm_rope(q[:, sl], qg_ref[...]) * (C_HD ** -0.5 * LOG2E)).astype(BF16)
    k = _dot(u, wk_ref[...])
    for hh in range(C_KV):
        sl = slice(hh * C_HD, (hh + 1) * C_HD)
        k_ref[:, sl] = norm_rope(k[:, sl], kg_ref[...]).astype(BF16)
    v_ref[...] = _dot(u, wv_ref[...]).astype(BF16)


def _c_proj(h, g, wq, wk, wv, qg, kg, cosf, sinf):
    b, lp, d = h.shape
    tm = 3 * BLK
    row = lambda n: pl.BlockSpec((None, tm, n), lambda i, j: (i, j, 0))
    full = lambda a: pl.BlockSpec(a.shape, lambda i, j: (0, 0))
    pos = pl.BlockSpec((tm, C_HD), lambda i, j: (j, 0))
    return pl.pallas_call(
        _c_proj_kernel,
        grid=(b, lp // tm),
        in_specs=[row(d), full(g), full(wq), full(wk), full(wv), full(qg), full(kg), pos, pos],
        out_specs=[row(C_HEADS * C_HD), row(C_KV * C_HD), row(C_KV * C_HD)],
        out_shape=[jax.ShapeDtypeStruct((b, lp, C_HEADS * C_HD), BF16),
                   jax.ShapeDtypeStruct((b, lp, C_KV * C_HD), BF16),
                   jax.ShapeDtypeStruct((b, lp, C_KV * C_HD), BF16)],
        compiler_params=_cparams("parallel", "parallel"),
        name="c_proj",
    )(h, g, wq, wk, wv, qg, kg, cosf, sinf)


ATT_TK = 512


def _dense_kernel(q_ref, k_ref, v_ref, y_ref, sa_s, sb_s, acc_s):
    i = pl.program_id(2)
    grp = C_HEADS // C_KV
    lp = k_ref.shape[0]
    nkb = (lp - BLK) // ATT_TK
    q = jnp.concatenate([q_ref[:, g * C_HD:(g + 1) * C_HD] for g in range(grp)], axis=0)
    m_rows = q.shape[0]

    def keys(t):
        return pl.ds(pl.multiple_of(BLK + t * ATT_TK, BLK), ATT_TK)

    def v_ones(rows, n):
        return jnp.concatenate([v_ref[rows, :], jnp.ones((n, C_HD), BF16)], axis=1)

    def scores(t, s_ref):
        s_ref[...] = _dot_nt(q, k_ref[keys(t), :])

    def step(t, m, s_ref):
        s = s_ref[...]
        m_new = jnp.maximum(m, jnp.max(s, axis=-1, keepdims=True))
        p = jnp.exp2(s - m_new).astype(BF16)
        acc_s[...] = jnp.exp2(m - m_new) * acc_s[...] + _dot(p, v_ones(keys(t), ATT_TK))
        return m_new

    scores(0, sa_s)
    s0 = _dot_nt(q, k_ref[0:BLK, :])
    kc = lax.broadcasted_iota(jnp.int32, (m_rows, BLK), 1)
    s0 = jnp.where(kc >= FRONT, s0, NEG)
    m = jnp.max(s0, axis=-1, keepdims=True)
    acc_s[...] = _dot(jnp.exp2(s0 - m).astype(BF16), v_ones(slice(0, BLK), BLK))

    def body(j, m):
        scores(2 * j + 1, sb_s)
        m = step(2 * j, m, sa_s)
        scores(2 * j + 2, sa_s)
        return step(2 * j + 1, m, sb_s)

    m = lax.fori_loop(0, nkb // 2 - 1, body, m)
    scores(nkb - 1, sb_s)
    m = step(nkb - 2, m, sa_s)
    m = step(nkb - 1, m, sb_s)
    acc = acc_s[...]
    o = acc[:, :C_HD] / acc[:, C_HD:C_HD + 1]
    if_first = i == 0
    rr = lax.broadcasted_iota(jnp.int32, (BLK, C_HD), 0)
    for g in range(grp):
        og = o[g * BLK:(g + 1) * BLK, :]
        og = jnp.where(if_first & (rr < FRONT), 0.0, og)
        y_ref[:, g * C_HD:(g + 1) * C_HD] = og.astype(y_ref.dtype)


def _dense(q, k, v):
    b, lp, _ = q.shape
    grp = C_HEADS // C_KV
    return pl.pallas_call(
        _dense_kernel,
        grid=(b, C_KV, lp // BLK),
        in_specs=[pl.BlockSpec((None, BLK, grp * C_HD), lambda i, j, t: (i, t, j)),
                  pl.BlockSpec((None, lp, C_HD), lambda i, j, t: (i, 0, j)),
                  pl.BlockSpec((None, lp, C_HD), lambda i, j, t: (i, 0, j))],
        out_specs=pl.BlockSpec((None, BLK, grp * C_HD), lambda i, j, t: (i, t, j)),
        out_shape=jax.ShapeDtypeStruct((b, lp, C_HEADS * C_HD), BF16),
        scratch_shapes=[pltpu.VMEM((grp * BLK, ATT_TK), F32), pltpu.VMEM((grp * BLK, ATT_TK), F32),
                        pltpu.VMEM((grp * BLK, 2 * C_HD), F32)],
        compiler_params=_cparams("parallel", "parallel", "arbitrary"),
        name="dense_mixer",
    )(q, k, v)


def _rope_tables(lp, n_tok):
    rows = n_tok // GRID_W
    row = jnp.repeat(jnp.arange(rows), GRID_W)
    col = jnp.tile(jnp.arange(GRID_W), rows)
    meta = jnp.arange(N_META) - N_META
    front = jnp.zeros((lp - N_META - n_tok,), jnp.int32)
    row = jnp.concatenate([front, meta, row]).astype(F32)
    col = jnp.concatenate([front, meta, col]).astype(F32)
    axis_dim = C_HD // 2
    freqs = ROPE_THETA ** (-jnp.arange(0, axis_dim, 2, dtype=F32) / axis_dim)
    ang = jnp.concatenate([row[:, None] * freqs, col[:, None] * freqs], axis=-1)
    cos, sin = jnp.cos(ang), jnp.sin(ang)
    return jnp.concatenate([cos, cos], axis=-1), jnp.concatenate([-sin, sin], axis=-1)


def _gate_weight(w_b, w_a):
    d = w_b.shape[0]
    w_b = w_b.reshape(d, 2, A_HEADS)
    w_a = w_a.reshape(d, 2, A_HEADS)
    per_head = jnp.concatenate([w_b, w_a, w_a], axis=1)
    per_head = jnp.transpose(per_head, (0, 2, 1))
    per_head = jnp.pad(per_head, ((0, 0), (0, 0), (0, BLK - 6)))
    return per_head.reshape(d, A_HEADS * BLK)


def _gate_rows(p):
    t = jnp.transpose(p.astype(F32), (1, 0))
    rows = jnp.concatenate([jnp.zeros_like(t), t, t], axis=1)
    return jnp.pad(rows, ((0, 0), (0, BLK - 6)))[:, None, :]


def kernel(x, meta_tokens, attn_norm_g, mlp_norm_g, w_in_ab, conv_w_a, a_log, dt_bias, a_out_norm_g,
           b_q_norm_g, b_k_norm_g, b_sink, w_out_ab, w_qkv_c, c_q_norm_g, c_k_norm_g, w_out_c, w_ff1, w_ff2):
    bsz, n_tok, d = x.shape
    lp = FRONT + N_META + n_tok
    depth = attn_norm_g.shape[0]
    meta = jnp.broadcast_to(meta_tokens.astype(x.dtype)[None], (bsz, N_META, d))
    h = jnp.concatenate([jnp.zeros((bsz, FRONT, d), x.dtype), meta, x], axis=1).reshape(bsz * lp, d)
    cosf, sinf = _rope_tables(lp, n_tok)
    deint = jnp.concatenate([jnp.arange(0, C_HD, 2), jnp.arange(1, C_HD, 2)])
    row2 = lambda v: v.astype(F32).reshape(1, -1)

    for layer in range(depth):
        i = layer // 2
        g_attn = row2(attn_norm_g[layer])
        if layer % 2 == 0:
            w = w_in_ab[i]
            qkv_w = w[:, :1536].astype(BF16)
            z_w = w[:, 1536:2048].astype(BF16)
            gate_w = _gate_weight(w[:, 2048:2056], w[:, 2056:2064]).astype(BF16)
            bq_w = w[:, 2064:2576].astype(BF16)
            bkv_w = w[:, 2576:2832].astype(BF16)
            qkv, z, gate_pre, qb, kvb = _ab_proj(h, g_attn, qkv_w, z_w, gate_w, bq_w, bkv_w,
                                                 row2(b_q_norm_g[i]), row2(b_k_norm_g[i]))
            r3 = lambda t: t.reshape(bsz, lp, t.shape[-1])
            bg, gt = _gates(r3(gate_pre), _gate_rows(a_log[i]), _gate_rows(dt_bias[i]))
            ya = _delta(r3(qkv), r3(z), bg, gt, conv_w_a[i].astype(F32), row2(a_out_norm_g[i]))
            sink_rows = jnp.broadcast_to(b_sink[i].astype(F32)[:, None] * LOG2E, (B_HEADS, BLK))
            yb = _window(r3(qb), r3(kvb), sink_rows)
            wo = w_out_ab[i].astype(BF16)
            mixes = [ya.reshape(bsz * lp, -1), yb.reshape(bsz * lp, -1)]
        else:
            w = w_qkv_c[i]
            perm = lambda wc, nh: wc.reshape(d, nh, C_HD)[:, :, deint].reshape(d, nh * C_HD)
            wq = perm(w[:, :C_HEADS * C_HD], C_HEADS).astype(BF16)
            wk = perm(w[:, C_HEADS * C_HD:(C_HEADS + C_KV) * C_HD], C_KV).astype(BF16)
            wv = w[:, (C_HEADS + C_KV) * C_HD:].astype(BF16)
            q, k, v = _c_proj(h.reshape(bsz, lp, d), g_attn, wq, wk, wv,
                              row2(c_q_norm_g[i][deint]), row2(c_k_norm_g[i][deint]), cosf, sinf)
            att = _dense(q, k, v)
            mixes = [att.reshape(bsz * lp, -1)]
            wo = w_out_c[i].astype(BF16)
        h = _out_mlp(h, mixes, wo, row2(mlp_norm_g[layer]), w_ff1[layer].astype(BF16),
                     w_ff2[layer].astype(BF16))
    return h.reshape(bsz, lp, d)[:, FRONT + N_META:]
```

```python
import functools
import math

import jax
import jax.numpy as jnp
from jax import lax
from jax.experimental import pallas as pl
from jax.experimental.pallas import tpu as pltpu

F32 = jnp.float32
BF16 = jnp.bfloat16

EPS = 1e-6
N_META = 16
BLK = 128
FRONT = BLK - N_META
GRID_W = 64
ROPE_THETA = 10000.0
A_HEADS, A_DK, A_CONV = 4, 128, 5
B_HEADS, B_KV, B_HD, B_WIN = 8, 2, 64, 128
C_HEADS, C_KV, C_HD = 8, 2, 128
NEG = -1e30
LOG2E = math.log2(math.e)

VMEM_LIMIT = 56 * 1024 * 1024
ROW_TILE = 512
FF_CHUNK = 512
INV_SQUARINGS = 6
PREP_UNROLL = 3


def _cparams(*sem):
    return pltpu.CompilerParams(dimension_semantics=sem, vmem_limit_bytes=VMEM_LIMIT)


def _sigmoid(x):
    return 1.0 / (1.0 + jnp.exp(-x))


def _silu(x):
    return x * _sigmoid(x)


def _softplus(x):
    return jnp.maximum(x, 0.0) + jnp.log1p(jnp.exp(-jnp.abs(x)))


def _rms(x, g):
    return x * lax.rsqrt(jnp.mean(x * x, axis=-1, keepdims=True) + EPS) * g


def _dot(a, b):
    return jnp.dot(a, b, preferred_element_type=F32)


def _dot_nt(a, b):
    return lax.dot_general(a, b, (((1,), (1,)), ((), ())), preferred_element_type=F32)


def _ab_proj_kernel(h_ref, g_ref, wqkv_ref, wz_ref, wg_ref, wq_ref, wkv_ref, bqg_ref, bkg_ref,
                    qkv_ref, z_ref, gate_ref, qb_ref, kvb_ref):
    u = _rms(h_ref[...], g_ref[...]).astype(BF16)
    qb = _dot(u, wq_ref[...])
    kv = _dot(u, wkv_ref[...])
    scale = B_HD ** -0.5 * LOG2E

    def q_heads(lo, hi):
        for hh in range(lo, hi):
            sl = slice(hh * B_HD, (hh + 1) * B_HD)
            qb_ref[:, sl] = (_rms(qb[:, sl], bqg_ref[...]) * scale).astype(BF16)

    qkv_ref[:, 0:512] = _dot(u, wqkv_ref[:, 0:512])
    q_heads(0, 3)
    qkv_ref[:, 512:1024] = _dot(u, wqkv_ref[:, 512:1024])
    q_heads(3, 6)
    qkv_ref[:, 1024:1536] = _dot(u, wqkv_ref[:, 1024:1536])
    q_heads(6, B_HEADS)
    z_ref[...] = _dot(u, wz_ref[...])
    for hh in range(B_KV):
        sl = slice(hh * B_HD, (hh + 1) * B_HD)
        kvb_ref[:, sl] = _rms(kv[:, sl], bkg_ref[...]).astype(BF16)
    kvb_ref[:, B_KV * B_HD:] = kv[:, B_KV * B_HD:].astype(BF16)
    gate_ref[...] = _dot(u, wg_ref[...])


def _ab_proj(h, g, wqkv, wz, wg, wq, wkv, bqg, bkg):
    r, d = h.shape
    row = lambda n: pl.BlockSpec((ROW_TILE, n), lambda i: (i, 0))
    full = lambda a: pl.BlockSpec(a.shape, lambda i: (0, 0))
    return pl.pallas_call(
        _ab_proj_kernel,
        grid=(r // ROW_TILE,),
        in_specs=[row(d), full(g), full(wqkv), full(wz), full(wg), full(wq), full(wkv), full(bqg), full(bkg)],
        out_specs=[row(1536), row(512), row(512), row(512), row(256)],
        out_shape=[jax.ShapeDtypeStruct((r, 1536), F32), jax.ShapeDtypeStruct((r, 512), F32),
                   jax.ShapeDtypeStruct((r, 512), F32), jax.ShapeDtypeStruct((r, 512), BF16),
                   jax.ShapeDtypeStruct((r, 256), BF16)],
        compiler_params=_cparams("parallel"),
        name="ab_proj",
    )(h, g, wqkv, wz, wg, wq, wkv, bqg, bkg)


def _split2(x):
    hi = x.astype(BF16)
    return hi, (x - hi.astype(F32)).astype(BF16)


def _split3(x):
    hi = x.astype(BF16)
    r1 = x - hi.astype(F32)
    mid = r1.astype(BF16)
    lo = (r1 - mid.astype(F32)).astype(BF16)
    return hi, mid, lo


def _gates_kernel(pre_ref, alog_ref, dtb_ref, bg_ref, gt_ref):
    nblk = pre_ref.shape[0] // BLK
    ri = lax.broadcasted_iota(jnp.int32, (BLK, BLK), 0)
    ci = lax.broadcasted_iota(jnp.int32, (BLK, BLK), 1)
    lower = (ri >= ci).astype(BF16)
    upper = (ri <= ci).astype(BF16)
    neg_a = -jnp.exp(alog_ref[...])
    dtb = dtb_ref[...]

    def one(n):
        x = pre_ref[pl.ds(pl.multiple_of(n * BLK, BLK), BLK), :]
        live = (ri + n * BLK) >= FRONT
        beta = jnp.where(live, _sigmoid(x), 0.0)
        g = jnp.where(live, neg_a * _softplus(x + dtb), 0.0)
        parts = _split3(g)
        pre = sum(_dot(lower, p) for p in parts)
        suf = sum(_dot(upper, p) for p in parts)
        tot = pre + suf - g
        return jnp.where(ci < 2, beta, jnp.where(ci == 2, pre, jnp.where(ci == 3, suf, tot)))

    def body(g, carry):
        outs = [one(g * PREP_UNROLL + j) for j in range(PREP_UNROLL)]
        for j, out in enumerate(outs):
            n = g * PREP_UNROLL + j
            bg_ref[pl.ds(pl.multiple_of(n * BLK, BLK), BLK), :] = out
            gt_ref[n] = out.T[0:8, :]
        return carry

    lax.fori_loop(0, nblk // PREP_UNROLL, body, 0)


def _gates(pre, alog_rows, dtb_rows):
    b, lp, _ = pre.shape
    nblk = lp // BLK
    return pl.pallas_call(
        _gates_kernel,
        grid=(b, A_HEADS),
        in_specs=[pl.BlockSpec((None, lp, BLK), lambda i, j: (i, 0, j)),
                  pl.BlockSpec((None, 1, BLK), lambda i, j: (j, 0, 0)),
                  pl.BlockSpec((None, 1, BLK), lambda i, j: (j, 0, 0))],
        out_specs=[pl.BlockSpec((None, lp, BLK), lambda i, j: (i, 0, j)),
                   pl.BlockSpec((None, None, nblk, 8, BLK), lambda i, j: (i, j, 0, 0, 0))],
        out_shape=[jax.ShapeDtypeStruct((b, lp, A_HEADS * BLK), F32),
                   jax.ShapeDtypeStruct((b, A_HEADS, nblk, 8, BLK), F32)],
        compiler_params=_cparams("parallel", "parallel"),
        name="delta_gates",
    )(pre, alog_rows, dtb_rows)


def _delta_kernel(q_ref, k_ref, v_ref, z_ref, bg_ref, gt_ref, cwq_ref, cwk_ref, cwv_ref, og_ref, y_ref,
                  sadd_s, smul_s, o_s, omul_s, gl_s):
    lp = q_ref.shape[0]
    nblk = lp // BLK
    ri = lax.broadcasted_iota(jnp.int32, (BLK, BLK), 0)
    ci = lax.broadcasted_iota(jnp.int32, (BLK, BLK), 1)
    eye = (ri == ci).astype(F32)
    incl = (ri >= ci, ri <= ci)
    strict = (ri > ci, ri < ci)

    def conv_silu(ref, w_ref, n):
        base = n * BLK
        cur = ref[pl.ds(pl.multiple_of(base, BLK), BLK), :]
        prev = ref[pl.ds(pl.multiple_of(jnp.maximum(base - 8, 0), 8), 8), :]
        nxt_start = jnp.minimum(base + BLK, lp - 8)
        nxt = ref[pl.ds(pl.multiple_of(nxt_start, 8), 8), :]
        nxt = jnp.where(n < nblk - 1, nxt, 0.0)
        win = jnp.concatenate([prev, cur, nxt], axis=0)
        h = A_CONV // 2
        acc = win[8 - h:8 - h + BLK, :] * w_ref[0:1, :]
        for j in range(1, A_CONV):
            acc = acc + win[8 - h + j:8 - h + j + BLK, :] * w_ref[j:j + 1, :]
        return _silu(acc)

    def l2n(x):
        return x * lax.rsqrt(jnp.sum(x * x, axis=-1, keepdims=True) + EPS)

    def chunk_inputs(n):
        rows = pl.ds(pl.multiple_of(n * BLK, BLK), BLK)
        live = (ri[:, 0:1] + n * BLK) >= FRONT
        qn = jnp.where(live, l2n(conv_silu(q_ref, cwq_ref, n)) * (A_DK ** -0.5), 0.0)
        kn = jnp.where(live, l2n(conv_silu(k_ref, cwk_ref, n)), 0.0)
        vv = jnp.where(live, conv_silu(v_ref, cwv_ref, n), 0.0)
        kn16 = kn.astype(BF16)
        kq = _dot_nt(jnp.concatenate([kn16, qn.astype(BF16)], axis=0), kn16)
        return dict(n=n, rows=rows, qn=qn, kn=kn, vv=vv, kk=kq[:BLK], qk=kq[BLK:], bg=bg_ref[rows, :], gt=gt_ref[n])

    def chain_setup(c, d):
        bg, gt = c["bg"], c["gt"]
        beta, ccol, tot = bg[:, d:d + 1], bg[:, 2 + d:3 + d], bg[:, 4 + d:5 + d]
        crow = gt[2 + d:3 + d, :]
        dec = jnp.exp(jnp.where(incl[d], ccol - crow, NEG))
        a = jnp.where(strict[d], beta * c["kk"] * dec, 0.0)
        return dict(c=c, d=d, beta=beta, ccol=ccol, tot=tot, dec=dec, a=a, t=eye - a, x=a.astype(BF16))

    def prep(g, carry):
        chunks = [chunk_inputs(g * PREP_UNROLL + j) for j in range(PREP_UNROLL)]
        chains = [chain_setup(c, d) for c in chunks for d in range(2)]
        for _ in range(INV_SQUARINGS):
            x2s = [_dot(ch["x"], ch["x"]) for ch in chains]
            for ch, x2 in zip(chains, x2s):
                ch["x"] = x2.astype(BF16)
            tx = [_dot(ch["t"].astype(BF16), ch["x"]) for ch in chains]
            for ch, v in zip(chains, tx):
                ch["t"] = ch["t"] + v
        for ch in chains:
            c = ch["c"]
            ch["ec"] = jnp.exp(ch["ccol"])
            ch["rhs"] = jnp.concatenate([ch["beta"] * c["vv"], ch["beta"] * c["kn"] * ch["ec"]], axis=1)
            ch["t16"] = ch["t"].astype(BF16)
        x0s = [_dot(ch["t16"], ch["rhs"].astype(BF16)) for ch in chains]
        res = []
        for ch, x0 in zip(chains, x0s):
            ah, al = _split2(ch["a"])
            xh, xl = _split2(x0)
            ax = _dot(jnp.concatenate([ah, al], axis=1), jnp.concatenate([xh, xh], axis=0)) + _dot(ah, xl)
            res.append((ch["rhs"] - x0 - ax).astype(BF16))
        uws = [(x0 + _dot(ch["t16"], e)).astype(BF16) for ch, x0, e in zip(chains, x0s, res)]
        kuws = [_dot((ch["c"]["kn"] * jnp.exp(ch["tot"] - ch["ccol"])).T.astype(BF16), uw)
                for ch, uw in zip(chains, uws)]
        quws = [_dot((ch["c"]["qk"] * ch["dec"]).astype(BF16), uw) for ch, uw in zip(chains, uws)]
        for ch, kuw, quw in zip(chains, kuws, quws):
            c, d = ch["c"], ch["d"]
            rows = c["rows"]
            sadd_s[d, rows, :] = kuw[:, :BLK]
            smul_s[d, rows, :] = (-kuw[:, BLK:]).astype(BF16)
            o_s[d, rows, :] = quw[:, :BLK]
            omul_s[d, rows, :] = (c["qn"] * ch["ec"] - quw[:, BLK:]).astype(BF16)
            gl_s[d * nblk + c["n"]] = jnp.broadcast_to(jnp.exp(ch["tot"]), (BLK, BLK))[0:8, :]
        return carry

    lax.fori_loop(0, nblk // PREP_UNROLL, prep, 0)

    def scan_step(d, n, s):
        rows = pl.ds(pl.multiple_of(n * BLK, BLK), BLK)
        both = _dot(jnp.concatenate([smul_s[d, rows, :], omul_s[d, rows, :]], axis=0), s.astype(BF16))
        o_s[d, rows, :] = o_s[d, rows, :] + both[BLK:]
        return s * gl_s[d * nblk + n][0:1, :] + both[:BLK] + sadd_s[d, rows, :]

    def scan(i, carry):
        sf, sb = carry
        sf = scan_step(0, i, sf)
        sb = scan_step(1, nblk - 1 - i, sb)
        return sf, sb

    s0 = jnp.zeros((BLK, BLK), F32)
    lax.fori_loop(0, nblk, scan, (s0, s0))

    def finish(n, carry):
        rows = pl.ds(pl.multiple_of(n * BLK, BLK), BLK)
        o = o_s[0, rows, :] + o_s[1, rows, :]
        y_ref[rows, :] = (_rms(o, og_ref[...]) * _silu(z_ref[rows, :])).astype(y_ref.dtype)
        return carry

    lax.fori_loop(0, nblk, finish, 0, unroll=PREP_UNROLL)


def _delta(qkv, z, bg, gt, conv_w, o_gain):
    b, lp, _ = qkv.shape
    nblk = lp // BLK
    col = lambda off: pl.BlockSpec((None, lp, BLK), lambda i, j: (i, 0, j + off))
    cw = lambda off: pl.BlockSpec((A_CONV, BLK), lambda i, j: (0, j + off))
    return pl.pallas_call(
        _delta_kernel,
        grid=(b, A_HEADS),
        in_specs=[col(0), col(A_HEADS), col(2 * A_HEADS), col(0), col(0),
                  pl.BlockSpec((None, None, nblk, 8, BLK), lambda i, j: (i, j, 0, 0, 0)),
                  cw(0), cw(A_HEADS), cw(2 * A_HEADS),
                  pl.BlockSpec((1, BLK), lambda i, j: (0, 0))],
        out_specs=col(0),
        out_shape=jax.ShapeDtypeStruct((b, lp, A_HEADS * BLK), BF16),
        scratch_shapes=[pltpu.VMEM((2, lp, BLK), F32), pltpu.VMEM((2, lp, BLK), BF16),
                        pltpu.VMEM((2, lp, BLK), F32), pltpu.VMEM((2, lp, BLK), BF16),
                        pltpu.VMEM((2 * nblk, 8, BLK), F32)],
        compiler_params=_cparams("parallel", "parallel"),
        name="delta_mixer",
    )(qkv, qkv, qkv, z, bg, gt, conv_w, conv_w, conv_w, o_gain)


def _window_kernel(q_ref, kp_ref, kc_ref, kn_ref, km_ref, bias_ref, sink_ref, y_ref):
    i = pl.program_id(1)
    nblk = pl.num_programs(1)
    grp = B_HEADS // B_KV
    nk = 4 * BLK
    c = lax.broadcasted_iota(jnp.int32, (1, nk), 1)
    kblk = i - 1 + (c >> 7)
    edge = jnp.where((c >= 3 * BLK) | ((kblk >= 1) & (kblk < nblk)), 0.0, NEG)
    q = q_ref[...]
    kvs = (kp_ref[...], kc_ref[...], kn_ref[...], km_ref[...])
    ones = jnp.ones((nk, B_HD), BF16)
    for kvh in range(B_KV):
        ks = jnp.concatenate([t[:, kvh * B_HD:(kvh + 1) * B_HD] for t in kvs], axis=0)
        vs = jnp.concatenate([t[:, (B_KV + kvh) * B_HD:(B_KV + kvh + 1) * B_HD] for t in kvs], axis=0)
        heads = range(kvh * grp, (kvh + 1) * grp)
        q4 = jnp.concatenate([q[:, hh * B_HD:(hh + 1) * B_HD] for hh in heads], axis=0)
        s4 = _dot_nt(q4, ks)
        ps, ms = [], []
        for gi, hh in enumerate(heads):
            s = s4[gi * BLK:(gi + 1) * BLK] + bias_ref[hh] + edge
            m = jnp.maximum(jnp.max(s, axis=-1, keepdims=True), sink_ref[hh:hh + 1, 0:1])
            ps.append(jnp.exp2(s - m).astype(BF16))
            ms.append(m)
        pv = _dot(jnp.concatenate(ps, axis=0), jnp.concatenate([vs, ones], axis=1))
        for gi, hh in enumerate(heads):
            o = pv[gi * BLK:(gi + 1) * BLK]
            den = o[:, B_HD:B_HD + 1] + jnp.exp2(sink_ref[hh:hh + 1, 0:1] - ms[gi])
            y_ref[:, hh * B_HD:(hh + 1) * B_HD] = (o[:, :B_HD] / den).astype(y_ref.dtype)

    @pl.when(i == 0)
    def _():
        rr = lax.broadcasted_iota(jnp.int32, y_ref.shape, 0)
        y_ref[...] = jnp.where(rr >= FRONT, y_ref[...], 0).astype(y_ref.dtype)


def _window_bias():
    r = jnp.arange(BLK)[:, None]
    c = jnp.arange(4 * BLK)[None, :]
    dist = jnp.abs(BLK + r - c)
    slopes = jnp.exp2(-8.0 * (jnp.arange(B_HEADS, dtype=F32) + 1.0) / B_HEADS)
    band = (c < 3 * BLK) & (dist <= B_WIN)
    alibi = -slopes[:, None, None] * dist.astype(F32)[None] * LOG2E
    rest = jnp.where(c >= 3 * BLK + FRONT, 0.0, NEG)
    return jnp.where(band[None], alibi, rest[None]).astype(F32)


def _window(qb, kvb, sink_rows):
    b, lp, _ = qb.shape
    nblk = lp // BLK
    bias = _window_bias()
    kv = lambda f: pl.BlockSpec((None, BLK, 2 * B_KV * B_HD), f)
    return pl.pallas_call(
        _window_kernel,
        grid=(b, nblk),
        in_specs=[pl.BlockSpec((None, BLK, B_HEADS * B_HD), lambda i, j: (i, j, 0)),
                  kv(lambda i, j: (i, jnp.maximum(j - 1, 0), 0)),
                  kv(lambda i, j: (i, j, 0)),
                  kv(lambda i, j: (i, jnp.minimum(j + 1, nblk - 1), 0)),
                  kv(lambda i, j: (i, 0, 0)),
                  pl.BlockSpec(bias.shape, lambda i, j: (0, 0, 0)),
                  pl.BlockSpec((B_HEADS, BLK), lambda i, j: (0, 0))],
        out_specs=pl.BlockSpec((None, BLK, B_HEADS * B_HD), lambda i, j: (i, j, 0)),
        out_shape=jax.ShapeDtypeStruct((b, lp, B_HEADS * B_HD), BF16),
        compiler_params=_cparams("parallel", "parallel"),
        name="window_mixer",
    )(qb, kvb, kvb, kvb, kvb, bias, sink_rows)


def _out_mlp_kernel(*refs, n_mix):
    h_ref = refs[0]
    mix_refs = refs[1:1 + n_mix]
    wo_ref, g_ref, w1_ref, w2_ref, o_ref = refs[1 + n_mix:]
    mix = jnp.concatenate([m[...] for m in mix_refs], axis=1)
    h = h_ref[...] + _dot(mix, wo_ref[...])
    u = _rms(h, g_ref[...]).astype(BF16)
    dff = w1_ref.shape[1]
    acc = h
    for c in range(dff // FF_CHUNK):
        sl = slice(c * FF_CHUNK, (c + 1) * FF_CHUNK)
        a = jnp.maximum(_dot(u, w1_ref[:, sl]), 0.0)
        acc = acc + _dot((a * a).astype(BF16), w2_ref[sl, :])
    o_ref[...] = acc


def _out_mlp(h, mixes, wo, g, w1, w2):
    r, d = h.shape
    row = lambda n: pl.BlockSpec((ROW_TILE, n), lambda i: (i, 0))
    full = lambda a: pl.BlockSpec(a.shape, lambda i: (0, 0))
    return pl.pallas_call(
        functools.partial(_out_mlp_kernel, n_mix=len(mixes)),
        grid=(r // ROW_TILE,),
        in_specs=[row(d)] + [row(m.shape[1]) for m in mixes] + [full(wo), full(g), full(w1), full(w2)],
        out_specs=row(d),
        out_shape=jax.ShapeDtypeStruct((r, d), F32),
        compiler_params=_cparams("parallel"),
        name="out_mlp",
    )(h, *mixes, wo, g, w1, w2)


def _c_proj_kernel(h_ref, g_ref, wq_ref, wk_ref, wv_ref, qg_ref, kg_ref, cos_ref, sin_ref,
                   q_ref, k_ref, v_ref):
    u = _rms(h_ref[...], g_ref[...]).astype(BF16)
    cosf = cos_ref[...]
    sinf = sin_ref[...]
    half = C_HD // 2

    def norm_rope(x, gain):
        x = _rms(x, gain)
        swapped = jnp.concatenate([x[:, half:], x[:, :half]], axis=1)
        return x * cosf + swapped * sinf

    k = _dot(u, wk_ref[...])
    half_w = C_HEADS * C_HD // 2
    q_lo = _dot(u, wq_ref[:, :half_w])
    for hh in range(C_KV):
        sl = slice(hh * C_HD, (hh + 1) * C_HD)
        k_ref[:, sl] = norm_rope(k[:, sl], kg_ref[...]).astype(BF16)
    q_hi = _dot(u, wq_ref[:, half_w:])
    for hh in range(C_HEADS // 2):
        sl = slice(hh * C_HD, (hh + 1) * C_HD)
        q_ref[:, sl] = (norm_rope(q_lo[:, sl], qg_ref[...]) * (C_HD ** -0.5 * LOG2E)).astype(BF16)
    v_ref[...] = _dot(u, wv_ref[...]).astype(BF16)
    for hh in range(C_HEADS // 2):
        sl = slice(hh * C_HD, (hh + 1) * C_HD)
        q_ref[:, half_w + hh * C_HD:half_w + (hh + 1) * C_HD] = (
            norm_rope(q_hi[:, sl], qg_ref[...]) * (C_HD ** -0.5 * LOG2E)).astype(BF16)


def _c_proj(h, g, wq, wk, wv, qg, kg, cosf, sinf):
    b, lp, d = h.shape
    tm = 3 * BLK
    row = lambda n: pl.BlockSpec((None, tm, n), lambda i, j: (i, j, 0))
    full = lambda a: pl.BlockSpec(a.shape, lambda i, j: (0, 0))
    pos = pl.BlockSpec((tm, C_HD), lambda i, j: (j, 0))
    return pl.pallas_call(
        _c_proj_kernel,
        grid=(b, lp // tm),
        in_specs=[row(d), full(g), full(wq), full(wk), full(wv), full(qg), full(kg), pos, pos],
        out_specs=[row(C_HEADS * C_HD), row(C_KV * C_HD), row(C_KV * C_HD)],
        out_shape=[jax.ShapeDtypeStruct((b, lp, C_HEADS * C_HD), BF16),
                   jax.ShapeDtypeStruct((b, lp, C_KV * C_HD), BF16),
                   jax.ShapeDtypeStruct((b, lp, C_KV * C_HD), BF16)],
        compiler_params=_cparams("parallel", "parallel"),
        name="c_proj",
    )(h, g, wq, wk, wv, qg, kg, cosf, sinf)


ATT_TK = 512
ATT_QB = 3


def _dense_kernel(q_ref, k_ref, v_ref, y_ref, *scratch):
    i = pl.program_id(2)
    grp = C_HEADS // C_KV
    lp = k_ref.shape[0]
    nkb = (lp - BLK) // ATT_TK
    nq = ATT_QB
    sa_s, sb_s, acc_s = scratch[:nq], scratch[nq:2 * nq], scratch[2 * nq:]
    qs = [jnp.concatenate([q_ref[c * BLK:(c + 1) * BLK, g * C_HD:(g + 1) * C_HD] for g in range(grp)], axis=0)
          for c in range(nq)]
    m_rows = grp * BLK

    def keys(t):
        return pl.ds(pl.multiple_of(BLK + t * ATT_TK, BLK), ATT_TK)

    def v_ones(rows, n):
        return jnp.concatenate([v_ref[rows, :], jnp.ones((n, C_HD), BF16)], axis=1)

    def scores(t, s_refs):
        kt = k_ref[keys(t), :]
        for c in range(nq):
            s_refs[c][...] = _dot_nt(qs[c], kt)

    def step(t, ms, s_refs):
        vt = v_ones(keys(t), ATT_TK)
        out = []
        for c in range(nq):
            s = s_refs[c][...]
            m_new = jnp.maximum(ms[c], jnp.max(s, axis=-1, keepdims=True))
            p = jnp.exp2(s - m_new).astype(BF16)
            acc_s[c][...] = jnp.exp2(ms[c] - m_new) * acc_s[c][...] + _dot(p, vt)
            out.append(m_new)
        return out

    scores(0, sa_s)
    k0 = k_ref[0:BLK, :]
    v0 = v_ones(slice(0, BLK), BLK)
    kc = lax.broadcasted_iota(jnp.int32, (m_rows, BLK), 1)
    ms = []
    for c in range(nq):
        s0 = jnp.where(kc >= FRONT, _dot_nt(qs[c], k0), NEG)
        m = jnp.max(s0, axis=-1, keepdims=True)
        acc_s[c][...] = _dot(jnp.exp2(s0 - m).astype(BF16), v0)
        ms.append(m)

    def body(j, ms):
        scores(2 * j + 1, sb_s)
        ms = step(2 * j, ms, sa_s)
        scores(2 * j + 2, sa_s)
        return step(2 * j + 1, ms, sb_s)

    ms = lax.fori_loop(0, nkb // 2 - 1, body, ms)
    scores(nkb - 1, sb_s)
    ms = step(nkb - 2, ms, sa_s)
    ms = step(nkb - 1, ms, sb_s)
    rr = lax.broadcasted_iota(jnp.int32, (BLK, C_HD), 0)
    for c in range(nq):
        acc = acc_s[c][...]
        o = acc[:, :C_HD] / acc[:, C_HD:C_HD + 1]
        for g in range(grp):
            og = o[g * BLK:(g + 1) * BLK, :]
            if c == 0:
                og = jnp.where((i == 0) & (rr < FRONT), 0.0, og)
            y_ref[c * BLK:(c + 1) * BLK, g * C_HD:(g + 1) * C_HD] = og.astype(y_ref.dtype)


def _dense(q, k, v):
    b, lp, _ = q.shape
    grp = C_HEADS // C_KV
    tq = ATT_QB * BLK
    assert lp % tq == 0 and ((lp - BLK) // ATT_TK) % 2 == 0 and (lp - BLK) % ATT_TK == 0
    score = pltpu.VMEM((grp * BLK, ATT_TK), F32)
    return pl.pallas_call(
        _dense_kernel,
        grid=(b, C_KV, lp // tq),
        in_specs=[pl.BlockSpec((None, tq, grp * C_HD), lambda i, j, t: (i, t, j)),
                  pl.BlockSpec((None, lp, C_HD), lambda i, j, t: (i, 0, j)),
                  pl.BlockSpec((None, lp, C_HD), lambda i, j, t: (i, 0, j))],
        out_specs=pl.BlockSpec((None, tq, grp * C_HD), lambda i, j, t: (i, t, j)),
        out_shape=jax.ShapeDtypeStruct((b, lp, C_HEADS * C_HD), BF16),
        scratch_shapes=[score] * (2 * ATT_QB) + [pltpu.VMEM((grp * BLK, 2 * C_HD), F32)] * ATT_QB,
        compiler_params=_cparams("parallel", "parallel", "arbitrary"),
        name="dense_mixer",
    )(q, k, v)


def _rope_tables(lp, n_tok):
    rows = n_tok // GRID_W
    row = jnp.repeat(jnp.arange(rows), GRID_W)
    col = jnp.tile(jnp.arange(GRID_W), rows)
    meta = jnp.arange(N_META) - N_META
    front = jnp.zeros((lp - N_META - n_tok,), jnp.int32)
    row = jnp.concatenate([front, meta, row]).astype(F32)
    col = jnp.concatenate([front, meta, col]).astype(F32)
    axis_dim = C_HD // 2
    freqs = ROPE_THETA ** (-jnp.arange(0, axis_dim, 2, dtype=F32) / axis_dim)
    ang = jnp.concatenate([row[:, None] * freqs, col[:, None] * freqs], axis=-1)
    cos, sin = jnp.cos(ang), jnp.sin(ang)
    return jnp.concatenate([cos, cos], axis=-1), jnp.concatenate([-sin, sin], axis=-1)


def _gate_weight(w_b, w_a):
    d = w_b.shape[0]
    w_b = w_b.reshape(d, 2, A_HEADS)
    w_a = w_a.reshape(d, 2, A_HEADS)
    per_head = jnp.concatenate([w_b, w_a, w_a], axis=1)
    per_head = jnp.transpose(per_head, (0, 2, 1))
    per_head = jnp.pad(per_head, ((0, 0), (0, 0), (0, BLK - 6)))
    return per_head.reshape(d, A_HEADS * BLK)


def _gate_rows(p):
    t = jnp.transpose(p.astype(F32), (1, 0))
    rows = jnp.concatenate([jnp.zeros_like(t), t, t], axis=1)
    return jnp.pad(rows, ((0, 0), (0, BLK - 6)))[:, None, :]


def kernel(x, meta_tokens, attn_norm_g, mlp_norm_g, w_in_ab, conv_w_a, a_log, dt_bias, a_out_norm_g,
           b_q_norm_g, b_k_norm_g, b_sink, w_out_ab, w_qkv_c, c_q_norm_g, c_k_norm_g, w_out_c, w_ff1, w_ff2):
    bsz, n_tok, d = x.shape
    lp = FRONT + N_META + n_tok
    depth = attn_norm_g.shape[0]
    meta = jnp.broadcast_to(meta_tokens.astype(x.dtype)[None], (bsz, N_META, d))
    h = jnp.concatenate([jnp.zeros((bsz, FRONT, d), x.dtype), meta, x], axis=1).reshape(bsz * lp, d)
    cosf, sinf = _rope_tables(lp, n_tok)
    deint = jnp.concatenate([jnp.arange(0, C_HD, 2), jnp.arange(1, C_HD, 2)])
    row2 = lambda v: v.astype(F32).reshape(1, -1)

    for layer in range(depth):
        i = layer // 2
        g_attn = row2(attn_norm_g[layer])
        if layer % 2 == 0:
            w = w_in_ab[i]
            qkv_w = w[:, :1536].astype(BF16)
            z_w = w[:, 1536:2048].astype(BF16)
            gate_w = _gate_weight(w[:, 2048:2056], w[:, 2056:2064]).astype(BF16)
            bq_w = w[:, 2064:2576].astype(BF16)
            bkv_w = w[:, 2576:2832].astype(BF16)
            qkv, z, gate_pre, qb, kvb = _ab_proj(h, g_attn, qkv_w, z_w, gate_w, bq_w, bkv_w,
                                                 row2(b_q_norm_g[i]), row2(b_k_norm_g[i]))
            r3 = lambda t: t.reshape(bsz, lp, t.shape[-1])
            bg, gt = _gates(r3(gate_pre), _gate_rows(a_log[i]), _gate_rows(dt_bias[i]))
            ya = _delta(r3(qkv), r3(z), bg, gt, conv_w_a[i].astype(F32), row2(a_out_norm_g[i]))
            sink_rows = jnp.broadcast_to(b_sink[i].astype(F32)[:, None] * LOG2E, (B_HEADS, BLK))
            yb = _window(r3(qb), r3(kvb), sink_rows)
            wo = w_out_ab[i].astype(BF16)
            mixes = [ya.reshape(bsz * lp, -1), yb.reshape(bsz * lp, -1)]
        else:
            w = w_qkv_c[i]
            perm = lambda wc, nh: wc.reshape(d, nh, C_HD)[:, :, deint].reshape(d, nh * C_HD)
            wq = perm(w[:, :C_HEADS * C_HD], C_HEADS).astype(BF16)
            wk = perm(w[:, C_HEADS * C_HD:(C_HEADS + C_KV) * C_HD], C_KV).astype(BF16)
            wv = w[:, (C_HEADS + C_KV) * C_HD:].astype(BF16)
            q, k, v = _c_proj(h.reshape(bsz, lp, d), g_attn, wq, wk, wv,
                              row2(c_q_norm_g[i][deint]), row2(c_k_norm_g[i][deint]), cosf, sinf)
            att = _dense(q, k, v)
            mixes = [att.reshape(bsz * lp, -1)]
            wo = w_out_c[i].astype(BF16)
        h = _out_mlp(h, mixes, wo, row2(mlp_norm_g[layer]), w_ff1[layer].astype(BF16),
                     w_ff2[layer].astype(BF16))
    return h.reshape(bsz, lp, d)[:, FRONT + N_META:]
```

```python
import functools
import math

import jax
import jax.numpy as jnp
from jax import lax
from jax.experimental import pallas as pl
from jax.experimental.pallas import tpu as pltpu

F32 = jnp.float32
BF16 = jnp.bfloat16

EPS = 1e-6
N_META = 16
BLK = 128
FRONT = BLK - N_META
GRID_W = 64
ROPE_THETA = 10000.0
A_HEADS, A_DK, A_CONV = 4, 128, 5
B_HEADS, B_KV, B_HD, B_WIN = 8, 2, 64, 128
C_HEADS, C_KV, C_HD = 8, 2, 128
NEG = -1e30
LOG2E = math.log2(math.e)

VMEM_LIMIT = 56 * 1024 * 1024
ROW_TILE = 512
FF_CHUNK = 512
INV_SQUARINGS = 6
PREP_UNROLL = 11


def _chunk_group(nblk):
    return max(g for g in range(1, PREP_UNROLL + 1) if nblk % g == 0)


def _cparams(*sem):
    return pltpu.CompilerParams(dimension_semantics=sem, vmem_limit_bytes=VMEM_LIMIT)


def _sigmoid(x):
    return 1.0 / (1.0 + jnp.exp(-x))


def _silu(x):
    return x * _sigmoid(x)


def _softplus(x):
    return jnp.maximum(x, 0.0) + jnp.log1p(jnp.exp(-jnp.abs(x)))


def _rms(x, g):
    return x * lax.rsqrt(jnp.mean(x * x, axis=-1, keepdims=True) + EPS) * g


def _dot(a, b):
    return jnp.dot(a, b, preferred_element_type=F32)


def _dot_nt(a, b):
    return lax.dot_general(a, b, (((1,), (1,)), ((), ())), preferred_element_type=F32)


def _ab_proj_kernel(h_ref, g_ref, wqkv_ref, wz_ref, wg_ref, wq_ref, wkv_ref, bqg_ref, bkg_ref,
                    qkv_ref, z_ref, gate_ref, qb_ref, kvb_ref):
    u = _rms(h_ref[...], g_ref[...]).astype(BF16)
    qb = _dot(u, wq_ref[...])
    kv = _dot(u, wkv_ref[...])
    scale = B_HD ** -0.5 * LOG2E

    def q_heads(lo, hi):
        for hh in range(lo, hi):
            sl = slice(hh * B_HD, (hh + 1) * B_HD)
            qb_ref[:, sl] = (_rms(qb[:, sl], bqg_ref[...]) * scale).astype(BF16)

    qkv_ref[:, 0:512] = _dot(u, wqkv_ref[:, 0:512])
    q_heads(0, 3)
    qkv_ref[:, 512:1024] = _dot(u, wqkv_ref[:, 512:1024])
    q_heads(3, 6)
    qkv_ref[:, 1024:1536] = _dot(u, wqkv_ref[:, 1024:1536])
    q_heads(6, B_HEADS)
    z_ref[...] = _dot(u, wz_ref[...])
    for hh in range(B_KV):
        sl = slice(hh * B_HD, (hh + 1) * B_HD)
        kvb_ref[:, sl] = _rms(kv[:, sl], bkg_ref[...]).astype(BF16)
    kvb_ref[:, B_KV * B_HD:] = kv[:, B_KV * B_HD:].astype(BF16)
    gate_ref[...] = _dot(u, wg_ref[...])


def _ab_proj(h, g, wqkv, wz, wg, wq, wkv, bqg, bkg):
    r, d = h.shape
    row = lambda n: pl.BlockSpec((ROW_TILE, n), lambda i: (i, 0))
    full = lambda a: pl.BlockSpec(a.shape, lambda i: (0, 0))
    return pl.pallas_call(
        _ab_proj_kernel,
        grid=(r // ROW_TILE,),
        in_specs=[row(d), full(g), full(wqkv), full(wz), full(wg), full(wq), full(wkv), full(bqg), full(bkg)],
        out_specs=[row(1536), row(512), row(512), row(512), row(256)],
        out_shape=[jax.ShapeDtypeStruct((r, 1536), F32), jax.ShapeDtypeStruct((r, 512), F32),
                   jax.ShapeDtypeStruct((r, 512), F32), jax.ShapeDtypeStruct((r, 512), BF16),
                   jax.ShapeDtypeStruct((r, 256), BF16)],
        compiler_params=_cparams("parallel"),
        name="ab_proj",
    )(h, g, wqkv, wz, wg, wq, wkv, bqg, bkg)


def _split2(x):
    hi = x.astype(BF16)
    return hi, (x - hi.astype(F32)).astype(BF16)


def _split3(x):
    hi = x.astype(BF16)
    r1 = x - hi.astype(F32)
    mid = r1.astype(BF16)
    lo = (r1 - mid.astype(F32)).astype(BF16)
    return hi, mid, lo


def _gates_kernel(pre_ref, alog_ref, dtb_ref, bg_ref, gt_ref):
    nblk = pre_ref.shape[0] // BLK
    ri = lax.broadcasted_iota(jnp.int32, (BLK, BLK), 0)
    ci = lax.broadcasted_iota(jnp.int32, (BLK, BLK), 1)
    lower = (ri >= ci).astype(BF16)
    upper = (ri <= ci).astype(BF16)
    neg_a = -jnp.exp(alog_ref[...])
    dtb = dtb_ref[...]

    def one(n):
        x = pre_ref[pl.ds(pl.multiple_of(n * BLK, BLK), BLK), :]
        live = (ri + n * BLK) >= FRONT
        beta = jnp.where(live, _sigmoid(x), 0.0)
        g = jnp.where(live, neg_a * _softplus(x + dtb), 0.0)
        parts = _split3(g)
        pre = sum(_dot(lower, p) for p in parts)
        suf = sum(_dot(upper, p) for p in parts)
        tot = pre + suf - g
        return jnp.where(ci < 2, beta, jnp.where(ci == 2, pre, jnp.where(ci == 3, suf, tot)))

    grp = _chunk_group(nblk)

    def body(g, carry):
        outs = [one(g * grp + j) for j in range(grp)]
        for j, out in enumerate(outs):
            n = g * grp + j
            bg_ref[pl.ds(pl.multiple_of(n * BLK, BLK), BLK), :] = out
            gt_ref[n] = out.T[0:8, :]
        return carry

    lax.fori_loop(0, nblk // grp, body, 0)


def _gates(pre, alog_rows, dtb_rows):
    b, lp, _ = pre.shape
    nblk = lp // BLK
    return pl.pallas_call(
        _gates_kernel,
        grid=(b, A_HEADS),
        in_specs=[pl.BlockSpec((None, lp, BLK), lambda i, j: (i, 0, j)),
                  pl.BlockSpec((None, 1, BLK), lambda i, j: (j, 0, 0)),
                  pl.BlockSpec((None, 1, BLK), lambda i, j: (j, 0, 0))],
        out_specs=[pl.BlockSpec((None, lp, BLK), lambda i, j: (i, 0, j)),
                   pl.BlockSpec((None, None, nblk, 8, BLK), lambda i, j: (i, j, 0, 0, 0))],
        out_shape=[jax.ShapeDtypeStruct((b, lp, A_HEADS * BLK), F32),
                   jax.ShapeDtypeStruct((b, A_HEADS, nblk, 8, BLK), F32)],
        compiler_params=_cparams("parallel", "parallel"),
        name="delta_gates",
    )(pre, alog_rows, dtb_rows)


def _delta_kernel(q_ref, k_ref, v_ref, z_ref, bg_ref, gt_ref, cwq_ref, cwk_ref, cwv_ref, og_ref, y_ref,
                  sadd_s, smul_s, o_s, omul_s, gl_s):
    lp = q_ref.shape[0]
    nblk = lp // BLK
    grp = _chunk_group(nblk)
    ri = lax.broadcasted_iota(jnp.int32, (BLK, BLK), 0)
    ci = lax.broadcasted_iota(jnp.int32, (BLK, BLK), 1)
    eye = (ri == ci).astype(F32)
    incl = (ri >= ci, ri <= ci)
    strict = (ri > ci, ri < ci)

    def conv_silu(ref, w_ref, n):
        base = n * BLK
        cur = ref[pl.ds(pl.multiple_of(base, BLK), BLK), :]
        prev = ref[pl.ds(pl.multiple_of(jnp.maximum(base - 8, 0), 8), 8), :]
        nxt_start = jnp.minimum(base + BLK, lp - 8)
        nxt = ref[pl.ds(pl.multiple_of(nxt_start, 8), 8), :]
        nxt = jnp.where(n < nblk - 1, nxt, 0.0)
        win = jnp.concatenate([prev, cur, nxt], axis=0)
        h = A_CONV // 2
        acc = win[8 - h:8 - h + BLK, :] * w_ref[0:1, :]
        for j in range(1, A_CONV):
            acc = acc + win[8 - h + j:8 - h + j + BLK, :] * w_ref[j:j + 1, :]
        return _silu(acc)

    def l2n(x):
        return x * lax.rsqrt(jnp.sum(x * x, axis=-1, keepdims=True) + EPS)

    def chunk_inputs(n):
        rows = pl.ds(pl.multiple_of(n * BLK, BLK), BLK)
        live = (ri[:, 0:1] + n * BLK) >= FRONT
        qn = jnp.where(live, l2n(conv_silu(q_ref, cwq_ref, n)) * (A_DK ** -0.5), 0.0)
        kn = jnp.where(live, l2n(conv_silu(k_ref, cwk_ref, n)), 0.0)
        vv = jnp.where(live, conv_silu(v_ref, cwv_ref, n), 0.0)
        kn16 = kn.astype(BF16)
        kq = _dot_nt(jnp.concatenate([kn16, qn.astype(BF16)], axis=0), kn16)
        return dict(n=n, rows=rows, qn=qn, kn=kn, vv=vv, kk=kq[:BLK], qk=kq[BLK:], bg=bg_ref[rows, :], gt=gt_ref[n])

    def chain_setup(c, d):
        bg, gt = c["bg"], c["gt"]
        beta, ccol, tot = bg[:, d:d + 1], bg[:, 2 + d:3 + d], bg[:, 4 + d:5 + d]
        crow = gt[2 + d:3 + d, :]
        dec = jnp.exp(jnp.where(incl[d], ccol - crow, NEG))
        a = jnp.where(strict[d], beta * c["kk"] * dec, 0.0)
        return dict(c=c, d=d, beta=beta, ccol=ccol, tot=tot, dec=dec, a=a, t=eye - a, x=a.astype(BF16))

    def prep(g, carry):
        chunks = [chunk_inputs(g * grp + j) for j in range(grp)]
        chains = [chain_setup(c, d) for c in chunks for d in range(2)]
        zero = jnp.zeros((BLK, BLK), BF16)

        def blockdiag(xp):
            return jnp.concatenate([jnp.concatenate([xp[:, :BLK], zero], axis=1),
                                    jnp.concatenate([zero, xp[:, BLK:]], axis=1)], axis=0)

        pairs = [(chains[2 * j], chains[2 * j + 1]) for j in range(grp)]
        xps = [jnp.concatenate([f["x"], b["x"]], axis=1) for f, b in pairs]
        tps = [jnp.concatenate([f["t"], b["t"]], axis=1) for f, b in pairs]
        for _ in range(INV_SQUARINGS):
            xps = [_dot(xp, blockdiag(xp)).astype(BF16) for xp in xps]
            txs = [_dot(tp.astype(BF16), blockdiag(xp)) for tp, xp in zip(tps, xps)]
            tps = [tp + tx for tp, tx in zip(tps, txs)]
        for (f, b), tp in zip(pairs, tps):
            f["t"], b["t"] = tp[:, :BLK], tp[:, BLK:]
        for ch in chains:
            c = ch["c"]
            ch["ec"] = jnp.exp(ch["ccol"])
            ch["rhs"] = jnp.concatenate([ch["beta"] * c["vv"], ch["beta"] * c["kn"] * ch["ec"]], axis=1)
            ch["t16"] = ch["t"].astype(BF16)
        x0s = [_dot(ch["t16"], ch["rhs"].astype(BF16)) for ch in chains]
        res = []
        for ch, x0 in zip(chains, x0s):
            ah, al = _split2(ch["a"])
            xh, xl = _split2(x0)
            ax = _dot(jnp.concatenate([ah, al], axis=1), jnp.concatenate([xh, xh], axis=0)) + _dot(ah, xl)
            res.append((ch["rhs"] - x0 - ax).astype(BF16))
        uws = [(x0 + _dot(ch["t16"], e)).astype(BF16) for ch, x0, e in zip(chains, x0s, res)]
        kuws = [_dot((ch["c"]["kn"] * jnp.exp(ch["tot"] - ch["ccol"])).T.astype(BF16), uw)
                for ch, uw in zip(chains, uws)]
        quws = [_dot((ch["c"]["qk"] * ch["dec"]).astype(BF16), uw) for ch, uw in zip(chains, uws)]
        for ch, kuw, quw in zip(chains, kuws, quws):
            c, d = ch["c"], ch["d"]
            rows = c["rows"]
            sadd_s[d, rows, :] = kuw[:, :BLK]
            smul_s[d, rows, :] = (-kuw[:, BLK:]).astype(BF16)
            o_s[d, rows, :] = quw[:, :BLK]
            omul_s[d, rows, :] = (c["qn"] * ch["ec"] - quw[:, BLK:]).astype(BF16)
            gl_s[d * nblk + c["n"]] = jnp.broadcast_to(jnp.exp(ch["tot"]), (BLK, BLK))[0:8, :]
        return carry

    lax.fori_loop(0, nblk // grp, prep, 0)

    def scan_step(d, n, s):
        rows = pl.ds(pl.multiple_of(n * BLK, BLK), BLK)
        both = _dot(jnp.concatenate([smul_s[d, rows, :], omul_s[d, rows, :]], axis=0), s.astype(BF16))
        o_s[d, rows, :] = o_s[d, rows, :] + both[BLK:]
        return s * gl_s[d * nblk + n][0:1, :] + both[:BLK] + sadd_s[d, rows, :]

    def scan(i, carry):
        sf, sb = carry
        sf = scan_step(0, i, sf)
        sb = scan_step(1, nblk - 1 - i, sb)
        return sf, sb

    s0 = jnp.zeros((BLK, BLK), F32)
    lax.fori_loop(0, nblk, scan, (s0, s0))

    def finish(n, carry):
        rows = pl.ds(pl.multiple_of(n * BLK, BLK), BLK)
        o = o_s[0, rows, :] + o_s[1, rows, :]
        y_ref[rows, :] = (_rms(o, og_ref[...]) * _silu(z_ref[rows, :])).astype(y_ref.dtype)
        return carry

    lax.fori_loop(0, nblk, finish, 0, unroll=3 if nblk % 3 == 0 else 1)


def _delta(qkv, z, bg, gt, conv_w, o_gain):
    b, lp, _ = qkv.shape
    nblk = lp // BLK
    col = lambda off: pl.BlockSpec((None, lp, BLK), lambda i, j: (i, 0, j + off))
    cw = lambda off: pl.BlockSpec((A_CONV, BLK), lambda i, j: (0, j + off))
    return pl.pallas_call(
        _delta_kernel,
        grid=(b, A_HEADS),
        in_specs=[col(0), col(A_HEADS), col(2 * A_HEADS), col(0), col(0),
                  pl.BlockSpec((None, None, nblk, 8, BLK), lambda i, j: (i, j, 0, 0, 0)),
                  cw(0), cw(A_HEADS), cw(2 * A_HEADS),
                  pl.BlockSpec((1, BLK), lambda i, j: (0, 0))],
        out_specs=col(0),
        out_shape=jax.ShapeDtypeStruct((b, lp, A_HEADS * BLK), BF16),
        scratch_shapes=[pltpu.VMEM((2, lp, BLK), F32), pltpu.VMEM((2, lp, BLK), BF16),
                        pltpu.VMEM((2, lp, BLK), F32), pltpu.VMEM((2, lp, BLK), BF16),
                        pltpu.VMEM((2 * nblk, 8, BLK), F32)],
        compiler_params=_cparams("parallel", "parallel"),
        name="delta_mixer",
    )(qkv, qkv, qkv, z, bg, gt, conv_w, conv_w, conv_w, o_gain)


def _window_kernel(q_ref, kp_ref, kc_ref, kn_ref, km_ref, bias_ref, sink_ref, y_ref):
    i = pl.program_id(1)
    nblk = pl.num_programs(1)
    grp = B_HEADS // B_KV
    nk = 4 * BLK
    c = lax.broadcasted_iota(jnp.int32, (1, nk), 1)
    kblk = i - 1 + (c >> 7)
    edge = jnp.where((c >= 3 * BLK) | ((kblk >= 1) & (kblk < nblk)), 0.0, NEG)
    q = q_ref[...]
    kvs = (kp_ref[...], kc_ref[...], kn_ref[...], km_ref[...])
    ones = jnp.ones((nk, 2 * B_HD), BF16)
    lane = lax.broadcasted_iota(jnp.int32, (BLK, 2 * B_HD), 1)
    s4s, vexts = [], []
    for kvh in range(B_KV):
        ks = jnp.concatenate([t[:, kvh * B_HD:(kvh + 1) * B_HD] for t in kvs], axis=0)
        vs = jnp.concatenate([t[:, (B_KV + kvh) * B_HD:(B_KV + kvh + 1) * B_HD] for t in kvs], axis=0)
        q4 = jnp.concatenate([q[:, hh * B_HD:(hh + 1) * B_HD] for hh in range(kvh * grp, (kvh + 1) * grp)],
                             axis=0)
        s4s.append(_dot_nt(q4, ks))
        vexts.append(jnp.concatenate([vs, vs, ones], axis=1))
    ms = []
    pvs = []
    for kvh in range(B_KV):
        ps = []
        for gi in range(grp):
            hh = kvh * grp + gi
            s = s4s[kvh][gi * BLK:(gi + 1) * BLK] + bias_ref[hh] + edge
            m = jnp.maximum(jnp.max(s, axis=-1, keepdims=True), sink_ref[hh:hh + 1, 0:1])
            ps.append(jnp.exp2(s - m).astype(BF16))
            ms.append(m)
        pvs.append(_dot(jnp.concatenate(ps, axis=0), vexts[kvh]))
    outs = []
    for hh in range(B_HEADS):
        kvh, gi = divmod(hh, grp)
        o = pvs[kvh][gi * BLK:(gi + 1) * BLK]
        den = o[:, 2 * B_HD:] + jnp.exp2(sink_ref[hh:hh + 1, 0:1] - ms[hh])
        outs.append(o[:, :2 * B_HD] / den)
    for j in range(B_HEADS // 2):
        pair = jnp.where(lane < B_HD, outs[2 * j], outs[2 * j + 1])
        y_ref[:, 2 * j * B_HD:(2 * j + 2) * B_HD] = pair.astype(y_ref.dtype)

    @pl.when(i == 0)
    def _():
        rr = lax.broadcasted_iota(jnp.int32, y_ref.shape, 0)
        y_ref[...] = jnp.where(rr >= FRONT, y_ref[...], 0).astype(y_ref.dtype)


def _window_bias():
    r = jnp.arange(BLK)[:, None]
    c = jnp.arange(4 * BLK)[None, :]
    dist = jnp.abs(BLK + r - c)
    slopes = jnp.exp2(-8.0 * (jnp.arange(B_HEADS, dtype=F32) + 1.0) / B_HEADS)
    band = (c < 3 * BLK) & (dist <= B_WIN)
    alibi = -slopes[:, None, None] * dist.astype(F32)[None] * LOG2E
    rest = jnp.where(c >= 3 * BLK + FRONT, 0.0, NEG)
    return jnp.where(band[None], alibi, rest[None]).astype(F32)


def _window(qb, kvb, sink_rows):
    b, lp, _ = qb.shape
    nblk = lp // BLK
    bias = _window_bias()
    kv = lambda f: pl.BlockSpec((None, BLK, 2 * B_KV * B_HD), f)
    return pl.pallas_call(
        _window_kernel,
        grid=(b, nblk),
        in_specs=[pl.BlockSpec((None, BLK, B_HEADS * B_HD), lambda i, j: (i, j, 0)),
                  kv(lambda i, j: (i, jnp.maximum(j - 1, 0), 0)),
                  kv(lambda i, j: (i, j, 0)),
                  kv(lambda i, j: (i, jnp.minimum(j + 1, nblk - 1), 0)),
                  kv(lambda i, j: (i, 0, 0)),
                  pl.BlockSpec(bias.shape, lambda i, j: (0, 0, 0)),
                  pl.BlockSpec((B_HEADS, BLK), lambda i, j: (0, 0))],
        out_specs=pl.BlockSpec((None, BLK, B_HEADS * B_HD), lambda i, j: (i, j, 0)),
        out_shape=jax.ShapeDtypeStruct((b, lp, B_HEADS * B_HD), BF16),
        compiler_params=_cparams("parallel", "parallel"),
        name="window_mixer",
    )(qb, kvb, kvb, kvb, kvb, bias, sink_rows)


def _out_mlp_kernel(*refs, n_mix):
    h_ref = refs[0]
    mix_refs = refs[1:1 + n_mix]
    wo_ref, g_ref, w1_ref, w2_ref, o_ref = refs[1 + n_mix:]
    mix = jnp.concatenate([m[...] for m in mix_refs], axis=1)
    h = h_ref[...] + _dot(mix, wo_ref[...])
    u = _rms(h, g_ref[...]).astype(BF16)
    dff = w1_ref.shape[1]
    acc = h
    for c in range(dff // FF_CHUNK):
        sl = slice(c * FF_CHUNK, (c + 1) * FF_CHUNK)
        a = jnp.maximum(_dot(u, w1_ref[:, sl]), 0.0)
        acc = acc + _dot((a * a).astype(BF16), w2_ref[sl, :])
    o_ref[...] = acc


def _out_mlp(h, mixes, wo, g, w1, w2):
    r, d = h.shape
    row = lambda n: pl.BlockSpec((ROW_TILE, n), lambda i: (i, 0))
    full = lambda a: pl.BlockSpec(a.shape, lambda i: (0, 0))
    return pl.pallas_call(
        functools.partial(_out_mlp_kernel, n_mix=len(mixes)),
        grid=(r // ROW_TILE,),
        in_specs=[row(d)] + [row(m.shape[1]) for m in mixes] + [full(wo), full(g), full(w1), full(w2)],
        out_specs=row(d),
        out_shape=jax.ShapeDtypeStruct((r, d), F32),
        compiler_params=_cparams("parallel"),
        name="out_mlp",
    )(h, *mixes, wo, g, w1, w2)


def _c_proj_kernel(h_ref, g_ref, wq_ref, wk_ref, wv_ref, qg_ref, kg_ref, cos_ref, sin_ref,
                   q_ref, k_ref, v_ref):
    u = _rms(h_ref[...], g_ref[...]).astype(BF16)
    cosf = cos_ref[...]
    sinf = sin_ref[...]
    half = C_HD // 2

    def norm_rope(x, gain):
        x = _rms(x, gain)
        swapped = jnp.concatenate([x[:, half:], x[:, :half]], axis=1)
        return x * cosf + swapped * sinf

    k = _dot(u, wk_ref[...])
    half_w = C_HEADS * C_HD // 2
    q_lo = _dot(u, wq_ref[:, :half_w])
    for hh in range(C_KV):
        sl = slice(hh * C_HD, (hh + 1) * C_HD)
        k_ref[:, sl] = norm_rope(k[:, sl], kg_ref[...]).astype(BF16)
    q_hi = _dot(u, wq_ref[:, half_w:])
    for hh in range(C_HEADS // 2):
        sl = slice(hh * C_HD, (hh + 1) * C_HD)
        q_ref[:, sl] = (norm_rope(q_lo[:, sl], qg_ref[...]) * (C_HD ** -0.5 * LOG2E)).astype(BF16)
    v_ref[...] = _dot(u, wv_ref[...]).astype(BF16)
    for hh in range(C_HEADS // 2):
        sl = slice(hh * C_HD, (hh + 1) * C_HD)
        q_ref[:, half_w + hh * C_HD:half_w + (hh + 1) * C_HD] = (
            norm_rope(q_hi[:, sl], qg_ref[...]) * (C_HD ** -0.5 * LOG2E)).astype(BF16)


def _c_proj(h, g, wq, wk, wv, qg, kg, cosf, sinf):
    b, lp, d = h.shape
    tm = 3 * BLK
    row = lambda n: pl.BlockSpec((None, tm, n), lambda i, j: (i, j, 0))
    full = lambda a: pl.BlockSpec(a.shape, lambda i, j: (0, 0))
    pos = pl.BlockSpec((tm, C_HD), lambda i, j: (j, 0))
    return pl.pallas_call(
        _c_proj_kernel,
        grid=(b, lp // tm),
        in_specs=[row(d), full(g), full(wq), full(wk), full(wv), full(qg), full(kg), pos, pos],
        out_specs=[row(C_HEADS * C_HD), row(C_KV * C_HD), row(C_KV * C_HD)],
        out_shape=[jax.ShapeDtypeStruct((b, lp, C_HEADS * C_HD), BF16),
                   jax.ShapeDtypeStruct((b, lp, C_KV * C_HD), BF16),
                   jax.ShapeDtypeStruct((b, lp, C_KV * C_HD), BF16)],
        compiler_params=_cparams("parallel", "parallel"),
        name="c_proj",
    )(h, g, wq, wk, wv, qg, kg, cosf, sinf)


ATT_TK = 512
ATT_QB = 3


def _dense_kernel(q_ref, k_ref, v_ref, y_ref, *scratch):
    i = pl.program_id(2)
    grp = C_HEADS // C_KV
    lp = k_ref.shape[0]
    nkb = (lp - BLK) // ATT_TK
    nq = ATT_QB
    sa_s, sb_s, acc_s = scratch[:nq], scratch[nq:2 * nq], scratch[2 * nq:]
    qs = [jnp.concatenate([q_ref[c * BLK:(c + 1) * BLK, g * C_HD:(g + 1) * C_HD] for g in range(grp)], axis=0)
          for c in range(nq)]
    m_rows = grp * BLK

    def keys(t):
        return pl.ds(pl.multiple_of(BLK + t * ATT_TK, BLK), ATT_TK)

    def v_ones(rows, n):
        return jnp.concatenate([v_ref[rows, :], jnp.ones((n, C_HD), BF16)], axis=1)

    def scores(t, s_refs):
        kt = k_ref[keys(t), :]
        for c in range(nq):
            s_refs[c][...] = _dot_nt(qs[c], kt)

    def step(t, ms, s_refs):
        vt = v_ones(keys(t), ATT_TK)
        out = []
        for c in range(nq):
            s = s_refs[c][...]
            m_new = jnp.maximum(ms[c], jnp.max(s, axis=-1, keepdims=True))
            p = jnp.exp2(s - m_new).astype(BF16)
            acc_s[c][...] = jnp.exp2(ms[c] - m_new) * acc_s[c][...] + _dot(p, vt)
            out.append(m_new)
        return out

    scores(0, sa_s)
    k0 = k_ref[0:BLK, :]
    v0 = v_ones(slice(0, BLK), BLK)
    kc = lax.broadcasted_iota(jnp.int32, (m_rows, BLK), 1)
    ms = []
    for c in range(nq):
        s0 = jnp.where(kc >= FRONT, _dot_nt(qs[c], k0), NEG)
        m = jnp.max(s0, axis=-1, keepdims=True)
        acc_s[c][...] = _dot(jnp.exp2(s0 - m).astype(BF16), v0)
        ms.append(m)

    def body(j, ms):
        scores(2 * j + 1, sb_s)
        ms = step(2 * j, ms, sa_s)
        scores(2 * j + 2, sa_s)
        return step(2 * j + 1, ms, sb_s)

    ms = lax.fori_loop(0, nkb // 2 - 1, body, ms)
    scores(nkb - 1, sb_s)
    ms = step(nkb - 2, ms, sa_s)
    ms = step(nkb - 1, ms, sb_s)
    rr = lax.broadcasted_iota(jnp.int32, (BLK, C_HD), 0)
    for c in range(nq):
        acc = acc_s[c][...]
        o = acc[:, :C_HD] / acc[:, C_HD:C_HD + 1]
        for g in range(grp):
            og = o[g * BLK:(g + 1) * BLK, :]
            if c == 0:
                og = jnp.where((i == 0) & (rr < FRONT), 0.0, og)
            y_ref[c * BLK:(c + 1) * BLK, g * C_HD:(g + 1) * C_HD] = og.astype(y_ref.dtype)


def _dense(q, k, v):
    b, lp, _ = q.shape
    grp = C_HEADS // C_KV
    tq = ATT_QB * BLK
    assert lp % tq == 0 and ((lp - BLK) // ATT_TK) % 2 == 0 and (lp - BLK) % ATT_TK == 0
    score = pltpu.VMEM((grp * BLK, ATT_TK), F32)
    return pl.pallas_call(
        _dense_kernel,
        grid=(b, C_KV, lp // tq),
        in_specs=[pl.BlockSpec((None, tq, grp * C_HD), lambda i, j, t: (i, t, j)),
                  pl.BlockSpec((None, lp, C_HD), lambda i, j, t: (i, 0, j)),
                  pl.BlockSpec((None, lp, C_HD), lambda i, j, t: (i, 0, j))],
        out_specs=pl.BlockSpec((None, tq, grp * C_HD), lambda i, j, t: (i, t, j)),
        out_shape=jax.ShapeDtypeStruct((b, lp, C_HEADS * C_HD), BF16),
        scratch_shapes=[score] * (2 * ATT_QB) + [pltpu.VMEM((grp * BLK, 2 * C_HD), F32)] * ATT_QB,
        compiler_params=_cparams("parallel", "parallel", "arbitrary"),
        name="dense_mixer",
    )(q, k, v)


def _rope_tables(lp, n_tok):
    rows = n_tok // GRID_W
    row = jnp.repeat(jnp.arange(rows), GRID_W)
    col = jnp.tile(jnp.arange(GRID_W), rows)
    meta = jnp.arange(N_META) - N_META
    front = jnp.zeros((lp - N_META - n_tok,), jnp.int32)
    row = jnp.concatenate([front, meta, row]).astype(F32)
    col = jnp.concatenate([front, meta, col]).astype(F32)
    axis_dim = C_HD // 2
    freqs = ROPE_THETA ** (-jnp.arange(0, axis_dim, 2, dtype=F32) / axis_dim)
    ang = jnp.concatenate([row[:, None] * freqs, col[:, None] * freqs], axis=-1)
    cos, sin = jnp.cos(ang), jnp.sin(ang)
    return jnp.concatenate([cos, cos], axis=-1), jnp.concatenate([-sin, sin], axis=-1)


def _gate_weight(w_b, w_a):
    d = w_b.shape[0]
    w_b = w_b.reshape(d, 2, A_HEADS)
    w_a = w_a.reshape(d, 2, A_HEADS)
    per_head = jnp.concatenate([w_b, w_a, w_a], axis=1)
    per_head = jnp.transpose(per_head, (0, 2, 1))
    per_head = jnp.pad(per_head, ((0, 0), (0, 0), (0, BLK - 6)))
    return per_head.reshape(d, A_HEADS * BLK)


def _gate_rows(p):
    t = jnp.transpose(p.astype(F32), (1, 0))
    rows = jnp.concatenate([jnp.zeros_like(t), t, t], axis=1)
    return jnp.pad(rows, ((0, 0), (0, BLK - 6)))[:, None, :]


def kernel(x, meta_tokens, attn_norm_g, mlp_norm_g, w_in_ab, conv_w_a, a_log, dt_bias, a_out_norm_g,
           b_q_norm_g, b_k_norm_g, b_sink, w_out_ab, w_qkv_c, c_q_norm_g, c_k_norm_g, w_out_c, w_ff1, w_ff2):
    bsz, n_tok, d = x.shape
    lp = FRONT + N_META + n_tok
    depth = attn_norm_g.shape[0]
    meta = jnp.broadcast_to(meta_tokens.astype(x.dtype)[None], (bsz, N_META, d))
    h = jnp.concatenate([jnp.zeros((bsz, FRONT, d), x.dtype), meta, x], axis=1).reshape(bsz * lp, d)
    cosf, sinf = _rope_tables(lp, n_tok)
    deint = jnp.concatenate([jnp.arange(0, C_HD, 2), jnp.arange(1, C_HD, 2)])
    row2 = lambda v: v.astype(F32).reshape(1, -1)

    for layer in range(depth):
        i = layer // 2
        g_attn = row2(attn_norm_g[layer])
        if layer % 2 == 0:
            w = w_in_ab[i]
            qkv_w = w[:, :1536].astype(BF16)
            z_w = w[:, 1536:2048].astype(BF16)
            gate_w = _gate_weight(w[:, 2048:2056], w[:, 2056:2064]).astype(BF16)
            bq_w = w[:, 2064:2576].astype(BF16)
            bkv_w = w[:, 2576:2832].astype(BF16)
            qkv, z, gate_pre, qb, kvb = _ab_proj(h, g_attn, qkv_w, z_w, gate_w, bq_w, bkv_w,
                                                 row2(b_q_norm_g[i]), row2(b_k_norm_g[i]))
            r3 = lambda t: t.reshape(bsz, lp, t.shape[-1])
            bg, gt = _gates(r3(gate_pre), _gate_rows(a_log[i]), _gate_rows(dt_bias[i]))
            ya = _delta(r3(qkv), r3(z), bg, gt, conv_w_a[i].astype(F32), row2(a_out_norm_g[i]))
            sink_rows = jnp.broadcast_to(b_sink[i].astype(F32)[:, None] * LOG2E, (B_HEADS, BLK))
            yb = _window(r3(qb), r3(kvb), sink_rows)
            wo = w_out_ab[i].astype(BF16)
            mixes = [ya.reshape(bsz * lp, -1), yb.reshape(bsz * lp, -1)]
        else:
            w = w_qkv_c[i]
            perm = lambda wc, nh: wc.reshape(d, nh, C_HD)[:, :, deint].reshape(d, nh * C_HD)
            wq = perm(w[:, :C_HEADS * C_HD], C_HEADS).astype(BF16)
            wk = perm(w[:, C_HEADS * C_HD:(C_HEADS + C_KV) * C_HD], C_KV).astype(BF16)
            wv = w[:, (C_HEADS + C_KV) * C_HD:].astype(BF16)
            q, k, v = _c_proj(h.reshape(bsz, lp, d), g_attn, wq, wk, wv,
                              row2(c_q_norm_g[i][deint]), row2(c_k_norm_g[i][deint]), cosf, sinf)
            att = _dense(q, k, v)
            mixes = [att.reshape(bsz * lp, -1)]
            wo = w_out_c[i].astype(BF16)
        h = _out_mlp(h, mixes, wo, row2(mlp_norm_g[layer]), w_ff1[layer].astype(BF16),
                     w_ff2[layer].astype(BF16))
    return h.reshape(bsz, lp, d)[:, FRONT + N_META:]
```

```python
import functools
import math

import jax
import jax.numpy as jnp
from jax import lax
from jax.experimental import pallas as pl
from jax.experimental.pallas import tpu as pltpu

F32 = jnp.float32
BF16 = jnp.bfloat16

EPS = 1e-6
N_META = 16
BLK = 128
FRONT = BLK - N_META
GRID_W = 64
ROPE_THETA = 10000.0
A_HEADS, A_DK, A_CONV = 4, 128, 5
B_HEADS, B_KV, B_HD, B_WIN = 8, 2, 64, 128
C_HEADS, C_KV, C_HD = 8, 2, 128
NEG = -1e30
LOG2E = math.log2(math.e)

VMEM_LIMIT = 56 * 1024 * 1024
ROW_TILE = 512
FF_CHUNK = 512
INV_SQUARINGS = 6
PREP_UNROLL = 11


def _chunk_group(nblk):
    return max(g for g in range(1, PREP_UNROLL + 1) if nblk % g == 0)


def _cparams(*sem):
    return pltpu.CompilerParams(dimension_semantics=sem, vmem_limit_bytes=VMEM_LIMIT)


def _sigmoid(x):
    return 1.0 / (1.0 + jnp.exp(-x))


def _silu(x):
    return x * _sigmoid(x)


def _softplus(x):
    return jnp.maximum(x, 0.0) + jnp.log1p(jnp.exp(-jnp.abs(x)))


def _rms(x, g):
    return x * lax.rsqrt(jnp.mean(x * x, axis=-1, keepdims=True) + EPS) * g


def _dot(a, b):
    return jnp.dot(a, b, preferred_element_type=F32)


def _dot_nt(a, b):
    return lax.dot_general(a, b, (((1,), (1,)), ((), ())), preferred_element_type=F32)


def _ab_proj_kernel(h_ref, g_ref, wqkv_ref, wz_ref, wg_ref, wq_ref, wkv_ref, bqg_ref, bkg_ref,
                    qkv_ref, z_ref, gate_ref, qb_ref, kvb_ref):
    u = _rms(h_ref[...], g_ref[...]).astype(BF16)
    qb = _dot(u, wq_ref[...])
    kv = _dot(u, wkv_ref[...])
    scale = B_HD ** -0.5 * LOG2E

    def q_heads(lo, hi):
        for hh in range(lo, hi):
            sl = slice(hh * B_HD, (hh + 1) * B_HD)
            qb_ref[:, sl] = (_rms(qb[:, sl], bqg_ref[...]) * scale).astype(BF16)

    qkv_ref[:, 0:512] = _dot(u, wqkv_ref[:, 0:512])
    q_heads(0, 3)
    qkv_ref[:, 512:1024] = _dot(u, wqkv_ref[:, 512:1024])
    q_heads(3, 6)
    qkv_ref[:, 1024:1536] = _dot(u, wqkv_ref[:, 1024:1536])
    q_heads(6, B_HEADS)
    z_ref[...] = _dot(u, wz_ref[...])
    for hh in range(B_KV):
        sl = slice(hh * B_HD, (hh + 1) * B_HD)
        kvb_ref[:, sl] = _rms(kv[:, sl], bkg_ref[...]).astype(BF16)
    kvb_ref[:, B_KV * B_HD:] = kv[:, B_KV * B_HD:].astype(BF16)
    gate_ref[...] = _dot(u, wg_ref[...])


def _ab_proj(h, g, wqkv, wz, wg, wq, wkv, bqg, bkg):
    r, d = h.shape
    row = lambda n: pl.BlockSpec((ROW_TILE, n), lambda i: (i, 0))
    full = lambda a: pl.BlockSpec(a.shape, lambda i: (0, 0))
    return pl.pallas_call(
        _ab_proj_kernel,
        grid=(r // ROW_TILE,),
        in_specs=[row(d), full(g), full(wqkv), full(wz), full(wg), full(wq), full(wkv), full(bqg), full(bkg)],
        out_specs=[row(1536), row(512), row(512), row(512), row(256)],
        out_shape=[jax.ShapeDtypeStruct((r, 1536), F32), jax.ShapeDtypeStruct((r, 512), F32),
                   jax.ShapeDtypeStruct((r, 512), F32), jax.ShapeDtypeStruct((r, 512), BF16),
                   jax.ShapeDtypeStruct((r, 256), BF16)],
        compiler_params=_cparams("parallel"),
        name="ab_proj",
    )(h, g, wqkv, wz, wg, wq, wkv, bqg, bkg)


def _split2(x):
    hi = x.astype(BF16)
    return hi, (x - hi.astype(F32)).astype(BF16)


def _split3(x):
    hi = x.astype(BF16)
    r1 = x - hi.astype(F32)
    mid = r1.astype(BF16)
    lo = (r1 - mid.astype(F32)).astype(BF16)
    return hi, mid, lo


def _gates_kernel(pre_ref, alog_ref, dtb_ref, bg_ref, gt_ref):
    nblk = pre_ref.shape[0] // BLK
    ri = lax.broadcasted_iota(jnp.int32, (BLK, BLK), 0)
    ci = lax.broadcasted_iota(jnp.int32, (BLK, BLK), 1)
    lower = (ri >= ci).astype(BF16)
    upper = (ri <= ci).astype(BF16)
    neg_a = -jnp.exp(alog_ref[...])
    dtb = dtb_ref[...]

    def one(n):
        x = pre_ref[pl.ds(pl.multiple_of(n * BLK, BLK), BLK), :]
        live = (ri + n * BLK) >= FRONT
        beta = jnp.where(live, _sigmoid(x), 0.0)
        g = jnp.where(live, neg_a * _softplus(x + dtb), 0.0)
        parts = _split3(g)
        pre = sum(_dot(lower, p) for p in parts)
        suf = sum(_dot(upper, p) for p in parts)
        tot = pre + suf - g
        return jnp.where(ci < 2, beta, jnp.where(ci == 2, pre, jnp.where(ci == 3, suf, tot)))

    grp = _chunk_group(nblk)

    def body(g, carry):
        outs = [one(g * grp + j) for j in range(grp)]
        for j, out in enumerate(outs):
            n = g * grp + j
            bg_ref[pl.ds(pl.multiple_of(n * BLK, BLK), BLK), :] = out
            gt_ref[n] = out.T[0:8, :]
        return carry

    lax.fori_loop(0, nblk // grp, body, 0)


def _gates(pre, alog_rows, dtb_rows):
    b, lp, _ = pre.shape
    nblk = lp // BLK
    return pl.pallas_call(
        _gates_kernel,
        grid=(b, A_HEADS),
        in_specs=[pl.BlockSpec((None, lp, BLK), lambda i, j: (i, 0, j)),
                  pl.BlockSpec((None, 1, BLK), lambda i, j: (j, 0, 0)),
                  pl.BlockSpec((None, 1, BLK), lambda i, j: (j, 0, 0))],
        out_specs=[pl.BlockSpec((None, lp, BLK), lambda i, j: (i, 0, j)),
                   pl.BlockSpec((None, None, nblk, 8, BLK), lambda i, j: (i, j, 0, 0, 0))],
        out_shape=[jax.ShapeDtypeStruct((b, lp, A_HEADS * BLK), F32),
                   jax.ShapeDtypeStruct((b, A_HEADS, nblk, 8, BLK), F32)],
        compiler_params=_cparams("parallel", "parallel"),
        name="delta_gates",
    )(pre, alog_rows, dtb_rows)


def _delta_kernel(q_ref, k_ref, v_ref, z_ref, bg_ref, gt_ref, cwq_ref, cwk_ref, cwv_ref, og_ref, y_ref,
                  sadd_s, smul_s, o_s, omul_s, gl_s):
    lp = q_ref.shape[0]
    nblk = lp // BLK
    grp = _chunk_group(nblk)
    ri = lax.broadcasted_iota(jnp.int32, (BLK, BLK), 0)
    ci = lax.broadcasted_iota(jnp.int32, (BLK, BLK), 1)
    eye = (ri == ci).astype(F32)
    incl = (ri >= ci, ri <= ci)
    strict = (ri > ci, ri < ci)

    def conv_silu(ref, w_ref, n):
        base = n * BLK
        cur = ref[pl.ds(pl.multiple_of(base, BLK), BLK), :]
        prev = ref[pl.ds(pl.multiple_of(jnp.maximum(base - 8, 0), 8), 8), :]
        nxt_start = jnp.minimum(base + BLK, lp - 8)
        nxt = ref[pl.ds(pl.multiple_of(nxt_start, 8), 8), :]
        nxt = jnp.where(n < nblk - 1, nxt, 0.0)
        win = jnp.concatenate([prev, cur, nxt], axis=0)
        h = A_CONV // 2
        acc = win[8 - h:8 - h + BLK, :] * w_ref[0:1, :]
        for j in range(1, A_CONV):
            acc = acc + win[8 - h + j:8 - h + j + BLK, :] * w_ref[j:j + 1, :]
        return _silu(acc)

    def l2n(x):
        return x * lax.rsqrt(jnp.sum(x * x, axis=-1, keepdims=True) + EPS)

    def chunk_inputs(n):
        rows = pl.ds(pl.multiple_of(n * BLK, BLK), BLK)
        live = (ri[:, 0:1] + n * BLK) >= FRONT
        qn = jnp.where(live, l2n(conv_silu(q_ref, cwq_ref, n)) * (A_DK ** -0.5), 0.0)
        kn = jnp.where(live, l2n(conv_silu(k_ref, cwk_ref, n)), 0.0)
        vv = jnp.where(live, conv_silu(v_ref, cwv_ref, n), 0.0)
        kn16 = kn.astype(BF16)
        kq = _dot_nt(jnp.concatenate([kn16, qn.astype(BF16)], axis=0), kn16)
        return dict(n=n, rows=rows, qn=qn, kn=kn, vv=vv, kk=kq[:BLK], qk=kq[BLK:], bg=bg_ref[rows, :], gt=gt_ref[n])

    def chain_setup(c, d):
        bg, gt = c["bg"], c["gt"]
        beta, ccol, tot = bg[:, d:d + 1], bg[:, 2 + d:3 + d], bg[:, 4 + d:5 + d]
        crow = gt[2 + d:3 + d, :]
        dec = jnp.exp(jnp.where(incl[d], ccol - crow, NEG))
        a = jnp.where(strict[d], beta * c["kk"] * dec, 0.0)
        return dict(c=c, d=d, beta=beta, ccol=ccol, tot=tot, dec=dec, a=a, t=eye - a, x=a.astype(BF16))

    def prep(g, carry):
        chunks = [chunk_inputs(g * grp + j) for j in range(grp)]
        chains = [chain_setup(c, d) for c in chunks for d in range(2)]
        zero = jnp.zeros((BLK, BLK), BF16)

        def blockdiag(xp):
            return jnp.concatenate([jnp.concatenate([xp[:, :BLK], zero], axis=1),
                                    jnp.concatenate([zero, xp[:, BLK:]], axis=1)], axis=0)

        pairs = [(chains[2 * j], chains[2 * j + 1]) for j in range(grp)]
        xps = [jnp.concatenate([f["x"], b["x"]], axis=1) for f, b in pairs]
        tps = [jnp.concatenate([f["t"], b["t"]], axis=1) for f, b in pairs]
        for _ in range(INV_SQUARINGS):
            xps = [_dot(xp, blockdiag(xp)).astype(BF16) for xp in xps]
            txs = [_dot(tp.astype(BF16), blockdiag(xp)) for tp, xp in zip(tps, xps)]
            tps = [tp + tx for tp, tx in zip(tps, txs)]
        for (f, b), tp in zip(pairs, tps):
            f["t"], b["t"] = tp[:, :BLK], tp[:, BLK:]
        for ch in chains:
            c = ch["c"]
            ch["ec"] = jnp.exp(ch["ccol"])
            ch["rhs"] = jnp.concatenate([ch["beta"] * c["vv"], ch["beta"] * c["kn"] * ch["ec"]], axis=1)
            ch["t16"] = ch["t"].astype(BF16)
        x0s = [_dot(ch["t16"], ch["rhs"].astype(BF16)) for ch in chains]
        res = []
        for ch, x0 in zip(chains, x0s):
            ah, al = _split2(ch["a"])
            xh, xl = _split2(x0)
            ax = _dot(jnp.concatenate([ah, al], axis=1), jnp.concatenate([xh, xh], axis=0)) + _dot(ah, xl)
            res.append((ch["rhs"] - x0 - ax).astype(BF16))
        uws = [(x0 + _dot(ch["t16"], e)).astype(BF16) for ch, x0, e in zip(chains, x0s, res)]
        kuws = [_dot((ch["c"]["kn"] * jnp.exp(ch["tot"] - ch["ccol"])).T.astype(BF16), uw)
                for ch, uw in zip(chains, uws)]
        quws = [_dot((ch["c"]["qk"] * ch["dec"]).astype(BF16), uw) for ch, uw in zip(chains, uws)]
        for ch, kuw, quw in zip(chains, kuws, quws):
            c, d = ch["c"], ch["d"]
            rows = c["rows"]
            sadd_s[d, rows, :] = kuw[:, :BLK]
            smul_s[d, rows, :] = (-kuw[:, BLK:]).astype(BF16)
            o_s[d, rows, :] = quw[:, :BLK]
            omul_s[d, rows, :] = (c["qn"] * ch["ec"] - quw[:, BLK:]).astype(BF16)
            gl_s[d * nblk + c["n"]] = jnp.broadcast_to(jnp.exp(ch["tot"]), (BLK, BLK))[0:8, :]
        return carry

    lax.fori_loop(0, nblk // grp, prep, 0)

    def scan_step(d, n, s):
        rows = pl.ds(pl.multiple_of(n * BLK, BLK), BLK)
        both = _dot(jnp.concatenate([smul_s[d, rows, :], omul_s[d, rows, :]], axis=0), s.astype(BF16))
        o_s[d, rows, :] = o_s[d, rows, :] + both[BLK:]
        return s * gl_s[d * nblk + n][0:1, :] + both[:BLK] + sadd_s[d, rows, :]

    def scan(i, carry):
        sf, sb = carry
        sf = scan_step(0, i, sf)
        sb = scan_step(1, nblk - 1 - i, sb)
        return sf, sb

    s0 = jnp.zeros((BLK, BLK), F32)
    lax.fori_loop(0, nblk, scan, (s0, s0))

    def finish(n, carry):
        rows = pl.ds(pl.multiple_of(n * BLK, BLK), BLK)
        o = o_s[0, rows, :] + o_s[1, rows, :]
        y_ref[rows, :] = (_rms(o, og_ref[...]) * _silu(z_ref[rows, :])).astype(y_ref.dtype)
        return carry

    lax.fori_loop(0, nblk, finish, 0, unroll=3 if nblk % 3 == 0 else 1)


def _delta(qkv, z, bg, gt, conv_w, o_gain):
    b, lp, _ = qkv.shape
    nblk = lp // BLK
    col = lambda off: pl.BlockSpec((None, lp, BLK), lambda i, j: (i, 0, j + off))
    cw = lambda off: pl.BlockSpec((A_CONV, BLK), lambda i, j: (0, j + off))
    return pl.pallas_call(
        _delta_kernel,
        grid=(b, A_HEADS),
        in_specs=[col(0), col(A_HEADS), col(2 * A_HEADS), col(0), col(0),
                  pl.BlockSpec((None, None, nblk, 8, BLK), lambda i, j: (i, j, 0, 0, 0)),
                  cw(0), cw(A_HEADS), cw(2 * A_HEADS),
                  pl.BlockSpec((1, BLK), lambda i, j: (0, 0))],
        out_specs=col(0),
        out_shape=jax.ShapeDtypeStruct((b, lp, A_HEADS * BLK), BF16),
        scratch_shapes=[pltpu.VMEM((2, lp, BLK), F32), pltpu.VMEM((2, lp, BLK), BF16),
                        pltpu.VMEM((2, lp, BLK), F32), pltpu.VMEM((2, lp, BLK), BF16),
                        pltpu.VMEM((2 * nblk, 8, BLK), F32)],
        compiler_params=_cparams("parallel", "parallel"),
        name="delta_mixer",
    )(qkv, qkv, qkv, z, bg, gt, conv_w, conv_w, conv_w, o_gain)


def _window_kernel(q_ref, kp_ref, kc_ref, kn_ref, km_ref, bias_ref, sink_ref, y_ref):
    i = pl.program_id(1)
    nblk = pl.num_programs(1)
    grp = B_HEADS // B_KV
    nk = 4 * BLK
    c = lax.broadcasted_iota(jnp.int32, (1, nk), 1)
    kblk = i - 1 + (c >> 7)
    edge = jnp.where((c >= 3 * BLK) | ((kblk >= 1) & (kblk < nblk)), 0.0, NEG)
    q = q_ref[...]
    kvs = (kp_ref[...], kc_ref[...], kn_ref[...], km_ref[...])
    ones = jnp.ones((nk, 2 * B_HD), BF16)
    lane = lax.broadcasted_iota(jnp.int32, (BLK, 2 * B_HD), 1)
    s4s, vexts = [], []
    for kvh in range(B_KV):
        ks = jnp.concatenate([t[:, kvh * B_HD:(kvh + 1) * B_HD] for t in kvs], axis=0)
        vs = jnp.concatenate([t[:, (B_KV + kvh) * B_HD:(B_KV + kvh + 1) * B_HD] for t in kvs], axis=0)
        q4 = jnp.concatenate([q[:, hh * B_HD:(hh + 1) * B_HD] for hh in range(kvh * grp, (kvh + 1) * grp)],
                             axis=0)
        s4s.append(_dot_nt(q4, ks))
        vexts.append(jnp.concatenate([vs, vs, ones], axis=1))
    ms = []
    pvs = []
    for kvh in range(B_KV):
        ps = []
        for gi in range(grp):
            hh = kvh * grp + gi
            s = s4s[kvh][gi * BLK:(gi + 1) * BLK] + bias_ref[hh] + edge
            m = jnp.maximum(jnp.max(s, axis=-1, keepdims=True), sink_ref[hh:hh + 1, 0:1])
            ps.append(jnp.exp2(s - m).astype(BF16))
            ms.append(m)
        pvs.append(_dot(jnp.concatenate(ps, axis=0), vexts[kvh]))
    outs = []
    for hh in range(B_HEADS):
        kvh, gi = divmod(hh, grp)
        o = pvs[kvh][gi * BLK:(gi + 1) * BLK]
        den = o[:, 2 * B_HD:] + jnp.exp2(sink_ref[hh:hh + 1, 0:1] - ms[hh])
        outs.append(o[:, :2 * B_HD] / den)
    for j in range(B_HEADS // 2):
        pair = jnp.where(lane < B_HD, outs[2 * j], outs[2 * j + 1])
        y_ref[:, 2 * j * B_HD:(2 * j + 2) * B_HD] = pair.astype(y_ref.dtype)

    @pl.when(i == 0)
    def _():
        rr = lax.broadcasted_iota(jnp.int32, y_ref.shape, 0)
        y_ref[...] = jnp.where(rr >= FRONT, y_ref[...], 0).astype(y_ref.dtype)


def _window_bias():
    r = jnp.arange(BLK)[:, None]
    c = jnp.arange(4 * BLK)[None, :]
    dist = jnp.abs(BLK + r - c)
    slopes = jnp.exp2(-8.0 * (jnp.arange(B_HEADS, dtype=F32) + 1.0) / B_HEADS)
    band = (c < 3 * BLK) & (dist <= B_WIN)
    alibi = -slopes[:, None, None] * dist.astype(F32)[None] * LOG2E
    rest = jnp.where(c >= 3 * BLK + FRONT, 0.0, NEG)
    return jnp.where(band[None], alibi, rest[None]).astype(F32)


def _window(qb, kvb, sink_rows):
    b, lp, _ = qb.shape
    nblk = lp // BLK
    bias = _window_bias()
    kv = lambda f: pl.BlockSpec((None, BLK, 2 * B_KV * B_HD), f)
    return pl.pallas_call(
        _window_kernel,
        grid=(b, nblk),
        in_specs=[pl.BlockSpec((None, BLK, B_HEADS * B_HD), lambda i, j: (i, j, 0)),
                  kv(lambda i, j: (i, jnp.maximum(j - 1, 0), 0)),
                  kv(lambda i, j: (i, j, 0)),
                  kv(lambda i, j: (i, jnp.minimum(j + 1, nblk - 1), 0)),
                  kv(lambda i, j: (i, 0, 0)),
                  pl.BlockSpec(bias.shape, lambda i, j: (0, 0, 0)),
                  pl.BlockSpec((B_HEADS, BLK), lambda i, j: (0, 0))],
        out_specs=pl.BlockSpec((None, BLK, B_HEADS * B_HD), lambda i, j: (i, j, 0)),
        out_shape=jax.ShapeDtypeStruct((b, lp, B_HEADS * B_HD), BF16),
        compiler_params=_cparams("parallel", "parallel"),
        name="window_mixer",
    )(qb, kvb, kvb, kvb, kvb, bias, sink_rows)


def _out_mlp_kernel(*refs, n_mix):
    h_ref = refs[0]
    mix_refs = refs[1:1 + n_mix]
    wo_ref, g_ref, w1_ref, w2_ref, o_ref = refs[1 + n_mix:]
    mix = jnp.concatenate([m[...] for m in mix_refs], axis=1)
    h = h_ref[...] + _dot(mix, wo_ref[...])
    u = _rms(h, g_ref[...]).astype(BF16)
    dff = w1_ref.shape[1]
    acc = h
    for c in range(dff // FF_CHUNK):
        sl = slice(c * FF_CHUNK, (c + 1) * FF_CHUNK)
        a = jnp.maximum(_dot(u, w1_ref[:, sl]), 0.0)
        acc = acc + _dot((a * a).astype(BF16), w2_ref[sl, :])
    o_ref[...] = acc


def _out_mlp(h, mixes, wo, g, w1, w2):
    r, d = h.shape
    row = lambda n: pl.BlockSpec((ROW_TILE, n), lambda i: (i, 0))
    full = lambda a: pl.BlockSpec(a.shape, lambda i: (0, 0))
    return pl.pallas_call(
        functools.partial(_out_mlp_kernel, n_mix=len(mixes)),
        grid=(r // ROW_TILE,),
        in_specs=[row(d)] + [row(m.shape[1]) for m in mixes] + [full(wo), full(g), full(w1), full(w2)],
        out_specs=row(d),
        out_shape=jax.ShapeDtypeStruct((r, d), F32),
        compiler_params=_cparams("parallel"),
        name="out_mlp",
    )(h, *mixes, wo, g, w1, w2)


def _c_proj_kernel(h_ref, g_ref, wq_ref, wk_ref, wv_ref, qg_ref, kg_ref, cos_ref, sin_ref,
                   q_ref, k_ref, v_ref):
    u = _rms(h_ref[...], g_ref[...]).astype(BF16)
    cosf = cos_ref[...]
    sinf = sin_ref[...]
    half = C_HD // 2

    def norm_rope(x, gain):
        x = _rms(x, gain)
        swapped = jnp.concatenate([x[:, half:], x[:, :half]], axis=1)
        return x * cosf + swapped * sinf

    k = _dot(u, wk_ref[...])
    half_w = C_HEADS * C_HD // 2
    q_lo = _dot(u, wq_ref[:, :half_w])
    for hh in range(C_KV):
        sl = slice(hh * C_HD, (hh + 1) * C_HD)
        k_ref[:, sl] = norm_rope(k[:, sl], kg_ref[...]).astype(BF16)
    q_hi = _dot(u, wq_ref[:, half_w:])
    for hh in range(C_HEADS // 2):
        sl = slice(hh * C_HD, (hh + 1) * C_HD)
        q_ref[:, sl] = (norm_rope(q_lo[:, sl], qg_ref[...]) * (C_HD ** -0.5 * LOG2E)).astype(BF16)
    v_ref[...] = _dot(u, wv_ref[...]).astype(BF16)
    for hh in range(C_HEADS // 2):
        sl = slice(hh * C_HD, (hh + 1) * C_HD)
        q_ref[:, half_w + hh * C_HD:half_w + (hh + 1) * C_HD] = (
            norm_rope(q_hi[:, sl], qg_ref[...]) * (C_HD ** -0.5 * LOG2E)).astype(BF16)


def _c_proj(h, g, wq, wk, wv, qg, kg, cosf, sinf):
    b, lp, d = h.shape
    tm = 3 * BLK
    row = lambda n: pl.BlockSpec((None, tm, n), lambda i, j: (i, j, 0))
    full = lambda a: pl.BlockSpec(a.shape, lambda i, j: (0, 0))
    pos = pl.BlockSpec((tm, C_HD), lambda i, j: (j, 0))
    return pl.pallas_call(
        _c_proj_kernel,
        grid=(b, lp // tm),
        in_specs=[row(d), full(g), full(wq), full(wk), full(wv), full(qg), full(kg), pos, pos],
        out_specs=[row(C_HEADS * C_HD), row(C_KV * C_HD), row(C_KV * C_HD)],
        out_shape=[jax.ShapeDtypeStruct((b, lp, C_HEADS * C_HD), BF16),
                   jax.ShapeDtypeStruct((b, lp, C_KV * C_HD), BF16),
                   jax.ShapeDtypeStruct((b, lp, C_KV * C_HD), BF16)],
        compiler_params=_cparams("parallel", "parallel"),
        name="c_proj",
    )(h, g, wq, wk, wv, qg, kg, cosf, sinf)


ATT_TK = 512
ATT_QB = 3


def _dense_kernel(q_ref, k_ref, v_ref, y_ref, *scratch):
    i = pl.program_id(2)
    grp = C_HEADS // C_KV
    lp = k_ref.shape[0]
    nkb = (lp - BLK) // ATT_TK
    nq = ATT_QB
    sa_s, sb_s, acc_s = scratch[:nq], scratch[nq:2 * nq], scratch[2 * nq:]
    qs = [jnp.concatenate([q_ref[c * BLK:(c + 1) * BLK, g * C_HD:(g + 1) * C_HD] for g in range(grp)], axis=0)
          for c in range(nq)]
    m_rows = grp * BLK

    def keys(t):
        return pl.ds(pl.multiple_of(BLK + t * ATT_TK, BLK), ATT_TK)

    def v_ones(rows, n):
        return jnp.concatenate([v_ref[rows, :], jnp.ones((n, C_HD), BF16)], axis=1)

    def scores(t, s_refs):
        kt = k_ref[keys(t), :]
        for c in range(nq):
            s_refs[c][...] = _dot_nt(qs[c], kt).astype(BF16)

    def step(t, ms, s_refs):
        vt = v_ones(keys(t), ATT_TK)
        out = []
        for c in range(nq):
            s = s_refs[c][...]
            m_new = jnp.maximum(ms[c], jnp.max(s, axis=-1, keepdims=True).astype(F32))
            p = jnp.exp2(s - m_new.astype(BF16))
            acc_s[c][...] = jnp.exp2(ms[c] - m_new) * acc_s[c][...] + _dot(p, vt)
            out.append(m_new)
        return out

    scores(0, sa_s)
    k0 = k_ref[0:BLK, :]
    v0 = v_ones(slice(0, BLK), BLK)
    kc = lax.broadcasted_iota(jnp.int32, (m_rows, BLK), 1)
    ms = []
    for c in range(nq):
        s0 = jnp.where(kc >= FRONT, _dot_nt(qs[c], k0), NEG).astype(BF16)
        m = jnp.max(s0, axis=-1, keepdims=True)
        acc_s[c][...] = _dot(jnp.exp2(s0 - m), v0)
        ms.append(m.astype(F32))

    def body(j, ms):
        scores(2 * j + 1, sb_s)
        ms = step(2 * j, ms, sa_s)
        scores(2 * j + 2, sa_s)
        return step(2 * j + 1, ms, sb_s)

    ms = lax.fori_loop(0, nkb // 2 - 1, body, ms)
    scores(nkb - 1, sb_s)
    ms = step(nkb - 2, ms, sa_s)
    ms = step(nkb - 1, ms, sb_s)
    rr = lax.broadcasted_iota(jnp.int32, (BLK, C_HD), 0)
    for c in range(nq):
        acc = acc_s[c][...]
        o = acc[:, :C_HD] / acc[:, C_HD:C_HD + 1]
        for g in range(grp):
            og = o[g * BLK:(g + 1) * BLK, :]
            if c == 0:
                og = jnp.where((i == 0) & (rr < FRONT), 0.0, og)
            y_ref[c * BLK:(c + 1) * BLK, g * C_HD:(g + 1) * C_HD] = og.astype(y_ref.dtype)


def _dense(q, k, v):
    b, lp, _ = q.shape
    grp = C_HEADS // C_KV
    tq = ATT_QB * BLK
    assert lp % tq == 0 and ((lp - BLK) // ATT_TK) % 2 == 0 and (lp - BLK) % ATT_TK == 0
    score = pltpu.VMEM((grp * BLK, ATT_TK), BF16)
    return pl.pallas_call(
        _dense_kernel,
        grid=(b, C_KV, lp // tq),
        in_specs=[pl.BlockSpec((None, tq, grp * C_HD), lambda i, j, t: (i, t, j)),
                  pl.BlockSpec((None, lp, C_HD), lambda i, j, t: (i, 0, j)),
                  pl.BlockSpec((None, lp, C_HD), lambda i, j, t: (i, 0, j))],
        out_specs=pl.BlockSpec((None, tq, grp * C_HD), lambda i, j, t: (i, t, j)),
        out_shape=jax.ShapeDtypeStruct((b, lp, C_HEADS * C_HD), BF16),
        scratch_shapes=[score] * (2 * ATT_QB) + [pltpu.VMEM((grp * BLK, 2 * C_HD), F32)] * ATT_QB,
        compiler_params=_cparams("parallel", "parallel", "arbitrary"),
        name="dense_mixer",
    )(q, k, v)


def _rope_tables(lp, n_tok):
    rows = n_tok // GRID_W
    row = jnp.repeat(jnp.arange(rows), GRID_W)
    col = jnp.tile(jnp.arange(GRID_W), rows)
    meta = jnp.arange(N_META) - N_META
    front = jnp.zeros((lp - N_META - n_tok,), jnp.int32)
    row = jnp.concatenate([front, meta, row]).astype(F32)
    col = jnp.concatenate([front, meta, col]).astype(F32)
    axis_dim = C_HD // 2
    freqs = ROPE_THETA ** (-jnp.arange(0, axis_dim, 2, dtype=F32) / axis_dim)
    ang = jnp.concatenate([row[:, None] * freqs, col[:, None] * freqs], axis=-1)
    cos, sin = jnp.cos(ang), jnp.sin(ang)
    return jnp.concatenate([cos, cos], axis=-1), jnp.concatenate([-sin, sin], axis=-1)


def _gate_weight(w_b, w_a):
    d = w_b.shape[0]
    w_b = w_b.reshape(d, 2, A_HEADS)
    w_a = w_a.reshape(d, 2, A_HEADS)
    per_head = jnp.concatenate([w_b, w_a, w_a], axis=1)
    per_head = jnp.transpose(per_head, (0, 2, 1))
    per_head = jnp.pad(per_head, ((0, 0), (0, 0), (0, BLK - 6)))
    return per_head.reshape(d, A_HEADS * BLK)


def _gate_rows(p):
    t = jnp.transpose(p.astype(F32), (1, 0))
    rows = jnp.concatenate([jnp.zeros_like(t), t, t], axis=1)
    return jnp.pad(rows, ((0, 0), (0, BLK - 6)))[:, None, :]


def kernel(x, meta_tokens, attn_norm_g, mlp_norm_g, w_in_ab, conv_w_a, a_log, dt_bias, a_out_norm_g,
           b_q_norm_g, b_k_norm_g, b_sink, w_out_ab, w_qkv_c, c_q_norm_g, c_k_norm_g, w_out_c, w_ff1, w_ff2):
    bsz, n_tok, d = x.shape
    lp = FRONT + N_META + n_tok
    depth = attn_norm_g.shape[0]
    meta = jnp.broadcast_to(meta_tokens.astype(x.dtype)[None], (bsz, N_META, d))
    h = jnp.concatenate([jnp.zeros((bsz, FRONT, d), x.dtype), meta, x], axis=1).reshape(bsz * lp, d)
    cosf, sinf = _rope_tables(lp, n_tok)
    deint = jnp.concatenate([jnp.arange(0, C_HD, 2), jnp.arange(1, C_HD, 2)])
    row2 = lambda v: v.astype(F32).reshape(1, -1)

    for layer in range(depth):
        i = layer // 2
        g_attn = row2(attn_norm_g[layer])
        if layer % 2 == 0:
            w = w_in_ab[i]
            qkv_w = w[:, :1536].astype(BF16)
            z_w = w[:, 1536:2048].astype(BF16)
            gate_w = _gate_weight(w[:, 2048:2056], w[:, 2056:2064]).astype(BF16)
            bq_w = w[:, 2064:2576].astype(BF16)
            bkv_w = w[:, 2576:2832].astype(BF16)
            qkv, z, gate_pre, qb, kvb = _ab_proj(h, g_attn, qkv_w, z_w, gate_w, bq_w, bkv_w,
                                                 row2(b_q_norm_g[i]), row2(b_k_norm_g[i]))
            r3 = lambda t: t.reshape(bsz, lp, t.shape[-1])
            bg, gt = _gates(r3(gate_pre), _gate_rows(a_log[i]), _gate_rows(dt_bias[i]))
            ya = _delta(r3(qkv), r3(z), bg, gt, conv_w_a[i].astype(F32), row2(a_out_norm_g[i]))
            sink_rows = jnp.broadcast_to(b_sink[i].astype(F32)[:, None] * LOG2E, (B_HEADS, BLK))
            yb = _window(r3(qb), r3(kvb), sink_rows)
            wo = w_out_ab[i].astype(BF16)
            mixes = [ya.reshape(bsz * lp, -1), yb.reshape(bsz * lp, -1)]
        else:
            w = w_qkv_c[i]
            perm = lambda wc, nh: wc.reshape(d, nh, C_HD)[:, :, deint].reshape(d, nh * C_HD)
            wq = perm(w[:, :C_HEADS * C_HD], C_HEADS).astype(BF16)
            wk = perm(w[:, C_HEADS * C_HD:(C_HEADS + C_KV) * C_HD], C_KV).astype(BF16)
            wv = w[:, (C_HEADS + C_KV) * C_HD:].astype(BF16)
            q, k, v = _c_proj(h.reshape(bsz, lp, d), g_attn, wq, wk, wv,
                              row2(c_q_norm_g[i][deint]), row2(c_k_norm_g[i][deint]), cosf, sinf)
            att = _dense(q, k, v)
            mixes = [att.reshape(bsz * lp, -1)]
            wo = w_out_c[i].astype(BF16)
        h = _out_mlp(h, mixes, wo, row2(mlp_norm_g[layer]), w_ff1[layer].astype(BF16),
                     w_ff2[layer].astype(BF16))
    return h.reshape(bsz, lp, d)[:, FRONT + N_META:]
```

```python
import functools
import math

import jax
import jax.numpy as jnp
from jax import lax
from jax.experimental import pallas as pl
from jax.experimental.pallas import tpu as pltpu

F32 = jnp.float32
BF16 = jnp.bfloat16

EPS = 1e-6
N_META = 16
BLK = 128
FRONT = BLK - N_META
GRID_W = 64
ROPE_THETA = 10000.0
A_HEADS, A_DK, A_CONV = 4, 128, 5
B_HEADS, B_KV, B_HD, B_WIN = 8, 2, 64, 128
C_HEADS, C_KV, C_HD = 8, 2, 128
NEG = -1e30
LOG2E = math.log2(math.e)

VMEM_LIMIT = 56 * 1024 * 1024
ROW_TILE = 512
FF_CHUNK = 512
INV_SQUARINGS = 6
PREP_UNROLL = 11


def _chunk_group(nblk):
    return max(g for g in range(1, PREP_UNROLL + 1) if nblk % g == 0)


def _cparams(*sem):
    return pltpu.CompilerParams(dimension_semantics=sem, vmem_limit_bytes=VMEM_LIMIT)


def _sigmoid(x):
    return 1.0 / (1.0 + jnp.exp(-x))


def _silu(x):
    return x * _sigmoid(x)


def _softplus(x):
    return jnp.maximum(x, 0.0) + jnp.log1p(jnp.exp(-jnp.abs(x)))


def _rms(x, g):
    return x * lax.rsqrt(jnp.mean(x * x, axis=-1, keepdims=True) + EPS) * g


def _dot(a, b):
    return jnp.dot(a, b, preferred_element_type=F32)


def _dot_nt(a, b):
    return lax.dot_general(a, b, (((1,), (1,)), ((), ())), preferred_element_type=F32)


def _ab_proj_kernel(h_ref, g_ref, wqkv_ref, wz_ref, wg_ref, wq_ref, wkv_ref, bqg_ref, bkg_ref,
                    qkv_ref, z_ref, gate_ref, qb_ref, kvb_ref):
    u = _rms(h_ref[...], g_ref[...]).astype(BF16)
    qb = _dot(u, wq_ref[...])
    kv = _dot(u, wkv_ref[...])
    scale = B_HD ** -0.5 * LOG2E

    def q_heads(lo, hi):
        for hh in range(lo, hi):
            sl = slice(hh * B_HD, (hh + 1) * B_HD)
            qb_ref[:, sl] = (_rms(qb[:, sl], bqg_ref[...]) * scale).astype(BF16)

    qkv_ref[:, 0:512] = _dot(u, wqkv_ref[:, 0:512])
    q_heads(0, 3)
    qkv_ref[:, 512:1024] = _dot(u, wqkv_ref[:, 512:1024])
    q_heads(3, 6)
    qkv_ref[:, 1024:1536] = _dot(u, wqkv_ref[:, 1024:1536])
    q_heads(6, B_HEADS)
    z_ref[...] = _dot(u, wz_ref[...])
    for hh in range(B_KV):
        sl = slice(hh * B_HD, (hh + 1) * B_HD)
        kvb_ref[:, sl] = _rms(kv[:, sl], bkg_ref[...]).astype(BF16)
    kvb_ref[:, B_KV * B_HD:] = kv[:, B_KV * B_HD:].astype(BF16)
    gate_ref[...] = _dot(u, wg_ref[...])


def _ab_proj(h, g, wqkv, wz, wg, wq, wkv, bqg, bkg):
    r, d = h.shape
    row = lambda n: pl.BlockSpec((ROW_TILE, n), lambda i: (i, 0))
    full = lambda a: pl.BlockSpec(a.shape, lambda i: (0, 0))
    return pl.pallas_call(
        _ab_proj_kernel,
        grid=(r // ROW_TILE,),
        in_specs=[row(d), full(g), full(wqkv), full(wz), full(wg), full(wq), full(wkv), full(bqg), full(bkg)],
        out_specs=[row(1536), row(512), row(512), row(512), row(256)],
        out_shape=[jax.ShapeDtypeStruct((r, 1536), F32), jax.ShapeDtypeStruct((r, 512), F32),
                   jax.ShapeDtypeStruct((r, 512), F32), jax.ShapeDtypeStruct((r, 512), BF16),
                   jax.ShapeDtypeStruct((r, 256), BF16)],
        compiler_params=_cparams("parallel"),
        name="ab_proj",
    )(h, g, wqkv, wz, wg, wq, wkv, bqg, bkg)


def _split2(x):
    hi = x.astype(BF16)
    return hi, (x - hi.astype(F32)).astype(BF16)


def _split3(x):
    hi = x.astype(BF16)
    r1 = x - hi.astype(F32)
    mid = r1.astype(BF16)
    lo = (r1 - mid.astype(F32)).astype(BF16)
    return hi, mid, lo


def _gates_kernel(pre_ref, alog_ref, dtb_ref, bg_ref, gt_ref):
    nblk = pre_ref.shape[0] // BLK
    ri = lax.broadcasted_iota(jnp.int32, (BLK, BLK), 0)
    ci = lax.broadcasted_iota(jnp.int32, (BLK, BLK), 1)
    lower = (ri >= ci).astype(BF16)
    upper = (ri <= ci).astype(BF16)
    neg_a = -jnp.exp(alog_ref[...])
    dtb = dtb_ref[...]

    def one(n):
        x = pre_ref[pl.ds(pl.multiple_of(n * BLK, BLK), BLK), :]
        live = (ri + n * BLK) >= FRONT
        beta = jnp.where(live, _sigmoid(x), 0.0)
        g = jnp.where(live, neg_a * _softplus(x + dtb), 0.0)
        parts = _split3(g)
        pre = sum(_dot(lower, p) for p in parts)
        suf = sum(_dot(upper, p) for p in parts)
        tot = pre + suf - g
        return jnp.where(ci < 2, beta, jnp.where(ci == 2, pre, jnp.where(ci == 3, suf, tot)))

    grp = _chunk_group(nblk)

    def body(g, carry):
        outs = [one(g * grp + j) for j in range(grp)]
        for j, out in enumerate(outs):
            n = g * grp + j
            bg_ref[pl.ds(pl.multiple_of(n * BLK, BLK), BLK), :] = out
            gt_ref[n] = out.T[0:8, :]
        return carry

    lax.fori_loop(0, nblk // grp, body, 0)


def _gates(pre, alog_rows, dtb_rows):
    b, lp, _ = pre.shape
    nblk = lp // BLK
    return pl.pallas_call(
        _gates_kernel,
        grid=(b, A_HEADS),
        in_specs=[pl.BlockSpec((None, lp, BLK), lambda i, j: (i, 0, j)),
                  pl.BlockSpec((None, 1, BLK), lambda i, j: (j, 0, 0)),
                  pl.BlockSpec((None, 1, BLK), lambda i, j: (j, 0, 0))],
        out_specs=[pl.BlockSpec((None, lp, BLK), lambda i, j: (i, 0, j)),
                   pl.BlockSpec((None, None, nblk, 8, BLK), lambda i, j: (i, j, 0, 0, 0))],
        out_shape=[jax.ShapeDtypeStruct((b, lp, A_HEADS * BLK), F32),
                   jax.ShapeDtypeStruct((b, A_HEADS, nblk, 8, BLK), F32)],
        compiler_params=_cparams("parallel", "parallel"),
        name="delta_gates",
    )(pre, alog_rows, dtb_rows)


def _delta_kernel(q_ref, k_ref, v_ref, z_ref, bg_ref, gt_ref, cwq_ref, cwk_ref, cwv_ref, og_ref, y_ref,
                  sadd_s, smul_s, o_s, omul_s, gl_s):
    lp = q_ref.shape[0]
    nblk = lp // BLK
    grp = _chunk_group(nblk)
    ri = lax.broadcasted_iota(jnp.int32, (BLK, BLK), 0)
    ci = lax.broadcasted_iota(jnp.int32, (BLK, BLK), 1)
    eye = (ri == ci).astype(F32)
    incl = (ri >= ci, ri <= ci)
    strict = (ri > ci, ri < ci)

    def conv_silu(ref, w_ref, n):
        base = n * BLK
        cur = ref[pl.ds(pl.multiple_of(base, BLK), BLK), :]
        prev = ref[pl.ds(pl.multiple_of(jnp.maximum(base - 8, 0), 8), 8), :]
        nxt_start = jnp.minimum(base + BLK, lp - 8)
        nxt = ref[pl.ds(pl.multiple_of(nxt_start, 8), 8), :]
        nxt = jnp.where(n < nblk - 1, nxt, 0.0)
        win = jnp.concatenate([prev, cur, nxt], axis=0)
        h = A_CONV // 2
        acc = win[8 - h:8 - h + BLK, :] * w_ref[0:1, :]
        for j in range(1, A_CONV):
            acc = acc + win[8 - h + j:8 - h + j + BLK, :] * w_ref[j:j + 1, :]
        return _silu(acc)

    def l2n(x):
        return x * lax.rsqrt(jnp.sum(x * x, axis=-1, keepdims=True) + EPS)

    def chunk_inputs(n):
        rows = pl.ds(pl.multiple_of(n * BLK, BLK), BLK)
        live = (ri[:, 0:1] + n * BLK) >= FRONT
        qn = jnp.where(live, l2n(conv_silu(q_ref, cwq_ref, n)) * (A_DK ** -0.5), 0.0)
        kn = jnp.where(live, l2n(conv_silu(k_ref, cwk_ref, n)), 0.0)
        vv = jnp.where(live, conv_silu(v_ref, cwv_ref, n), 0.0)
        kn16 = kn.astype(BF16)
        kq = _dot_nt(jnp.concatenate([kn16, qn.astype(BF16)], axis=0), kn16)
        return dict(n=n, rows=rows, qn=qn, kn=kn, vv=vv, kk=kq[:BLK], qk=kq[BLK:], bg=bg_ref[rows, :], gt=gt_ref[n])

    def chain_setup(c, d):
        bg, gt = c["bg"], c["gt"]
        beta, ccol, tot = bg[:, d:d + 1], bg[:, 2 + d:3 + d], bg[:, 4 + d:5 + d]
        crow = gt[2 + d:3 + d, :]
        dec = jnp.exp(jnp.where(incl[d], ccol - crow, NEG))
        a = jnp.where(strict[d], beta * c["kk"] * dec, 0.0)
        return dict(c=c, d=d, beta=beta, ccol=ccol, tot=tot, dec=dec, a=a, t=eye - a, x=a.astype(BF16))

    def prep(g, carry):
        chunks = [chunk_inputs(g * grp + j) for j in range(grp)]
        chains = [chain_setup(c, d) for c in chunks for d in range(2)]
        zero = jnp.zeros((BLK, BLK), BF16)

        def blockdiag(xp):
            return jnp.concatenate([jnp.concatenate([xp[:, :BLK], zero], axis=1),
                                    jnp.concatenate([zero, xp[:, BLK:]], axis=1)], axis=0)

        pairs = [(chains[2 * j], chains[2 * j + 1]) for j in range(grp)]
        xps = [jnp.concatenate([f["x"], b["x"]], axis=1) for f, b in pairs]
        tps = [jnp.concatenate([f["t"], b["t"]], axis=1) for f, b in pairs]
        for _ in range(INV_SQUARINGS):
            xps = [_dot(xp, blockdiag(xp)).astype(BF16) for xp in xps]
            txs = [_dot(tp.astype(BF16), blockdiag(xp)) for tp, xp in zip(tps, xps)]
            tps = [tp + tx for tp, tx in zip(tps, txs)]
        for (f, b), tp in zip(pairs, tps):
            f["t"], b["t"] = tp[:, :BLK], tp[:, BLK:]
        for ch in chains:
            c = ch["c"]
            ch["ec"] = jnp.exp(ch["ccol"])
            ch["rhs"] = jnp.concatenate([ch["beta"] * c["vv"], ch["beta"] * c["kn"] * ch["ec"]], axis=1)
            ch["t16"] = ch["t"].astype(BF16)
        x0s = [_dot(ch["t16"], ch["rhs"].astype(BF16)) for ch in chains]
        res = []
        for ch, x0 in zip(chains, x0s):
            ah, al = _split2(ch["a"])
            xh, xl = _split2(x0)
            ax = _dot(jnp.concatenate([ah, al], axis=1), jnp.concatenate([xh, xh], axis=0)) + _dot(ah, xl)
            res.append((ch["rhs"] - x0 - ax).astype(BF16))
        uws = [(x0 + _dot(ch["t16"], e)).astype(BF16) for ch, x0, e in zip(chains, x0s, res)]
        kuws = [_dot((ch["c"]["kn"] * jnp.exp(ch["tot"] - ch["ccol"])).T.astype(BF16), uw)
                for ch, uw in zip(chains, uws)]
        quws = [_dot((ch["c"]["qk"] * ch["dec"]).astype(BF16), uw) for ch, uw in zip(chains, uws)]
        for ch, kuw, quw in zip(chains, kuws, quws):
            c, d = ch["c"], ch["d"]
            rows = c["rows"]
            sadd_s[d, rows, :] = kuw[:, :BLK]
            smul_s[d, rows, :] = (-kuw[:, BLK:]).astype(BF16)
            o_s[d, rows, :] = quw[:, :BLK]
            omul_s[d, rows, :] = (c["qn"] * ch["ec"] - quw[:, BLK:]).astype(BF16)
            gl_s[d * nblk + c["n"]] = jnp.broadcast_to(jnp.exp(ch["tot"]), (BLK, BLK))[0:8, :]
        return carry

    lax.fori_loop(0, nblk // grp, prep, 0)

    def scan_step(d, n, s):
        rows = pl.ds(pl.multiple_of(n * BLK, BLK), BLK)
        both = _dot(jnp.concatenate([smul_s[d, rows, :], omul_s[d, rows, :]], axis=0), s.astype(BF16))
        o_s[d, rows, :] = o_s[d, rows, :] + both[BLK:]
        return s * gl_s[d * nblk + n][0:1, :] + both[:BLK] + sadd_s[d, rows, :]

    def scan(i, carry):
        sf, sb = carry
        sf = scan_step(0, i, sf)
        sb = scan_step(1, nblk - 1 - i, sb)
        return sf, sb

    s0 = jnp.zeros((BLK, BLK), F32)
    lax.fori_loop(0, nblk, scan, (s0, s0))

    def finish(n, carry):
        rows = pl.ds(pl.multiple_of(n * BLK, BLK), BLK)
        o = o_s[0, rows, :] + o_s[1, rows, :]
        y_ref[rows, :] = (_rms(o, og_ref[...]) * _silu(z_ref[rows, :])).astype(y_ref.dtype)
        return carry

    lax.fori_loop(0, nblk, finish, 0, unroll=3 if nblk % 3 == 0 else 1)


def _delta(qkv, z, bg, gt, conv_w, o_gain):
    b, lp, _ = qkv.shape
    nblk = lp // BLK
    col = lambda off: pl.BlockSpec((None, lp, BLK), lambda i, j: (i, 0, j + off))
    cw = lambda off: pl.BlockSpec((A_CONV, BLK), lambda i, j: (0, j + off))
    return pl.pallas_call(
        _delta_kernel,
        grid=(b, A_HEADS),
        in_specs=[col(0), col(A_HEADS), col(2 * A_HEADS), col(0), col(0),
                  pl.BlockSpec((None, None, nblk, 8, BLK), lambda i, j: (i, j, 0, 0, 0)),
                  cw(0), cw(A_HEADS), cw(2 * A_HEADS),
                  pl.BlockSpec((1, BLK), lambda i, j: (0, 0))],
        out_specs=col(0),
        out_shape=jax.ShapeDtypeStruct((b, lp, A_HEADS * BLK), BF16),
        scratch_shapes=[pltpu.VMEM((2, lp, BLK), F32), pltpu.VMEM((2, lp, BLK), BF16),
                        pltpu.VMEM((2, lp, BLK), F32), pltpu.VMEM((2, lp, BLK), BF16),
                        pltpu.VMEM((2 * nblk, 8, BLK), F32)],
        compiler_params=_cparams("parallel", "parallel"),
        name="delta_mixer",
    )(qkv, qkv, qkv, z, bg, gt, conv_w, conv_w, conv_w, o_gain)


def _window_kernel(q_ref, kp_ref, kc_ref, kn_ref, km_ref, bias_ref, sink_ref, y_ref):
    i = pl.program_id(1)
    nblk = pl.num_programs(1)
    grp = B_HEADS // B_KV
    nk = 4 * BLK
    c = lax.broadcasted_iota(jnp.int32, (1, nk), 1)
    kblk = i - 1 + (c >> 7)
    edge = jnp.where((c >= 3 * BLK) | ((kblk >= 1) & (kblk < nblk)), 0.0, NEG)
    q = q_ref[...]
    kvs = (kp_ref[...], kc_ref[...], kn_ref[...], km_ref[...])
    ones = jnp.ones((nk, 2 * B_HD), BF16)
    lane = lax.broadcasted_iota(jnp.int32, (BLK, 2 * B_HD), 1)
    s4s, vexts = [], []
    for kvh in range(B_KV):
        ks = jnp.concatenate([t[:, kvh * B_HD:(kvh + 1) * B_HD] for t in kvs], axis=0)
        vs = jnp.concatenate([t[:, (B_KV + kvh) * B_HD:(B_KV + kvh + 1) * B_HD] for t in kvs], axis=0)
        q4 = jnp.concatenate([q[:, hh * B_HD:(hh + 1) * B_HD] for hh in range(kvh * grp, (kvh + 1) * grp)],
                             axis=0)
        s4s.append(_dot_nt(q4, ks))
        vexts.append(jnp.concatenate([vs, vs, ones], axis=1))
    ms = []
    pvs = []
    for kvh in range(B_KV):
        ps = []
        for gi in range(grp):
            hh = kvh * grp + gi
            s = s4s[kvh][gi * BLK:(gi + 1) * BLK] + bias_ref[hh] + edge
            m = jnp.maximum(jnp.max(s, axis=-1, keepdims=True), sink_ref[hh:hh + 1, 0:1])
            ps.append(jnp.exp2(s - m).astype(BF16))
            ms.append(m)
        pvs.append(_dot(jnp.concatenate(ps, axis=0), vexts[kvh]))
    outs = []
    for hh in range(B_HEADS):
        kvh, gi = divmod(hh, grp)
        o = pvs[kvh][gi * BLK:(gi + 1) * BLK]
        den = o[:, 2 * B_HD:] + jnp.exp2(sink_ref[hh:hh + 1, 0:1] - ms[hh])
        outs.append(o[:, :2 * B_HD] / den)
    for j in range(B_HEADS // 2):
        pair = jnp.where(lane < B_HD, outs[2 * j], outs[2 * j + 1])
        y_ref[:, 2 * j * B_HD:(2 * j + 2) * B_HD] = pair.astype(y_ref.dtype)

    @pl.when(i == 0)
    def _():
        rr = lax.broadcasted_iota(jnp.int32, y_ref.shape, 0)
        y_ref[...] = jnp.where(rr >= FRONT, y_ref[...], 0).astype(y_ref.dtype)


def _window_bias():
    r = jnp.arange(BLK)[:, None]
    c = jnp.arange(4 * BLK)[None, :]
    dist = jnp.abs(BLK + r - c)
    slopes = jnp.exp2(-8.0 * (jnp.arange(B_HEADS, dtype=F32) + 1.0) / B_HEADS)
    band = (c < 3 * BLK) & (dist <= B_WIN)
    alibi = -slopes[:, None, None] * dist.astype(F32)[None] * LOG2E
    rest = jnp.where(c >= 3 * BLK + FRONT, 0.0, NEG)
    return jnp.where(band[None], alibi, rest[None]).astype(F32)


def _window(qb, kvb, sink_rows):
    b, lp, _ = qb.shape
    nblk = lp // BLK
    bias = _window_bias()
    kv = lambda f: pl.BlockSpec((None, BLK, 2 * B_KV * B_HD), f)
    return pl.pallas_call(
        _window_kernel,
        grid=(b, nblk),
        in_specs=[pl.BlockSpec((None, BLK, B_HEADS * B_HD), lambda i, j: (i, j, 0)),
                  kv(lambda i, j: (i, jnp.maximum(j - 1, 0), 0)),
                  kv(lambda i, j: (i, j, 0)),
                  kv(lambda i, j: (i, jnp.minimum(j + 1, nblk - 1), 0)),
                  kv(lambda i, j: (i, 0, 0)),
                  pl.BlockSpec(bias.shape, lambda i, j: (0, 0, 0)),
                  pl.BlockSpec((B_HEADS, BLK), lambda i, j: (0, 0))],
        out_specs=pl.BlockSpec((None, BLK, B_HEADS * B_HD), lambda i, j: (i, j, 0)),
        out_shape=jax.ShapeDtypeStruct((b, lp, B_HEADS * B_HD), BF16),
        compiler_params=_cparams("parallel", "parallel"),
        name="window_mixer",
    )(qb, kvb, kvb, kvb, kvb, bias, sink_rows)


def _out_mlp_kernel(*refs, n_mix):
    h_ref = refs[0]
    mix_refs = refs[1:1 + n_mix]
    wo_ref, g_ref, w1_ref, w2_ref, o_ref = refs[1 + n_mix:]
    mix = jnp.concatenate([m[...] for m in mix_refs], axis=1)
    h = h_ref[...] + _dot(mix, wo_ref[...])
    u = _rms(h, g_ref[...]).astype(BF16)
    dff = w1_ref.shape[1]
    acc = h
    for c in range(dff // FF_CHUNK):
        sl = slice(c * FF_CHUNK, (c + 1) * FF_CHUNK)
        a = jnp.maximum(_dot(u, w1_ref[:, sl]), 0.0)
        acc = acc + _dot((a * a).astype(BF16), w2_ref[sl, :])
    o_ref[...] = acc


def _out_mlp(h, mixes, wo, g, w1, w2):
    r, d = h.shape
    row = lambda n: pl.BlockSpec((ROW_TILE, n), lambda i: (i, 0))
    full = lambda a: pl.BlockSpec(a.shape, lambda i: (0, 0))
    return pl.pallas_call(
        functools.partial(_out_mlp_kernel, n_mix=len(mixes)),
        grid=(r // ROW_TILE,),
        in_specs=[row(d)] + [row(m.shape[1]) for m in mixes] + [full(wo), full(g), full(w1), full(w2)],
        out_specs=row(d),
        out_shape=jax.ShapeDtypeStruct((r, d), F32),
        compiler_params=_cparams("parallel"),
        name="out_mlp",
    )(h, *mixes, wo, g, w1, w2)


def _c_proj_kernel(h_ref, g_ref, wq_ref, wk_ref, wv_ref, qg_ref, kg_ref, cos_ref, sin_ref,
                   q_ref, k_ref, v_ref):
    u = _rms(h_ref[...], g_ref[...]).astype(BF16)
    cosf = cos_ref[...]
    sinf = sin_ref[...]
    half = C_HD // 2

    def norm_rope(x, gain):
        x = _rms(x, gain)
        swapped = jnp.concatenate([x[:, half:], x[:, :half]], axis=1)
        return x * cosf + swapped * sinf

    k = _dot(u, wk_ref[...])
    half_w = C_HEADS * C_HD // 2
    q_lo = _dot(u, wq_ref[:, :half_w])
    for hh in range(C_KV):
        sl = slice(hh * C_HD, (hh + 1) * C_HD)
        k_ref[:, sl] = norm_rope(k[:, sl], kg_ref[...]).astype(BF16)
    q_hi = _dot(u, wq_ref[:, half_w:])
    for hh in range(C_HEADS // 2):
        sl = slice(hh * C_HD, (hh + 1) * C_HD)
        q_ref[:, sl] = (norm_rope(q_lo[:, sl], qg_ref[...]) * (C_HD ** -0.5 * LOG2E)).astype(BF16)
    v_ref[...] = _dot(u, wv_ref[...]).astype(BF16)
    for hh in range(C_HEADS // 2):
        sl = slice(hh * C_HD, (hh + 1) * C_HD)
        q_ref[:, half_w + hh * C_HD:half_w + (hh + 1) * C_HD] = (
            norm_rope(q_hi[:, sl], qg_ref[...]) * (C_HD ** -0.5 * LOG2E)).astype(BF16)


def _c_proj(h, g, wq, wk, wv, qg, kg, cosf, sinf):
    b, lp, d = h.shape
    tm = 3 * BLK
    row = lambda n: pl.BlockSpec((None, tm, n), lambda i, j: (i, j, 0))
    full = lambda a: pl.BlockSpec(a.shape, lambda i, j: (0, 0))
    pos = pl.BlockSpec((tm, C_HD), lambda i, j: (j, 0))
    return pl.pallas_call(
        _c_proj_kernel,
        grid=(b, lp // tm),
        in_specs=[row(d), full(g), full(wq), full(wk), full(wv), full(qg), full(kg), pos, pos],
        out_specs=[row(C_HEADS * C_HD), row(C_KV * C_HD), row(C_KV * C_HD)],
        out_shape=[jax.ShapeDtypeStruct((b, lp, C_HEADS * C_HD), BF16),
                   jax.ShapeDtypeStruct((b, lp, C_KV * C_HD), BF16),
                   jax.ShapeDtypeStruct((b, lp, C_KV * C_HD), BF16)],
        compiler_params=_cparams("parallel", "parallel"),
        name="c_proj",
    )(h, g, wq, wk, wv, qg, kg, cosf, sinf)


ATT_TK = 512
ATT_QB = 3


def _dense_kernel(q_ref, k_ref, v_ref, y_ref, *scratch):
    i = pl.program_id(2)
    grp = C_HEADS // C_KV
    lp = k_ref.shape[0]
    nkb = (lp - BLK) // ATT_TK
    nq = ATT_QB
    sa_s, sb_s, acc_s = scratch[:nq], scratch[nq:2 * nq], scratch[2 * nq:3 * nq]
    kt_s, k0t_s = scratch[3 * nq:]
    qs = [jnp.concatenate([q_ref[c * BLK:(c + 1) * BLK, g * C_HD:(g + 1) * C_HD] for g in range(grp)], axis=0)
          for c in range(nq)]
    m_rows = grp * BLK

    def keys(t):
        return pl.ds(pl.multiple_of(BLK + t * ATT_TK, BLK), ATT_TK)

    def v_ones(rows, n):
        return jnp.concatenate([v_ref[rows, :], jnp.ones((n, C_HD), BF16)], axis=1)

    @pl.when(i == 0)
    def _():
        k0t_s[...] = k_ref[0:BLK, :].astype(F32).T.astype(BF16)

        def transpose_block(t, carry):
            for jj in range(ATT_TK // BLK):
                rows = pl.ds(pl.multiple_of(BLK + t * ATT_TK + jj * BLK, BLK), BLK)
                kt_s[t, :, jj * BLK:(jj + 1) * BLK] = k_ref[rows, :].astype(F32).T.astype(BF16)
            return carry

        lax.fori_loop(0, nkb, transpose_block, 0)

    def scores(t, s_refs):
        kt = kt_s[t]
        for c in range(nq):
            s_refs[c][...] = _dot(qs[c], kt).astype(BF16)

    def step(t, ms, s_refs):
        vt = v_ones(keys(t), ATT_TK)
        out = []
        for c in range(nq):
            s = s_refs[c][...]
            m_new = jnp.maximum(ms[c], jnp.max(s, axis=-1, keepdims=True).astype(F32))
            p = jnp.exp2(s - m_new.astype(BF16))
            acc_s[c][...] = jnp.exp2(ms[c] - m_new) * acc_s[c][...] + _dot(p, vt)
            out.append(m_new)
        return out

    scores(0, sa_s)
    k0t = k0t_s[...]
    v0 = v_ones(slice(0, BLK), BLK)
    kc = lax.broadcasted_iota(jnp.int32, (m_rows, BLK), 1)
    ms = []
    for c in range(nq):
        s0 = jnp.where(kc >= FRONT, _dot(qs[c], k0t), NEG).astype(BF16)
        m = jnp.max(s0, axis=-1, keepdims=True)
        acc_s[c][...] = _dot(jnp.exp2(s0 - m), v0)
        ms.append(m.astype(F32))

    def body(j, ms):
        scores(2 * j + 1, sb_s)
        ms = step(2 * j, ms, sa_s)
        scores(2 * j + 2, sa_s)
        return step(2 * j + 1, ms, sb_s)

    ms = lax.fori_loop(0, nkb // 2 - 1, body, ms)
    scores(nkb - 1, sb_s)
    ms = step(nkb - 2, ms, sa_s)
    ms = step(nkb - 1, ms, sb_s)
    rr = lax.broadcasted_iota(jnp.int32, (BLK, C_HD), 0)
    for c in range(nq):
        acc = acc_s[c][...]
        o = acc[:, :C_HD] / acc[:, C_HD:C_HD + 1]
        for g in range(grp):
            og = o[g * BLK:(g + 1) * BLK, :]
            if c == 0:
                og = jnp.where((i == 0) & (rr < FRONT), 0.0, og)
            y_ref[c * BLK:(c + 1) * BLK, g * C_HD:(g + 1) * C_HD] = og.astype(y_ref.dtype)


def _dense(q, k, v):
    b, lp, _ = q.shape
    grp = C_HEADS // C_KV
    tq = ATT_QB * BLK
    assert lp % tq == 0 and ((lp - BLK) // ATT_TK) % 2 == 0 and (lp - BLK) % ATT_TK == 0
    score = pltpu.VMEM((grp * BLK, ATT_TK), BF16)
    return pl.pallas_call(
        _dense_kernel,
        grid=(b, C_KV, lp // tq),
        in_specs=[pl.BlockSpec((None, tq, grp * C_HD), lambda i, j, t: (i, t, j)),
                  pl.BlockSpec((None, lp, C_HD), lambda i, j, t: (i, 0, j)),
                  pl.BlockSpec((None, lp, C_HD), lambda i, j, t: (i, 0, j))],
        out_specs=pl.BlockSpec((None, tq, grp * C_HD), lambda i, j, t: (i, t, j)),
        out_shape=jax.ShapeDtypeStruct((b, lp, C_HEADS * C_HD), BF16),
        scratch_shapes=[score] * (2 * ATT_QB) + [pltpu.VMEM((grp * BLK, 2 * C_HD), F32)] * ATT_QB
                       + [pltpu.VMEM(((lp - BLK) // ATT_TK, C_HD, ATT_TK), BF16), pltpu.VMEM((C_HD, BLK), BF16)],
        compiler_params=_cparams("parallel", "parallel", "arbitrary"),
        name="dense_mixer",
    )(q, k, v)


def _rope_tables(lp, n_tok):
    rows = n_tok // GRID_W
    row = jnp.repeat(jnp.arange(rows), GRID_W)
    col = jnp.tile(jnp.arange(GRID_W), rows)
    meta = jnp.arange(N_META) - N_META
    front = jnp.zeros((lp - N_META - n_tok,), jnp.int32)
    row = jnp.concatenate([front, meta, row]).astype(F32)
    col = jnp.concatenate([front, meta, col]).astype(F32)
    axis_dim = C_HD // 2
    freqs = ROPE_THETA ** (-jnp.arange(0, axis_dim, 2, dtype=F32) / axis_dim)
    ang = jnp.concatenate([row[:, None] * freqs, col[:, None] * freqs], axis=-1)
    cos, sin = jnp.cos(ang), jnp.sin(ang)
    return jnp.concatenate([cos, cos], axis=-1), jnp.concatenate([-sin, sin], axis=-1)


def _gate_weight(w_b, w_a):
    d = w_b.shape[0]
    w_b = w_b.reshape(d, 2, A_HEADS)
    w_a = w_a.reshape(d, 2, A_HEADS)
    per_head = jnp.concatenate([w_b, w_a, w_a], axis=1)
    per_head = jnp.transpose(per_head, (0, 2, 1))
    per_head = jnp.pad(per_head, ((0, 0), (0, 0), (0, BLK - 6)))
    return per_head.reshape(d, A_HEADS * BLK)


def _gate_rows(p):
    t = jnp.transpose(p.astype(F32), (1, 0))
    rows = jnp.concatenate([jnp.zeros_like(t), t, t], axis=1)
    return jnp.pad(rows, ((0, 0), (0, BLK - 6)))[:, None, :]


def kernel(x, meta_tokens, attn_norm_g, mlp_norm_g, w_in_ab, conv_w_a, a_log, dt_bias, a_out_norm_g,
           b_q_norm_g, b_k_norm_g, b_sink, w_out_ab, w_qkv_c, c_q_norm_g, c_k_norm_g, w_out_c, w_ff1, w_ff2):
    bsz, n_tok, d = x.shape
    lp = FRONT + N_META + n_tok
    depth = attn_norm_g.shape[0]
    meta = jnp.broadcast_to(meta_tokens.astype(x.dtype)[None], (bsz, N_META, d))
    h = jnp.concatenate([jnp.zeros((bsz, FRONT, d), x.dtype), meta, x], axis=1).reshape(bsz * lp, d)
    cosf, sinf = _rope_tables(lp, n_tok)
    deint = jnp.concatenate([jnp.arange(0, C_HD, 2), jnp.arange(1, C_HD, 2)])
    row2 = lambda v: v.astype(F32).reshape(1, -1)

    for layer in range(depth):
        i = layer // 2
        g_attn = row2(attn_norm_g[layer])
        if layer % 2 == 0:
            w = w_in_ab[i]
            qkv_w = w[:, :1536].astype(BF16)
            z_w = w[:, 1536:2048].astype(BF16)
            gate_w = _gate_weight(w[:, 2048:2056], w[:, 2056:2064]).astype(BF16)
            bq_w = w[:, 2064:2576].astype(BF16)
            bkv_w = w[:, 2576:2832].astype(BF16)
            qkv, z, gate_pre, qb, kvb = _ab_proj(h, g_attn, qkv_w, z_w, gate_w, bq_w, bkv_w,
                                                 row2(b_q_norm_g[i]), row2(b_k_norm_g[i]))
            r3 = lambda t: t.reshape(bsz, lp, t.shape[-1])
            bg, gt = _gates(r3(gate_pre), _gate_rows(a_log[i]), _gate_rows(dt_bias[i]))
            ya = _delta(r3(qkv), r3(z), bg, gt, conv_w_a[i].astype(F32), row2(a_out_norm_g[i]))
            sink_rows = jnp.broadcast_to(b_sink[i].astype(F32)[:, None] * LOG2E, (B_HEADS, BLK))
            yb = _window(r3(qb), r3(kvb), sink_rows)
            wo = w_out_ab[i].astype(BF16)
            mixes = [ya.reshape(bsz * lp, -1), yb.reshape(bsz * lp, -1)]
        else:
            w = w_qkv_c[i]
            perm = lambda wc, nh: wc.reshape(d, nh, C_HD)[:, :, deint].reshape(d, nh * C_HD)
            wq = perm(w[:, :C_HEADS * C_HD], C_HEADS).astype(BF16)
            wk = perm(w[:, C_HEADS * C_HD:(C_HEADS + C_KV) * C_HD], C_KV).astype(BF16)
            wv = w[:, (C_HEADS + C_KV) * C_HD:].astype(BF16)
            q, k, v = _c_proj(h.reshape(bsz, lp, d), g_attn, wq, wk, wv,
                              row2(c_q_norm_g[i][deint]), row2(c_k_norm_g[i][deint]), cosf, sinf)
            att = _dense(q, k, v)
            mixes = [att.reshape(bsz * lp, -1)]
            wo = w_out_c[i].astype(BF16)
        h = _out_mlp(h, mixes, wo, row2(mlp_norm_g[layer]), w_ff1[layer].astype(BF16),
                     w_ff2[layer].astype(BF16))
    return h.reshape(bsz, lp, d)[:, FRONT + N_META:]
```

```python
import functools
import math

import jax
import jax.numpy as jnp
from jax import lax
from jax.experimental import pallas as pl
from jax.experimental.pallas import tpu as pltpu

F32 = jnp.float32
BF16 = jnp.bfloat16

EPS = 1e-6
N_META = 16
BLK = 128
FRONT = BLK - N_META
GRID_W = 64
ROPE_THETA = 10000.0
A_HEADS, A_DK, A_CONV = 4, 128, 5
B_HEADS, B_KV, B_HD, B_WIN = 8, 2, 64, 128
C_HEADS, C_KV, C_HD = 8, 2, 128
NEG = -1e30
LOG2E = math.log2(math.e)

VMEM_LIMIT = 56 * 1024 * 1024
ROW_TILE = 512
C_ROW_TILE = 256
FF_CHUNK = 512
INV_SQUARINGS = 6
PREP_UNROLL = 11


def _chunk_group(nblk):
    return max(g for g in range(1, PREP_UNROLL + 1) if nblk % g == 0)


def _cparams(*sem):
    return pltpu.CompilerParams(dimension_semantics=sem, vmem_limit_bytes=VMEM_LIMIT)


def _sigmoid(x):
    return 1.0 / (1.0 + jnp.exp(-x))


def _silu(x):
    return x * _sigmoid(x)


def _softplus(x):
    return jnp.maximum(x, 0.0) + jnp.log1p(jnp.exp(-jnp.abs(x)))


def _rms(x, g):
    return x * lax.rsqrt(jnp.mean(x * x, axis=-1, keepdims=True) + EPS) * g


def _dot(a, b):
    return jnp.dot(a, b, preferred_element_type=F32)


def _dot_nt(a, b):
    return lax.dot_general(a, b, (((1,), (1,)), ((), ())), preferred_element_type=F32)


def _pair_specs(tok, head, width):
    ntt = tok.shape[0] // ROW_TILE
    return [pl.BlockSpec((ROW_TILE, width), lambda i: (jnp.minimum(i, ntt - 1), 0)),
            pl.BlockSpec((ROW_TILE, width), lambda i: (jnp.maximum(i - ntt, 0), 0))]


def _pick(tok_ref, head_ref, ntt):
    return jnp.where(pl.program_id(0) < ntt, tok_ref[...], head_ref[...])


def _ab_proj_kernel(x_ref, hb_ref, g_ref, wqkv_ref, wz_ref, wg_ref, wq_ref, wkv_ref, bqg_ref, bkg_ref,
                    qkv_ref, z_ref, gate_ref, qb_ref, kvb_ref, *, ntt):
    u = _rms(_pick(x_ref, hb_ref, ntt), g_ref[...]).astype(BF16)
    qb = _dot(u, wq_ref[...])
    kv = _dot(u, wkv_ref[...])
    scale = B_HD ** -0.5 * LOG2E

    def q_heads(lo, hi):
        for hh in range(lo, hi):
            sl = slice(hh * B_HD, (hh + 1) * B_HD)
            qb_ref[:, sl] = (_rms(qb[:, sl], bqg_ref[...]) * scale).astype(BF16)

    qkv_ref[:, 0:512] = _dot(u, wqkv_ref[:, 0:512])
    q_heads(0, 3)
    qkv_ref[:, 512:1024] = _dot(u, wqkv_ref[:, 512:1024])
    q_heads(3, 6)
    qkv_ref[:, 1024:1536] = _dot(u, wqkv_ref[:, 1024:1536])
    q_heads(6, B_HEADS)
    z_ref[...] = _dot(u, wz_ref[...])
    for hh in range(B_KV):
        sl = slice(hh * B_HD, (hh + 1) * B_HD)
        kvb_ref[:, sl] = _rms(kv[:, sl], bkg_ref[...]).astype(BF16)
    kvb_ref[:, B_KV * B_HD:] = kv[:, B_KV * B_HD:].astype(BF16)
    gate_ref[...] = _dot(u, wg_ref[...])


def _ab_proj(x, hb, g, wqkv, wz, wg, wq, wkv, bqg, bkg):
    d = x.shape[1]
    r = x.shape[0] + hb.shape[0]
    row = lambda n: pl.BlockSpec((ROW_TILE, n), lambda i: (i, 0))
    full = lambda a: pl.BlockSpec(a.shape, lambda i: (0, 0))
    return pl.pallas_call(
        functools.partial(_ab_proj_kernel, ntt=x.shape[0] // ROW_TILE),
        grid=(r // ROW_TILE,),
        in_specs=_pair_specs(x, hb, d) + [full(g), full(wqkv), full(wz), full(wg), full(wq), full(wkv), full(bqg),
                                          full(bkg)],
        out_specs=[row(1536), row(512), row(512), row(512), row(256)],
        out_shape=[jax.ShapeDtypeStruct((r, 1536), F32), jax.ShapeDtypeStruct((r, 512), F32),
                   jax.ShapeDtypeStruct((r, 512), F32), jax.ShapeDtypeStruct((r, 512), BF16),
                   jax.ShapeDtypeStruct((r, 256), BF16)],
        compiler_params=_cparams("parallel"),
        name="ab_proj",
    )(x, hb, g, wqkv, wz, wg, wq, wkv, bqg, bkg)


def _split2(x):
    hi = x.astype(BF16)
    return hi, (x - hi.astype(F32)).astype(BF16)


def _split3(x):
    hi = x.astype(BF16)
    r1 = x - hi.astype(F32)
    mid = r1.astype(BF16)
    lo = (r1 - mid.astype(F32)).astype(BF16)
    return hi, mid, lo


def _seq_block(tok_ref, head_ref, n):
    tok = tok_ref[pl.ds(pl.multiple_of(jnp.maximum(n - 1, 0) * BLK, BLK), BLK), :]
    return jnp.where(n == 0, head_ref[...], tok)


def _seq_views(n_tok, seq, off):
    head0 = n_tok // BLK
    return [pl.BlockSpec((seq, BLK), lambda i, j: (i, j + off)),
            pl.BlockSpec((BLK, BLK), lambda i, j: (head0 + i, j + off))]


def _gates_kernel(tok_ref, head_ref, alog_ref, dtb_ref, bg_ref, gt_ref):
    nblk = tok_ref.shape[0] // BLK + 1
    ri = lax.broadcasted_iota(jnp.int32, (BLK, BLK), 0)
    ci = lax.broadcasted_iota(jnp.int32, (BLK, BLK), 1)
    lower = (ri >= ci).astype(BF16)
    upper = (ri <= ci).astype(BF16)
    neg_a = -jnp.exp(alog_ref[...])
    dtb = dtb_ref[...]

    def one(n):
        x = _seq_block(tok_ref, head_ref, n)
        live = (ri + n * BLK) >= FRONT
        beta = jnp.where(live, _sigmoid(x), 0.0)
        g = jnp.where(live, neg_a * _softplus(x + dtb), 0.0)
        parts = _split3(g)
        pre = sum(_dot(lower, p) for p in parts)
        suf = sum(_dot(upper, p) for p in parts)
        tot = pre + suf - g
        return jnp.where(ci < 2, beta, jnp.where(ci == 2, pre, jnp.where(ci == 3, suf, tot)))

    grp = _chunk_group(nblk)

    def body(g, carry):
        outs = [one(g * grp + j) for j in range(grp)]
        for j, out in enumerate(outs):
            n = g * grp + j
            bg_ref[pl.ds(pl.multiple_of(n * BLK, BLK), BLK), :] = out
            gt_ref[n] = out.T[0:8, :]
        return carry

    lax.fori_loop(0, nblk // grp, body, 0)


def _gates(pre, b, n_tok, alog_rows, dtb_rows):
    seq = n_tok // b
    lp = seq + BLK
    nblk = lp // BLK
    return pl.pallas_call(
        _gates_kernel,
        grid=(b, A_HEADS),
        in_specs=_seq_views(n_tok, seq, 0)
                 + [pl.BlockSpec((None, 1, BLK), lambda i, j: (j, 0, 0)),
                    pl.BlockSpec((None, 1, BLK), lambda i, j: (j, 0, 0))],
        out_specs=[pl.BlockSpec((None, lp, BLK), lambda i, j: (i, 0, j)),
                   pl.BlockSpec((None, None, nblk, 8, BLK), lambda i, j: (i, j, 0, 0, 0))],
        out_shape=[jax.ShapeDtypeStruct((b, lp, A_HEADS * BLK), F32),
                   jax.ShapeDtypeStruct((b, A_HEADS, nblk, 8, BLK), F32)],
        compiler_params=_cparams("parallel", "parallel"),
        name="delta_gates",
    )(pre, pre, alog_rows, dtb_rows)


def _delta_kernel(q_ref, qh_ref, k_ref, kh_ref, v_ref, vh_ref, z_ref, zh_ref, bg_ref, gt_ref,
                  cwq_ref, cwk_ref, cwv_ref, og_ref, y_ref, yh_ref, sadd_s, smul_s, o_s, omul_s, gl_s):
    seq = q_ref.shape[0]
    nblk = seq // BLK + 1
    grp = _chunk_group(nblk)
    ri = lax.broadcasted_iota(jnp.int32, (BLK, BLK), 0)
    ci = lax.broadcasted_iota(jnp.int32, (BLK, BLK), 1)
    eye = (ri == ci).astype(F32)
    incl = (ri >= ci, ri <= ci)
    strict = (ri > ci, ri < ci)

    def conv_silu(ref, head_ref, w_ref, n):
        cur = _seq_block(ref, head_ref, n)
        tok_prev = ref[pl.ds(pl.multiple_of(jnp.maximum((n - 1) * BLK - 8, 0), 8), 8), :]
        prev = jnp.where(n == 0, 0.0, jnp.where(n == 1, head_ref[BLK - 8:, :], tok_prev))
        nxt = ref[pl.ds(pl.multiple_of(jnp.minimum(n * BLK, seq - 8), 8), 8), :]
        nxt = jnp.where(n < nblk - 1, nxt, 0.0)
        win = jnp.concatenate([prev, cur, nxt], axis=0)
        h = A_CONV // 2
        acc = win[8 - h:8 - h + BLK, :] * w_ref[0:1, :]
        for j in range(1, A_CONV):
            acc = acc + win[8 - h + j:8 - h + j + BLK, :] * w_ref[j:j + 1, :]
        return _silu(acc)

    def l2n(x):
        return x * lax.rsqrt(jnp.sum(x * x, axis=-1, keepdims=True) + EPS)

    def chunk_inputs(n):
        rows = pl.ds(pl.multiple_of(n * BLK, BLK), BLK)
        live = (ri[:, 0:1] + n * BLK) >= FRONT
        qn = jnp.where(live, l2n(conv_silu(q_ref, qh_ref, cwq_ref, n)) * (A_DK ** -0.5), 0.0)
        kn = jnp.where(live, l2n(conv_silu(k_ref, kh_ref, cwk_ref, n)), 0.0)
        vv = jnp.where(live, conv_silu(v_ref, vh_ref, cwv_ref, n), 0.0)
        kn16 = kn.astype(BF16)
        kq = _dot_nt(jnp.concatenate([kn16, qn.astype(BF16)], axis=0), kn16)
        return dict(n=n, rows=rows, qn=qn, kn=kn, vv=vv, kk=kq[:BLK], qk=kq[BLK:], bg=bg_ref[rows, :], gt=gt_ref[n])

    def chain_setup(c, d):
        bg, gt = c["bg"], c["gt"]
        beta, ccol, tot = bg[:, d:d + 1], bg[:, 2 + d:3 + d], bg[:, 4 + d:5 + d]
        crow = gt[2 + d:3 + d, :]
        dec = jnp.exp(jnp.where(incl[d], ccol - crow, NEG))
        a = jnp.where(strict[d], beta * c["kk"] * dec, 0.0)
        return dict(c=c, d=d, beta=beta, ccol=ccol, tot=tot, dec=dec, a=a, t=eye - a, x=a.astype(BF16))

    def prep(g, carry):
        chunks = [chunk_inputs(g * grp + j) for j in range(grp)]
        chains = [chain_setup(c, d) for c in chunks for d in range(2)]
        zero = jnp.zeros((BLK, BLK), BF16)

        def blockdiag(xp):
            return jnp.concatenate([jnp.concatenate([xp[:, :BLK], zero], axis=1),
                                    jnp.concatenate([zero, xp[:, BLK:]], axis=1)], axis=0)

        pairs = [(chains[2 * j], chains[2 * j + 1]) for j in range(grp)]
        xps = [jnp.concatenate([f["x"], b["x"]], axis=1) for f, b in pairs]
        tps = [jnp.concatenate([f["t"], b["t"]], axis=1) for f, b in pairs]
        for _ in range(INV_SQUARINGS):
            xps = [_dot(xp, blockdiag(xp)).astype(BF16) for xp in xps]
            txs = [_dot(tp.astype(BF16), blockdiag(xp)) for tp, xp in zip(tps, xps)]
            tps = [tp + tx for tp, tx in zip(tps, txs)]
        for (f, b), tp in zip(pairs, tps):
            f["t"], b["t"] = tp[:, :BLK], tp[:, BLK:]
        for ch in chains:
            c = ch["c"]
            ch["ec"] = jnp.exp(ch["ccol"])
            ch["rhs"] = jnp.concatenate([ch["beta"] * c["vv"], ch["beta"] * c["kn"] * ch["ec"]], axis=1)
            ch["t16"] = ch["t"].astype(BF16)
        x0s = [_dot(ch["t16"], ch["rhs"].astype(BF16)) for ch in chains]
        res = []
        for ch, x0 in zip(chains, x0s):
            ah, al = _split2(ch["a"])
            xh, xl = _split2(x0)
            ax = _dot(jnp.concatenate([ah, al], axis=1), jnp.concatenate([xh, xh], axis=0)) + _dot(ah, xl)
            res.append((ch["rhs"] - x0 - ax).astype(BF16))
        uws = [(x0 + _dot(ch["t16"], e)).astype(BF16) for ch, x0, e in zip(chains, x0s, res)]
        kuws = [_dot((ch["c"]["kn"] * jnp.exp(ch["tot"] - ch["ccol"])).T.astype(BF16), uw)
                for ch, uw in zip(chains, uws)]
        quws = [_dot((ch["c"]["qk"] * ch["dec"]).astype(BF16), uw) for ch, uw in zip(chains, uws)]
        for ch, kuw, quw in zip(chains, kuws, quws):
            c, d = ch["c"], ch["d"]
            rows = c["rows"]
            sadd_s[d, rows, :] = kuw[:, :BLK]
            smul_s[d, rows, :] = (-kuw[:, BLK:]).astype(BF16)
            o_s[d, rows, :] = quw[:, :BLK]
            omul_s[d, rows, :] = (c["qn"] * ch["ec"] - quw[:, BLK:]).astype(BF16)
            gl_s[d * nblk + c["n"]] = jnp.broadcast_to(jnp.exp(ch["tot"]), (BLK, BLK))[0:8, :]
        return carry

    lax.fori_loop(0, nblk // grp, prep, 0)

    def scan_step(d, n, s):
        rows = pl.ds(pl.multiple_of(n * BLK, BLK), BLK)
        both = _dot(jnp.concatenate([smul_s[d, rows, :], omul_s[d, rows, :]], axis=0), s.astype(BF16))
        o_s[d, rows, :] = o_s[d, rows, :] + both[BLK:]
        return s * gl_s[d * nblk + n][0:1, :] + both[:BLK] + sadd_s[d, rows, :]

    def scan(i, carry):
        sf, sb = carry
        sf = scan_step(0, i, sf)
        sb = scan_step(1, nblk - 1 - i, sb)
        return sf, sb

    s0 = jnp.zeros((BLK, BLK), F32)
    lax.fori_loop(0, nblk, scan, (s0, s0))

    def gated(rows, z):
        o = o_s[0, rows, :] + o_s[1, rows, :]
        return (_rms(o, og_ref[...]) * _silu(z)).astype(y_ref.dtype)

    yh_ref[...] = gated(slice(0, BLK), zh_ref[...])

    def finish(n, carry):
        tok_rows = pl.ds(pl.multiple_of((n - 1) * BLK, BLK), BLK)
        y_ref[tok_rows, :] = gated(pl.ds(pl.multiple_of(n * BLK, BLK), BLK), z_ref[tok_rows, :])
        return carry

    lax.fori_loop(1, nblk, finish, 0, unroll=4 if (nblk - 1) % 4 == 0 else 1)


def _delta(qkv, z, b, n_tok, bg, gt, conv_w, o_gain):
    seq = n_tok // b
    lp = seq + BLK
    nblk = lp // BLK
    col = lambda off: pl.BlockSpec((None, lp, BLK), lambda i, j: (i, 0, j + off))
    cw = lambda off: pl.BlockSpec((A_CONV, BLK), lambda i, j: (0, j + off))
    views = lambda off: _seq_views(n_tok, seq, off)
    return pl.pallas_call(
        _delta_kernel,
        grid=(b, A_HEADS),
        in_specs=views(0) + views(A_HEADS) + views(2 * A_HEADS) + views(0)
                 + [col(0), pl.BlockSpec((None, None, nblk, 8, BLK), lambda i, j: (i, j, 0, 0, 0)),
                    cw(0), cw(A_HEADS), cw(2 * A_HEADS),
                    pl.BlockSpec((1, BLK), lambda i, j: (0, 0))],
        out_specs=[pl.BlockSpec((seq, BLK), lambda i, j: (i, j)), pl.BlockSpec((BLK, BLK), lambda i, j: (i, j))],
        out_shape=[jax.ShapeDtypeStruct((n_tok, A_HEADS * BLK), BF16),
                   jax.ShapeDtypeStruct((b * BLK, A_HEADS * BLK), BF16)],
        scratch_shapes=[pltpu.VMEM((2, lp, BLK), F32), pltpu.VMEM((2, lp, BLK), BF16),
                        pltpu.VMEM((2, lp, BLK), F32), pltpu.VMEM((2, lp, BLK), BF16),
                        pltpu.VMEM((2 * nblk, 8, BLK), F32)],
        compiler_params=_cparams("parallel", "parallel"),
        name="delta_mixer",
    )(qkv, qkv, qkv, qkv, qkv, qkv, z, z, bg, gt, conv_w, conv_w, conv_w, o_gain)


def _window_kernel(q_ref, kp_ref, kc_ref, kn_ref, km_ref, bias_ref, sink_ref, y_ref):
    i = pl.program_id(1)
    nblk = pl.num_programs(1)
    grp = B_HEADS // B_KV
    nk = 4 * BLK
    c = lax.broadcasted_iota(jnp.int32, (1, nk), 1)
    kblk = i - 1 + (c >> 7)
    edge = jnp.where((c >= 3 * BLK) | ((kblk >= 1) & (kblk < nblk)), 0.0, NEG)
    q = q_ref[...]
    kvs = (kp_ref[...], kc_ref[...], kn_ref[...], km_ref[...])
    ones = jnp.ones((nk, 2 * B_HD), BF16)
    lane = lax.broadcasted_iota(jnp.int32, (BLK, 2 * B_HD), 1)
    s4s, vexts = [], []
    for kvh in range(B_KV):
        ks = jnp.concatenate([t[:, kvh * B_HD:(kvh + 1) * B_HD] for t in kvs], axis=0)
        vs = jnp.concatenate([t[:, (B_KV + kvh) * B_HD:(B_KV + kvh + 1) * B_HD] for t in kvs], axis=0)
        q4 = jnp.concatenate([q[:, hh * B_HD:(hh + 1) * B_HD] for hh in range(kvh * grp, (kvh + 1) * grp)],
                             axis=0)
        s4s.append(_dot_nt(q4, ks))
        vexts.append(jnp.concatenate([vs, vs, ones], axis=1))
    ms = []
    pvs = []
    for kvh in range(B_KV):
        ps = []
        for gi in range(grp):
            hh = kvh * grp + gi
            s = s4s[kvh][gi * BLK:(gi + 1) * BLK] + bias_ref[hh] + edge
            m = jnp.maximum(jnp.max(s, axis=-1, keepdims=True), sink_ref[hh:hh + 1, 0:1])
            ps.append(jnp.exp2(s - m).astype(BF16))
            ms.append(m)
        pvs.append(_dot(jnp.concatenate(ps, axis=0), vexts[kvh]))
    outs = []
    for hh in range(B_HEADS):
        kvh, gi = divmod(hh, grp)
        o = pvs[kvh][gi * BLK:(gi + 1) * BLK]
        den = o[:, 2 * B_HD:] + jnp.exp2(sink_ref[hh:hh + 1, 0:1] - ms[hh])
        outs.append(o[:, :2 * B_HD] / den)
    for j in range(B_HEADS // 2):
        pair = jnp.where(lane < B_HD, outs[2 * j], outs[2 * j + 1])
        y_ref[:, 2 * j * B_HD:(2 * j + 2) * B_HD] = pair.astype(y_ref.dtype)

    @pl.when(i == 0)
    def _():
        rr = lax.broadcasted_iota(jnp.int32, y_ref.shape, 0)
        y_ref[...] = jnp.where(rr >= FRONT, y_ref[...], 0).astype(y_ref.dtype)


def _window_bias():
    r = jnp.arange(BLK)[:, None]
    c = jnp.arange(4 * BLK)[None, :]
    dist = jnp.abs(BLK + r - c)
    slopes = jnp.exp2(-8.0 * (jnp.arange(B_HEADS, dtype=F32) + 1.0) / B_HEADS)
    band = (c < 3 * BLK) & (dist <= B_WIN)
    alibi = -slopes[:, None, None] * dist.astype(F32)[None] * LOG2E
    rest = jnp.where(c >= 3 * BLK + FRONT, 0.0, NEG)
    return jnp.where(band[None], alibi, rest[None]).astype(F32)


def _window(qb, kvb, b, n_tok, sink_rows):
    nblk = n_tok // b // BLK + 1
    bias = _window_bias()
    head0 = n_tok // BLK

    def blk(i, j):
        return jnp.where(j == 0, head0 + i, i * (nblk - 1) + j - 1)

    kv = lambda f: pl.BlockSpec((BLK, 2 * B_KV * B_HD), f)
    return pl.pallas_call(
        _window_kernel,
        grid=(b, nblk),
        in_specs=[pl.BlockSpec((BLK, B_HEADS * B_HD), lambda i, j: (blk(i, j), 0)),
                  kv(lambda i, j: (blk(i, jnp.maximum(j - 1, 0)), 0)),
                  kv(lambda i, j: (blk(i, j), 0)),
                  kv(lambda i, j: (blk(i, jnp.minimum(j + 1, nblk - 1)), 0)),
                  kv(lambda i, j: (head0 + i, 0)),
                  pl.BlockSpec(bias.shape, lambda i, j: (0, 0, 0)),
                  pl.BlockSpec((B_HEADS, BLK), lambda i, j: (0, 0))],
        out_specs=pl.BlockSpec((BLK, B_HEADS * B_HD), lambda i, j: (blk(i, j), 0)),
        out_shape=jax.ShapeDtypeStruct((qb.shape[0], B_HEADS * B_HD), BF16),
        compiler_params=_cparams("parallel", "parallel"),
        name="window_mixer",
    )(qb, kvb, kvb, kvb, kvb, bias, sink_rows)


def _out_mlp_kernel(*refs, arity, ntt):
    vals, pos = [], 0
    for a in arity:
        vals.append(refs[pos][...] if a == 1 else _pick(refs[pos], refs[pos + 1], ntt))
        pos += a
    wo_ref, g_ref, w1_ref, w2_ref, o_ref = refs[pos:]
    mix = jnp.concatenate(vals[1:], axis=1)
    h = vals[0] + _dot(mix, wo_ref[...])
    u = _rms(h, g_ref[...]).astype(BF16)
    dff = w1_ref.shape[1]
    acc = h
    for c in range(dff // FF_CHUNK):
        sl = slice(c * FF_CHUNK, (c + 1) * FF_CHUNK)
        a = jnp.maximum(_dot(u, w1_ref[:, sl]), 0.0)
        acc = acc + _dot((a * a).astype(BF16), w2_ref[sl, :])
    o_ref[...] = acc


def _out_mlp(rows_out, ntt, tensors, wo, g, w1, w2):
    d = wo.shape[1]
    row = lambda n: pl.BlockSpec((ROW_TILE, n), lambda i: (i, 0))
    full = lambda a: pl.BlockSpec(a.shape, lambda i: (0, 0))
    specs, args, arity = [], [], []
    for t in tensors:
        if isinstance(t, tuple):
            specs += _pair_specs(t[0], t[1], t[0].shape[1])
            args += list(t)
            arity.append(2)
        else:
            specs.append(row(t.shape[1]))
            args.append(t)
            arity.append(1)
    return pl.pallas_call(
        functools.partial(_out_mlp_kernel, arity=tuple(arity), ntt=ntt),
        grid=(rows_out // ROW_TILE,),
        in_specs=specs + [full(wo), full(g), full(w1), full(w2)],
        out_specs=row(d),
        out_shape=jax.ShapeDtypeStruct((rows_out, d), F32),
        compiler_params=_cparams("parallel"),
        name="out_mlp",
    )(*args, wo, g, w1, w2)


def _c_proj_kernel(h_ref, g_ref, wq_ref, wk_ref, wv_ref, qg_ref, kg_ref, cos_ref, sin_ref,
                   q_ref, k_ref, v_ref):
    u = _rms(h_ref[...], g_ref[...]).astype(BF16)
    cosf = cos_ref[...]
    sinf = sin_ref[...]
    half = C_HD // 2

    def norm_rope(x, gain):
        x = _rms(x, gain)
        swapped = jnp.concatenate([x[:, half:], x[:, :half]], axis=1)
        return x * cosf + swapped * sinf

    k = _dot(u, wk_ref[...])
    half_w = C_HEADS * C_HD // 2
    q_lo = _dot(u, wq_ref[:, :half_w])
    for hh in range(C_KV):
        sl = slice(hh * C_HD, (hh + 1) * C_HD)
        k_ref[:, sl] = norm_rope(k[:, sl], kg_ref[...]).astype(BF16)
    q_hi = _dot(u, wq_ref[:, half_w:])
    for hh in range(C_HEADS // 2):
        sl = slice(hh * C_HD, (hh + 1) * C_HD)
        q_ref[:, sl] = (norm_rope(q_lo[:, sl], qg_ref[...]) * (C_HD ** -0.5 * LOG2E)).astype(BF16)
    v_ref[...] = _dot(u, wv_ref[...]).astype(BF16)
    for hh in range(C_HEADS // 2):
        sl = slice(hh * C_HD, (hh + 1) * C_HD)
        q_ref[:, half_w + hh * C_HD:half_w + (hh + 1) * C_HD] = (
            norm_rope(q_hi[:, sl], qg_ref[...]) * (C_HD ** -0.5 * LOG2E)).astype(BF16)


def _c_proj(h, ntt, seq, g, wq, wk, wv, qg, kg, cosf, sinf):
    r, d = h.shape
    tm = C_ROW_TILE
    per_seq = seq // tm
    n_tok_tiles = ntt * (ROW_TILE // tm)
    row = lambda n: pl.BlockSpec((tm, n), lambda i: (i, 0))
    full = lambda a: pl.BlockSpec(a.shape, lambda i: (0, 0))
    pos = pl.BlockSpec((tm, C_HD), lambda i: (jnp.where(i < n_tok_tiles, i % per_seq, per_seq), 0))
    return pl.pallas_call(
        _c_proj_kernel,
        grid=(r // tm,),
        in_specs=[row(d), full(g), full(wq), full(wk), full(wv), full(qg), full(kg), pos, pos],
        out_specs=[row(C_HEADS * C_HD), row(C_KV * C_HD), row(C_KV * C_HD)],
        out_shape=[jax.ShapeDtypeStruct((r, C_HEADS * C_HD), BF16),
                   jax.ShapeDtypeStruct((r, C_KV * C_HD), BF16),
                   jax.ShapeDtypeStruct((r, C_KV * C_HD), BF16)],
        compiler_params=_cparams("parallel"),
        name="c_proj",
    )(h, g, wq, wk, wv, qg, kg, cosf, sinf)


ATT_TK = 512
ATT_QB = 4


def _dense_kernel(q_ref, k_ref, kh_ref, v_ref, vh_ref, y_ref, *scratch):
    grp = C_HEADS // C_KV
    nkb = k_ref.shape[0] // ATT_TK
    nq = ATT_QB
    sa_s, sb_s, acc_s = scratch[:nq], scratch[nq:2 * nq], scratch[2 * nq:]
    qs = [jnp.concatenate([q_ref[c * BLK:(c + 1) * BLK, g * C_HD:(g + 1) * C_HD] for g in range(grp)], axis=0)
          for c in range(nq)]
    m_rows = grp * BLK

    def keys(t):
        return pl.ds(pl.multiple_of(t * ATT_TK, ATT_TK), ATT_TK)

    def v_ones(v):
        return jnp.concatenate([v, jnp.ones(v.shape, BF16)], axis=1)

    def scores(t, s_refs):
        kt = k_ref[keys(t), :]
        for c in range(nq):
            s_refs[c][...] = _dot_nt(qs[c], kt)

    def step(t, ms, s_refs):
        vt = v_ones(v_ref[keys(t), :])
        out = []
        for c in range(nq):
            s = s_refs[c][...]
            m_new = jnp.maximum(ms[c], jnp.max(s, axis=-1, keepdims=True))
            p = jnp.exp2(s - m_new).astype(BF16)
            acc_s[c][...] = jnp.exp2(ms[c] - m_new) * acc_s[c][...] + _dot(p, vt)
            out.append(m_new)
        return out

    scores(0, sa_s)
    k0 = kh_ref[...]
    v0 = v_ones(vh_ref[...])
    kc = lax.broadcasted_iota(jnp.int32, (m_rows, BLK), 1)
    ms = []
    for c in range(nq):
        s0 = jnp.where(kc >= FRONT, _dot_nt(qs[c], k0), NEG)
        m = jnp.max(s0, axis=-1, keepdims=True)
        acc_s[c][...] = _dot(jnp.exp2(s0 - m).astype(BF16), v0)
        ms.append(m)

    def body(j, ms):
        scores(2 * j + 1, sb_s)
        ms = step(2 * j, ms, sa_s)
        scores(2 * j + 2, sa_s)
        return step(2 * j + 1, ms, sb_s)

    ms = lax.fori_loop(0, nkb // 2 - 1, body, ms)
    scores(nkb - 1, sb_s)
    ms = step(nkb - 2, ms, sa_s)
    ms = step(nkb - 1, ms, sb_s)
    for c in range(nq):
        acc = acc_s[c][...]
        o = acc[:, :C_HD] / acc[:, C_HD:C_HD + 1]
        for g in range(grp):
            y_ref[c * BLK:(c + 1) * BLK, g * C_HD:(g + 1) * C_HD] = o[g * BLK:(g + 1) * BLK, :].astype(y_ref.dtype)


def _dense(q, k, v, b, n_tok):
    seq = n_tok // b
    grp = C_HEADS // C_KV
    tq = ATT_QB * BLK
    assert seq % tq == 0 and seq % (2 * ATT_TK) == 0
    per_seq = seq // tq
    head0 = n_tok // BLK
    score = pltpu.VMEM((grp * BLK, ATT_TK), F32)
    tok = pl.BlockSpec((seq, C_HD), lambda i, j, t: (i, j))
    head = pl.BlockSpec((BLK, C_HD), lambda i, j, t: (head0 + i, j))
    return pl.pallas_call(
        _dense_kernel,
        grid=(b, C_KV, per_seq),
        in_specs=[pl.BlockSpec((tq, grp * C_HD), lambda i, j, t: (i * per_seq + t, j)), tok, head, tok, head],
        out_specs=pl.BlockSpec((tq, grp * C_HD), lambda i, j, t: (i * per_seq + t, j)),
        out_shape=jax.ShapeDtypeStruct((n_tok, C_HEADS * C_HD), BF16),
        scratch_shapes=[score] * (2 * ATT_QB) + [pltpu.VMEM((grp * BLK, 2 * C_HD), F32)] * ATT_QB,
        compiler_params=_cparams("parallel", "parallel", "arbitrary"),
        name="dense_mixer",
    )(q, k, k, v, v)


def _rope_tables(seq):
    rows = seq // GRID_W
    row = jnp.repeat(jnp.arange(rows), GRID_W)
    col = jnp.tile(jnp.arange(GRID_W), rows)
    head = jnp.tile(jnp.concatenate([jnp.zeros((FRONT,), jnp.int32), jnp.arange(N_META) - N_META]), ROW_TILE // BLK)
    row = jnp.concatenate([row, head]).astype(F32)
    col = jnp.concatenate([col, head]).astype(F32)
    axis_dim = C_HD // 2
    freqs = ROPE_THETA ** (-jnp.arange(0, axis_dim, 2, dtype=F32) / axis_dim)
    ang = jnp.concatenate([row[:, None] * freqs, col[:, None] * freqs], axis=-1)
    cos, sin = jnp.cos(ang), jnp.sin(ang)
    return jnp.concatenate([cos, cos], axis=-1), jnp.concatenate([-sin, sin], axis=-1)


def _gate_weight(w_b, w_a):
    d = w_b.shape[0]
    w_b = w_b.reshape(d, 2, A_HEADS)
    w_a = w_a.reshape(d, 2, A_HEADS)
    per_head = jnp.concatenate([w_b, w_a, w_a], axis=1)
    per_head = jnp.transpose(per_head, (0, 2, 1))
    per_head = jnp.pad(per_head, ((0, 0), (0, 0), (0, BLK - 6)))
    return per_head.reshape(d, A_HEADS * BLK)


def _gate_rows(p):
    t = jnp.transpose(p.astype(F32), (1, 0))
    rows = jnp.concatenate([jnp.zeros_like(t), t, t], axis=1)
    return jnp.pad(rows, ((0, 0), (0, BLK - 6)))[:, None, :]


def kernel(x, meta_tokens, attn_norm_g, mlp_norm_g, w_in_ab, conv_w_a, a_log, dt_bias, a_out_norm_g,
           b_q_norm_g, b_k_norm_g, b_sink, w_out_ab, w_qkv_c, c_q_norm_g, c_k_norm_g, w_out_c, w_ff1, w_ff2):
    bsz, seq, d = x.shape
    n_tok = bsz * seq
    n_rows = n_tok + bsz * BLK
    ntt = n_tok // ROW_TILE
    assert attn_norm_g.shape[0] == 2 and seq % ROW_TILE == 0 and (bsz * BLK) % ROW_TILE == 0
    x2 = x.reshape(n_tok, d)
    meta = jnp.broadcast_to(meta_tokens.astype(x.dtype)[None], (bsz, N_META, d))
    head = jnp.concatenate([jnp.zeros((bsz, FRONT, d), x.dtype), meta], axis=1).reshape(bsz * BLK, d)
    row2 = lambda v: v.astype(F32).reshape(1, -1)

    w = w_in_ab[0]
    qkv_w = w[:, :1536].astype(BF16)
    z_w = w[:, 1536:2048].astype(BF16)
    gate_w = _gate_weight(w[:, 2048:2056], w[:, 2056:2064]).astype(BF16)
    bq_w = w[:, 2064:2576].astype(BF16)
    bkv_w = w[:, 2576:2832].astype(BF16)
    qkv, z, gate_pre, qb, kvb = _ab_proj(x2, head, row2(attn_norm_g[0]), qkv_w, z_w, gate_w, bq_w, bkv_w,
                                         row2(b_q_norm_g[0]), row2(b_k_norm_g[0]))
    bg, gt = _gates(gate_pre, bsz, n_tok, _gate_rows(a_log[0]), _gate_rows(dt_bias[0]))
    ya = _delta(qkv, z, bsz, n_tok, bg, gt, conv_w_a[0].astype(F32), row2(a_out_norm_g[0]))
    sink_rows = jnp.broadcast_to(b_sink[0].astype(F32)[:, None] * LOG2E, (B_HEADS, BLK))
    yb = _window(qb, kvb, bsz, n_tok, sink_rows)
    h = _out_mlp(n_rows, ntt, [(x2, head), tuple(ya), yb], w_out_ab[0].astype(BF16), row2(mlp_norm_g[0]),
                 w_ff1[0].astype(BF16), w_ff2[0].astype(BF16))

    w = w_qkv_c[0]
    deint = jnp.concatenate([jnp.arange(0, C_HD, 2), jnp.arange(1, C_HD, 2)])
    perm = lambda wc, nh: wc.reshape(d, nh, C_HD)[:, :, deint].reshape(d, nh * C_HD)
    wq = perm(w[:, :C_HEADS * C_HD], C_HEADS).astype(BF16)
    wk = perm(w[:, C_HEADS * C_HD:(C_HEADS + C_KV) * C_HD], C_KV).astype(BF16)
    wv = w[:, (C_HEADS + C_KV) * C_HD:].astype(BF16)
    cosf, sinf = _rope_tables(seq)
    q, k, v = _c_proj(h, ntt, seq, row2(attn_norm_g[1]), wq, wk, wv,
                      row2(c_q_norm_g[0][deint]), row2(c_k_norm_g[0][deint]), cosf, sinf)
    att = _dense(q, k, v, bsz, n_tok)
    out = _out_mlp(n_tok, ntt, [h, att], w_out_c[0].astype(BF16), row2(mlp_norm_g[1]),
                   w_ff1[1].astype(BF16), w_ff2[1].astype(BF16))
    return out.reshape(bsz, seq, d)
```

```python
import functools
import math

import jax
import jax.numpy as jnp
from jax import lax
from jax.experimental import pallas as pl
from jax.experimental.pallas import tpu as pltpu

F32 = jnp.float32
BF16 = jnp.bfloat16

EPS = 1e-6
N_META = 16
BLK = 128
FRONT = BLK - N_META
GRID_W = 64
ROPE_THETA = 10000.0
A_HEADS, A_DK, A_CONV = 4, 128, 5
B_HEADS, B_KV, B_HD, B_WIN = 8, 2, 64, 128
C_HEADS, C_KV, C_HD = 8, 2, 128
NEG = -1e30
LOG2E = math.log2(math.e)

VMEM_LIMIT = 56 * 1024 * 1024
ROW_TILE = 512
C_ROW_TILE = 256
FF_CHUNK = 512
INV_SQUARINGS = 6
PREP_UNROLL = 11


def _chunk_group(nblk):
    return max(g for g in range(1, PREP_UNROLL + 1) if nblk % g == 0)


def _cparams(*sem):
    return pltpu.CompilerParams(dimension_semantics=sem, vmem_limit_bytes=VMEM_LIMIT)


def _sigmoid(x):
    return 1.0 / (1.0 + jnp.exp(-x))


def _silu(x):
    return x * _sigmoid(x)


def _softplus(x):
    return jnp.maximum(x, 0.0) + jnp.log1p(jnp.exp(-jnp.abs(x)))


def _rms(x, g):
    return x * lax.rsqrt(jnp.mean(x * x, axis=-1, keepdims=True) + EPS) * g


def _dot(a, b):
    return jnp.dot(a, b, preferred_element_type=F32)


def _dot_nt(a, b):
    return lax.dot_general(a, b, (((1,), (1,)), ((), ())), preferred_element_type=F32)


def _pair_specs(tok, head, width):
    ntt = tok.shape[0] // ROW_TILE
    return [pl.BlockSpec((ROW_TILE, width), lambda i: (jnp.minimum(i, ntt - 1), 0)),
            pl.BlockSpec((ROW_TILE, width), lambda i: (jnp.maximum(i - ntt, 0), 0))]


def _pick(tok_ref, head_ref, ntt):
    return jnp.where(pl.program_id(0) < ntt, tok_ref[...], head_ref[...])


def _ab_proj_kernel(x_ref, hb_ref, g_ref, wqkv_ref, wz_ref, wg_ref, wq_ref, wkv_ref, bqg_ref, bkg_ref,
                    qkv_ref, z_ref, gate_ref, qb_ref, kvb_ref, *, ntt):
    u = _rms(_pick(x_ref, hb_ref, ntt), g_ref[...]).astype(BF16)
    qb = _dot(u, wq_ref[...])
    kv = _dot(u, wkv_ref[...])
    scale = B_HD ** -0.5 * LOG2E

    def q_heads(lo, hi):
        for hh in range(lo, hi):
            sl = slice(hh * B_HD, (hh + 1) * B_HD)
            qb_ref[:, sl] = (_rms(qb[:, sl], bqg_ref[...]) * scale).astype(BF16)

    qkv_ref[:, 0:512] = _dot(u, wqkv_ref[:, 0:512])
    q_heads(0, 3)
    qkv_ref[:, 512:1024] = _dot(u, wqkv_ref[:, 512:1024])
    q_heads(3, 6)
    qkv_ref[:, 1024:1536] = _dot(u, wqkv_ref[:, 1024:1536])
    q_heads(6, B_HEADS)
    z_ref[...] = _dot(u, wz_ref[...])
    for hh in range(B_KV):
        sl = slice(hh * B_HD, (hh + 1) * B_HD)
        kvb_ref[:, sl] = _rms(kv[:, sl], bkg_ref[...]).astype(BF16)
    kvb_ref[:, B_KV * B_HD:] = kv[:, B_KV * B_HD:].astype(BF16)
    gate_ref[...] = _dot(u, wg_ref[...])


def _ab_proj(x, hb, g, wqkv, wz, wg, wq, wkv, bqg, bkg):
    d = x.shape[1]
    r = x.shape[0] + hb.shape[0]
    row = lambda n: pl.BlockSpec((ROW_TILE, n), lambda i: (i, 0))
    full = lambda a: pl.BlockSpec(a.shape, lambda i: (0, 0))
    return pl.pallas_call(
        functools.partial(_ab_proj_kernel, ntt=x.shape[0] // ROW_TILE),
        grid=(r // ROW_TILE,),
        in_specs=_pair_specs(x, hb, d) + [full(g), full(wqkv), full(wz), full(wg), full(wq), full(wkv), full(bqg),
                                          full(bkg)],
        out_specs=[row(1536), row(512), row(512), row(512), row(256)],
        out_shape=[jax.ShapeDtypeStruct((r, 1536), F32), jax.ShapeDtypeStruct((r, 512), F32),
                   jax.ShapeDtypeStruct((r, 512), F32), jax.ShapeDtypeStruct((r, 512), BF16),
                   jax.ShapeDtypeStruct((r, 256), BF16)],
        compiler_params=_cparams("parallel"),
        name="ab_proj",
    )(x, hb, g, wqkv, wz, wg, wq, wkv, bqg, bkg)


def _split2(x):
    hi = x.astype(BF16)
    return hi, (x - hi.astype(F32)).astype(BF16)


def _split3(x):
    hi = x.astype(BF16)
    r1 = x - hi.astype(F32)
    mid = r1.astype(BF16)
    lo = (r1 - mid.astype(F32)).astype(BF16)
    return hi, mid, lo


def _seq_block(tok_ref, head_ref, n):
    tok = tok_ref[pl.ds(pl.multiple_of(jnp.maximum(n - 1, 0) * BLK, BLK), BLK), :]
    return jnp.where(n == 0, head_ref[...], tok)


def _seq_views(n_tok, seq, off):
    head0 = n_tok // BLK
    return [pl.BlockSpec((seq, BLK), lambda i, j: (i, j + off)),
            pl.BlockSpec((BLK, BLK), lambda i, j: (head0 + i, j + off))]


def _gates_kernel(tok_ref, head_ref, alog_ref, dtb_ref, bg_ref, gt_ref):
    nblk = tok_ref.shape[0] // BLK + 1
    ri = lax.broadcasted_iota(jnp.int32, (BLK, BLK), 0)
    ci = lax.broadcasted_iota(jnp.int32, (BLK, BLK), 1)
    lower = (ri >= ci).astype(BF16)
    upper = (ri <= ci).astype(BF16)
    neg_a = -jnp.exp(alog_ref[...])
    dtb = dtb_ref[...]

    def one(n):
        x = _seq_block(tok_ref, head_ref, n)
        live = (ri + n * BLK) >= FRONT
        beta = jnp.where(live, _sigmoid(x), 0.0)
        g = jnp.where(live, neg_a * _softplus(x + dtb), 0.0)
        parts = _split3(g)
        pre = sum(_dot(lower, p) for p in parts)
        suf = sum(_dot(upper, p) for p in parts)
        tot = pre + suf - g
        return jnp.where(ci < 2, beta, jnp.where(ci == 2, pre, jnp.where(ci == 3, suf, tot)))

    grp = _chunk_group(nblk)

    def body(g, carry):
        outs = [one(g * grp + j) for j in range(grp)]
        for j, out in enumerate(outs):
            n = g * grp + j
            bg_ref[pl.ds(pl.multiple_of(n * BLK, BLK), BLK), :] = out
            gt_ref[n] = out.T[0:8, :]
        return carry

    lax.fori_loop(0, nblk // grp, body, 0)


def _gates(pre, b, n_tok, alog_rows, dtb_rows):
    seq = n_tok // b
    lp = seq + BLK
    nblk = lp // BLK
    return pl.pallas_call(
        _gates_kernel,
        grid=(b, A_HEADS),
        in_specs=_seq_views(n_tok, seq, 0)
                 + [pl.BlockSpec((None, 1, BLK), lambda i, j: (j, 0, 0)),
                    pl.BlockSpec((None, 1, BLK), lambda i, j: (j, 0, 0))],
        out_specs=[pl.BlockSpec((None, lp, BLK), lambda i, j: (i, 0, j)),
                   pl.BlockSpec((None, None, nblk, 8, BLK), lambda i, j: (i, j, 0, 0, 0))],
        out_shape=[jax.ShapeDtypeStruct((b, lp, A_HEADS * BLK), F32),
                   jax.ShapeDtypeStruct((b, A_HEADS, nblk, 8, BLK), F32)],
        compiler_params=_cparams("parallel", "parallel"),
        name="delta_gates",
    )(pre, pre, alog_rows, dtb_rows)


def _delta_kernel(q_ref, qh_ref, k_ref, kh_ref, v_ref, vh_ref, z_ref, zh_ref, bg_ref, gt_ref,
                  cwq_ref, cwk_ref, cwv_ref, og_ref, y_ref, yh_ref, sadd_s, smul_s, o_s, omul_s, gl_s):
    seq = q_ref.shape[0]
    nblk = seq // BLK + 1
    grp = _chunk_group(nblk)
    ri = lax.broadcasted_iota(jnp.int32, (BLK, BLK), 0)
    ci = lax.broadcasted_iota(jnp.int32, (BLK, BLK), 1)
    eye = (ri == ci).astype(F32)
    incl = (ri >= ci, ri <= ci)
    strict = (ri > ci, ri < ci)

    def conv_silu(ref, head_ref, w_ref, n):
        cur = _seq_block(ref, head_ref, n)
        tok_prev = ref[pl.ds(pl.multiple_of(jnp.maximum((n - 1) * BLK - 8, 0), 8), 8), :]
        prev = jnp.where(n == 0, 0.0, jnp.where(n == 1, head_ref[BLK - 8:, :], tok_prev))
        nxt = ref[pl.ds(pl.multiple_of(jnp.minimum(n * BLK, seq - 8), 8), 8), :]
        nxt = jnp.where(n < nblk - 1, nxt, 0.0)
        win = jnp.concatenate([prev, cur, nxt], axis=0)
        h = A_CONV // 2
        acc = win[8 - h:8 - h + BLK, :] * w_ref[0:1, :]
        for j in range(1, A_CONV):
            acc = acc + win[8 - h + j:8 - h + j + BLK, :] * w_ref[j:j + 1, :]
        return _silu(acc)

    def l2n(x):
        return x * lax.rsqrt(jnp.sum(x * x, axis=-1, keepdims=True) + EPS)

    def chunk_inputs(n):
        rows = pl.ds(pl.multiple_of(n * BLK, BLK), BLK)
        live = (ri[:, 0:1] + n * BLK) >= FRONT
        qn = jnp.where(live, l2n(conv_silu(q_ref, qh_ref, cwq_ref, n)) * (A_DK ** -0.5), 0.0)
        kn = jnp.where(live, l2n(conv_silu(k_ref, kh_ref, cwk_ref, n)), 0.0)
        vv = jnp.where(live, conv_silu(v_ref, vh_ref, cwv_ref, n), 0.0)
        kn16 = kn.astype(BF16)
        kq = _dot_nt(jnp.concatenate([kn16, qn.astype(BF16)], axis=0), kn16)
        return dict(n=n, rows=rows, qn=qn, kn=kn, vv=vv, kk=kq[:BLK], qk=kq[BLK:], bg=bg_ref[rows, :], gt=gt_ref[n])

    def chain_setup(c, d):
        bg, gt = c["bg"], c["gt"]
        beta, ccol, tot = bg[:, d:d + 1], bg[:, 2 + d:3 + d], bg[:, 4 + d:5 + d]
        crow = gt[2 + d:3 + d, :]
        dec = jnp.exp(jnp.where(incl[d], ccol - crow, NEG))
        a = jnp.where(strict[d], beta * c["kk"] * dec, 0.0)
        return dict(c=c, d=d, beta=beta, ccol=ccol, tot=tot, dec=dec, a=a, t=eye - a, x=a.astype(BF16))

    def prep(g, carry):
        chunks = [chunk_inputs(g * grp + j) for j in range(grp)]
        chains = [chain_setup(c, d) for c in chunks for d in range(2)]
        zero = jnp.zeros((BLK, BLK), BF16)

        def blockdiag(xp):
            return jnp.concatenate([jnp.concatenate([xp[:, :BLK], zero], axis=1),
                                    jnp.concatenate([zero, xp[:, BLK:]], axis=1)], axis=0)

        pairs = [(chains[2 * j], chains[2 * j + 1]) for j in range(grp)]
        xps = [jnp.concatenate([f["x"], b["x"]], axis=1) for f, b in pairs]
        tps = [jnp.concatenate([f["t"], b["t"]], axis=1) for f, b in pairs]
        for _ in range(INV_SQUARINGS):
            xps = [_dot(xp, blockdiag(xp)).astype(BF16) for xp in xps]
            txs = [_dot(tp.astype(BF16), blockdiag(xp)) for tp, xp in zip(tps, xps)]
            tps = [tp + tx for tp, tx in zip(tps, txs)]
        for (f, b), tp in zip(pairs, tps):
            f["t"], b["t"] = tp[:, :BLK], tp[:, BLK:]
        for ch in chains:
            c = ch["c"]
            ch["ec"] = jnp.exp(ch["ccol"])
            ch["rhs"] = jnp.concatenate([ch["beta"] * c["vv"], ch["beta"] * c["kn"] * ch["ec"]], axis=1)
            ch["t16"] = ch["t"].astype(BF16)
        x0s = [_dot(ch["t16"], ch["rhs"].astype(BF16)) for ch in chains]
        res = []
        for ch, x0 in zip(chains, x0s):
            ah, al = _split2(ch["a"])
            xh, xl = _split2(x0)
            ax = _dot(jnp.concatenate([ah, al], axis=1), jnp.concatenate([xh, xh], axis=0)) + _dot(ah, xl)
            res.append((ch["rhs"] - x0 - ax).astype(BF16))
        uws = [(x0 + _dot(ch["t16"], e)).astype(BF16) for ch, x0, e in zip(chains, x0s, res)]
        kuws = [_dot((ch["c"]["kn"] * jnp.exp(ch["tot"] - ch["ccol"])).T.astype(BF16), uw)
                for ch, uw in zip(chains, uws)]
        quws = [_dot((ch["c"]["qk"] * ch["dec"]).astype(BF16), uw) for ch, uw in zip(chains, uws)]
        for ch, kuw, quw in zip(chains, kuws, quws):
            c, d = ch["c"], ch["d"]
            rows = c["rows"]
            sadd_s[d, rows, :] = kuw[:, :BLK]
            smul_s[d, rows, :] = (-kuw[:, BLK:]).astype(BF16)
            o_s[d, rows, :] = quw[:, :BLK]
            omul_s[d, rows, :] = (c["qn"] * ch["ec"] - quw[:, BLK:]).astype(BF16)
            gl_s[d * nblk + c["n"]] = jnp.broadcast_to(jnp.exp(ch["tot"]), (BLK, BLK))[0:8, :]
        return carry

    lax.fori_loop(0, nblk // grp, prep, 0)

    def scan_step(d, n, s):
        rows = pl.ds(pl.multiple_of(n * BLK, BLK), BLK)
        both = _dot(jnp.concatenate([smul_s[d, rows, :], omul_s[d, rows, :]], axis=0), s.astype(BF16))
        o_s[d, rows, :] = o_s[d, rows, :] + both[BLK:]
        return s * gl_s[d * nblk + n][0:1, :] + both[:BLK] + sadd_s[d, rows, :]

    def scan(i, carry):
        sf, sb = carry
        sf = scan_step(0, i, sf)
        sb = scan_step(1, nblk - 1 - i, sb)
        return sf, sb

    s0 = jnp.zeros((BLK, BLK), F32)
    lax.fori_loop(0, nblk, scan, (s0, s0))

    def gated(rows, z):
        o = o_s[0, rows, :] + o_s[1, rows, :]
        return (_rms(o, og_ref[...]) * _silu(z)).astype(y_ref.dtype)

    yh_ref[...] = gated(slice(0, BLK), zh_ref[...])

    def finish(n, carry):
        tok_rows = pl.ds(pl.multiple_of((n - 1) * BLK, BLK), BLK)
        y_ref[tok_rows, :] = gated(pl.ds(pl.multiple_of(n * BLK, BLK), BLK), z_ref[tok_rows, :])
        return carry

    lax.fori_loop(1, nblk, finish, 0, unroll=4 if (nblk - 1) % 4 == 0 else 1)


def _delta(qkv, z, b, n_tok, bg, gt, conv_w, o_gain):
    seq = n_tok // b
    lp = seq + BLK
    nblk = lp // BLK
    col = lambda off: pl.BlockSpec((None, lp, BLK), lambda i, j: (i, 0, j + off))
    cw = lambda off: pl.BlockSpec((A_CONV, BLK), lambda i, j: (0, j + off))
    views = lambda off: _seq_views(n_tok, seq, off)
    return pl.pallas_call(
        _delta_kernel,
        grid=(b, A_HEADS),
        in_specs=views(0) + views(A_HEADS) + views(2 * A_HEADS) + views(0)
                 + [col(0), pl.BlockSpec((None, None, nblk, 8, BLK), lambda i, j: (i, j, 0, 0, 0)),
                    cw(0), cw(A_HEADS), cw(2 * A_HEADS),
                    pl.BlockSpec((1, BLK), lambda i, j: (0, 0))],
        out_specs=[pl.BlockSpec((seq, BLK), lambda i, j: (i, j)), pl.BlockSpec((BLK, BLK), lambda i, j: (i, j))],
        out_shape=[jax.ShapeDtypeStruct((n_tok, A_HEADS * BLK), BF16),
                   jax.ShapeDtypeStruct((b * BLK, A_HEADS * BLK), BF16)],
        scratch_shapes=[pltpu.VMEM((2, lp, BLK), F32), pltpu.VMEM((2, lp, BLK), BF16),
                        pltpu.VMEM((2, lp, BLK), F32), pltpu.VMEM((2, lp, BLK), BF16),
                        pltpu.VMEM((2 * nblk, 8, BLK), F32)],
        compiler_params=_cparams("parallel", "parallel"),
        name="delta_mixer",
    )(qkv, qkv, qkv, qkv, qkv, qkv, z, z, bg, gt, conv_w, conv_w, conv_w, o_gain)


def _window_kernel(q_ref, kp_ref, kc_ref, kn_ref, km_ref, bias_ref, sink_ref, y_ref):
    i = pl.program_id(1)
    nblk = pl.num_programs(1)
    grp = B_HEADS // B_KV
    nk = 4 * BLK
    c = lax.broadcasted_iota(jnp.int32, (1, nk), 1)
    kblk = i - 1 + (c >> 7)
    edge = jnp.where((c >= 3 * BLK) | ((kblk >= 1) & (kblk < nblk)), 0.0, NEG)
    q = q_ref[...]
    kvs = (kp_ref[...], kc_ref[...], kn_ref[...], km_ref[...])
    ones = jnp.ones((nk, 2 * B_HD), BF16)
    lane = lax.broadcasted_iota(jnp.int32, (BLK, 2 * B_HD), 1)
    s4s, vexts = [], []
    for kvh in range(B_KV):
        ks = jnp.concatenate([t[:, kvh * B_HD:(kvh + 1) * B_HD] for t in kvs], axis=0)
        vs = jnp.concatenate([t[:, (B_KV + kvh) * B_HD:(B_KV + kvh + 1) * B_HD] for t in kvs], axis=0)
        q4 = jnp.concatenate([q[:, hh * B_HD:(hh + 1) * B_HD] for hh in range(kvh * grp, (kvh + 1) * grp)],
                             axis=0)
        s4s.append(_dot_nt(q4, ks))
        vexts.append(jnp.concatenate([vs, vs, ones], axis=1))
    ms = []
    pvs = []
    for kvh in range(B_KV):
        ps = []
        for gi in range(grp):
            hh = kvh * grp + gi
            s = s4s[kvh][gi * BLK:(gi + 1) * BLK] + bias_ref[hh] + edge
            m = jnp.maximum(jnp.max(s, axis=-1, keepdims=True), sink_ref[hh:hh + 1, 0:1])
            ps.append(jnp.exp2(s - m).astype(BF16))
            ms.append(m)
        pvs.append(_dot(jnp.concatenate(ps, axis=0), vexts[kvh]))
    outs = []
    for hh in range(B_HEADS):
        kvh, gi = divmod(hh, grp)
        o = pvs[kvh][gi * BLK:(gi + 1) * BLK]
        den = o[:, 2 * B_HD:] + jnp.exp2(sink_ref[hh:hh + 1, 0:1] - ms[hh])
        outs.append(o[:, :2 * B_HD] / den)
    for j in range(B_HEADS // 2):
        pair = jnp.where(lane < B_HD, outs[2 * j], outs[2 * j + 1])
        y_ref[:, 2 * j * B_HD:(2 * j + 2) * B_HD] = pair.astype(y_ref.dtype)

    @pl.when(i == 0)
    def _():
        rr = lax.broadcasted_iota(jnp.int32, y_ref.shape, 0)
        y_ref[...] = jnp.where(rr >= FRONT, y_ref[...], 0).astype(y_ref.dtype)


def _window_bias():
    r = jnp.arange(BLK)[:, None]
    c = jnp.arange(4 * BLK)[None, :]
    dist = jnp.abs(BLK + r - c)
    slopes = jnp.exp2(-8.0 * (jnp.arange(B_HEADS, dtype=F32) + 1.0) / B_HEADS)
    band = (c < 3 * BLK) & (dist <= B_WIN)
    alibi = -slopes[:, None, None] * dist.astype(F32)[None] * LOG2E
    rest = jnp.where(c >= 3 * BLK + FRONT, 0.0, NEG)
    return jnp.where(band[None], alibi, rest[None]).astype(F32)


def _window(qb, kvb, b, n_tok, sink_rows):
    nblk = n_tok // b // BLK + 1
    bias = _window_bias()
    head0 = n_tok // BLK

    def blk(i, j):
        return jnp.where(j == 0, head0 + i, i * (nblk - 1) + j - 1)

    kv = lambda f: pl.BlockSpec((BLK, 2 * B_KV * B_HD), f)
    return pl.pallas_call(
        _window_kernel,
        grid=(b, nblk),
        in_specs=[pl.BlockSpec((BLK, B_HEADS * B_HD), lambda i, j: (blk(i, j), 0)),
                  kv(lambda i, j: (blk(i, jnp.maximum(j - 1, 0)), 0)),
                  kv(lambda i, j: (blk(i, j), 0)),
                  kv(lambda i, j: (blk(i, jnp.minimum(j + 1, nblk - 1)), 0)),
                  kv(lambda i, j: (head0 + i, 0)),
                  pl.BlockSpec(bias.shape, lambda i, j: (0, 0, 0)),
                  pl.BlockSpec((B_HEADS, BLK), lambda i, j: (0, 0))],
        out_specs=pl.BlockSpec((BLK, B_HEADS * B_HD), lambda i, j: (blk(i, j), 0)),
        out_shape=jax.ShapeDtypeStruct((qb.shape[0], B_HEADS * B_HD), BF16),
        compiler_params=_cparams("parallel", "parallel"),
        name="window_mixer",
    )(qb, kvb, kvb, kvb, kvb, bias, sink_rows)


def _out_mlp_kernel(*refs, arity, ntt):
    vals, pos = [], 0
    for a in arity:
        vals.append(refs[pos][...] if a == 1 else _pick(refs[pos], refs[pos + 1], ntt))
        pos += a
    wo_ref, g_ref, w1_ref, w2_ref, o_ref = refs[pos:]
    mix = jnp.concatenate(vals[1:], axis=1)
    h = vals[0] + _dot(mix, wo_ref[...])
    u = _rms(h, g_ref[...]).astype(BF16)
    dff = w1_ref.shape[1]
    acc = h
    for c in range(dff // FF_CHUNK):
        sl = slice(c * FF_CHUNK, (c + 1) * FF_CHUNK)
        a = jnp.maximum(_dot(u, w1_ref[:, sl]), 0.0)
        acc = acc + _dot((a * a).astype(BF16), w2_ref[sl, :])
    o_ref[...] = acc


def _out_mlp(rows_out, ntt, tensors, wo, g, w1, w2):
    d = wo.shape[1]
    row = lambda n: pl.BlockSpec((ROW_TILE, n), lambda i: (i, 0))
    full = lambda a: pl.BlockSpec(a.shape, lambda i: (0, 0))
    specs, args, arity = [], [], []
    for t in tensors:
        if isinstance(t, tuple):
            specs += _pair_specs(t[0], t[1], t[0].shape[1])
            args += list(t)
            arity.append(2)
        else:
            specs.append(row(t.shape[1]))
            args.append(t)
            arity.append(1)
    return pl.pallas_call(
        functools.partial(_out_mlp_kernel, arity=tuple(arity), ntt=ntt),
        grid=(rows_out // ROW_TILE,),
        in_specs=specs + [full(wo), full(g), full(w1), full(w2)],
        out_specs=row(d),
        out_shape=jax.ShapeDtypeStruct((rows_out, d), F32),
        compiler_params=_cparams("parallel"),
        name="out_mlp",
    )(*args, wo, g, w1, w2)


def _c_proj_kernel(h_ref, g_ref, wq_ref, wk_ref, wv_ref, qg_ref, kg_ref, cos_ref, sin_ref,
                   q_ref, k_ref, v_ref):
    u = _rms(h_ref[...], g_ref[...]).astype(BF16)
    cosf = cos_ref[...]
    sinf = sin_ref[...]
    half = C_HD // 2

    def norm_rope(x, gain):
        x = _rms(x, gain)
        swapped = jnp.concatenate([x[:, half:], x[:, :half]], axis=1)
        return x * cosf + swapped * sinf

    k = _dot(u, wk_ref[...])
    half_w = C_HEADS * C_HD // 2
    q_lo = _dot(u, wq_ref[:, :half_w])
    for hh in range(C_KV):
        sl = slice(hh * C_HD, (hh + 1) * C_HD)
        k_ref[:, sl] = norm_rope(k[:, sl], kg_ref[...]).astype(BF16)
    q_hi = _dot(u, wq_ref[:, half_w:])
    for hh in range(C_HEADS // 2):
        sl = slice(hh * C_HD, (hh + 1) * C_HD)
        q_ref[:, sl] = (norm_rope(q_lo[:, sl], qg_ref[...]) * (C_HD ** -0.5 * LOG2E)).astype(BF16)
    v_ref[...] = _dot(u, wv_ref[...]).astype(BF16)
    for hh in range(C_HEADS // 2):
        sl = slice(hh * C_HD, (hh + 1) * C_HD)
        q_ref[:, half_w + hh * C_HD:half_w + (hh + 1) * C_HD] = (
            norm_rope(q_hi[:, sl], qg_ref[...]) * (C_HD ** -0.5 * LOG2E)).astype(BF16)


def _c_proj(h, ntt, seq, g, wq, wk, wv, qg, kg, cosf, sinf):
    r, d = h.shape
    tm = C_ROW_TILE
    per_seq = seq // tm
    n_tok_tiles = ntt * (ROW_TILE // tm)
    row = lambda n: pl.BlockSpec((tm, n), lambda i: (i, 0))
    full = lambda a: pl.BlockSpec(a.shape, lambda i: (0, 0))
    pos = pl.BlockSpec((tm, C_HD), lambda i: (jnp.where(i < n_tok_tiles, i % per_seq, per_seq), 0))
    return pl.pallas_call(
        _c_proj_kernel,
        grid=(r // tm,),
        in_specs=[row(d), full(g), full(wq), full(wk), full(wv), full(qg), full(kg), pos, pos],
        out_specs=[row(C_HEADS * C_HD), row(C_KV * C_HD), row(C_KV * C_HD)],
        out_shape=[jax.ShapeDtypeStruct((r, C_HEADS * C_HD), BF16),
                   jax.ShapeDtypeStruct((r, C_KV * C_HD), BF16),
                   jax.ShapeDtypeStruct((r, C_KV * C_HD), BF16)],
        compiler_params=_cparams("parallel"),
        name="c_proj",
    )(h, g, wq, wk, wv, qg, kg, cosf, sinf)


ATT_TK = 1024
ATT_QB = 4


def _dense_kernel(q_ref, k_ref, kh_ref, v_ref, vh_ref, y_ref, *scratch):
    grp = C_HEADS // C_KV
    nkb = k_ref.shape[0] // ATT_TK
    nq = ATT_QB
    sa_s, sb_s, acc_s = scratch[:nq], scratch[nq:2 * nq], scratch[2 * nq:]
    qs = [jnp.concatenate([q_ref[c * BLK:(c + 1) * BLK, g * C_HD:(g + 1) * C_HD] for g in range(grp)], axis=0)
          for c in range(nq)]
    m_rows = grp * BLK

    def keys(t):
        return pl.ds(pl.multiple_of(t * ATT_TK, ATT_TK), ATT_TK)

    def v_ones(v):
        return jnp.concatenate([v, jnp.ones(v.shape, BF16)], axis=1)

    def scores(t, s_refs):
        kt = k_ref[keys(t), :]
        for c in range(nq):
            s_refs[c][...] = _dot_nt(qs[c], kt)

    def step(t, ms, s_refs):
        vt = v_ones(v_ref[keys(t), :])
        out = []
        for c in range(nq):
            s = s_refs[c][...]
            m_new = jnp.maximum(ms[c], jnp.max(s, axis=-1, keepdims=True))
            p = jnp.exp2(s - m_new).astype(BF16)
            acc_s[c][...] = jnp.exp2(ms[c] - m_new) * acc_s[c][...] + _dot(p, vt)
            out.append(m_new)
        return out

    scores(0, sa_s)
    k0 = kh_ref[...]
    v0 = v_ones(vh_ref[...])
    kc = lax.broadcasted_iota(jnp.int32, (m_rows, BLK), 1)
    ms = []
    for c in range(nq):
        s0 = jnp.where(kc >= FRONT, _dot_nt(qs[c], k0), NEG)
        m = jnp.max(s0, axis=-1, keepdims=True)
        acc_s[c][...] = _dot(jnp.exp2(s0 - m).astype(BF16), v0)
        ms.append(m)

    def body(j, ms):
        scores(2 * j + 1, sb_s)
        ms = step(2 * j, ms, sa_s)
        scores(2 * j + 2, sa_s)
        return step(2 * j + 1, ms, sb_s)

    ms = lax.fori_loop(0, nkb // 2 - 1, body, ms)
    scores(nkb - 1, sb_s)
    ms = step(nkb - 2, ms, sa_s)
    ms = step(nkb - 1, ms, sb_s)
    for c in range(nq):
        acc = acc_s[c][...]
        o = acc[:, :C_HD] / acc[:, C_HD:C_HD + 1]
        for g in range(grp):
            y_ref[c * BLK:(c + 1) * BLK, g * C_HD:(g + 1) * C_HD] = o[g * BLK:(g + 1) * BLK, :].astype(y_ref.dtype)


def _dense(q, k, v, b, n_tok):
    seq = n_tok // b
    grp = C_HEADS // C_KV
    tq = ATT_QB * BLK
    assert seq % tq == 0 and seq % (2 * ATT_TK) == 0
    per_seq = seq // tq
    head0 = n_tok // BLK
    score = pltpu.VMEM((grp * BLK, ATT_TK), F32)
    tok = pl.BlockSpec((seq, C_HD), lambda i, j, t: (i, j))
    head = pl.BlockSpec((BLK, C_HD), lambda i, j, t: (head0 + i, j))
    return pl.pallas_call(
        _dense_kernel,
        grid=(b, C_KV, per_seq),
        in_specs=[pl.BlockSpec((tq, grp * C_HD), lambda i, j, t: (i * per_seq + t, j)), tok, head, tok, head],
        out_specs=pl.BlockSpec((tq, grp * C_HD), lambda i, j, t: (i * per_seq + t, j)),
        out_shape=jax.ShapeDtypeStruct((n_tok, C_HEADS * C_HD), BF16),
        scratch_shapes=[score] * (2 * ATT_QB) + [pltpu.VMEM((grp * BLK, 2 * C_HD), F32)] * ATT_QB,
        compiler_params=_cparams("parallel", "parallel", "arbitrary"),
        name="dense_mixer",
    )(q, k, k, v, v)


def _rope_tables(seq):
    rows = seq // GRID_W
    row = jnp.repeat(jnp.arange(rows), GRID_W)
    col = jnp.tile(jnp.arange(GRID_W), rows)
    head = jnp.tile(jnp.concatenate([jnp.zeros((FRONT,), jnp.int32), jnp.arange(N_META) - N_META]), ROW_TILE // BLK)
    row = jnp.concatenate([row, head]).astype(F32)
    col = jnp.concatenate([col, head]).astype(F32)
    axis_dim = C_HD // 2
    freqs = ROPE_THETA ** (-jnp.arange(0, axis_dim, 2, dtype=F32) / axis_dim)
    ang = jnp.concatenate([row[:, None] * freqs, col[:, None] * freqs], axis=-1)
    cos, sin = jnp.cos(ang), jnp.sin(ang)
    return jnp.concatenate([cos, cos], axis=-1), jnp.concatenate([-sin, sin], axis=-1)


def _gate_weight(w_b, w_a):
    d = w_b.shape[0]
    w_b = w_b.reshape(d, 2, A_HEADS)
    w_a = w_a.reshape(d, 2, A_HEADS)
    per_head = jnp.concatenate([w_b, w_a, w_a], axis=1)
    per_head = jnp.transpose(per_head, (0, 2, 1))
    per_head = jnp.pad(per_head, ((0, 0), (0, 0), (0, BLK - 6)))
    return per_head.reshape(d, A_HEADS * BLK)


def _gate_rows(p):
    t = jnp.transpose(p.astype(F32), (1, 0))
    rows = jnp.concatenate([jnp.zeros_like(t), t, t], axis=1)
    return jnp.pad(rows, ((0, 0), (0, BLK - 6)))[:, None, :]


def kernel(x, meta_tokens, attn_norm_g, mlp_norm_g, w_in_ab, conv_w_a, a_log, dt_bias, a_out_norm_g,
           b_q_norm_g, b_k_norm_g, b_sink, w_out_ab, w_qkv_c, c_q_norm_g, c_k_norm_g, w_out_c, w_ff1, w_ff2):
    bsz, seq, d = x.shape
    n_tok = bsz * seq
    n_rows = n_tok + bsz * BLK
    ntt = n_tok // ROW_TILE
    assert attn_norm_g.shape[0] == 2 and seq % ROW_TILE == 0 and (bsz * BLK) % ROW_TILE == 0
    x2 = x.reshape(n_tok, d)
    meta = jnp.broadcast_to(meta_tokens.astype(x.dtype)[None], (bsz, N_META, d))
    head = jnp.concatenate([jnp.zeros((bsz, FRONT, d), x.dtype), meta], axis=1).reshape(bsz * BLK, d)
    row2 = lambda v: v.astype(F32).reshape(1, -1)

    w = w_in_ab[0]
    qkv_w = w[:, :1536].astype(BF16)
    z_w = w[:, 1536:2048].astype(BF16)
    gate_w = _gate_weight(w[:, 2048:2056], w[:, 2056:2064]).astype(BF16)
    bq_w = w[:, 2064:2576].astype(BF16)
    bkv_w = w[:, 2576:2832].astype(BF16)
    qkv, z, gate_pre, qb, kvb = _ab_proj(x2, head, row2(attn_norm_g[0]), qkv_w, z_w, gate_w, bq_w, bkv_w,
                                         row2(b_q_norm_g[0]), row2(b_k_norm_g[0]))
    bg, gt = _gates(gate_pre, bsz, n_tok, _gate_rows(a_log[0]), _gate_rows(dt_bias[0]))
    ya = _delta(qkv, z, bsz, n_tok, bg, gt, conv_w_a[0].astype(F32), row2(a_out_norm_g[0]))
    sink_rows = jnp.broadcast_to(b_sink[0].astype(F32)[:, None] * LOG2E, (B_HEADS, BLK))
    yb = _window(qb, kvb, bsz, n_tok, sink_rows)
    h = _out_mlp(n_rows, ntt, [(x2, head), tuple(ya), yb], w_out_ab[0].astype(BF16), row2(mlp_norm_g[0]),
                 w_ff1[0].astype(BF16), w_ff2[0].astype(BF16))

    w = w_qkv_c[0]
    deint = jnp.concatenate([jnp.arange(0, C_HD, 2), jnp.arange(1, C_HD, 2)])
    perm = lambda wc, nh: wc.reshape(d, nh, C_HD)[:, :, deint].reshape(d, nh * C_HD)
    wq = perm(w[:, :C_HEADS * C_HD], C_HEADS).astype(BF16)
    wk = perm(w[:, C_HEADS * C_HD:(C_HEADS + C_KV) * C_HD], C_KV).astype(BF16)
    wv = w[:, (C_HEADS + C_KV) * C_HD:].astype(BF16)
    cosf, sinf = _rope_tables(seq)
    q, k, v = _c_proj(h, ntt, seq, row2(attn_norm_g[1]), wq, wk, wv,
                      row2(c_q_norm_g[0][deint]), row2(c_k_norm_g[0][deint]), cosf, sinf)
    att = _dense(q, k, v, bsz, n_tok)
    out = _out_mlp(n_tok, ntt, [h, att], w_out_c[0].astype(BF16), row2(mlp_norm_g[1]),
                   w_ff1[1].astype(BF16), w_ff2[1].astype(BF16))
    return out.reshape(bsz, seq, d)
```

```python
import functools
import math

import jax
import jax.numpy as jnp
from jax import lax
from jax.experimental import pallas as pl
from jax.experimental.pallas import tpu as pltpu

F32 = jnp.float32
BF16 = jnp.bfloat16

EPS = 1e-6
N_META = 16
BLK = 128
FRONT = BLK - N_META
GRID_W = 64
ROPE_THETA = 10000.0
A_HEADS, A_DK, A_CONV = 4, 128, 5
B_HEADS, B_KV, B_HD, B_WIN = 8, 2, 64, 128
C_HEADS, C_KV, C_HD = 8, 2, 128
NEG = -1e30
LOG2E = math.log2(math.e)

VMEM_LIMIT = 56 * 1024 * 1024
ROW_TILE = 512
C_ROW_TILE = 256
FF_CHUNK = 512
INV_SQUARINGS = 6
PREP_UNROLL = 11


def _chunk_group(nblk):
    return max(g for g in range(1, PREP_UNROLL + 1) if nblk % g == 0)


def _cparams(*sem):
    return pltpu.CompilerParams(dimension_semantics=sem, vmem_limit_bytes=VMEM_LIMIT)


def _sigmoid(x):
    return 1.0 / (1.0 + jnp.exp(-x))


def _silu(x):
    return x * _sigmoid(x)


def _softplus(x):
    return jnp.maximum(x, 0.0) + jnp.log1p(jnp.exp(-jnp.abs(x)))


def _rms(x, g):
    return x * lax.rsqrt(jnp.mean(x * x, axis=-1, keepdims=True) + EPS) * g


def _dot(a, b):
    return jnp.dot(a, b, preferred_element_type=F32)


def _dot_nt(a, b):
    return lax.dot_general(a, b, (((1,), (1,)), ((), ())), preferred_element_type=F32)


def _pair_specs(tok, head, width):
    ntt = tok.shape[0] // ROW_TILE
    return [pl.BlockSpec((ROW_TILE, width), lambda i: (jnp.minimum(i, ntt - 1), 0)),
            pl.BlockSpec((ROW_TILE, width), lambda i: (jnp.maximum(i - ntt, 0), 0))]


def _pick(tok_ref, head_ref, ntt):
    return jnp.where(pl.program_id(0) < ntt, tok_ref[...], head_ref[...])


def _ab_proj_kernel(x_ref, hb_ref, g_ref, wqkv_ref, wz_ref, wg_ref, wq_ref, wkv_ref, bqg_ref, bkg_ref,
                    qkv_ref, z_ref, gate_ref, qb_ref, kvb_ref, *, ntt):
    u = _rms(_pick(x_ref, hb_ref, ntt), g_ref[...]).astype(BF16)
    qb = _dot(u, wq_ref[...])
    kv = _dot(u, wkv_ref[...])
    scale = B_HD ** -0.5 * LOG2E

    def q_heads(lo, hi):
        for hh in range(lo, hi):
            sl = slice(hh * B_HD, (hh + 1) * B_HD)
            qb_ref[:, sl] = (_rms(qb[:, sl], bqg_ref[...]) * scale).astype(BF16)

    qkv_ref[:, 0:512] = _dot(u, wqkv_ref[:, 0:512])
    q_heads(0, 3)
    qkv_ref[:, 512:1024] = _dot(u, wqkv_ref[:, 512:1024])
    q_heads(3, 6)
    qkv_ref[:, 1024:1536] = _dot(u, wqkv_ref[:, 1024:1536])
    q_heads(6, B_HEADS)
    z_ref[...] = _dot(u, wz_ref[...])
    for hh in range(B_KV):
        sl = slice(hh * B_HD, (hh + 1) * B_HD)
        kvb_ref[:, sl] = _rms(kv[:, sl], bkg_ref[...]).astype(BF16)
    kvb_ref[:, B_KV * B_HD:] = kv[:, B_KV * B_HD:].astype(BF16)
    gate_ref[...] = _dot(u, wg_ref[...])


def _ab_proj(x, hb, g, wqkv, wz, wg, wq, wkv, bqg, bkg):
    d = x.shape[1]
    r = x.shape[0] + hb.shape[0]
    row = lambda n: pl.BlockSpec((ROW_TILE, n), lambda i: (i, 0))
    full = lambda a: pl.BlockSpec(a.shape, lambda i: (0, 0))
    return pl.pallas_call(
        functools.partial(_ab_proj_kernel, ntt=x.shape[0] // ROW_TILE),
        grid=(r // ROW_TILE,),
        in_specs=_pair_specs(x, hb, d) + [full(g), full(wqkv), full(wz), full(wg), full(wq), full(wkv), full(bqg),
                                          full(bkg)],
        out_specs=[row(1536), row(512), row(512), row(512), row(256)],
        out_shape=[jax.ShapeDtypeStruct((r, 1536), F32), jax.ShapeDtypeStruct((r, 512), F32),
                   jax.ShapeDtypeStruct((r, 512), F32), jax.ShapeDtypeStruct((r, 512), BF16),
                   jax.ShapeDtypeStruct((r, 256), BF16)],
        compiler_params=_cparams("parallel"),
        name="ab_proj",
    )(x, hb, g, wqkv, wz, wg, wq, wkv, bqg, bkg)


def _split2(x):
    hi = x.astype(BF16)
    return hi, (x - hi.astype(F32)).astype(BF16)


def _split3(x):
    hi = x.astype(BF16)
    r1 = x - hi.astype(F32)
    mid = r1.astype(BF16)
    lo = (r1 - mid.astype(F32)).astype(BF16)
    return hi, mid, lo


def _seq_block(tok_ref, head_ref, n):
    tok = tok_ref[pl.ds(pl.multiple_of(jnp.maximum(n - 1, 0) * BLK, BLK), BLK), :]
    return jnp.where(n == 0, head_ref[...], tok)


def _seq_views(n_tok, seq, off):
    head0 = n_tok // BLK
    return [pl.BlockSpec((seq, BLK), lambda i, j: (i, j + off)),
            pl.BlockSpec((BLK, BLK), lambda i, j: (head0 + i, j + off))]


def _gates_kernel(tok_ref, head_ref, alog_ref, dtb_ref, bg_ref, gt_ref):
    nblk = tok_ref.shape[0] // BLK + 1
    ri = lax.broadcasted_iota(jnp.int32, (BLK, BLK), 0)
    ci = lax.broadcasted_iota(jnp.int32, (BLK, BLK), 1)
    lower = (ri >= ci).astype(BF16)
    upper = (ri <= ci).astype(BF16)
    neg_a = -jnp.exp(alog_ref[...])
    dtb = dtb_ref[...]

    def one(n):
        x = _seq_block(tok_ref, head_ref, n)
        live = (ri + n * BLK) >= FRONT
        beta = jnp.where(live, _sigmoid(x), 0.0)
        g = jnp.where(live, neg_a * _softplus(x + dtb), 0.0)
        parts = _split3(g)
        pre = sum(_dot(lower, p) for p in parts)
        suf = sum(_dot(upper, p) for p in parts)
        tot = pre + suf - g
        return jnp.where(ci < 2, beta, jnp.where(ci == 2, pre, jnp.where(ci == 3, suf, tot)))

    grp = _chunk_group(nblk)

    def body(g, carry):
        outs = [one(g * grp + j) for j in range(grp)]
        for j, out in enumerate(outs):
            n = g * grp + j
            bg_ref[pl.ds(pl.multiple_of(n * BLK, BLK), BLK), :] = out
            gt_ref[n] = out.T[0:8, :]
        return carry

    lax.fori_loop(0, nblk // grp, body, 0)


def _gates(pre, b, n_tok, alog_rows, dtb_rows):
    seq = n_tok // b
    lp = seq + BLK
    nblk = lp // BLK
    return pl.pallas_call(
        _gates_kernel,
        grid=(b, A_HEADS),
        in_specs=_seq_views(n_tok, seq, 0)
                 + [pl.BlockSpec((None, 1, BLK), lambda i, j: (j, 0, 0)),
                    pl.BlockSpec((None, 1, BLK), lambda i, j: (j, 0, 0))],
        out_specs=[pl.BlockSpec((None, lp, BLK), lambda i, j: (i, 0, j)),
                   pl.BlockSpec((None, None, nblk, 8, BLK), lambda i, j: (i, j, 0, 0, 0))],
        out_shape=[jax.ShapeDtypeStruct((b, lp, A_HEADS * BLK), F32),
                   jax.ShapeDtypeStruct((b, A_HEADS, nblk, 8, BLK), F32)],
        compiler_params=_cparams("parallel", "parallel"),
        name="delta_gates",
    )(pre, pre, alog_rows, dtb_rows)


def _delta_kernel(q_ref, qh_ref, k_ref, kh_ref, v_ref, vh_ref, z_ref, zh_ref, bg_ref, gt_ref,
                  cwq_ref, cwk_ref, cwv_ref, og_ref, y_ref, yh_ref, sadd_s, smul_s, o_s, omul_s, gl_s):
    seq = q_ref.shape[0]
    nblk = seq // BLK + 1
    grp = _chunk_group(nblk)
    ri = lax.broadcasted_iota(jnp.int32, (BLK, BLK), 0)
    ci = lax.broadcasted_iota(jnp.int32, (BLK, BLK), 1)
    eye = (ri == ci).astype(F32)
    incl = (ri >= ci, ri <= ci)
    strict = (ri > ci, ri < ci)

    def conv_silu(ref, head_ref, w_ref, n):
        cur = _seq_block(ref, head_ref, n)
        tok_prev = ref[pl.ds(pl.multiple_of(jnp.maximum((n - 1) * BLK - 8, 0), 8), 8), :]
        prev = jnp.where(n == 0, 0.0, jnp.where(n == 1, head_ref[BLK - 8:, :], tok_prev))
        nxt = ref[pl.ds(pl.multiple_of(jnp.minimum(n * BLK, seq - 8), 8), 8), :]
        nxt = jnp.where(n < nblk - 1, nxt, 0.0)
        win = jnp.concatenate([prev, cur, nxt], axis=0)
        h = A_CONV // 2
        acc = win[8 - h:8 - h + BLK, :] * w_ref[0:1, :]
        for j in range(1, A_CONV):
            acc = acc + win[8 - h + j:8 - h + j + BLK, :] * w_ref[j:j + 1, :]
        return _silu(acc)

    def l2n(x):
        return x * lax.rsqrt(jnp.sum(x * x, axis=-1, keepdims=True) + EPS)

    def chunk_inputs(n):
        rows = pl.ds(pl.multiple_of(n * BLK, BLK), BLK)
        live = (ri[:, 0:1] + n * BLK) >= FRONT
        qn = jnp.where(live, l2n(conv_silu(q_ref, qh_ref, cwq_ref, n)) * (A_DK ** -0.5), 0.0)
        kn = jnp.where(live, l2n(conv_silu(k_ref, kh_ref, cwk_ref, n)), 0.0)
        vv = jnp.where(live, conv_silu(v_ref, vh_ref, cwv_ref, n), 0.0)
        kn16 = kn.astype(BF16)
        kq = _dot_nt(jnp.concatenate([kn16, qn.astype(BF16)], axis=0), kn16)
        return dict(n=n, rows=rows, qn=qn, kn=kn, vv=vv, kk=kq[:BLK], qk=kq[BLK:], bg=bg_ref[rows, :], gt=gt_ref[n])

    def chain_setup(c, d):
        bg, gt = c["bg"], c["gt"]
        beta, ccol, tot = bg[:, d:d + 1], bg[:, 2 + d:3 + d], bg[:, 4 + d:5 + d]
        crow = gt[2 + d:3 + d, :]
        dec = jnp.exp(jnp.where(incl[d], ccol - crow, NEG))
        a = jnp.where(strict[d], beta * c["kk"] * dec, 0.0)
        return dict(c=c, d=d, beta=beta, ccol=ccol, tot=tot, dec=dec, a=a, t=eye - a, x=a.astype(BF16))

    def prep(g, carry):
        chunks = [chunk_inputs(g * grp + j) for j in range(grp)]
        chains = [chain_setup(c, d) for c in chunks for d in range(2)]
        zero = jnp.zeros((BLK, BLK), BF16)

        def blockdiag(xp):
            return jnp.concatenate([jnp.concatenate([xp[:, :BLK], zero], axis=1),
                                    jnp.concatenate([zero, xp[:, BLK:]], axis=1)], axis=0)

        pairs = [(chains[2 * j], chains[2 * j + 1]) for j in range(grp)]
        xps = [jnp.concatenate([f["x"], b["x"]], axis=1) for f, b in pairs]
        tps = [jnp.concatenate([f["t"], b["t"]], axis=1) for f, b in pairs]
        for _ in range(INV_SQUARINGS):
            xps = [_dot(xp, blockdiag(xp)).astype(BF16) for xp in xps]
            txs = [_dot(tp.astype(BF16), blockdiag(xp)) for tp, xp in zip(tps, xps)]
            tps = [tp + tx for tp, tx in zip(tps, txs)]
        for (f, b), tp in zip(pairs, tps):
            f["t"], b["t"] = tp[:, :BLK], tp[:, BLK:]
        for ch in chains:
            c = ch["c"]
            ch["ec"] = jnp.exp(ch["ccol"])
            ch["rhs"] = jnp.concatenate([ch["beta"] * c["vv"], ch["beta"] * c["kn"] * ch["ec"]], axis=1)
            ch["t16"] = ch["t"].astype(BF16)
        x0s = [_dot(ch["t16"], ch["rhs"].astype(BF16)) for ch in chains]
        res = []
        for ch, x0 in zip(chains, x0s):
            ah, al = _split2(ch["a"])
            xh, xl = _split2(x0)
            ax = _dot(jnp.concatenate([ah, al], axis=1), jnp.concatenate([xh, xh], axis=0)) + _dot(ah, xl)
            res.append((ch["rhs"] - x0 - ax).astype(BF16))
        uws = [(x0 + _dot(ch["t16"], e)).astype(BF16) for ch, x0, e in zip(chains, x0s, res)]
        kuws = [_dot((ch["c"]["kn"] * jnp.exp(ch["tot"] - ch["ccol"])).T.astype(BF16), uw)
                for ch, uw in zip(chains, uws)]
        quws = [_dot((ch["c"]["qk"] * ch["dec"]).astype(BF16), uw) for ch, uw in zip(chains, uws)]
        for ch, kuw, quw in zip(chains, kuws, quws):
            c, d = ch["c"], ch["d"]
            rows = c["rows"]
            sadd_s[d, rows, :] = kuw[:, :BLK]
            smul_s[d, rows, :] = (-kuw[:, BLK:]).astype(BF16)
            o_s[d, rows, :] = quw[:, :BLK]
            omul_s[d, rows, :] = (c["qn"] * ch["ec"] - quw[:, BLK:]).astype(BF16)
            gl_s[d * nblk + c["n"]] = jnp.broadcast_to(jnp.exp(ch["tot"]), (BLK, BLK))[0:8, :]
        return carry

    lax.fori_loop(0, nblk // grp, prep, 0)

    def scan_step(d, n, s):
        rows = pl.ds(pl.multiple_of(n * BLK, BLK), BLK)
        both = _dot(jnp.concatenate([smul_s[d, rows, :], omul_s[d, rows, :]], axis=0), s.astype(BF16))
        o_s[d, rows, :] = o_s[d, rows, :] + both[BLK:]
        return s * gl_s[d * nblk + n][0:1, :] + both[:BLK] + sadd_s[d, rows, :]

    def scan(i, carry):
        sf, sb = carry
        sf = scan_step(0, i, sf)
        sb = scan_step(1, nblk - 1 - i, sb)
        return sf, sb

    s0 = jnp.zeros((BLK, BLK), F32)
    lax.fori_loop(0, nblk, scan, (s0, s0))

    def gated(rows, z):
        o = o_s[0, rows, :] + o_s[1, rows, :]
        return (_rms(o, og_ref[...]) * _silu(z)).astype(y_ref.dtype)

    yh_ref[...] = gated(slice(0, BLK), zh_ref[...])

    def finish(n, carry):
        tok_rows = pl.ds(pl.multiple_of((n - 1) * BLK, BLK), BLK)
        y_ref[tok_rows, :] = gated(pl.ds(pl.multiple_of(n * BLK, BLK), BLK), z_ref[tok_rows, :])
        return carry

    lax.fori_loop(1, nblk, finish, 0, unroll=4 if (nblk - 1) % 4 == 0 else 1)


def _delta(qkv, z, b, n_tok, bg, gt, conv_w, o_gain):
    seq = n_tok // b
    lp = seq + BLK
    nblk = lp // BLK
    col = lambda off: pl.BlockSpec((None, lp, BLK), lambda i, j: (i, 0, j + off))
    cw = lambda off: pl.BlockSpec((A_CONV, BLK), lambda i, j: (0, j + off))
    views = lambda off: _seq_views(n_tok, seq, off)
    return pl.pallas_call(
        _delta_kernel,
        grid=(b, A_HEADS),
        in_specs=views(0) + views(A_HEADS) + views(2 * A_HEADS) + views(0)
                 + [col(0), pl.BlockSpec((None, None, nblk, 8, BLK), lambda i, j: (i, j, 0, 0, 0)),
                    cw(0), cw(A_HEADS), cw(2 * A_HEADS),
                    pl.BlockSpec((1, BLK), lambda i, j: (0, 0))],
        out_specs=[pl.BlockSpec((seq, BLK), lambda i, j: (i, j)), pl.BlockSpec((BLK, BLK), lambda i, j: (i, j))],
        out_shape=[jax.ShapeDtypeStruct((n_tok, A_HEADS * BLK), BF16),
                   jax.ShapeDtypeStruct((b * BLK, A_HEADS * BLK), BF16)],
        scratch_shapes=[pltpu.VMEM((2, lp, BLK), F32), pltpu.VMEM((2, lp, BLK), BF16),
                        pltpu.VMEM((2, lp, BLK), F32), pltpu.VMEM((2, lp, BLK), BF16),
                        pltpu.VMEM((2 * nblk, 8, BLK), F32)],
        compiler_params=_cparams("parallel", "parallel"),
        name="delta_mixer",
    )(qkv, qkv, qkv, qkv, qkv, qkv, z, z, bg, gt, conv_w, conv_w, conv_w, o_gain)


def _window_kernel(q_ref, kp_ref, kc_ref, kn_ref, km_ref, bias_ref, sink_ref, y_ref):
    i = pl.program_id(1)
    nblk = pl.num_programs(1)
    grp = B_HEADS // B_KV
    nk = 4 * BLK
    c = lax.broadcasted_iota(jnp.int32, (1, nk), 1)
    kblk = i - 1 + (c >> 7)
    edge = jnp.where((c >= 3 * BLK) | ((kblk >= 1) & (kblk < nblk)), 0.0, NEG)
    q = q_ref[...]
    kvs = (kp_ref[...], kc_ref[...], kn_ref[...], km_ref[...])
    ones = jnp.ones((nk, 2 * B_HD), BF16)
    lane = lax.broadcasted_iota(jnp.int32, (BLK, 2 * B_HD), 1)
    s4s, vexts = [], []
    for kvh in range(B_KV):
        ks = jnp.concatenate([t[:, kvh * B_HD:(kvh + 1) * B_HD] for t in kvs], axis=0)
        vs = jnp.concatenate([t[:, (B_KV + kvh) * B_HD:(B_KV + kvh + 1) * B_HD] for t in kvs], axis=0)
        q4 = jnp.concatenate([q[:, hh * B_HD:(hh + 1) * B_HD] for hh in range(kvh * grp, (kvh + 1) * grp)],
                             axis=0)
        s4s.append(_dot_nt(q4, ks))
        vexts.append(jnp.concatenate([vs, vs, ones], axis=1))
    ms = []
    pvs = []
    for kvh in range(B_KV):
        ps = []
        for gi in range(grp):
            hh = kvh * grp + gi
            s = s4s[kvh][gi * BLK:(gi + 1) * BLK] + bias_ref[hh] + edge
            m = jnp.maximum(jnp.max(s, axis=-1, keepdims=True), sink_ref[hh:hh + 1, 0:1])
            ps.append(jnp.exp2(s - m).astype(BF16))
            ms.append(m)
        pvs.append(_dot(jnp.concatenate(ps, axis=0), vexts[kvh]))
    outs = []
    for hh in range(B_HEADS):
        kvh, gi = divmod(hh, grp)
        o = pvs[kvh][gi * BLK:(gi + 1) * BLK]
        den = o[:, 2 * B_HD:] + jnp.exp2(sink_ref[hh:hh + 1, 0:1] - ms[hh])
        outs.append(o[:, :2 * B_HD] / den)
    for j in range(B_HEADS // 2):
        pair = jnp.where(lane < B_HD, outs[2 * j], outs[2 * j + 1])
        y_ref[:, 2 * j * B_HD:(2 * j + 2) * B_HD] = pair.astype(y_ref.dtype)

    @pl.when(i == 0)
    def _():
        rr = lax.broadcasted_iota(jnp.int32, y_ref.shape, 0)
        y_ref[...] = jnp.where(rr >= FRONT, y_ref[...], 0).astype(y_ref.dtype)


def _window_bias():
    r = jnp.arange(BLK)[:, None]
    c = jnp.arange(4 * BLK)[None, :]
    dist = jnp.abs(BLK + r - c)
    slopes = jnp.exp2(-8.0 * (jnp.arange(B_HEADS, dtype=F32) + 1.0) / B_HEADS)
    band = (c < 3 * BLK) & (dist <= B_WIN)
    alibi = -slopes[:, None, None] * dist.astype(F32)[None] * LOG2E
    rest = jnp.where(c >= 3 * BLK + FRONT, 0.0, NEG)
    return jnp.where(band[None], alibi, rest[None]).astype(F32)


def _window(qb, kvb, b, n_tok, sink_rows):
    nblk = n_tok // b // BLK + 1
    bias = _window_bias()
    head0 = n_tok // BLK

    def blk(i, j):
        return jnp.where(j == 0, head0 + i, i * (nblk - 1) + j - 1)

    kv = lambda f: pl.BlockSpec((BLK, 2 * B_KV * B_HD), f)
    return pl.pallas_call(
        _window_kernel,
        grid=(b, nblk),
        in_specs=[pl.BlockSpec((BLK, B_HEADS * B_HD), lambda i, j: (blk(i, j), 0)),
                  kv(lambda i, j: (blk(i, jnp.maximum(j - 1, 0)), 0)),
                  kv(lambda i, j: (blk(i, j), 0)),
                  kv(lambda i, j: (blk(i, jnp.minimum(j + 1, nblk - 1)), 0)),
                  kv(lambda i, j: (head0 + i, 0)),
                  pl.BlockSpec(bias.shape, lambda i, j: (0, 0, 0)),
                  pl.BlockSpec((B_HEADS, BLK), lambda i, j: (0, 0))],
        out_specs=pl.BlockSpec((BLK, B_HEADS * B_HD), lambda i, j: (blk(i, j), 0)),
        out_shape=jax.ShapeDtypeStruct((qb.shape[0], B_HEADS * B_HD), BF16),
        compiler_params=_cparams("parallel", "parallel"),
        name="window_mixer",
    )(qb, kvb, kvb, kvb, kvb, bias, sink_rows)


def _out_mlp_kernel(*refs, arity, ntt):
    vals, pos = [], 0
    for a in arity:
        vals.append(refs[pos][...] if a == 1 else _pick(refs[pos], refs[pos + 1], ntt))
        pos += a
    wo_ref, g_ref, w1_ref, w2_ref, o_ref = refs[pos:]
    mix = jnp.concatenate(vals[1:], axis=1)
    h = vals[0] + _dot(mix, wo_ref[...])
    u = _rms(h, g_ref[...]).astype(BF16)
    dff = w1_ref.shape[1]
    acc = h
    for c in range(dff // FF_CHUNK):
        sl = slice(c * FF_CHUNK, (c + 1) * FF_CHUNK)
        a = jnp.maximum(_dot(u, w1_ref[:, sl]), 0.0)
        acc = acc + _dot((a * a).astype(BF16), w2_ref[sl, :])
    o_ref[...] = acc


def _out_mlp(rows_out, ntt, tensors, wo, g, w1, w2):
    d = wo.shape[1]
    row = lambda n: pl.BlockSpec((ROW_TILE, n), lambda i: (i, 0))
    full = lambda a: pl.BlockSpec(a.shape, lambda i: (0, 0))
    specs, args, arity = [], [], []
    for t in tensors:
        if isinstance(t, tuple):
            specs += _pair_specs(t[0], t[1], t[0].shape[1])
            args += list(t)
            arity.append(2)
        else:
            specs.append(row(t.shape[1]))
            args.append(t)
            arity.append(1)
    return pl.pallas_call(
        functools.partial(_out_mlp_kernel, arity=tuple(arity), ntt=ntt),
        grid=(rows_out // ROW_TILE,),
        in_specs=specs + [full(wo), full(g), full(w1), full(w2)],
        out_specs=row(d),
        out_shape=jax.ShapeDtypeStruct((rows_out, d), F32),
        compiler_params=_cparams("parallel"),
        name="out_mlp",
    )(*args, wo, g, w1, w2)


def _c_proj_kernel(h_ref, g_ref, wq_ref, wk_ref, wv_ref, qg_ref, kg_ref, cos_ref, sin_ref,
                   q_ref, k_ref, v_ref):
    u = _rms(h_ref[...], g_ref[...]).astype(BF16)
    cosf = cos_ref[...]
    sinf = sin_ref[...]
    half = C_HD // 2

    def norm_rope(x, gain):
        x = _rms(x, gain)
        swapped = jnp.concatenate([x[:, half:], x[:, :half]], axis=1)
        return x * cosf + swapped * sinf

    k = _dot(u, wk_ref[...])
    half_w = C_HEADS * C_HD // 2
    q_lo = _dot(u, wq_ref[:, :half_w])
    for hh in range(C_KV):
        sl = slice(hh * C_HD, (hh + 1) * C_HD)
        k_ref[:, sl] = norm_rope(k[:, sl], kg_ref[...]).astype(BF16)
    q_hi = _dot(u, wq_ref[:, half_w:])
    for hh in range(C_HEADS // 2):
        sl = slice(hh * C_HD, (hh + 1) * C_HD)
        q_ref[:, sl] = (norm_rope(q_lo[:, sl], qg_ref[...]) * (C_HD ** -0.5 * LOG2E)).astype(BF16)
    v_ref[...] = _dot(u, wv_ref[...]).astype(BF16)
    for hh in range(C_HEADS // 2):
        sl = slice(hh * C_HD, (hh + 1) * C_HD)
        q_ref[:, half_w + hh * C_HD:half_w + (hh + 1) * C_HD] = (
            norm_rope(q_hi[:, sl], qg_ref[...]) * (C_HD ** -0.5 * LOG2E)).astype(BF16)


def _c_proj(h, ntt, seq, g, wq, wk, wv, qg, kg, cosf, sinf):
    r, d = h.shape
    tm = C_ROW_TILE
    per_seq = seq // tm
    n_tok_tiles = ntt * (ROW_TILE // tm)
    row = lambda n: pl.BlockSpec((tm, n), lambda i: (i, 0))
    full = lambda a: pl.BlockSpec(a.shape, lambda i: (0, 0))
    pos = pl.BlockSpec((tm, C_HD), lambda i: (jnp.where(i < n_tok_tiles, i % per_seq, per_seq), 0))
    return pl.pallas_call(
        _c_proj_kernel,
        grid=(r // tm,),
        in_specs=[row(d), full(g), full(wq), full(wk), full(wv), full(qg), full(kg), pos, pos],
        out_specs=[row(C_HEADS * C_HD), row(C_KV * C_HD), row(C_KV * C_HD)],
        out_shape=[jax.ShapeDtypeStruct((r, C_HEADS * C_HD), BF16),
                   jax.ShapeDtypeStruct((r, C_KV * C_HD), BF16),
                   jax.ShapeDtypeStruct((r, C_KV * C_HD), BF16)],
        compiler_params=_cparams("parallel"),
        name="c_proj",
    )(h, g, wq, wk, wv, qg, kg, cosf, sinf)


ATT_TK = 2048
ATT_QB = 4


def _dense_kernel(q_ref, k_ref, kh_ref, v_ref, vh_ref, y_ref, *scratch):
    grp = C_HEADS // C_KV
    nkb = k_ref.shape[0] // ATT_TK
    nq = ATT_QB
    sa_s, sb_s, acc_s = scratch[:nq], scratch[nq:2 * nq], scratch[2 * nq:]
    qs = [jnp.concatenate([q_ref[c * BLK:(c + 1) * BLK, g * C_HD:(g + 1) * C_HD] for g in range(grp)], axis=0)
          for c in range(nq)]
    m_rows = grp * BLK

    def keys(t):
        return pl.ds(pl.multiple_of(t * ATT_TK, ATT_TK), ATT_TK)

    def v_ones(v):
        return jnp.concatenate([v, jnp.ones(v.shape, BF16)], axis=1)

    def scores(t, s_refs):
        kt = k_ref[keys(t), :]
        for c in range(nq):
            s_refs[c][...] = _dot_nt(qs[c], kt)

    def step(t, ms, s_refs):
        vt = v_ones(v_ref[keys(t), :])
        out = []
        for c in range(nq):
            s = s_refs[c][...]
            m_new = jnp.maximum(ms[c], jnp.max(s, axis=-1, keepdims=True))
            p = jnp.exp2(s - m_new).astype(BF16)
            acc_s[c][...] = jnp.exp2(ms[c] - m_new) * acc_s[c][...] + _dot(p, vt)
            out.append(m_new)
        return out

    scores(0, sa_s)
    k0 = kh_ref[...]
    v0 = v_ones(vh_ref[...])
    kc = lax.broadcasted_iota(jnp.int32, (m_rows, BLK), 1)
    ms = []
    for c in range(nq):
        s0 = jnp.where(kc >= FRONT, _dot_nt(qs[c], k0), NEG)
        m = jnp.max(s0, axis=-1, keepdims=True)
        acc_s[c][...] = _dot(jnp.exp2(s0 - m).astype(BF16), v0)
        ms.append(m)

    def body(j, ms):
        scores(2 * j + 1, sb_s)
        ms = step(2 * j, ms, sa_s)
        scores(2 * j + 2, sa_s)
        return step(2 * j + 1, ms, sb_s)

    ms = lax.fori_loop(0, nkb // 2 - 1, body, ms)
    scores(nkb - 1, sb_s)
    ms = step(nkb - 2, ms, sa_s)
    ms = step(nkb - 1, ms, sb_s)
    for c in range(nq):
        acc = acc_s[c][...]
        o = acc[:, :C_HD] / acc[:, C_HD:C_HD + 1]
        for g in range(grp):
            y_ref[c * BLK:(c + 1) * BLK, g * C_HD:(g + 1) * C_HD] = o[g * BLK:(g + 1) * BLK, :].astype(y_ref.dtype)


def _dense(q, k, v, b, n_tok):
    seq = n_tok // b
    grp = C_HEADS // C_KV
    tq = ATT_QB * BLK
    assert seq % tq == 0 and seq % (2 * ATT_TK) == 0
    per_seq = seq // tq
    head0 = n_tok // BLK
    score = pltpu.VMEM((grp * BLK, ATT_TK), F32)
    tok = pl.BlockSpec((seq, C_HD), lambda i, j, t: (i, j))
    head = pl.BlockSpec((BLK, C_HD), lambda i, j, t: (head0 + i, j))
    return pl.pallas_call(
        _dense_kernel,
        grid=(b, C_KV, per_seq),
        in_specs=[pl.BlockSpec((tq, grp * C_HD), lambda i, j, t: (i * per_seq + t, j)), tok, head, tok, head],
        out_specs=pl.BlockSpec((tq, grp * C_HD), lambda i, j, t: (i * per_seq + t, j)),
        out_shape=jax.ShapeDtypeStruct((n_tok, C_HEADS * C_HD), BF16),
        scratch_shapes=[score] * (2 * ATT_QB) + [pltpu.VMEM((grp * BLK, 2 * C_HD), F32)] * ATT_QB,
        compiler_params=_cparams("parallel", "parallel", "arbitrary"),
        name="dense_mixer",
    )(q, k, k, v, v)


def _rope_tables(seq):
    rows = seq // GRID_W
    row = jnp.repeat(jnp.arange(rows), GRID_W)
    col = jnp.tile(jnp.arange(GRID_W), rows)
    head = jnp.tile(jnp.concatenate([jnp.zeros((FRONT,), jnp.int32), jnp.arange(N_META) - N_META]), ROW_TILE // BLK)
    row = jnp.concatenate([row, head]).astype(F32)
    col = jnp.concatenate([col, head]).astype(F32)
    axis_dim = C_HD // 2
    freqs = ROPE_THETA ** (-jnp.arange(0, axis_dim, 2, dtype=F32) / axis_dim)
    ang = jnp.concatenate([row[:, None] * freqs, col[:, None] * freqs], axis=-1)
    cos, sin = jnp.cos(ang), jnp.sin(ang)
    return jnp.concatenate([cos, cos], axis=-1), jnp.concatenate([-sin, sin], axis=-1)


def _gate_weight(w_b, w_a):
    d = w_b.shape[0]
    w_b = w_b.reshape(d, 2, A_HEADS)
    w_a = w_a.reshape(d, 2, A_HEADS)
    per_head = jnp.concatenate([w_b, w_a, w_a], axis=1)
    per_head = jnp.transpose(per_head, (0, 2, 1))
    per_head = jnp.pad(per_head, ((0, 0), (0, 0), (0, BLK - 6)))
    return per_head.reshape(d, A_HEADS * BLK)


def _gate_rows(p):
    t = jnp.transpose(p.astype(F32), (1, 0))
    rows = jnp.concatenate([jnp.zeros_like(t), t, t], axis=1)
    return jnp.pad(rows, ((0, 0), (0, BLK - 6)))[:, None, :]


def kernel(x, meta_tokens, attn_norm_g, mlp_norm_g, w_in_ab, conv_w_a, a_log, dt_bias, a_out_norm_g,
           b_q_norm_g, b_k_norm_g, b_sink, w_out_ab, w_qkv_c, c_q_norm_g, c_k_norm_g, w_out_c, w_ff1, w_ff2):
    bsz, seq, d = x.shape
    n_tok = bsz * seq
    n_rows = n_tok + bsz * BLK
    ntt = n_tok // ROW_TILE
    assert attn_norm_g.shape[0] == 2 and seq % ROW_TILE == 0 and (bsz * BLK) % ROW_TILE == 0
    x2 = x.reshape(n_tok, d)
    meta = jnp.broadcast_to(meta_tokens.astype(x.dtype)[None], (bsz, N_META, d))
    head = jnp.concatenate([jnp.zeros((bsz, FRONT, d), x.dtype), meta], axis=1).reshape(bsz * BLK, d)
    row2 = lambda v: v.astype(F32).reshape(1, -1)

    w = w_in_ab[0]
    qkv_w = w[:, :1536].astype(BF16)
    z_w = w[:, 1536:2048].astype(BF16)
    gate_w = _gate_weight(w[:, 2048:2056], w[:, 2056:2064]).astype(BF16)
    bq_w = w[:, 2064:2576].astype(BF16)
    bkv_w = w[:, 2576:2832].astype(BF16)
    qkv, z, gate_pre, qb, kvb = _ab_proj(x2, head, row2(attn_norm_g[0]), qkv_w, z_w, gate_w, bq_w, bkv_w,
                                         row2(b_q_norm_g[0]), row2(b_k_norm_g[0]))
    bg, gt = _gates(gate_pre, bsz, n_tok, _gate_rows(a_log[0]), _gate_rows(dt_bias[0]))
    ya = _delta(qkv, z, bsz, n_tok, bg, gt, conv_w_a[0].astype(F32), row2(a_out_norm_g[0]))
    sink_rows = jnp.broadcast_to(b_sink[0].astype(F32)[:, None] * LOG2E, (B_HEADS, BLK))
    yb = _window(qb, kvb, bsz, n_tok, sink_rows)
    h = _out_mlp(n_rows, ntt, [(x2, head), tuple(ya), yb], w_out_ab[0].astype(BF16), row2(mlp_norm_g[0]),
                 w_ff1[0].astype(BF16), w_ff2[0].astype(BF16))

    w = w_qkv_c[0]
    deint = jnp.concatenate([jnp.arange(0, C_HD, 2), jnp.arange(1, C_HD, 2)])
    perm = lambda wc, nh: wc.reshape(d, nh, C_HD)[:, :, deint].reshape(d, nh * C_HD)
    wq = perm(w[:, :C_HEADS * C_HD], C_HEADS).astype(BF16)
    wk = perm(w[:, C_HEADS * C_HD:(C_HEADS + C_KV) * C_HD], C_KV).astype(BF16)
    wv = w[:, (C_HEADS + C_KV) * C_HD:].astype(BF16)
    cosf, sinf = _rope_tables(seq)
    q, k, v = _c_proj(h, ntt, seq, row2(attn_norm_g[1]), wq, wk, wv,
                      row2(c_q_norm_g[0][deint]), row2(c_k_norm_g[0][deint]), cosf, sinf)
    att = _dense(q, k, v, bsz, n_tok)
    out = _out_mlp(n_tok, ntt, [h, att], w_out_c[0].astype(BF16), row2(mlp_norm_g[1]),
                   w_ff1[1].astype(BF16), w_ff2[1].astype(BF16))
    return out.reshape(bsz, seq, d)
```

```python
import functools
import math

import jax
import jax.numpy as jnp
from jax import lax
from jax.experimental import pallas as pl
from jax.experimental.pallas import tpu as pltpu

F32 = jnp.float32
BF16 = jnp.bfloat16

EPS = 1e-6
N_META = 16
BLK = 128
FRONT = BLK - N_META
GRID_W = 64
ROPE_THETA = 10000.0
A_HEADS, A_DK, A_CONV = 4, 128, 5
B_HEADS, B_KV, B_HD, B_WIN = 8, 2, 64, 128
C_HEADS, C_KV, C_HD = 8, 2, 128
NEG = -1e30
LOG2E = math.log2(math.e)

VMEM_LIMIT = 56 * 1024 * 1024
ROW_TILE = 512
C_ROW_TILE = 256
FF_CHUNK = 512
INV_SQUARINGS = 6
PREP_UNROLL = 11


def _chunk_group(nblk):
    return max(g for g in range(1, PREP_UNROLL + 1) if nblk % g == 0)


def _cparams(*sem):
    return pltpu.CompilerParams(dimension_semantics=sem, vmem_limit_bytes=VMEM_LIMIT)


def _sigmoid(x):
    return 1.0 / (1.0 + jnp.exp(-x))


def _silu(x):
    return x * _sigmoid(x)


def _softplus(x):
    return jnp.maximum(x, 0.0) + jnp.log1p(jnp.exp(-jnp.abs(x)))


def _rms(x, g):
    return x * lax.rsqrt(jnp.mean(x * x, axis=-1, keepdims=True) + EPS) * g


def _dot(a, b):
    return jnp.dot(a, b, preferred_element_type=F32)


def _dot_nt(a, b):
    return lax.dot_general(a, b, (((1,), (1,)), ((), ())), preferred_element_type=F32)


def _slab_spec(n_slabs, index):
    return pl.BlockSpec((n_slabs, ROW_TILE, BLK), lambda i: (0, index(i), 0))


def _pair_specs(tok, head):
    ntt = tok.shape[-2] // ROW_TILE
    index = (lambda i: jnp.minimum(i, ntt - 1), lambda i: jnp.maximum(i - ntt, 0))
    if tok.ndim == 3:
        return [_slab_spec(tok.shape[0], ix) for ix in index]
    flat = lambda ix: pl.BlockSpec((ROW_TILE, tok.shape[1]), lambda i: (ix(i), 0))
    return [flat(ix) for ix in index]


def _rows(v):
    return jnp.concatenate([v[c] for c in range(v.shape[0])], axis=1) if v.ndim == 3 else v


def _pick(tok_ref, head_ref, ntt):
    return _rows(jnp.where(pl.program_id(0) < ntt, tok_ref[...], head_ref[...]))


def _store_slabs(ref, first, value):
    for c in range(value.shape[1] // BLK):
        ref[first + c] = value[:, c * BLK:(c + 1) * BLK]


def _ab_proj_kernel(x_ref, hb_ref, g_ref, wqkv_ref, wz_ref, wg_ref, wq_ref, wkv_ref, bqg_ref, bkg_ref,
                    qkv_ref, z_ref, gate_ref, qb_ref, kvb_ref, *, ntt):
    u = _rms(_pick(x_ref, hb_ref, ntt), g_ref[...]).astype(BF16)
    qb = _dot(u, wq_ref[...])
    kv = _dot(u, wkv_ref[...])
    scale = B_HD ** -0.5 * LOG2E

    def q_heads(lo, hi):
        for hh in range(lo, hi):
            sl = slice(hh * B_HD, (hh + 1) * B_HD)
            qb_ref[:, sl] = (_rms(qb[:, sl], bqg_ref[...]) * scale).astype(BF16)

    _store_slabs(qkv_ref, 0, _dot(u, wqkv_ref[:, 0:512]))
    q_heads(0, 3)
    _store_slabs(qkv_ref, A_HEADS, _dot(u, wqkv_ref[:, 512:1024]))
    q_heads(3, 6)
    _store_slabs(qkv_ref, 2 * A_HEADS, _dot(u, wqkv_ref[:, 1024:1536]))
    q_heads(6, B_HEADS)
    _store_slabs(z_ref, 0, _dot(u, wz_ref[...]))
    for hh in range(B_KV):
        sl = slice(hh * B_HD, (hh + 1) * B_HD)
        kvb_ref[:, sl] = _rms(kv[:, sl], bkg_ref[...]).astype(BF16)
    kvb_ref[:, B_KV * B_HD:] = kv[:, B_KV * B_HD:].astype(BF16)
    _store_slabs(gate_ref, 0, _dot(u, wg_ref[...]))


def _ab_proj(x, hb, g, wqkv, wz, wg, wq, wkv, bqg, bkg):
    d = x.shape[1]
    r = x.shape[0] + hb.shape[0]
    row = lambda n: pl.BlockSpec((ROW_TILE, n), lambda i: (i, 0))
    full = lambda a: pl.BlockSpec(a.shape, lambda i: (0, 0))
    return pl.pallas_call(
        functools.partial(_ab_proj_kernel, ntt=x.shape[0] // ROW_TILE),
        grid=(r // ROW_TILE,),
        in_specs=_pair_specs(x, hb) + [full(g), full(wqkv), full(wz), full(wg), full(wq), full(wkv), full(bqg),
                                       full(bkg)],
        out_specs=[_slab_spec(3 * A_HEADS, lambda i: i), _slab_spec(A_HEADS, lambda i: i),
                   _slab_spec(A_HEADS, lambda i: i), row(512), row(256)],
        out_shape=[jax.ShapeDtypeStruct((3 * A_HEADS, r, BLK), F32), jax.ShapeDtypeStruct((A_HEADS, r, BLK), F32),
                   jax.ShapeDtypeStruct((A_HEADS, r, BLK), F32), jax.ShapeDtypeStruct((r, 512), BF16),
                   jax.ShapeDtypeStruct((r, 256), BF16)],
        compiler_params=_cparams("parallel"),
        name="ab_proj",
    )(x, hb, g, wqkv, wz, wg, wq, wkv, bqg, bkg)


def _split2(x):
    hi = x.astype(BF16)
    return hi, (x - hi.astype(F32)).astype(BF16)


def _split3(x):
    hi = x.astype(BF16)
    r1 = x - hi.astype(F32)
    mid = r1.astype(BF16)
    lo = (r1 - mid.astype(F32)).astype(BF16)
    return hi, mid, lo


def _seq_block(tok_ref, head_ref, n):
    tok = tok_ref[pl.ds(pl.multiple_of(jnp.maximum(n - 1, 0) * BLK, BLK), BLK), :]
    return jnp.where(n == 0, head_ref[...], tok)


def _seq_views(n_tok, seq, off):
    head0 = n_tok // BLK
    return [pl.BlockSpec((None, seq, BLK), lambda i, j: (j + off, i, 0)),
            pl.BlockSpec((None, BLK, BLK), lambda i, j: (j + off, head0 + i, 0))]


def _gates_kernel(tok_ref, head_ref, alog_ref, dtb_ref, bg_ref, gt_ref):
    nblk = tok_ref.shape[0] // BLK + 1
    ri = lax.broadcasted_iota(jnp.int32, (BLK, BLK), 0)
    ci = lax.broadcasted_iota(jnp.int32, (BLK, BLK), 1)
    lower = (ri >= ci).astype(BF16)
    upper = (ri <= ci).astype(BF16)
    neg_a = -jnp.exp(alog_ref[...])
    dtb = dtb_ref[...]

    def one(n):
        x = _seq_block(tok_ref, head_ref, n)
        live = (ri + n * BLK) >= FRONT
        beta = jnp.where(live, _sigmoid(x), 0.0)
        g = jnp.where(live, neg_a * _softplus(x + dtb), 0.0)
        parts = _split3(g)
        pre = sum(_dot(lower, p) for p in parts)
        suf = sum(_dot(upper, p) for p in parts)
        tot = pre + suf - g
        return jnp.where(ci < 2, beta, jnp.where(ci == 2, pre, jnp.where(ci == 3, suf, tot)))

    grp = _chunk_group(nblk)

    def body(g, carry):
        outs = [one(g * grp + j) for j in range(grp)]
        for j, out in enumerate(outs):
            n = g * grp + j
            bg_ref[pl.ds(pl.multiple_of(n * BLK, BLK), BLK), :] = out
            gt_ref[n] = out.T[0:8, :]
        return carry

    lax.fori_loop(0, nblk // grp, body, 0)


def _gates(pre, b, n_tok, alog_rows, dtb_rows):
    seq = n_tok // b
    lp = seq + BLK
    nblk = lp // BLK
    return pl.pallas_call(
        _gates_kernel,
        grid=(b, A_HEADS),
        in_specs=_seq_views(n_tok, seq, 0)
                 + [pl.BlockSpec((None, 1, BLK), lambda i, j: (j, 0, 0)),
                    pl.BlockSpec((None, 1, BLK), lambda i, j: (j, 0, 0))],
        out_specs=[pl.BlockSpec((None, None, lp, BLK), lambda i, j: (i, j, 0, 0)),
                   pl.BlockSpec((None, None, nblk, 8, BLK), lambda i, j: (i, j, 0, 0, 0))],
        out_shape=[jax.ShapeDtypeStruct((b, A_HEADS, lp, BLK), F32),
                   jax.ShapeDtypeStruct((b, A_HEADS, nblk, 8, BLK), F32)],
        compiler_params=_cparams("parallel", "parallel"),
        name="delta_gates",
    )(pre, pre, alog_rows, dtb_rows)


def _delta_kernel(q_ref, qh_ref, k_ref, kh_ref, v_ref, vh_ref, z_ref, zh_ref, bg_ref, gt_ref,
                  cwq_ref, cwk_ref, cwv_ref, og_ref, y_ref, yh_ref, sadd_s, smul_s, o_s, omul_s, gl_s):
    seq = q_ref.shape[0]
    nblk = seq // BLK + 1
    grp = _chunk_group(nblk)
    ri = lax.broadcasted_iota(jnp.int32, (BLK, BLK), 0)
    ci = lax.broadcasted_iota(jnp.int32, (BLK, BLK), 1)
    eye = (ri == ci).astype(F32)
    incl = (ri >= ci, ri <= ci)
    strict = (ri > ci, ri < ci)

    def conv_silu(ref, head_ref, w_ref, n):
        cur = _seq_block(ref, head_ref, n)
        tok_prev = ref[pl.ds(pl.multiple_of(jnp.maximum((n - 1) * BLK - 8, 0), 8), 8), :]
        prev = jnp.where(n == 0, 0.0, jnp.where(n == 1, head_ref[BLK - 8:, :], tok_prev))
        nxt = ref[pl.ds(pl.multiple_of(jnp.minimum(n * BLK, seq - 8), 8), 8), :]
        nxt = jnp.where(n < nblk - 1, nxt, 0.0)
        win = jnp.concatenate([prev, cur, nxt], axis=0)
        h = A_CONV // 2
        acc = win[8 - h:8 - h + BLK, :] * w_ref[0:1, :]
        for j in range(1, A_CONV):
            acc = acc + win[8 - h + j:8 - h + j + BLK, :] * w_ref[j:j + 1, :]
        return _silu(acc)

    def l2n(x):
        return x * lax.rsqrt(jnp.sum(x * x, axis=-1, keepdims=True) + EPS)

    def chunk_inputs(n):
        rows = pl.ds(pl.multiple_of(n * BLK, BLK), BLK)
        live = (ri[:, 0:1] + n * BLK) >= FRONT
        qn = jnp.where(live, l2n(conv_silu(q_ref, qh_ref, cwq_ref, n)) * (A_DK ** -0.5), 0.0)
        kn = jnp.where(live, l2n(conv_silu(k_ref, kh_ref, cwk_ref, n)), 0.0)
        vv = jnp.where(live, conv_silu(v_ref, vh_ref, cwv_ref, n), 0.0)
        kn16 = kn.astype(BF16)
        kq = _dot_nt(jnp.concatenate([kn16, qn.astype(BF16)], axis=0), kn16)
        return dict(n=n, rows=rows, qn=qn, kn=kn, vv=vv, kk=kq[:BLK], qk=kq[BLK:], bg=bg_ref[rows, :], gt=gt_ref[n])

    def chain_setup(c, d):
        bg, gt = c["bg"], c["gt"]
        beta, ccol, tot = bg[:, d:d + 1], bg[:, 2 + d:3 + d], bg[:, 4 + d:5 + d]
        crow = gt[2 + d:3 + d, :]
        dec = jnp.exp(jnp.where(incl[d], ccol - crow, NEG))
        a = jnp.where(strict[d], beta * c["kk"] * dec, 0.0)
        return dict(c=c, d=d, beta=beta, ccol=ccol, tot=tot, dec=dec, a=a, t=eye - a, x=a.astype(BF16))

    def prep(g, carry):
        chunks = [chunk_inputs(g * grp + j) for j in range(grp)]
        chains = [chain_setup(c, d) for c in chunks for d in range(2)]
        zero = jnp.zeros((BLK, BLK), BF16)

        def blockdiag(xp):
            return jnp.concatenate([jnp.concatenate([xp[:, :BLK], zero], axis=1),
                                    jnp.concatenate([zero, xp[:, BLK:]], axis=1)], axis=0)

        pairs = [(chains[2 * j], chains[2 * j + 1]) for j in range(grp)]
        xps = [jnp.concatenate([f["x"], b["x"]], axis=1) for f, b in pairs]
        tps = [jnp.concatenate([f["t"], b["t"]], axis=1) for f, b in pairs]
        for _ in range(INV_SQUARINGS):
            xps = [_dot(xp, blockdiag(xp)).astype(BF16) for xp in xps]
            txs = [_dot(tp.astype(BF16), blockdiag(xp)) for tp, xp in zip(tps, xps)]
            tps = [tp + tx for tp, tx in zip(tps, txs)]
        for (f, b), tp in zip(pairs, tps):
            f["t"], b["t"] = tp[:, :BLK], tp[:, BLK:]
        for ch in chains:
            c = ch["c"]
            ch["ec"] = jnp.exp(ch["ccol"])
            ch["rhs"] = jnp.concatenate([ch["beta"] * c["vv"], ch["beta"] * c["kn"] * ch["ec"]], axis=1)
            ch["t16"] = ch["t"].astype(BF16)
        x0s = [_dot(ch["t16"], ch["rhs"].astype(BF16)) for ch in chains]
        res = []
        for ch, x0 in zip(chains, x0s):
            ah, al = _split2(ch["a"])
            xh, xl = _split2(x0)
            ax = _dot(jnp.concatenate([ah, al], axis=1), jnp.concatenate([xh, xh], axis=0)) + _dot(ah, xl)
            res.append((ch["rhs"] - x0 - ax).astype(BF16))
        uws = [(x0 + _dot(ch["t16"], e)).astype(BF16) for ch, x0, e in zip(chains, x0s, res)]
        kuws = [_dot((ch["c"]["kn"] * jnp.exp(ch["tot"] - ch["ccol"])).T.astype(BF16), uw)
                for ch, uw in zip(chains, uws)]
        quws = [_dot((ch["c"]["qk"] * ch["dec"]).astype(BF16), uw) for ch, uw in zip(chains, uws)]
        for ch, kuw, quw in zip(chains, kuws, quws):
            c, d = ch["c"], ch["d"]
            rows = c["rows"]
            sadd_s[d, rows, :] = kuw[:, :BLK]
            smul_s[d, rows, :] = (-kuw[:, BLK:]).astype(BF16)
            o_s[d, rows, :] = quw[:, :BLK]
            omul_s[d, rows, :] = (c["qn"] * ch["ec"] - quw[:, BLK:]).astype(BF16)
            gl_s[d * nblk + c["n"]] = jnp.broadcast_to(jnp.exp(ch["tot"]), (BLK, BLK))[0:8, :]
        return carry

    lax.fori_loop(0, nblk // grp, prep, 0)

    def scan_step(d, n, s):
        rows = pl.ds(pl.multiple_of(n * BLK, BLK), BLK)
        both = _dot(jnp.concatenate([smul_s[d, rows, :], omul_s[d, rows, :]], axis=0), s.astype(BF16))
        o_s[d, rows, :] = o_s[d, rows, :] + both[BLK:]
        return s * gl_s[d * nblk + n][0:1, :] + both[:BLK] + sadd_s[d, rows, :]

    def scan(i, carry):
        sf, sb = carry
        sf = scan_step(0, i, sf)
        sb = scan_step(1, nblk - 1 - i, sb)
        return sf, sb

    s0 = jnp.zeros((BLK, BLK), F32)
    lax.fori_loop(0, nblk, scan, (s0, s0))

    def gated(rows, z):
        o = o_s[0, rows, :] + o_s[1, rows, :]
        return (_rms(o, og_ref[...]) * _silu(z)).astype(y_ref.dtype)

    yh_ref[...] = gated(slice(0, BLK), zh_ref[...])

    def finish(n, carry):
        tok_rows = pl.ds(pl.multiple_of((n - 1) * BLK, BLK), BLK)
        y_ref[tok_rows, :] = gated(pl.ds(pl.multiple_of(n * BLK, BLK), BLK), z_ref[tok_rows, :])
        return carry

    lax.fori_loop(1, nblk, finish, 0, unroll=4 if (nblk - 1) % 4 == 0 else 1)


def _delta(qkv, z, b, n_tok, bg, gt, conv_w, o_gain):
    seq = n_tok // b
    lp = seq + BLK
    nblk = lp // BLK
    cw = lambda off: pl.BlockSpec((A_CONV, BLK), lambda i, j: (0, j + off))
    views = lambda off: _seq_views(n_tok, seq, off)
    return pl.pallas_call(
        _delta_kernel,
        grid=(b, A_HEADS),
        in_specs=views(0) + views(A_HEADS) + views(2 * A_HEADS) + views(0)
                 + [pl.BlockSpec((None, None, lp, BLK), lambda i, j: (i, j, 0, 0)),
                    pl.BlockSpec((None, None, nblk, 8, BLK), lambda i, j: (i, j, 0, 0, 0)),
                    cw(0), cw(A_HEADS), cw(2 * A_HEADS),
                    pl.BlockSpec((1, BLK), lambda i, j: (0, 0))],
        out_specs=[pl.BlockSpec((None, seq, BLK), lambda i, j: (j, i, 0)),
                   pl.BlockSpec((None, BLK, BLK), lambda i, j: (j, i, 0))],
        out_shape=[jax.ShapeDtypeStruct((A_HEADS, n_tok, BLK), BF16),
                   jax.ShapeDtypeStruct((A_HEADS, b * BLK, BLK), BF16)],
        scratch_shapes=[pltpu.VMEM((2, lp, BLK), F32), pltpu.VMEM((2, lp, BLK), BF16),
                        pltpu.VMEM((2, lp, BLK), F32), pltpu.VMEM((2, lp, BLK), BF16),
                        pltpu.VMEM((2 * nblk, 8, BLK), F32)],
        compiler_params=_cparams("parallel", "parallel"),
        name="delta_mixer",
    )(qkv, qkv, qkv, qkv, qkv, qkv, z, z, bg, gt, conv_w, conv_w, conv_w, o_gain)


def _window_kernel(q_ref, kp_ref, kc_ref, kn_ref, km_ref, bias_ref, sink_ref, y_ref):
    i = pl.program_id(1)
    nblk = pl.num_programs(1)
    grp = B_HEADS // B_KV
    nk = 4 * BLK
    c = lax.broadcasted_iota(jnp.int32, (1, nk), 1)
    kblk = i - 1 + (c >> 7)
    edge = jnp.where((c >= 3 * BLK) | ((kblk >= 1) & (kblk < nblk)), 0.0, NEG)
    q = q_ref[...]
    kvs = (kp_ref[...], kc_ref[...], kn_ref[...], km_ref[...])
    ones = jnp.ones((nk, 2 * B_HD), BF16)
    lane = lax.broadcasted_iota(jnp.int32, (BLK, 2 * B_HD), 1)
    s4s, vexts = [], []
    for kvh in range(B_KV):
        ks = jnp.concatenate([t[:, kvh * B_HD:(kvh + 1) * B_HD] for t in kvs], axis=0)
        vs = jnp.concatenate([t[:, (B_KV + kvh) * B_HD:(B_KV + kvh + 1) * B_HD] for t in kvs], axis=0)
        q4 = jnp.concatenate([q[:, hh * B_HD:(hh + 1) * B_HD] for hh in range(kvh * grp, (kvh + 1) * grp)],
                             axis=0)
        s4s.append(_dot_nt(q4, ks))
        vexts.append(jnp.concatenate([vs, vs, ones], axis=1))
    ms = []
    pvs = []
    for kvh in range(B_KV):
        ps = []
        for gi in range(grp):
            hh = kvh * grp + gi
            s = s4s[kvh][gi * BLK:(gi + 1) * BLK] + bias_ref[hh] + edge
            m = jnp.maximum(jnp.max(s, axis=-1, keepdims=True), sink_ref[hh:hh + 1, 0:1])
            ps.append(jnp.exp2(s - m).astype(BF16))
            ms.append(m)
        pvs.append(_dot(jnp.concatenate(ps, axis=0), vexts[kvh]))
    outs = []
    for hh in range(B_HEADS):
        kvh, gi = divmod(hh, grp)
        o = pvs[kvh][gi * BLK:(gi + 1) * BLK]
        den = o[:, 2 * B_HD:] + jnp.exp2(sink_ref[hh:hh + 1, 0:1] - ms[hh])
        outs.append(o[:, :2 * B_HD] / den)
    for j in range(B_HEADS // 2):
        pair = jnp.where(lane < B_HD, outs[2 * j], outs[2 * j + 1])
        y_ref[:, 2 * j * B_HD:(2 * j + 2) * B_HD] = pair.astype(y_ref.dtype)

    @pl.when(i == 0)
    def _():
        rr = lax.broadcasted_iota(jnp.int32, y_ref.shape, 0)
        y_ref[...] = jnp.where(rr >= FRONT, y_ref[...], 0).astype(y_ref.dtype)


def _window_bias():
    r = jnp.arange(BLK)[:, None]
    c = jnp.arange(4 * BLK)[None, :]
    dist = jnp.abs(BLK + r - c)
    slopes = jnp.exp2(-8.0 * (jnp.arange(B_HEADS, dtype=F32) + 1.0) / B_HEADS)
    band = (c < 3 * BLK) & (dist <= B_WIN)
    alibi = -slopes[:, None, None] * dist.astype(F32)[None] * LOG2E
    rest = jnp.where(c >= 3 * BLK + FRONT, 0.0, NEG)
    return jnp.where(band[None], alibi, rest[None]).astype(F32)


def _window(qb, kvb, b, n_tok, sink_rows):
    nblk = n_tok // b // BLK + 1
    bias = _window_bias()
    head0 = n_tok // BLK

    def blk(i, j):
        return jnp.where(j == 0, head0 + i, i * (nblk - 1) + j - 1)

    kv = lambda f: pl.BlockSpec((BLK, 2 * B_KV * B_HD), f)
    return pl.pallas_call(
        _window_kernel,
        grid=(b, nblk),
        in_specs=[pl.BlockSpec((BLK, B_HEADS * B_HD), lambda i, j: (blk(i, j), 0)),
                  kv(lambda i, j: (blk(i, jnp.maximum(j - 1, 0)), 0)),
                  kv(lambda i, j: (blk(i, j), 0)),
                  kv(lambda i, j: (blk(i, jnp.minimum(j + 1, nblk - 1)), 0)),
                  kv(lambda i, j: (head0 + i, 0)),
                  pl.BlockSpec(bias.shape, lambda i, j: (0, 0, 0)),
                  pl.BlockSpec((B_HEADS, BLK), lambda i, j: (0, 0))],
        out_specs=pl.BlockSpec((BLK, B_HEADS * B_HD), lambda i, j: (blk(i, j), 0)),
        out_shape=jax.ShapeDtypeStruct((qb.shape[0], B_HEADS * B_HD), BF16),
        compiler_params=_cparams("parallel", "parallel"),
        name="window_mixer",
    )(qb, kvb, kvb, kvb, kvb, bias, sink_rows)


def _out_mlp_kernel(*refs, arity, ntt):
    vals, pos = [], 0
    for a in arity:
        vals.append(refs[pos][...] if a == 1 else _pick(refs[pos], refs[pos + 1], ntt))
        pos += a
    wo_ref, g_ref, w1_ref, w2_ref, o_ref = refs[pos:]
    mix = jnp.concatenate(vals[1:], axis=1)
    h = vals[0] + _dot(mix, wo_ref[...])
    u = _rms(h, g_ref[...]).astype(BF16)
    dff = w1_ref.shape[1]
    acc = h
    for c in range(dff // FF_CHUNK):
        sl = slice(c * FF_CHUNK, (c + 1) * FF_CHUNK)
        a = jnp.maximum(_dot(u, w1_ref[:, sl]), 0.0)
        acc = acc + _dot((a * a).astype(BF16), w2_ref[sl, :])
    o_ref[...] = acc


def _out_mlp(rows_out, ntt, tensors, wo, g, w1, w2):
    d = wo.shape[1]
    row = lambda n: pl.BlockSpec((ROW_TILE, n), lambda i: (i, 0))
    full = lambda a: pl.BlockSpec(a.shape, lambda i: (0, 0))
    specs, args, arity = [], [], []
    for t in tensors:
        if isinstance(t, tuple):
            specs += _pair_specs(t[0], t[1])
            args += list(t)
            arity.append(2)
        else:
            specs.append(row(t.shape[1]))
            args.append(t)
            arity.append(1)
    return pl.pallas_call(
        functools.partial(_out_mlp_kernel, arity=tuple(arity), ntt=ntt),
        grid=(rows_out // ROW_TILE,),
        in_specs=specs + [full(wo), full(g), full(w1), full(w2)],
        out_specs=row(d),
        out_shape=jax.ShapeDtypeStruct((rows_out, d), F32),
        compiler_params=_cparams("parallel"),
        name="out_mlp",
    )(*args, wo, g, w1, w2)


def _c_proj_kernel(h_ref, g_ref, wq_ref, wk_ref, wv_ref, qg_ref, kg_ref, cos_ref, sin_ref,
                   q_ref, k_ref, v_ref):
    u = _rms(h_ref[...], g_ref[...]).astype(BF16)
    cosf = cos_ref[...]
    sinf = sin_ref[...]
    half = C_HD // 2

    def norm_rope(x, gain):
        x = _rms(x, gain)
        swapped = jnp.concatenate([x[:, half:], x[:, :half]], axis=1)
        return x * cosf + swapped * sinf

    k = _dot(u, wk_ref[...])
    half_w = C_HEADS * C_HD // 2
    q_lo = _dot(u, wq_ref[:, :half_w])
    for hh in range(C_KV):
        sl = slice(hh * C_HD, (hh + 1) * C_HD)
        k_ref[:, sl] = norm_rope(k[:, sl], kg_ref[...]).astype(BF16)
    q_hi = _dot(u, wq_ref[:, half_w:])
    for hh in range(C_HEADS // 2):
        sl = slice(hh * C_HD, (hh + 1) * C_HD)
        q_ref[:, sl] = (norm_rope(q_lo[:, sl], qg_ref[...]) * (C_HD ** -0.5 * LOG2E)).astype(BF16)
    v_ref[...] = _dot(u, wv_ref[...]).astype(BF16)
    for hh in range(C_HEADS // 2):
        sl = slice(hh * C_HD, (hh + 1) * C_HD)
        q_ref[:, half_w + hh * C_HD:half_w + (hh + 1) * C_HD] = (
            norm_rope(q_hi[:, sl], qg_ref[...]) * (C_HD ** -0.5 * LOG2E)).astype(BF16)


def _c_proj(h, ntt, seq, g, wq, wk, wv, qg, kg, cosf, sinf):
    r, d = h.shape
    tm = C_ROW_TILE
    per_seq = seq // tm
    n_tok_tiles = ntt * (ROW_TILE // tm)
    row = lambda n: pl.BlockSpec((tm, n), lambda i: (i, 0))
    full = lambda a: pl.BlockSpec(a.shape, lambda i: (0, 0))
    pos = pl.BlockSpec((tm, C_HD), lambda i: (jnp.where(i < n_tok_tiles, i % per_seq, per_seq), 0))
    return pl.pallas_call(
        _c_proj_kernel,
        grid=(r // tm,),
        in_specs=[row(d), full(g), full(wq), full(wk), full(wv), full(qg), full(kg), pos, pos],
        out_specs=[row(C_HEADS * C_HD), row(C_KV * C_HD), row(C_KV * C_HD)],
        out_shape=[jax.ShapeDtypeStruct((r, C_HEADS * C_HD), BF16),
                   jax.ShapeDtypeStruct((r, C_KV * C_HD), BF16),
                   jax.ShapeDtypeStruct((r, C_KV * C_HD), BF16)],
        compiler_params=_cparams("parallel"),
        name="c_proj",
    )(h, g, wq, wk, wv, qg, kg, cosf, sinf)


ATT_TK = 2048
ATT_QB = 4


def _dense_kernel(q_ref, k_ref, kh_ref, v_ref, vh_ref, y_ref, *scratch):
    grp = C_HEADS // C_KV
    nkb = k_ref.shape[0] // ATT_TK
    nq = ATT_QB
    sa_s, sb_s, acc_s = scratch[:nq], scratch[nq:2 * nq], scratch[2 * nq:]
    qs = [jnp.concatenate([q_ref[c * BLK:(c + 1) * BLK, g * C_HD:(g + 1) * C_HD] for g in range(grp)], axis=0)
          for c in range(nq)]
    m_rows = grp * BLK

    def keys(t):
        return pl.ds(pl.multiple_of(t * ATT_TK, ATT_TK), ATT_TK)

    def v_ones(v):
        return jnp.concatenate([v, jnp.ones(v.shape, BF16)], axis=1)

    def scores(t, s_refs):
        kt = k_ref[keys(t), :]
        for c in range(nq):
            s_refs[c][...] = _dot_nt(qs[c], kt)

    def step(t, ms, s_refs):
        vt = v_ones(v_ref[keys(t), :])
        out = []
        for c in range(nq):
            s = s_refs[c][...]
            m_new = jnp.maximum(ms[c], jnp.max(s, axis=-1, keepdims=True))
            p = jnp.exp2(s - m_new).astype(BF16)
            acc_s[c][...] = jnp.exp2(ms[c] - m_new) * acc_s[c][...] + _dot(p, vt)
            out.append(m_new)
        return out

    scores(0, sa_s)
    k0 = kh_ref[...]
    v0 = v_ones(vh_ref[...])
    kc = lax.broadcasted_iota(jnp.int32, (m_rows, BLK), 1)
    ms = []
    for c in range(nq):
        s0 = jnp.where(kc >= FRONT, _dot_nt(qs[c], k0), NEG)
        m = jnp.max(s0, axis=-1, keepdims=True)
        acc_s[c][...] = _dot(jnp.exp2(s0 - m).astype(BF16), v0)
        ms.append(m)

    def body(j, ms):
        scores(2 * j + 1, sb_s)
        ms = step(2 * j, ms, sa_s)
        scores(2 * j + 2, sa_s)
        return step(2 * j + 1, ms, sb_s)

    ms = lax.fori_loop(0, nkb // 2 - 1, body, ms)
    scores(nkb - 1, sb_s)
    ms = step(nkb - 2, ms, sa_s)
    ms = step(nkb - 1, ms, sb_s)
    for c in range(nq):
        acc = acc_s[c][...]
        o = acc[:, :C_HD] / acc[:, C_HD:C_HD + 1]
        for g in range(grp):
            y_ref[c * BLK:(c + 1) * BLK, g * C_HD:(g + 1) * C_HD] = o[g * BLK:(g + 1) * BLK, :].astype(y_ref.dtype)


def _dense(q, k, v, b, n_tok):
    seq = n_tok // b
    grp = C_HEADS // C_KV
    tq = ATT_QB * BLK
    assert seq % tq == 0 and seq % (2 * ATT_TK) == 0
    per_seq = seq // tq
    head0 = n_tok // BLK
    score = pltpu.VMEM((grp * BLK, ATT_TK), F32)
    tok = pl.BlockSpec((seq, C_HD), lambda i, j, t: (i, j))
    head = pl.BlockSpec((BLK, C_HD), lambda i, j, t: (head0 + i, j))
    return pl.pallas_call(
        _dense_kernel,
        grid=(b, C_KV, per_seq),
        in_specs=[pl.BlockSpec((tq, grp * C_HD), lambda i, j, t: (i * per_seq + t, j)), tok, head, tok, head],
        out_specs=pl.BlockSpec((tq, grp * C_HD), lambda i, j, t: (i * per_seq + t, j)),
        out_shape=jax.ShapeDtypeStruct((n_tok, C_HEADS * C_HD), BF16),
        scratch_shapes=[score] * (2 * ATT_QB) + [pltpu.VMEM((grp * BLK, 2 * C_HD), F32)] * ATT_QB,
        compiler_params=_cparams("parallel", "parallel", "arbitrary"),
        name="dense_mixer",
    )(q, k, k, v, v)


def _rope_tables(seq):
    rows = seq // GRID_W
    row = jnp.repeat(jnp.arange(rows), GRID_W)
    col = jnp.tile(jnp.arange(GRID_W), rows)
    head = jnp.tile(jnp.concatenate([jnp.zeros((FRONT,), jnp.int32), jnp.arange(N_META) - N_META]), ROW_TILE // BLK)
    row = jnp.concatenate([row, head]).astype(F32)
    col = jnp.concatenate([col, head]).astype(F32)
    axis_dim = C_HD // 2
    freqs = ROPE_THETA ** (-jnp.arange(0, axis_dim, 2, dtype=F32) / axis_dim)
    ang = jnp.concatenate([row[:, None] * freqs, col[:, None] * freqs], axis=-1)
    cos, sin = jnp.cos(ang), jnp.sin(ang)
    return jnp.concatenate([cos, cos], axis=-1), jnp.concatenate([-sin, sin], axis=-1)


def _gate_weight(w_b, w_a):
    d = w_b.shape[0]
    w_b = w_b.reshape(d, 2, A_HEADS)
    w_a = w_a.reshape(d, 2, A_HEADS)
    per_head = jnp.concatenate([w_b, w_a, w_a], axis=1)
    per_head = jnp.transpose(per_head, (0, 2, 1))
    per_head = jnp.pad(per_head, ((0, 0), (0, 0), (0, BLK - 6)))
    return per_head.reshape(d, A_HEADS * BLK)


def _gate_rows(p):
    t = jnp.transpose(p.astype(F32), (1, 0))
    rows = jnp.concatenate([jnp.zeros_like(t), t, t], axis=1)
    return jnp.pad(rows, ((0, 0), (0, BLK - 6)))[:, None, :]


def kernel(x, meta_tokens, attn_norm_g, mlp_norm_g, w_in_ab, conv_w_a, a_log, dt_bias, a_out_norm_g,
           b_q_norm_g, b_k_norm_g, b_sink, w_out_ab, w_qkv_c, c_q_norm_g, c_k_norm_g, w_out_c, w_ff1, w_ff2):
    bsz, seq, d = x.shape
    n_tok = bsz * seq
    n_rows = n_tok + bsz * BLK
    ntt = n_tok // ROW_TILE
    assert attn_norm_g.shape[0] == 2 and seq % ROW_TILE == 0 and (bsz * BLK) % ROW_TILE == 0
    x2 = x.reshape(n_tok, d)
    meta = jnp.broadcast_to(meta_tokens.astype(x.dtype)[None], (bsz, N_META, d))
    head = jnp.concatenate([jnp.zeros((bsz, FRONT, d), x.dtype), meta], axis=1).reshape(bsz * BLK, d)
    row2 = lambda v: v.astype(F32).reshape(1, -1)

    w = w_in_ab[0]
    qkv_w = w[:, :1536].astype(BF16)
    z_w = w[:, 1536:2048].astype(BF16)
    gate_w = _gate_weight(w[:, 2048:2056], w[:, 2056:2064]).astype(BF16)
    bq_w = w[:, 2064:2576].astype(BF16)
    bkv_w = w[:, 2576:2832].astype(BF16)
    qkv, z, gate_pre, qb, kvb = _ab_proj(x2, head, row2(attn_norm_g[0]), qkv_w, z_w, gate_w, bq_w, bkv_w,
                                         row2(b_q_norm_g[0]), row2(b_k_norm_g[0]))
    bg, gt = _gates(gate_pre, bsz, n_tok, _gate_rows(a_log[0]), _gate_rows(dt_bias[0]))
    ya = _delta(qkv, z, bsz, n_tok, bg, gt, conv_w_a[0].astype(F32), row2(a_out_norm_g[0]))
    sink_rows = jnp.broadcast_to(b_sink[0].astype(F32)[:, None] * LOG2E, (B_HEADS, BLK))
    yb = _window(qb, kvb, bsz, n_tok, sink_rows)
    h = _out_mlp(n_rows, ntt, [(x2, head), tuple(ya), yb], w_out_ab[0].astype(BF16), row2(mlp_norm_g[0]),
                 w_ff1[0].astype(BF16), w_ff2[0].astype(BF16))

    w = w_qkv_c[0]
    deint = jnp.concatenate([jnp.arange(0, C_HD, 2), jnp.arange(1, C_HD, 2)])
    perm = lambda wc, nh: wc.reshape(d, nh, C_HD)[:, :, deint].reshape(d, nh * C_HD)
    wq = perm(w[:, :C_HEADS * C_HD], C_HEADS).astype(BF16)
    wk = perm(w[:, C_HEADS * C_HD:(C_HEADS + C_KV) * C_HD], C_KV).astype(BF16)
    wv = w[:, (C_HEADS + C_KV) * C_HD:].astype(BF16)
    cosf, sinf = _rope_tables(seq)
    q, k, v = _c_proj(h, ntt, seq, row2(attn_norm_g[1]), wq, wk, wv,
                      row2(c_q_norm_g[0][deint]), row2(c_k_norm_g[0][deint]), cosf, sinf)
    att = _dense(q, k, v, bsz, n_tok)
    out = _out_mlp(n_tok, ntt, [h, att], w_out_c[0].astype(BF16), row2(mlp_norm_g[1]),
                   w_ff1[1].astype(BF16), w_ff2[1].astype(BF16))
    return out.reshape(bsz, seq, d)
```

```python
import functools
import math

import jax
import jax.numpy as jnp
from jax import lax
from jax.experimental import pallas as pl
from jax.experimental.pallas import tpu as pltpu

F32 = jnp.float32
BF16 = jnp.bfloat16

EPS = 1e-6
N_META = 16
BLK = 128
FRONT = BLK - N_META
GRID_W = 64
ROPE_THETA = 10000.0
A_HEADS, A_DK, A_CONV = 4, 128, 5
B_HEADS, B_KV, B_HD, B_WIN = 8, 2, 64, 128
C_HEADS, C_KV, C_HD = 8, 2, 128
NEG = -1e30
LOG2E = math.log2(math.e)

VMEM_LIMIT = 56 * 1024 * 1024
ROW_TILE = 512
C_ROW_TILE = 256
FF_CHUNK = 512
INV_SQUARINGS = 6
PREP_UNROLL = 11


def _chunk_group(nblk):
    return max(g for g in range(1, PREP_UNROLL + 1) if nblk % g == 0)


def _cparams(*sem):
    return pltpu.CompilerParams(dimension_semantics=sem, vmem_limit_bytes=VMEM_LIMIT)


def _sigmoid(x):
    return 1.0 / (1.0 + jnp.exp(-x))


def _silu(x):
    return x * _sigmoid(x)


def _softplus(x):
    return jnp.maximum(x, 0.0) + jnp.log1p(jnp.exp(-jnp.abs(x)))


def _rms(x, g):
    return x * lax.rsqrt(jnp.mean(x * x, axis=-1, keepdims=True) + EPS) * g


def _dot(a, b):
    return jnp.dot(a, b, preferred_element_type=F32)


def _dot_nt(a, b):
    return lax.dot_general(a, b, (((1,), (1,)), ((), ())), preferred_element_type=F32)


def _slab_spec(n_slabs, index):
    return pl.BlockSpec((n_slabs, ROW_TILE, BLK), lambda i: (0, index(i), 0))


def _pair_specs(tok, head):
    ntt = tok.shape[-2] // ROW_TILE
    index = (lambda i: jnp.minimum(i, ntt - 1), lambda i: jnp.maximum(i - ntt, 0))
    if tok.ndim == 3:
        return [_slab_spec(tok.shape[0], ix) for ix in index]
    flat = lambda ix: pl.BlockSpec((ROW_TILE, tok.shape[1]), lambda i: (ix(i), 0))
    return [flat(ix) for ix in index]


def _rows(v):
    return jnp.concatenate([v[c] for c in range(v.shape[0])], axis=1) if v.ndim == 3 else v


def _pick(tok_ref, head_ref, ntt):
    return _rows(jnp.where(pl.program_id(0) < ntt, tok_ref[...], head_ref[...]))


def _store_slabs(ref, first, value):
    for c in range(value.shape[1] // BLK):
        ref[first + c] = value[:, c * BLK:(c + 1) * BLK]


def _ab_proj_kernel(x_ref, hb_ref, g_ref, wqkv_ref, wz_ref, wg_ref, wq_ref, wkv_ref, bqg_ref, bkg_ref,
                    qkv_ref, z_ref, gate_ref, qb_ref, kvb_ref, *, ntt):
    u = _rms(_pick(x_ref, hb_ref, ntt), g_ref[...]).astype(BF16)
    qb = _dot(u, wq_ref[...])
    kv = _dot(u, wkv_ref[...])
    scale = B_HD ** -0.5 * LOG2E

    def q_heads(lo, hi):
        for hh in range(lo, hi):
            sl = slice(hh * B_HD, (hh + 1) * B_HD)
            qb_ref[:, sl] = (_rms(qb[:, sl], bqg_ref[...]) * scale).astype(BF16)

    _store_slabs(qkv_ref, 0, _dot(u, wqkv_ref[:, 0:512]))
    q_heads(0, 3)
    _store_slabs(qkv_ref, A_HEADS, _dot(u, wqkv_ref[:, 512:1024]))
    q_heads(3, 6)
    _store_slabs(qkv_ref, 2 * A_HEADS, _dot(u, wqkv_ref[:, 1024:1536]))
    q_heads(6, B_HEADS)
    _store_slabs(z_ref, 0, _dot(u, wz_ref[...]))
    for hh in range(B_KV):
        sl = slice(hh * B_HD, (hh + 1) * B_HD)
        kvb_ref[:, sl] = _rms(kv[:, sl], bkg_ref[...]).astype(BF16)
    kvb_ref[:, B_KV * B_HD:] = kv[:, B_KV * B_HD:].astype(BF16)
    _store_slabs(gate_ref, 0, _dot(u, wg_ref[...]))


def _ab_proj(x, hb, g, wqkv, wz, wg, wq, wkv, bqg, bkg):
    d = x.shape[1]
    r = x.shape[0] + hb.shape[0]
    row = lambda n: pl.BlockSpec((ROW_TILE, n), lambda i: (i, 0))
    full = lambda a: pl.BlockSpec(a.shape, lambda i: (0, 0))
    return pl.pallas_call(
        functools.partial(_ab_proj_kernel, ntt=x.shape[0] // ROW_TILE),
        grid=(r // ROW_TILE,),
        in_specs=_pair_specs(x, hb) + [full(g), full(wqkv), full(wz), full(wg), full(wq), full(wkv), full(bqg),
                                       full(bkg)],
        out_specs=[_slab_spec(3 * A_HEADS, lambda i: i), _slab_spec(A_HEADS, lambda i: i),
                   _slab_spec(A_HEADS, lambda i: i), row(512), row(256)],
        out_shape=[jax.ShapeDtypeStruct((3 * A_HEADS, r, BLK), F32), jax.ShapeDtypeStruct((A_HEADS, r, BLK), F32),
                   jax.ShapeDtypeStruct((A_HEADS, r, BLK), F32), jax.ShapeDtypeStruct((r, 512), BF16),
                   jax.ShapeDtypeStruct((r, 256), BF16)],
        compiler_params=_cparams("parallel"),
        name="ab_proj",
    )(x, hb, g, wqkv, wz, wg, wq, wkv, bqg, bkg)


GATE_ROWS = 8


def _split2(x):
    hi = x.astype(BF16)
    return hi, (x - hi.astype(F32)).astype(BF16)


def _split3(x):
    hi = x.astype(BF16)
    r1 = x - hi.astype(F32)
    mid = r1.astype(BF16)
    lo = (r1 - mid.astype(F32)).astype(BF16)
    return hi, mid, lo


def _seq_block(tok_ref, head_ref, n):
    tok = tok_ref[pl.ds(pl.multiple_of(jnp.maximum(n - 1, 0) * BLK, BLK), BLK), :]
    return jnp.where(n == 0, head_ref[...], tok)


def _seq_views(n_tok, seq, off):
    head0 = n_tok // BLK
    return [pl.BlockSpec((None, seq, BLK), lambda i, j: (j + off, i, 0)),
            pl.BlockSpec((None, BLK, BLK), lambda i, j: (j + off, head0 + i, 0))]


def _gate_block(pre, n, neg_a, dtb):
    nr = GATE_ROWS
    t = pre.T[0:nr, :]
    ri = lax.broadcasted_iota(jnp.int32, (BLK, BLK), 0)
    ci = lax.broadcasted_iota(jnp.int32, (BLK, BLK), 1)
    role = ri[0:nr, :]
    live = (ci[0:nr, :] + n * BLK) >= FRONT
    beta = jnp.where(live, _sigmoid(t), 0.0)
    g = jnp.where(live, neg_a * _softplus(t + dtb), 0.0)
    parts = [p.astype(F32) for p in _split3(g)]
    parts = jnp.concatenate(parts + [jnp.zeros((BLK - 3 * nr, BLK), F32)], axis=0).astype(BF16)
    tri = jnp.concatenate([(ri <= ci).astype(BF16), (ri >= ci).astype(BF16)], axis=1)
    sums = _dot(parts, tri)
    both = sums[0:nr] + sums[nr:2 * nr] + sums[2 * nr:3 * nr]
    pre_sum, suf_sum = both[:, :BLK], both[:, BLK:]
    tot = pre_sum + suf_sum - g
    row = jnp.where(role < 2, beta, jnp.where(role == 2, pre_sum, jnp.where(role == 3, suf_sum, tot)))
    col = jnp.concatenate([row, jnp.zeros((BLK - nr, BLK), F32)], axis=0).T
    return col, row


def _delta_kernel(q_ref, qh_ref, k_ref, kh_ref, v_ref, vh_ref, z_ref, zh_ref, gp_ref, gph_ref, alog_ref, dtb_ref,
                  cwq_ref, cwk_ref, cwv_ref, og_ref, y_ref, yh_ref, sadd_s, smul_s, o_s, omul_s, gl_s):
    seq = q_ref.shape[0]
    nblk = seq // BLK + 1
    grp = _chunk_group(nblk)
    neg_a = -jnp.exp(alog_ref[...])
    dtb = dtb_ref[...]
    ri = lax.broadcasted_iota(jnp.int32, (BLK, BLK), 0)
    ci = lax.broadcasted_iota(jnp.int32, (BLK, BLK), 1)
    eye = (ri == ci).astype(F32)
    incl = (ri >= ci, ri <= ci)
    strict = (ri > ci, ri < ci)

    def conv_silu(ref, head_ref, w_ref, n):
        cur = _seq_block(ref, head_ref, n)
        tok_prev = ref[pl.ds(pl.multiple_of(jnp.maximum((n - 1) * BLK - 8, 0), 8), 8), :]
        prev = jnp.where(n == 0, 0.0, jnp.where(n == 1, head_ref[BLK - 8:, :], tok_prev))
        nxt = ref[pl.ds(pl.multiple_of(jnp.minimum(n * BLK, seq - 8), 8), 8), :]
        nxt = jnp.where(n < nblk - 1, nxt, 0.0)
        win = jnp.concatenate([prev, cur, nxt], axis=0)
        h = A_CONV // 2
        acc = win[8 - h:8 - h + BLK, :] * w_ref[0:1, :]
        for j in range(1, A_CONV):
            acc = acc + win[8 - h + j:8 - h + j + BLK, :] * w_ref[j:j + 1, :]
        return _silu(acc)

    def l2n(x):
        return x * lax.rsqrt(jnp.sum(x * x, axis=-1, keepdims=True) + EPS)

    def chunk_inputs(n):
        rows = pl.ds(pl.multiple_of(n * BLK, BLK), BLK)
        live = (ri[:, 0:1] + n * BLK) >= FRONT
        qn = jnp.where(live, l2n(conv_silu(q_ref, qh_ref, cwq_ref, n)) * (A_DK ** -0.5), 0.0)
        kn = jnp.where(live, l2n(conv_silu(k_ref, kh_ref, cwk_ref, n)), 0.0)
        vv = jnp.where(live, conv_silu(v_ref, vh_ref, cwv_ref, n), 0.0)
        kn16 = kn.astype(BF16)
        kq = _dot_nt(jnp.concatenate([kn16, qn.astype(BF16)], axis=0), kn16)
        bg, gt = _gate_block(_seq_block(gp_ref, gph_ref, n), n, neg_a, dtb)
        return dict(n=n, rows=rows, qn=qn, kn=kn, vv=vv, kk=kq[:BLK], qk=kq[BLK:], bg=bg, gt=gt)

    def chain_setup(c, d):
        bg, gt = c["bg"], c["gt"]
        beta, ccol, tot = bg[:, d:d + 1], bg[:, 2 + d:3 + d], bg[:, 4 + d:5 + d]
        crow = gt[2 + d:3 + d, :]
        dec = jnp.exp(jnp.where(incl[d], ccol - crow, NEG))
        a = jnp.where(strict[d], beta * c["kk"] * dec, 0.0)
        return dict(c=c, d=d, beta=beta, ccol=ccol, tot=tot, dec=dec, a=a, t=eye - a, x=a.astype(BF16))

    def prep(g, carry):
        chunks = [chunk_inputs(g * grp + j) for j in range(grp)]
        chains = [chain_setup(c, d) for c in chunks for d in range(2)]
        zero = jnp.zeros((BLK, BLK), BF16)

        def blockdiag(xp):
            return jnp.concatenate([jnp.concatenate([xp[:, :BLK], zero], axis=1),
                                    jnp.concatenate([zero, xp[:, BLK:]], axis=1)], axis=0)

        pairs = [(chains[2 * j], chains[2 * j + 1]) for j in range(grp)]
        xps = [jnp.concatenate([f["x"], b["x"]], axis=1) for f, b in pairs]
        tps = [jnp.concatenate([f["t"], b["t"]], axis=1) for f, b in pairs]
        for _ in range(INV_SQUARINGS):
            xps = [_dot(xp, blockdiag(xp)).astype(BF16) for xp in xps]
            txs = [_dot(tp.astype(BF16), blockdiag(xp)) for tp, xp in zip(tps, xps)]
            tps = [tp + tx for tp, tx in zip(tps, txs)]
        for (f, b), tp in zip(pairs, tps):
            f["t"], b["t"] = tp[:, :BLK], tp[:, BLK:]
        for ch in chains:
            c = ch["c"]
            ch["ec"] = jnp.exp(ch["ccol"])
            ch["rhs"] = jnp.concatenate([ch["beta"] * c["vv"], ch["beta"] * c["kn"] * ch["ec"]], axis=1)
            ch["t16"] = ch["t"].astype(BF16)
        x0s = [_dot(ch["t16"], ch["rhs"].astype(BF16)) for ch in chains]
        res = []
        for ch, x0 in zip(chains, x0s):
            ah, al = _split2(ch["a"])
            xh, xl = _split2(x0)
            ax = _dot(jnp.concatenate([ah, al], axis=1), jnp.concatenate([xh, xh], axis=0)) + _dot(ah, xl)
            res.append((ch["rhs"] - x0 - ax).astype(BF16))
        uws = [(x0 + _dot(ch["t16"], e)).astype(BF16) for ch, x0, e in zip(chains, x0s, res)]
        kuws = [_dot((ch["c"]["kn"] * jnp.exp(ch["tot"] - ch["ccol"])).T.astype(BF16), uw)
                for ch, uw in zip(chains, uws)]
        quws = [_dot((ch["c"]["qk"] * ch["dec"]).astype(BF16), uw) for ch, uw in zip(chains, uws)]
        for ch, kuw, quw in zip(chains, kuws, quws):
            c, d = ch["c"], ch["d"]
            rows = c["rows"]
            sadd_s[d, rows, :] = kuw[:, :BLK]
            smul_s[d, rows, :] = (-kuw[:, BLK:]).astype(BF16)
            o_s[d, rows, :] = quw[:, :BLK]
            omul_s[d, rows, :] = (c["qn"] * ch["ec"] - quw[:, BLK:]).astype(BF16)
            gl_s[d * nblk + c["n"]] = jnp.broadcast_to(jnp.exp(ch["tot"]), (BLK, BLK))[0:8, :]
        return carry

    lax.fori_loop(0, nblk // grp, prep, 0)

    def scan_step(d, n, s):
        rows = pl.ds(pl.multiple_of(n * BLK, BLK), BLK)
        both = _dot(jnp.concatenate([smul_s[d, rows, :], omul_s[d, rows, :]], axis=0), s.astype(BF16))
        o_s[d, rows, :] = o_s[d, rows, :] + both[BLK:]
        return s * gl_s[d * nblk + n][0:1, :] + both[:BLK] + sadd_s[d, rows, :]

    def scan(i, carry):
        sf, sb = carry
        sf = scan_step(0, i, sf)
        sb = scan_step(1, nblk - 1 - i, sb)
        return sf, sb

    s0 = jnp.zeros((BLK, BLK), F32)
    lax.fori_loop(0, nblk, scan, (s0, s0))

    def gated(rows, z):
        o = o_s[0, rows, :] + o_s[1, rows, :]
        return (_rms(o, og_ref[...]) * _silu(z)).astype(y_ref.dtype)

    yh_ref[...] = gated(slice(0, BLK), zh_ref[...])

    def finish(n, carry):
        tok_rows = pl.ds(pl.multiple_of((n - 1) * BLK, BLK), BLK)
        y_ref[tok_rows, :] = gated(pl.ds(pl.multiple_of(n * BLK, BLK), BLK), z_ref[tok_rows, :])
        return carry

    lax.fori_loop(1, nblk, finish, 0, unroll=4 if (nblk - 1) % 4 == 0 else 1)


def _delta(qkv, z, gate_pre, b, n_tok, alog_rows, dtb_rows, conv_w, o_gain):
    seq = n_tok // b
    lp = seq + BLK
    nblk = lp // BLK
    cw = lambda off: pl.BlockSpec((A_CONV, BLK), lambda i, j: (0, j + off))
    views = lambda off: _seq_views(n_tok, seq, off)
    per_head = pl.BlockSpec((None, GATE_ROWS, BLK), lambda i, j: (j, 0, 0))
    return pl.pallas_call(
        _delta_kernel,
        grid=(b, A_HEADS),
        in_specs=views(0) + views(A_HEADS) + views(2 * A_HEADS) + views(0) + views(0)
                 + [per_head, per_head, cw(0), cw(A_HEADS), cw(2 * A_HEADS),
                    pl.BlockSpec((1, BLK), lambda i, j: (0, 0))],
        out_specs=[pl.BlockSpec((None, seq, BLK), lambda i, j: (j, i, 0)),
                   pl.BlockSpec((None, BLK, BLK), lambda i, j: (j, i, 0))],
        out_shape=[jax.ShapeDtypeStruct((A_HEADS, n_tok, BLK), BF16),
                   jax.ShapeDtypeStruct((A_HEADS, b * BLK, BLK), BF16)],
        scratch_shapes=[pltpu.VMEM((2, lp, BLK), F32), pltpu.VMEM((2, lp, BLK), BF16),
                        pltpu.VMEM((2, lp, BLK), F32), pltpu.VMEM((2, lp, BLK), BF16),
                        pltpu.VMEM((2 * nblk, 8, BLK), F32)],
        compiler_params=_cparams("parallel", "parallel"),
        name="delta_mixer",
    )(qkv, qkv, qkv, qkv, qkv, qkv, z, z, gate_pre, gate_pre, alog_rows, dtb_rows, conv_w, conv_w, conv_w, o_gain)


def _window_kernel(q_ref, kp_ref, kc_ref, kn_ref, km_ref, bias_ref, sink_ref, y_ref):
    i = pl.program_id(1)
    nblk = pl.num_programs(1)
    grp = B_HEADS // B_KV
    nk = 4 * BLK
    c = lax.broadcasted_iota(jnp.int32, (1, nk), 1)
    kblk = i - 1 + (c >> 7)
    edge = jnp.where((c >= 3 * BLK) | ((kblk >= 1) & (kblk < nblk)), 0.0, NEG)
    q = q_ref[...]
    kvs = (kp_ref[...], kc_ref[...], kn_ref[...], km_ref[...])
    ones = jnp.ones((nk, 2 * B_HD), BF16)
    lane = lax.broadcasted_iota(jnp.int32, (BLK, 2 * B_HD), 1)
    s4s, vexts = [], []
    for kvh in range(B_KV):
        ks = jnp.concatenate([t[:, kvh * B_HD:(kvh + 1) * B_HD] for t in kvs], axis=0)
        vs = jnp.concatenate([t[:, (B_KV + kvh) * B_HD:(B_KV + kvh + 1) * B_HD] for t in kvs], axis=0)
        q4 = jnp.concatenate([q[:, hh * B_HD:(hh + 1) * B_HD] for hh in range(kvh * grp, (kvh + 1) * grp)],
                             axis=0)
        s4s.append(_dot_nt(q4, ks))
        vexts.append(jnp.concatenate([vs, vs, ones], axis=1))
    ms = []
    pvs = []
    for kvh in range(B_KV):
        ps = []
        for gi in range(grp):
            hh = kvh * grp + gi
            s = s4s[kvh][gi * BLK:(gi + 1) * BLK] + bias_ref[hh] + edge
            m = jnp.maximum(jnp.max(s, axis=-1, keepdims=True), sink_ref[hh:hh + 1, 0:1])
            ps.append(jnp.exp2(s - m).astype(BF16))
            ms.append(m)
        pvs.append(_dot(jnp.concatenate(ps, axis=0), vexts[kvh]))
    outs = []
    for hh in range(B_HEADS):
        kvh, gi = divmod(hh, grp)
        o = pvs[kvh][gi * BLK:(gi + 1) * BLK]
        den = o[:, 2 * B_HD:] + jnp.exp2(sink_ref[hh:hh + 1, 0:1] - ms[hh])
        outs.append(o[:, :2 * B_HD] / den)
    for j in range(B_HEADS // 2):
        pair = jnp.where(lane < B_HD, outs[2 * j], outs[2 * j + 1])
        y_ref[:, 2 * j * B_HD:(2 * j + 2) * B_HD] = pair.astype(y_ref.dtype)

    @pl.when(i == 0)
    def _():
        rr = lax.broadcasted_iota(jnp.int32, y_ref.shape, 0)
        y_ref[...] = jnp.where(rr >= FRONT, y_ref[...], 0).astype(y_ref.dtype)


def _window_bias():
    r = jnp.arange(BLK)[:, None]
    c = jnp.arange(4 * BLK)[None, :]
    dist = jnp.abs(BLK + r - c)
    slopes = jnp.exp2(-8.0 * (jnp.arange(B_HEADS, dtype=F32) + 1.0) / B_HEADS)
    band = (c < 3 * BLK) & (dist <= B_WIN)
    alibi = -slopes[:, None, None] * dist.astype(F32)[None] * LOG2E
    rest = jnp.where(c >= 3 * BLK + FRONT, 0.0, NEG)
    return jnp.where(band[None], alibi, rest[None]).astype(F32)


def _window(qb, kvb, b, n_tok, sink_rows):
    nblk = n_tok // b // BLK + 1
    bias = _window_bias()
    head0 = n_tok // BLK

    def blk(i, j):
        return jnp.where(j == 0, head0 + i, i * (nblk - 1) + j - 1)

    kv = lambda f: pl.BlockSpec((BLK, 2 * B_KV * B_HD), f)
    return pl.pallas_call(
        _window_kernel,
        grid=(b, nblk),
        in_specs=[pl.BlockSpec((BLK, B_HEADS * B_HD), lambda i, j: (blk(i, j), 0)),
                  kv(lambda i, j: (blk(i, jnp.maximum(j - 1, 0)), 0)),
                  kv(lambda i, j: (blk(i, j), 0)),
                  kv(lambda i, j: (blk(i, jnp.minimum(j + 1, nblk - 1)), 0)),
                  kv(lambda i, j: (head0 + i, 0)),
                  pl.BlockSpec(bias.shape, lambda i, j: (0, 0, 0)),
                  pl.BlockSpec((B_HEADS, BLK), lambda i, j: (0, 0))],
        out_specs=pl.BlockSpec((BLK, B_HEADS * B_HD), lambda i, j: (blk(i, j), 0)),
        out_shape=jax.ShapeDtypeStruct((qb.shape[0], B_HEADS * B_HD), BF16),
        compiler_params=_cparams("parallel", "parallel"),
        name="window_mixer",
    )(qb, kvb, kvb, kvb, kvb, bias, sink_rows)


def _out_mlp_kernel(*refs, arity, ntt):
    vals, pos = [], 0
    for a in arity:
        vals.append(refs[pos][...] if a == 1 else _pick(refs[pos], refs[pos + 1], ntt))
        pos += a
    wo_ref, g_ref, w1_ref, w2_ref, o_ref = refs[pos:]
    mix = jnp.concatenate(vals[1:], axis=1)
    h = vals[0] + _dot(mix, wo_ref[...])
    u = _rms(h, g_ref[...]).astype(BF16)
    dff = w1_ref.shape[1]
    acc = h
    for c in range(dff // FF_CHUNK):
        sl = slice(c * FF_CHUNK, (c + 1) * FF_CHUNK)
        a = jnp.maximum(_dot(u, w1_ref[:, sl]), 0.0)
        acc = acc + _dot((a * a).astype(BF16), w2_ref[sl, :])
    o_ref[...] = acc


def _out_mlp(rows_out, ntt, tensors, wo, g, w1, w2):
    d = wo.shape[1]
    row = lambda n: pl.BlockSpec((ROW_TILE, n), lambda i: (i, 0))
    full = lambda a: pl.BlockSpec(a.shape, lambda i: (0, 0))
    specs, args, arity = [], [], []
    for t in tensors:
        if isinstance(t, tuple):
            specs += _pair_specs(t[0], t[1])
            args += list(t)
            arity.append(2)
        else:
            specs.append(row(t.shape[1]))
            args.append(t)
            arity.append(1)
    return pl.pallas_call(
        functools.partial(_out_mlp_kernel, arity=tuple(arity), ntt=ntt),
        grid=(rows_out // ROW_TILE,),
        in_specs=specs + [full(wo), full(g), full(w1), full(w2)],
        out_specs=row(d),
        out_shape=jax.ShapeDtypeStruct((rows_out, d), F32),
        compiler_params=_cparams("parallel"),
        name="out_mlp",
    )(*args, wo, g, w1, w2)


def _c_proj_kernel(h_ref, g_ref, wq_ref, wk_ref, wv_ref, qg_ref, kg_ref, cos_ref, sin_ref,
                   q_ref, k_ref, v_ref):
    u = _rms(h_ref[...], g_ref[...]).astype(BF16)
    cosf = cos_ref[...]
    sinf = sin_ref[...]
    half = C_HD // 2

    def norm_rope(x, gain):
        x = _rms(x, gain)
        swapped = jnp.concatenate([x[:, half:], x[:, :half]], axis=1)
        return x * cosf + swapped * sinf

    k = _dot(u, wk_ref[...])
    half_w = C_HEADS * C_HD // 2
    q_lo = _dot(u, wq_ref[:, :half_w])
    for hh in range(C_KV):
        sl = slice(hh * C_HD, (hh + 1) * C_HD)
        k_ref[:, sl] = norm_rope(k[:, sl], kg_ref[...]).astype(BF16)
    q_hi = _dot(u, wq_ref[:, half_w:])
    for hh in range(C_HEADS // 2):
        sl = slice(hh * C_HD, (hh + 1) * C_HD)
        q_ref[:, sl] = (norm_rope(q_lo[:, sl], qg_ref[...]) * (C_HD ** -0.5 * LOG2E)).astype(BF16)
    v_ref[...] = _dot(u, wv_ref[...]).astype(BF16)
    for hh in range(C_HEADS // 2):
        sl = slice(hh * C_HD, (hh + 1) * C_HD)
        q_ref[:, half_w + hh * C_HD:half_w + (hh + 1) * C_HD] = (
            norm_rope(q_hi[:, sl], qg_ref[...]) * (C_HD ** -0.5 * LOG2E)).astype(BF16)


def _c_proj(h, ntt, seq, g, wq, wk, wv, qg, kg, cosf, sinf):
    r, d = h.shape
    tm = C_ROW_TILE
    per_seq = seq // tm
    n_tok_tiles = ntt * (ROW_TILE // tm)
    row = lambda n: pl.BlockSpec((tm, n), lambda i: (i, 0))
    full = lambda a: pl.BlockSpec(a.shape, lambda i: (0, 0))
    pos = pl.BlockSpec((tm, C_HD), lambda i: (jnp.where(i < n_tok_tiles, i % per_seq, per_seq), 0))
    return pl.pallas_call(
        _c_proj_kernel,
        grid=(r // tm,),
        in_specs=[row(d), full(g), full(wq), full(wk), full(wv), full(qg), full(kg), pos, pos],
        out_specs=[row(C_HEADS * C_HD), row(C_KV * C_HD), row(C_KV * C_HD)],
        out_shape=[jax.ShapeDtypeStruct((r, C_HEADS * C_HD), BF16),
                   jax.ShapeDtypeStruct((r, C_KV * C_HD), BF16),
                   jax.ShapeDtypeStruct((r, C_KV * C_HD), BF16)],
        compiler_params=_cparams("parallel"),
        name="c_proj",
    )(h, g, wq, wk, wv, qg, kg, cosf, sinf)


ATT_TK = 2048
ATT_QB = 4


def _dense_kernel(q_ref, k_ref, kh_ref, v_ref, vh_ref, y_ref, *scratch):
    grp = C_HEADS // C_KV
    nkb = k_ref.shape[0] // ATT_TK
    nq = ATT_QB
    sa_s, sb_s, acc_s = scratch[:nq], scratch[nq:2 * nq], scratch[2 * nq:]
    qs = [jnp.concatenate([q_ref[c * BLK:(c + 1) * BLK, g * C_HD:(g + 1) * C_HD] for g in range(grp)], axis=0)
          for c in range(nq)]
    m_rows = grp * BLK

    def keys(t):
        return pl.ds(pl.multiple_of(t * ATT_TK, ATT_TK), ATT_TK)

    def v_ones(v):
        return jnp.concatenate([v, jnp.ones(v.shape, BF16)], axis=1)

    def scores(t, s_refs):
        kt = k_ref[keys(t), :]
        for c in range(nq):
            s_refs[c][...] = _dot_nt(qs[c], kt)

    def step(t, ms, s_refs):
        vt = v_ones(v_ref[keys(t), :])
        out = []
        for c in range(nq):
            s = s_refs[c][...]
            m_new = jnp.maximum(ms[c], jnp.max(s, axis=-1, keepdims=True))
            p = jnp.exp2(s - m_new).astype(BF16)
            acc_s[c][...] = jnp.exp2(ms[c] - m_new) * acc_s[c][...] + _dot(p, vt)
            out.append(m_new)
        return out

    scores(0, sa_s)
    k0 = kh_ref[...]
    v0 = v_ones(vh_ref[...])
    kc = lax.broadcasted_iota(jnp.int32, (m_rows, BLK), 1)
    ms = []
    for c in range(nq):
        s0 = jnp.where(kc >= FRONT, _dot_nt(qs[c], k0), NEG)
        m = jnp.max(s0, axis=-1, keepdims=True)
        acc_s[c][...] = _dot(jnp.exp2(s0 - m).astype(BF16), v0)
        ms.append(m)

    def body(j, ms):
        scores(2 * j + 1, sb_s)
        ms = step(2 * j, ms, sa_s)
        scores(2 * j + 2, sa_s)
        return step(2 * j + 1, ms, sb_s)

    ms = lax.fori_loop(0, nkb // 2 - 1, body, ms)
    scores(nkb - 1, sb_s)
    ms = step(nkb - 2, ms, sa_s)
    ms = step(nkb - 1, ms, sb_s)
    for c in range(nq):
        acc = acc_s[c][...]
        o = acc[:, :C_HD] / acc[:, C_HD:C_HD + 1]
        for g in range(grp):
            y_ref[c * BLK:(c + 1) * BLK, g * C_HD:(g + 1) * C_HD] = o[g * BLK:(g + 1) * BLK, :].astype(y_ref.dtype)


def _dense(q, k, v, b, n_tok):
    seq = n_tok // b
    grp = C_HEADS // C_KV
    tq = ATT_QB * BLK
    assert seq % tq == 0 and seq % (2 * ATT_TK) == 0
    per_seq = seq // tq
    head0 = n_tok // BLK
    score = pltpu.VMEM((grp * BLK, ATT_TK), F32)
    tok = pl.BlockSpec((seq, C_HD), lambda i, j, t: (i, j))
    head = pl.BlockSpec((BLK, C_HD), lambda i, j, t: (head0 + i, j))
    return pl.pallas_call(
        _dense_kernel,
        grid=(b, C_KV, per_seq),
        in_specs=[pl.BlockSpec((tq, grp * C_HD), lambda i, j, t: (i * per_seq + t, j)), tok, head, tok, head],
        out_specs=pl.BlockSpec((tq, grp * C_HD), lambda i, j, t: (i * per_seq + t, j)),
        out_shape=jax.ShapeDtypeStruct((n_tok, C_HEADS * C_HD), BF16),
        scratch_shapes=[score] * (2 * ATT_QB) + [pltpu.VMEM((grp * BLK, 2 * C_HD), F32)] * ATT_QB,
        compiler_params=_cparams("parallel", "parallel", "arbitrary"),
        name="dense_mixer",
    )(q, k, k, v, v)


def _rope_tables(seq):
    rows = seq // GRID_W
    row = jnp.repeat(jnp.arange(rows), GRID_W)
    col = jnp.tile(jnp.arange(GRID_W), rows)
    head = jnp.tile(jnp.concatenate([jnp.zeros((FRONT,), jnp.int32), jnp.arange(N_META) - N_META]), ROW_TILE // BLK)
    row = jnp.concatenate([row, head]).astype(F32)
    col = jnp.concatenate([col, head]).astype(F32)
    axis_dim = C_HD // 2
    freqs = ROPE_THETA ** (-jnp.arange(0, axis_dim, 2, dtype=F32) / axis_dim)
    ang = jnp.concatenate([row[:, None] * freqs, col[:, None] * freqs], axis=-1)
    cos, sin = jnp.cos(ang), jnp.sin(ang)
    return jnp.concatenate([cos, cos], axis=-1), jnp.concatenate([-sin, sin], axis=-1)


def _gate_weight(w_b, w_a):
    d = w_b.shape[0]
    w_b = w_b.reshape(d, 2, A_HEADS)
    w_a = w_a.reshape(d, 2, A_HEADS)
    per_head = jnp.concatenate([w_b, w_a, w_a], axis=1)
    per_head = jnp.transpose(per_head, (0, 2, 1))
    per_head = jnp.pad(per_head, ((0, 0), (0, 0), (0, BLK - 6)))
    return per_head.reshape(d, A_HEADS * BLK)


def _gate_rows(p):
    t = jnp.transpose(p.astype(F32), (1, 0))
    rows = jnp.concatenate([jnp.zeros_like(t), t, t], axis=1)
    rows = jnp.pad(rows, ((0, 0), (0, GATE_ROWS - 6)))
    return jnp.broadcast_to(rows[:, :, None], (A_HEADS, GATE_ROWS, BLK))


def kernel(x, meta_tokens, attn_norm_g, mlp_norm_g, w_in_ab, conv_w_a, a_log, dt_bias, a_out_norm_g,
           b_q_norm_g, b_k_norm_g, b_sink, w_out_ab, w_qkv_c, c_q_norm_g, c_k_norm_g, w_out_c, w_ff1, w_ff2):
    bsz, seq, d = x.shape
    n_tok = bsz * seq
    n_rows = n_tok + bsz * BLK
    ntt = n_tok // ROW_TILE
    assert attn_norm_g.shape[0] == 2 and seq % ROW_TILE == 0 and (bsz * BLK) % ROW_TILE == 0
    x2 = x.reshape(n_tok, d)
    meta = jnp.broadcast_to(meta_tokens.astype(x.dtype)[None], (bsz, N_META, d))
    head = jnp.concatenate([jnp.zeros((bsz, FRONT, d), x.dtype), meta], axis=1).reshape(bsz * BLK, d)
    row2 = lambda v: v.astype(F32).reshape(1, -1)

    w = w_in_ab[0]
    qkv_w = w[:, :1536].astype(BF16)
    z_w = w[:, 1536:2048].astype(BF16)
    gate_w = _gate_weight(w[:, 2048:2056], w[:, 2056:2064]).astype(BF16)
    bq_w = w[:, 2064:2576].astype(BF16)
    bkv_w = w[:, 2576:2832].astype(BF16)
    qkv, z, gate_pre, qb, kvb = _ab_proj(x2, head, row2(attn_norm_g[0]), qkv_w, z_w, gate_w, bq_w, bkv_w,
                                         row2(b_q_norm_g[0]), row2(b_k_norm_g[0]))
    ya = _delta(qkv, z, gate_pre, bsz, n_tok, _gate_rows(a_log[0]), _gate_rows(dt_bias[0]),
                conv_w_a[0].astype(F32), row2(a_out_norm_g[0]))
    sink_rows = jnp.broadcast_to(b_sink[0].astype(F32)[:, None] * LOG2E, (B_HEADS, BLK))
    yb = _window(qb, kvb, bsz, n_tok, sink_rows)
    h = _out_mlp(n_rows, ntt, [(x2, head), tuple(ya), yb], w_out_ab[0].astype(BF16), row2(mlp_norm_g[0]),
                 w_ff1[0].astype(BF16), w_ff2[0].astype(BF16))

    w = w_qkv_c[0]
    deint = jnp.concatenate([jnp.arange(0, C_HD, 2), jnp.arange(1, C_HD, 2)])
    perm = lambda wc, nh: wc.reshape(d, nh, C_HD)[:, :, deint].reshape(d, nh * C_HD)
    wq = perm(w[:, :C_HEADS * C_HD], C_HEADS).astype(BF16)
    wk = perm(w[:, C_HEADS * C_HD:(C_HEADS + C_KV) * C_HD], C_KV).astype(BF16)
    wv = w[:, (C_HEADS + C_KV) * C_HD:].astype(BF16)
    cosf, sinf = _rope_tables(seq)
    q, k, v = _c_proj(h, ntt, seq, row2(attn_norm_g[1]), wq, wk, wv,
                      row2(c_q_norm_g[0][deint]), row2(c_k_norm_g[0][deint]), cosf, sinf)
    att = _dense(q, k, v, bsz, n_tok)
    out = _out_mlp(n_tok, ntt, [h, att], w_out_c[0].astype(BF16), row2(mlp_norm_g[1]),
                   w_ff1[1].astype(BF16), w_ff2[1].astype(BF16))
    return out.reshape(bsz, seq, d)
```

```python
import functools
import math

import jax
import jax.numpy as jnp
from jax import lax
from jax.experimental import pallas as pl
from jax.experimental.pallas import tpu as pltpu

F32 = jnp.float32
BF16 = jnp.bfloat16

EPS = 1e-6
N_META = 16
BLK = 128
FRONT = BLK - N_META
GRID_W = 64
ROPE_THETA = 10000.0
A_HEADS, A_DK, A_CONV = 4, 128, 5
B_HEADS, B_KV, B_HD, B_WIN = 8, 2, 64, 128
C_HEADS, C_KV, C_HD = 8, 2, 128
NEG = -1e30
LOG2E = math.log2(math.e)

VMEM_LIMIT = 56 * 1024 * 1024
ROW_TILE = 512
C_ROW_TILE = 256
FF_CHUNK = 512
INV_SQUARINGS = 6
PREP_UNROLL = 11


def _chunk_group(nblk):
    return max(g for g in range(1, PREP_UNROLL + 1) if nblk % g == 0)


def _cparams(*sem):
    return pltpu.CompilerParams(dimension_semantics=sem, vmem_limit_bytes=VMEM_LIMIT)


def _sigmoid(x):
    return 1.0 / (1.0 + jnp.exp(-x))


def _silu(x):
    return x * _sigmoid(x)


def _softplus(x):
    return jnp.maximum(x, 0.0) + jnp.log1p(jnp.exp(-jnp.abs(x)))


def _rms(x, g):
    return x * lax.rsqrt(jnp.mean(x * x, axis=-1, keepdims=True) + EPS) * g


def _dot(a, b):
    return jnp.dot(a, b, preferred_element_type=F32)


def _dot_nt(a, b):
    return lax.dot_general(a, b, (((1,), (1,)), ((), ())), preferred_element_type=F32)


def _slab_spec(n_slabs, index):
    return pl.BlockSpec((n_slabs, ROW_TILE, BLK), lambda i: (0, index(i), 0))


def _pair_specs(tok, head):
    ntt = tok.shape[-2] // ROW_TILE
    index = (lambda i: jnp.minimum(i, ntt - 1), lambda i: jnp.maximum(i - ntt, 0))
    if tok.ndim == 3:
        return [_slab_spec(tok.shape[0], ix) for ix in index]
    flat = lambda ix: pl.BlockSpec((ROW_TILE, tok.shape[1]), lambda i: (ix(i), 0))
    return [flat(ix) for ix in index]


def _rows(v):
    return jnp.concatenate([v[c] for c in range(v.shape[0])], axis=1) if v.ndim == 3 else v


def _pick(tok_ref, head_ref, ntt):
    return _rows(jnp.where(pl.program_id(0) < ntt, tok_ref[...], head_ref[...]))


def _store_slabs(ref, first, value):
    for c in range(value.shape[1] // BLK):
        ref[first + c] = value[:, c * BLK:(c + 1) * BLK]


def _ab_proj_kernel(x_ref, hb_ref, g_ref, wqkv_ref, wz_ref, wg_ref, wq_ref, wkv_ref, bqg_ref, bkg_ref,
                    qkv_ref, z_ref, gate_ref, qb_ref, kvb_ref, *, ntt):
    u = _rms(_pick(x_ref, hb_ref, ntt), g_ref[...]).astype(BF16)
    qb = _dot(u, wq_ref[...])
    kv = _dot(u, wkv_ref[...])
    scale = B_HD ** -0.5 * LOG2E

    def q_heads(lo, hi):
        for hh in range(lo, hi):
            sl = slice(hh * B_HD, (hh + 1) * B_HD)
            qb_ref[:, sl] = (_rms(qb[:, sl], bqg_ref[...]) * scale).astype(BF16)

    _store_slabs(qkv_ref, 0, _dot(u, wqkv_ref[:, 0:512]))
    q_heads(0, 3)
    _store_slabs(qkv_ref, A_HEADS, _dot(u, wqkv_ref[:, 512:1024]))
    q_heads(3, 6)
    _store_slabs(qkv_ref, 2 * A_HEADS, _dot(u, wqkv_ref[:, 1024:1536]))
    q_heads(6, B_HEADS)
    _store_slabs(z_ref, 0, _dot(u, wz_ref[...]))
    for hh in range(B_KV):
        sl = slice(hh * B_HD, (hh + 1) * B_HD)
        kvb_ref[:, sl] = _rms(kv[:, sl], bkg_ref[...]).astype(BF16)
    kvb_ref[:, B_KV * B_HD:] = kv[:, B_KV * B_HD:].astype(BF16)
    _store_slabs(gate_ref, 0, _dot(u, wg_ref[...]))


def _ab_proj(x, hb, g, wqkv, wz, wg, wq, wkv, bqg, bkg):
    d = x.shape[1]
    r = x.shape[0] + hb.shape[0]
    row = lambda n: pl.BlockSpec((ROW_TILE, n), lambda i: (i, 0))
    full = lambda a: pl.BlockSpec(a.shape, lambda i: (0, 0))
    return pl.pallas_call(
        functools.partial(_ab_proj_kernel, ntt=x.shape[0] // ROW_TILE),
        grid=(r // ROW_TILE,),
        in_specs=_pair_specs(x, hb) + [full(g), full(wqkv), full(wz), full(wg), full(wq), full(wkv), full(bqg),
                                       full(bkg)],
        out_specs=[_slab_spec(3 * A_HEADS, lambda i: i), _slab_spec(A_HEADS, lambda i: i),
                   _slab_spec(A_HEADS, lambda i: i), row(512), row(256)],
        out_shape=[jax.ShapeDtypeStruct((3 * A_HEADS, r, BLK), F32), jax.ShapeDtypeStruct((A_HEADS, r, BLK), F32),
                   jax.ShapeDtypeStruct((A_HEADS, r, BLK), F32), jax.ShapeDtypeStruct((r, 512), BF16),
                   jax.ShapeDtypeStruct((r, 256), BF16)],
        compiler_params=_cparams("parallel"),
        name="ab_proj",
    )(x, hb, g, wqkv, wz, wg, wq, wkv, bqg, bkg)


GATE_ROWS = 8


def _split2(x):
    hi = x.astype(BF16)
    return hi, (x - hi.astype(F32)).astype(BF16)


def _split3(x):
    hi = x.astype(BF16)
    r1 = x - hi.astype(F32)
    mid = r1.astype(BF16)
    lo = (r1 - mid.astype(F32)).astype(BF16)
    return hi, mid, lo


def _seq_block(tok_ref, head_ref, n):
    tok = tok_ref[pl.ds(pl.multiple_of(jnp.maximum(n - 1, 0) * BLK, BLK), BLK), :]
    return jnp.where(n == 0, head_ref[...], tok)


def _seq_views(n_tok, seq, off):
    head0 = n_tok // BLK
    return [pl.BlockSpec((None, seq, BLK), lambda i, j: (j + off, i, 0)),
            pl.BlockSpec((None, BLK, BLK), lambda i, j: (j + off, head0 + i, 0))]


def _gate_block(pre, n, neg_a, dtb):
    nr = GATE_ROWS
    t = pre.T[0:nr, :]
    ri = lax.broadcasted_iota(jnp.int32, (BLK, BLK), 0)
    ci = lax.broadcasted_iota(jnp.int32, (BLK, BLK), 1)
    role = ri[0:nr, :]
    live = (ci[0:nr, :] + n * BLK) >= FRONT
    beta = jnp.where(live, _sigmoid(t), 0.0)
    g = jnp.where(live, neg_a * _softplus(t + dtb), 0.0)
    parts = [p.astype(F32) for p in _split3(g)]
    parts = jnp.concatenate(parts + [jnp.zeros((BLK - 3 * nr, BLK), F32)], axis=0).astype(BF16)
    tri = jnp.concatenate([(ri <= ci).astype(BF16), (ri >= ci).astype(BF16)], axis=1)
    sums = _dot(parts, tri)
    both = sums[0:nr] + sums[nr:2 * nr] + sums[2 * nr:3 * nr]
    pre_sum, suf_sum = both[:, :BLK], both[:, BLK:]
    tot = pre_sum + suf_sum - g
    row = jnp.where(role < 2, beta, jnp.where(role == 2, pre_sum, jnp.where(role == 3, suf_sum, tot)))
    col = jnp.concatenate([row, jnp.zeros((BLK - nr, BLK), F32)], axis=0).T
    return col, row


def _delta_kernel(q_ref, qh_ref, k_ref, kh_ref, v_ref, vh_ref, z_ref, zh_ref, gp_ref, gph_ref, alog_ref, dtb_ref,
                  cwq_ref, cwk_ref, cwv_ref, og_ref, y_ref, yh_ref, sadd_s, smul_s, o_s, omul_s, gl_s):
    seq = q_ref.shape[0]
    nblk = seq // BLK + 1
    grp = _chunk_group(nblk)
    neg_a = -jnp.exp(alog_ref[...])
    dtb = dtb_ref[...]
    ri = lax.broadcasted_iota(jnp.int32, (BLK, BLK), 0)
    ci = lax.broadcasted_iota(jnp.int32, (BLK, BLK), 1)
    eye = (ri == ci).astype(F32)
    incl = (ri >= ci, ri <= ci)
    strict = (ri > ci, ri < ci)

    hw = A_CONV // 2

    def conv_silu(ref, head_ref, w_ref, n, maybe_edge):
        base = jnp.clip((n - 1) * BLK, hw, seq - BLK - hw)
        acc = ref[pl.ds(base - hw, BLK), :] * w_ref[0:1, :]
        for j in range(1, A_CONV):
            acc = acc + ref[pl.ds(base - hw + j, BLK), :] * w_ref[j:j + 1, :]
        if maybe_edge:
            cur = _seq_block(ref, head_ref, n)
            tok_prev = ref[pl.ds(pl.multiple_of(jnp.maximum((n - 1) * BLK - 8, 0), 8), 8), :]
            prev = jnp.where(n == 0, 0.0, jnp.where(n == 1, head_ref[BLK - 8:, :], tok_prev))
            nxt = ref[pl.ds(pl.multiple_of(jnp.minimum(n * BLK, seq - 8), 8), 8), :]
            nxt = jnp.where(n < nblk - 1, nxt, 0.0)
            win = jnp.concatenate([prev, cur, nxt], axis=0)
            edge = win[8 - hw:8 - hw + BLK, :] * w_ref[0:1, :]
            for j in range(1, A_CONV):
                edge = edge + win[8 - hw + j:8 - hw + j + BLK, :] * w_ref[j:j + 1, :]
            acc = jnp.where((n <= 1) | (n == nblk - 1), edge, acc)
        return _silu(acc)

    def l2n(x):
        return x * lax.rsqrt(jnp.sum(x * x, axis=-1, keepdims=True) + EPS)

    edge_slots = {0 % grp, 1 % grp, (nblk - 1) % grp}

    def chunk_inputs(n, slot):
        rows = pl.ds(pl.multiple_of(n * BLK, BLK), BLK)
        live = (ri[:, 0:1] + n * BLK) >= FRONT
        edge = slot in edge_slots
        qn = jnp.where(live, l2n(conv_silu(q_ref, qh_ref, cwq_ref, n, edge)) * (A_DK ** -0.5), 0.0)
        kn = jnp.where(live, l2n(conv_silu(k_ref, kh_ref, cwk_ref, n, edge)), 0.0)
        vv = jnp.where(live, conv_silu(v_ref, vh_ref, cwv_ref, n, edge), 0.0)
        kn16 = kn.astype(BF16)
        kq = _dot_nt(jnp.concatenate([kn16, qn.astype(BF16)], axis=0), kn16)
        bg, gt = _gate_block(_seq_block(gp_ref, gph_ref, n), n, neg_a, dtb)
        return dict(n=n, rows=rows, qn=qn, kn=kn, vv=vv, kk=kq[:BLK], qk=kq[BLK:], bg=bg, gt=gt)

    def chain_setup(c, d):
        bg, gt = c["bg"], c["gt"]
        beta, ccol, tot = bg[:, d:d + 1], bg[:, 2 + d:3 + d], bg[:, 4 + d:5 + d]
        crow = gt[2 + d:3 + d, :]
        dec = jnp.exp(jnp.where(incl[d], ccol - crow, NEG))
        a = jnp.where(strict[d], beta * c["kk"] * dec, 0.0)
        return dict(c=c, d=d, beta=beta, ccol=ccol, tot=tot, dec=dec, a=a, t=eye - a, x=a.astype(BF16))

    def prep(g, carry):
        chunks = [chunk_inputs(g * grp + j, j) for j in range(grp)]
        chains = [chain_setup(c, d) for c in chunks for d in range(2)]
        zero = jnp.zeros((BLK, BLK), BF16)

        def blockdiag(xp):
            return jnp.concatenate([jnp.concatenate([xp[:, :BLK], zero], axis=1),
                                    jnp.concatenate([zero, xp[:, BLK:]], axis=1)], axis=0)

        pairs = [(chains[2 * j], chains[2 * j + 1]) for j in range(grp)]
        xps = [jnp.concatenate([f["x"], b["x"]], axis=1) for f, b in pairs]
        tps = [jnp.concatenate([f["t"], b["t"]], axis=1) for f, b in pairs]
        for _ in range(INV_SQUARINGS):
            xps = [_dot(xp, blockdiag(xp)).astype(BF16) for xp in xps]
            txs = [_dot(tp.astype(BF16), blockdiag(xp)) for tp, xp in zip(tps, xps)]
            tps = [tp + tx for tp, tx in zip(tps, txs)]
        for (f, b), tp in zip(pairs, tps):
            f["t"], b["t"] = tp[:, :BLK], tp[:, BLK:]
        for ch in chains:
            c = ch["c"]
            ch["ec"] = jnp.exp(ch["ccol"])
            ch["rhs"] = jnp.concatenate([ch["beta"] * c["vv"], ch["beta"] * c["kn"] * ch["ec"]], axis=1)
            ch["t16"] = ch["t"].astype(BF16)
        x0s = [_dot(ch["t16"], ch["rhs"].astype(BF16)) for ch in chains]
        res = []
        for ch, x0 in zip(chains, x0s):
            ah, al = _split2(ch["a"])
            xh, xl = _split2(x0)
            ax = _dot(jnp.concatenate([ah, al], axis=1), jnp.concatenate([xh, xh], axis=0)) + _dot(ah, xl)
            res.append((ch["rhs"] - x0 - ax).astype(BF16))
        uws = [(x0 + _dot(ch["t16"], e)).astype(BF16) for ch, x0, e in zip(chains, x0s, res)]
        kuws = [_dot((ch["c"]["kn"] * jnp.exp(ch["tot"] - ch["ccol"])).T.astype(BF16), uw)
                for ch, uw in zip(chains, uws)]
        quws = [_dot((ch["c"]["qk"] * ch["dec"]).astype(BF16), uw) for ch, uw in zip(chains, uws)]
        for ch, kuw, quw in zip(chains, kuws, quws):
            c, d = ch["c"], ch["d"]
            rows = c["rows"]
            sadd_s[d, rows, :] = kuw[:, :BLK]
            smul_s[d, rows, :] = (-kuw[:, BLK:]).astype(BF16)
            o_s[d, rows, :] = quw[:, :BLK]
            omul_s[d, rows, :] = (c["qn"] * ch["ec"] - quw[:, BLK:]).astype(BF16)
            gl_s[d * nblk + c["n"]] = jnp.broadcast_to(jnp.exp(ch["tot"]), (BLK, BLK))[0:8, :]
        return carry

    lax.fori_loop(0, nblk // grp, prep, 0)

    def scan_step(d, n, s):
        rows = pl.ds(pl.multiple_of(n * BLK, BLK), BLK)
        both = _dot(jnp.concatenate([smul_s[d, rows, :], omul_s[d, rows, :]], axis=0), s.astype(BF16))
        o_s[d, rows, :] = o_s[d, rows, :] + both[BLK:]
        return s * gl_s[d * nblk + n][0:1, :] + both[:BLK] + sadd_s[d, rows, :]

    def scan(i, carry):
        sf, sb = carry
        sf = scan_step(0, i, sf)
        sb = scan_step(1, nblk - 1 - i, sb)
        return sf, sb

    s0 = jnp.zeros((BLK, BLK), F32)
    lax.fori_loop(0, nblk, scan, (s0, s0))

    def gated(rows, z):
        o = o_s[0, rows, :] + o_s[1, rows, :]
        return (_rms(o, og_ref[...]) * _silu(z)).astype(y_ref.dtype)

    yh_ref[...] = gated(slice(0, BLK), zh_ref[...])

    def finish(n, carry):
        tok_rows = pl.ds(pl.multiple_of((n - 1) * BLK, BLK), BLK)
        y_ref[tok_rows, :] = gated(pl.ds(pl.multiple_of(n * BLK, BLK), BLK), z_ref[tok_rows, :])
        return carry

    lax.fori_loop(1, nblk, finish, 0, unroll=4 if (nblk - 1) % 4 == 0 else 1)


def _delta(qkv, z, gate_pre, b, n_tok, alog_rows, dtb_rows, conv_w, o_gain):
    seq = n_tok // b
    lp = seq + BLK
    nblk = lp // BLK
    cw = lambda off: pl.BlockSpec((A_CONV, BLK), lambda i, j: (0, j + off))
    views = lambda off: _seq_views(n_tok, seq, off)
    per_head = pl.BlockSpec((None, GATE_ROWS, BLK), lambda i, j: (j, 0, 0))
    return pl.pallas_call(
        _delta_kernel,
        grid=(b, A_HEADS),
        in_specs=views(0) + views(A_HEADS) + views(2 * A_HEADS) + views(0) + views(0)
                 + [per_head, per_head, cw(0), cw(A_HEADS), cw(2 * A_HEADS),
                    pl.BlockSpec((1, BLK), lambda i, j: (0, 0))],
        out_specs=[pl.BlockSpec((None, seq, BLK), lambda i, j: (j, i, 0)),
                   pl.BlockSpec((None, BLK, BLK), lambda i, j: (j, i, 0))],
        out_shape=[jax.ShapeDtypeStruct((A_HEADS, n_tok, BLK), BF16),
                   jax.ShapeDtypeStruct((A_HEADS, b * BLK, BLK), BF16)],
        scratch_shapes=[pltpu.VMEM((2, lp, BLK), F32), pltpu.VMEM((2, lp, BLK), BF16),
                        pltpu.VMEM((2, lp, BLK), F32), pltpu.VMEM((2, lp, BLK), BF16),
                        pltpu.VMEM((2 * nblk, 8, BLK), F32)],
        compiler_params=_cparams("parallel", "parallel"),
        name="delta_mixer",
    )(qkv, qkv, qkv, qkv, qkv, qkv, z, z, gate_pre, gate_pre, alog_rows, dtb_rows, conv_w, conv_w, conv_w, o_gain)


def _window_kernel(q_ref, kp_ref, kc_ref, kn_ref, km_ref, bias_ref, sink_ref, y_ref):
    i = pl.program_id(1)
    nblk = pl.num_programs(1)
    grp = B_HEADS // B_KV
    nk = 4 * BLK
    c = lax.broadcasted_iota(jnp.int32, (1, nk), 1)
    kblk = i - 1 + (c >> 7)
    edge = jnp.where((c >= 3 * BLK) | ((kblk >= 1) & (kblk < nblk)), 0.0, NEG)
    q = q_ref[...]
    kvs = (kp_ref[...], kc_ref[...], kn_ref[...], km_ref[...])
    ones = jnp.ones((nk, 2 * B_HD), BF16)
    lane = lax.broadcasted_iota(jnp.int32, (BLK, 2 * B_HD), 1)
    s4s, vexts = [], []
    for kvh in range(B_KV):
        ks = jnp.concatenate([t[:, kvh * B_HD:(kvh + 1) * B_HD] for t in kvs], axis=0)
        vs = jnp.concatenate([t[:, (B_KV + kvh) * B_HD:(B_KV + kvh + 1) * B_HD] for t in kvs], axis=0)
        q4 = jnp.concatenate([q[:, hh * B_HD:(hh + 1) * B_HD] for hh in range(kvh * grp, (kvh + 1) * grp)],
                             axis=0)
        s4s.append(_dot_nt(q4, ks))
        vexts.append(jnp.concatenate([vs, vs, ones], axis=1))
    ms = []
    pvs = []
    for kvh in range(B_KV):
        ps = []
        for gi in range(grp):
            hh = kvh * grp + gi
            s = s4s[kvh][gi * BLK:(gi + 1) * BLK] + bias_ref[hh] + edge
            m = jnp.maximum(jnp.max(s, axis=-1, keepdims=True), sink_ref[hh:hh + 1, 0:1])
            ps.append(jnp.exp2(s - m).astype(BF16))
            ms.append(m)
        pvs.append(_dot(jnp.concatenate(ps, axis=0), vexts[kvh]))
    outs = []
    for hh in range(B_HEADS):
        kvh, gi = divmod(hh, grp)
        o = pvs[kvh][gi * BLK:(gi + 1) * BLK]
        den = o[:, 2 * B_HD:] + jnp.exp2(sink_ref[hh:hh + 1, 0:1] - ms[hh])
        outs.append(o[:, :2 * B_HD] / den)
    for j in range(B_HEADS // 2):
        pair = jnp.where(lane < B_HD, outs[2 * j], outs[2 * j + 1])
        y_ref[:, 2 * j * B_HD:(2 * j + 2) * B_HD] = pair.astype(y_ref.dtype)

    @pl.when(i == 0)
    def _():
        rr = lax.broadcasted_iota(jnp.int32, y_ref.shape, 0)
        y_ref[...] = jnp.where(rr >= FRONT, y_ref[...], 0).astype(y_ref.dtype)


def _window_bias():
    r = jnp.arange(BLK)[:, None]
    c = jnp.arange(4 * BLK)[None, :]
    dist = jnp.abs(BLK + r - c)
    slopes = jnp.exp2(-8.0 * (jnp.arange(B_HEADS, dtype=F32) + 1.0) / B_HEADS)
    band = (c < 3 * BLK) & (dist <= B_WIN)
    alibi = -slopes[:, None, None] * dist.astype(F32)[None] * LOG2E
    rest = jnp.where(c >= 3 * BLK + FRONT, 0.0, NEG)
    return jnp.where(band[None], alibi, rest[None]).astype(F32)


def _window(qb, kvb, b, n_tok, sink_rows):
    nblk = n_tok // b // BLK + 1
    bias = _window_bias()
    head0 = n_tok // BLK

    def blk(i, j):
        return jnp.where(j == 0, head0 + i, i * (nblk - 1) + j - 1)

    kv = lambda f: pl.BlockSpec((BLK, 2 * B_KV * B_HD), f)
    return pl.pallas_call(
        _window_kernel,
        grid=(b, nblk),
        in_specs=[pl.BlockSpec((BLK, B_HEADS * B_HD), lambda i, j: (blk(i, j), 0)),
                  kv(lambda i, j: (blk(i, jnp.maximum(j - 1, 0)), 0)),
                  kv(lambda i, j: (blk(i, j), 0)),
                  kv(lambda i, j: (blk(i, jnp.minimum(j + 1, nblk - 1)), 0)),
                  kv(lambda i, j: (head0 + i, 0)),
                  pl.BlockSpec(bias.shape, lambda i, j: (0, 0, 0)),
                  pl.BlockSpec((B_HEADS, BLK), lambda i, j: (0, 0))],
        out_specs=pl.BlockSpec((BLK, B_HEADS * B_HD), lambda i, j: (blk(i, j), 0)),
        out_shape=jax.ShapeDtypeStruct((qb.shape[0], B_HEADS * B_HD), BF16),
        compiler_params=_cparams("parallel", "parallel"),
        name="window_mixer",
    )(qb, kvb, kvb, kvb, kvb, bias, sink_rows)


def _out_mlp_kernel(*refs, arity, ntt):
    vals, pos = [], 0
    for a in arity:
        vals.append(refs[pos][...] if a == 1 else _pick(refs[pos], refs[pos + 1], ntt))
        pos += a
    wo_ref, g_ref, w1_ref, w2_ref, o_ref = refs[pos:]
    mix = jnp.concatenate(vals[1:], axis=1)
    h = vals[0] + _dot(mix, wo_ref[...])
    u = _rms(h, g_ref[...]).astype(BF16)
    dff = w1_ref.shape[1]
    acc = h
    for c in range(dff // FF_CHUNK):
        sl = slice(c * FF_CHUNK, (c + 1) * FF_CHUNK)
        a = jnp.maximum(_dot(u, w1_ref[:, sl]), 0.0)
        acc = acc + _dot((a * a).astype(BF16), w2_ref[sl, :])
    o_ref[...] = acc


def _out_mlp(rows_out, ntt, tensors, wo, g, w1, w2):
    d = wo.shape[1]
    row = lambda n: pl.BlockSpec((ROW_TILE, n), lambda i: (i, 0))
    full = lambda a: pl.BlockSpec(a.shape, lambda i: (0, 0))
    specs, args, arity = [], [], []
    for t in tensors:
        if isinstance(t, tuple):
            specs += _pair_specs(t[0], t[1])
            args += list(t)
            arity.append(2)
        else:
            specs.append(row(t.shape[1]))
            args.append(t)
            arity.append(1)
    return pl.pallas_call(
        functools.partial(_out_mlp_kernel, arity=tuple(arity), ntt=ntt),
        grid=(rows_out // ROW_TILE,),
        in_specs=specs + [full(wo), full(g), full(w1), full(w2)],
        out_specs=row(d),
        out_shape=jax.ShapeDtypeStruct((rows_out, d), F32),
        compiler_params=_cparams("parallel"),
        name="out_mlp",
    )(*args, wo, g, w1, w2)


def _c_proj_kernel(h_ref, g_ref, wq_ref, wk_ref, wv_ref, qg_ref, kg_ref, cos_ref, sin_ref,
                   q_ref, k_ref, v_ref):
    u = _rms(h_ref[...], g_ref[...]).astype(BF16)
    cosf = cos_ref[...]
    sinf = sin_ref[...]
    half = C_HD // 2

    def norm_rope(x, gain):
        x = _rms(x, gain)
        swapped = jnp.concatenate([x[:, half:], x[:, :half]], axis=1)
        return x * cosf + swapped * sinf

    k = _dot(u, wk_ref[...])
    half_w = C_HEADS * C_HD // 2
    q_lo = _dot(u, wq_ref[:, :half_w])
    for hh in range(C_KV):
        sl = slice(hh * C_HD, (hh + 1) * C_HD)
        k_ref[:, sl] = norm_rope(k[:, sl], kg_ref[...]).astype(BF16)
    q_hi = _dot(u, wq_ref[:, half_w:])
    for hh in range(C_HEADS // 2):
        sl = slice(hh * C_HD, (hh + 1) * C_HD)
        q_ref[:, sl] = (norm_rope(q_lo[:, sl], qg_ref[...]) * (C_HD ** -0.5 * LOG2E)).astype(BF16)
    v_ref[...] = _dot(u, wv_ref[...]).astype(BF16)
    for hh in range(C_HEADS // 2):
        sl = slice(hh * C_HD, (hh + 1) * C_HD)
        q_ref[:, half_w + hh * C_HD:half_w + (hh + 1) * C_HD] = (
            norm_rope(q_hi[:, sl], qg_ref[...]) * (C_HD ** -0.5 * LOG2E)).astype(BF16)


def _c_proj(h, ntt, seq, g, wq, wk, wv, qg, kg, cosf, sinf):
    r, d = h.shape
    tm = C_ROW_TILE
    per_seq = seq // tm
    n_tok_tiles = ntt * (ROW_TILE // tm)
    row = lambda n: pl.BlockSpec((tm, n), lambda i: (i, 0))
    full = lambda a: pl.BlockSpec(a.shape, lambda i: (0, 0))
    pos = pl.BlockSpec((tm, C_HD), lambda i: (jnp.where(i < n_tok_tiles, i % per_seq, per_seq), 0))
    return pl.pallas_call(
        _c_proj_kernel,
        grid=(r // tm,),
        in_specs=[row(d), full(g), full(wq), full(wk), full(wv), full(qg), full(kg), pos, pos],
        out_specs=[row(C_HEADS * C_HD), row(C_KV * C_HD), row(C_KV * C_HD)],
        out_shape=[jax.ShapeDtypeStruct((r, C_HEADS * C_HD), BF16),
                   jax.ShapeDtypeStruct((r, C_KV * C_HD), BF16),
                   jax.ShapeDtypeStruct((r, C_KV * C_HD), BF16)],
        compiler_params=_cparams("parallel"),
        name="c_proj",
    )(h, g, wq, wk, wv, qg, kg, cosf, sinf)


ATT_TK = 2048
ATT_QB = 4


def _dense_kernel(q_ref, k_ref, kh_ref, v_ref, vh_ref, y_ref, *scratch):
    grp = C_HEADS // C_KV
    nkb = k_ref.shape[0] // ATT_TK
    nq = ATT_QB
    sa_s, sb_s, acc_s = scratch[:nq], scratch[nq:2 * nq], scratch[2 * nq:]
    qs = [jnp.concatenate([q_ref[c * BLK:(c + 1) * BLK, g * C_HD:(g + 1) * C_HD] for g in range(grp)], axis=0)
          for c in range(nq)]
    m_rows = grp * BLK

    def keys(t):
        return pl.ds(pl.multiple_of(t * ATT_TK, ATT_TK), ATT_TK)

    def v_ones(v):
        return jnp.concatenate([v, jnp.ones(v.shape, BF16)], axis=1)

    def scores(t, s_refs):
        kt = k_ref[keys(t), :]
        for c in range(nq):
            s_refs[c][...] = _dot_nt(qs[c], kt)

    def step(t, ms, s_refs):
        vt = v_ones(v_ref[keys(t), :])
        out = []
        for c in range(nq):
            s = s_refs[c][...]
            m_new = jnp.maximum(ms[c], jnp.max(s, axis=-1, keepdims=True))
            p = jnp.exp2(s - m_new).astype(BF16)
            acc_s[c][...] = jnp.exp2(ms[c] - m_new) * acc_s[c][...] + _dot(p, vt)
            out.append(m_new)
        return out

    scores(0, sa_s)
    k0 = kh_ref[...]
    v0 = v_ones(vh_ref[...])
    kc = lax.broadcasted_iota(jnp.int32, (m_rows, BLK), 1)
    ms = []
    for c in range(nq):
        s0 = jnp.where(kc >= FRONT, _dot_nt(qs[c], k0), NEG)
        m = jnp.max(s0, axis=-1, keepdims=True)
        acc_s[c][...] = _dot(jnp.exp2(s0 - m).astype(BF16), v0)
        ms.append(m)

    def body(j, ms):
        scores(2 * j + 1, sb_s)
        ms = step(2 * j, ms, sa_s)
        scores(2 * j + 2, sa_s)
        return step(2 * j + 1, ms, sb_s)

    ms = lax.fori_loop(0, nkb // 2 - 1, body, ms)
    scores(nkb - 1, sb_s)
    ms = step(nkb - 2, ms, sa_s)
    ms = step(nkb - 1, ms, sb_s)
    for c in range(nq):
        acc = acc_s[c][...]
        o = acc[:, :C_HD] / acc[:, C_HD:C_HD + 1]
        for g in range(grp):
            y_ref[c * BLK:(c + 1) * BLK, g * C_HD:(g + 1) * C_HD] = o[g * BLK:(g + 1) * BLK, :].astype(y_ref.dtype)


def _dense(q, k, v, b, n_tok):
    seq = n_tok // b
    grp = C_HEADS // C_KV
    tq = ATT_QB * BLK
    assert seq % tq == 0 and seq % (2 * ATT_TK) == 0
    per_seq = seq // tq
    head0 = n_tok // BLK
    score = pltpu.VMEM((grp * BLK, ATT_TK), F32)
    tok = pl.BlockSpec((seq, C_HD), lambda i, j, t: (i, j))
    head = pl.BlockSpec((BLK, C_HD), lambda i, j, t: (head0 + i, j))
    return pl.pallas_call(
        _dense_kernel,
        grid=(b, C_KV, per_seq),
        in_specs=[pl.BlockSpec((tq, grp * C_HD), lambda i, j, t: (i * per_seq + t, j)), tok, head, tok, head],
        out_specs=pl.BlockSpec((tq, grp * C_HD), lambda i, j, t: (i * per_seq + t, j)),
        out_shape=jax.ShapeDtypeStruct((n_tok, C_HEADS * C_HD), BF16),
        scratch_shapes=[score] * (2 * ATT_QB) + [pltpu.VMEM((grp * BLK, 2 * C_HD), F32)] * ATT_QB,
        compiler_params=_cparams("parallel", "parallel", "arbitrary"),
        name="dense_mixer",
    )(q, k, k, v, v)


def _rope_tables(seq):
    rows = seq // GRID_W
    row = jnp.repeat(jnp.arange(rows), GRID_W)
    col = jnp.tile(jnp.arange(GRID_W), rows)
    head = jnp.tile(jnp.concatenate([jnp.zeros((FRONT,), jnp.int32), jnp.arange(N_META) - N_META]), ROW_TILE // BLK)
    row = jnp.concatenate([row, head]).astype(F32)
    col = jnp.concatenate([col, head]).astype(F32)
    axis_dim = C_HD // 2
    freqs = ROPE_THETA ** (-jnp.arange(0, axis_dim, 2, dtype=F32) / axis_dim)
    ang = jnp.concatenate([row[:, None] * freqs, col[:, None] * freqs], axis=-1)
    cos, sin = jnp.cos(ang), jnp.sin(ang)
    return jnp.concatenate([cos, cos], axis=-1), jnp.concatenate([-sin, sin], axis=-1)


def _gate_weight(w_b, w_a):
    d = w_b.shape[0]
    w_b = w_b.reshape(d, 2, A_HEADS)
    w_a = w_a.reshape(d, 2, A_HEADS)
    per_head = jnp.concatenate([w_b, w_a, w_a], axis=1)
    per_head = jnp.transpose(per_head, (0, 2, 1))
    per_head = jnp.pad(per_head, ((0, 0), (0, 0), (0, BLK - 6)))
    return per_head.reshape(d, A_HEADS * BLK)


def _gate_rows(p):
    t = jnp.transpose(p.astype(F32), (1, 0))
    rows = jnp.concatenate([jnp.zeros_like(t), t, t], axis=1)
    rows = jnp.pad(rows, ((0, 0), (0, GATE_ROWS - 6)))
    return jnp.broadcast_to(rows[:, :, None], (A_HEADS, GATE_ROWS, BLK))


def kernel(x, meta_tokens, attn_norm_g, mlp_norm_g, w_in_ab, conv_w_a, a_log, dt_bias, a_out_norm_g,
           b_q_norm_g, b_k_norm_g, b_sink, w_out_ab, w_qkv_c, c_q_norm_g, c_k_norm_g, w_out_c, w_ff1, w_ff2):
    bsz, seq, d = x.shape
    n_tok = bsz * seq
    n_rows = n_tok + bsz * BLK
    ntt = n_tok // ROW_TILE
    assert attn_norm_g.shape[0] == 2 and seq % ROW_TILE == 0 and (bsz * BLK) % ROW_TILE == 0
    x2 = x.reshape(n_tok, d)
    meta = jnp.broadcast_to(meta_tokens.astype(x.dtype)[None], (bsz, N_META, d))
    head = jnp.concatenate([jnp.zeros((bsz, FRONT, d), x.dtype), meta], axis=1).reshape(bsz * BLK, d)
    row2 = lambda v: v.astype(F32).reshape(1, -1)

    w = w_in_ab[0]
    qkv_w = w[:, :1536].astype(BF16)
    z_w = w[:, 1536:2048].astype(BF16)
    gate_w = _gate_weight(w[:, 2048:2056], w[:, 2056:2064]).astype(BF16)
    bq_w = w[:, 2064:2576].astype(BF16)
    bkv_w = w[:, 2576:2832].astype(BF16)
    qkv, z, gate_pre, qb, kvb = _ab_proj(x2, head, row2(attn_norm_g[0]), qkv_w, z_w, gate_w, bq_w, bkv_w,
                                         row2(b_q_norm_g[0]), row2(b_k_norm_g[0]))
    ya = _delta(qkv, z, gate_pre, bsz, n_tok, _gate_rows(a_log[0]), _gate_rows(dt_bias[0]),
                conv_w_a[0].astype(F32), row2(a_out_norm_g[0]))
    sink_rows = jnp.broadcast_to(b_sink[0].astype(F32)[:, None] * LOG2E, (B_HEADS, BLK))
    yb = _window(qb, kvb, bsz, n_tok, sink_rows)
    h = _out_mlp(n_rows, ntt, [(x2, head), tuple(ya), yb], w_out_ab[0].astype(BF16), row2(mlp_norm_g[0]),
                 w_ff1[0].astype(BF16), w_ff2[0].astype(BF16))

    w = w_qkv_c[0]
    deint = jnp.concatenate([jnp.arange(0, C_HD, 2), jnp.arange(1, C_HD, 2)])
    perm = lambda wc, nh: wc.reshape(d, nh, C_HD)[:, :, deint].reshape(d, nh * C_HD)
    wq = perm(w[:, :C_HEADS * C_HD], C_HEADS).astype(BF16)
    wk = perm(w[:, C_HEADS * C_HD:(C_HEADS + C_KV) * C_HD], C_KV).astype(BF16)
    wv = w[:, (C_HEADS + C_KV) * C_HD:].astype(BF16)
    cosf, sinf = _rope_tables(seq)
    q, k, v = _c_proj(h, ntt, seq, row2(attn_norm_g[1]), wq, wk, wv,
                      row2(c_q_norm_g[0][deint]), row2(c_k_norm_g[0][deint]), cosf, sinf)
    att = _dense(q, k, v, bsz, n_tok)
    out = _out_mlp(n_tok, ntt, [h, att], w_out_c[0].astype(BF16), row2(mlp_norm_g[1]),
                   w_ff1[1].astype(BF16), w_ff2[1].astype(BF16))
    return out.reshape(bsz, seq, d)
```

```python
import functools
import math

import jax
import jax.numpy as jnp
from jax import lax
from jax.experimental import pallas as pl
from jax.experimental.pallas import tpu as pltpu

F32 = jnp.float32
BF16 = jnp.bfloat16

EPS = 1e-6
N_META = 16
BLK = 128
FRONT = BLK - N_META
GRID_W = 64
ROPE_THETA = 10000.0
A_HEADS, A_DK, A_CONV = 4, 128, 5
B_HEADS, B_KV, B_HD, B_WIN = 8, 2, 64, 128
C_HEADS, C_KV, C_HD = 8, 2, 128
NEG = -1e30
LOG2E = math.log2(math.e)

VMEM_LIMIT = 56 * 1024 * 1024
ROW_TILE = 512
C_ROW_TILE = 256
FF_CHUNK = 512
INV_SQUARINGS = BLK.bit_length() - 2
PREP_UNROLL = 11


def _chunk_group(nblk):
    return max(g for g in range(1, PREP_UNROLL + 1) if nblk % g == 0)


def _cparams(*sem):
    return pltpu.CompilerParams(dimension_semantics=sem, vmem_limit_bytes=VMEM_LIMIT)


def _sigmoid(x):
    return 1.0 / (1.0 + jnp.exp(-x))


def _silu(x):
    return x * _sigmoid(x)


def _softplus(x):
    return jnp.maximum(x, 0.0) + jnp.log1p(jnp.exp(-jnp.abs(x)))


def _rms(x, g):
    return x * lax.rsqrt(jnp.mean(x * x, axis=-1, keepdims=True) + EPS) * g


def _dot(a, b):
    return jnp.dot(a, b, preferred_element_type=F32)


def _dot_nt(a, b):
    return lax.dot_general(a, b, (((1,), (1,)), ((), ())), preferred_element_type=F32)


def _slab_spec(n_slabs, index):
    return pl.BlockSpec((n_slabs, ROW_TILE, BLK), lambda i: (0, index(i), 0))


def _pair_specs(tok, head):
    ntt = tok.shape[-2] // ROW_TILE
    index = (lambda i: jnp.minimum(i, ntt - 1), lambda i: jnp.maximum(i - ntt, 0))
    if tok.ndim == 3:
        return [_slab_spec(tok.shape[0], ix) for ix in index]
    flat = lambda ix: pl.BlockSpec((ROW_TILE, tok.shape[1]), lambda i: (ix(i), 0))
    return [flat(ix) for ix in index]


def _rows(v):
    return jnp.concatenate([v[c] for c in range(v.shape[0])], axis=1) if v.ndim == 3 else v


def _pick(tok_ref, head_ref, ntt):
    return _rows(jnp.where(pl.program_id(0) < ntt, tok_ref[...], head_ref[...]))


def _store_slabs(ref, first, value):
    for c in range(value.shape[1] // BLK):
        ref[first + c] = value[:, c * BLK:(c + 1) * BLK]


def _ab_proj_kernel(x_ref, hb_ref, g_ref, wqkv_ref, wz_ref, wg_ref, wq_ref, wkv_ref, bqg_ref, bkg_ref,
                    qkv_ref, z_ref, gate_ref, qb_ref, kvb_ref, *, ntt):
    u = _rms(_pick(x_ref, hb_ref, ntt), g_ref[...]).astype(BF16)
    qb = _dot(u, wq_ref[...])
    kv = _dot(u, wkv_ref[...])
    scale = B_HD ** -0.5 * LOG2E

    def q_heads(lo, hi):
        for hh in range(lo, hi):
            sl = slice(hh * B_HD, (hh + 1) * B_HD)
            qb_ref[:, sl] = (_rms(qb[:, sl], bqg_ref[...]) * scale).astype(BF16)

    _store_slabs(qkv_ref, 0, _dot(u, wqkv_ref[:, 0:512]))
    q_heads(0, 3)
    _store_slabs(qkv_ref, A_HEADS, _dot(u, wqkv_ref[:, 512:1024]))
    q_heads(3, 6)
    _store_slabs(qkv_ref, 2 * A_HEADS, _dot(u, wqkv_ref[:, 1024:1536]))
    q_heads(6, B_HEADS)
    _store_slabs(z_ref, 0, _dot(u, wz_ref[...]))
    for hh in range(B_KV):
        sl = slice(hh * B_HD, (hh + 1) * B_HD)
        kvb_ref[:, sl] = _rms(kv[:, sl], bkg_ref[...]).astype(BF16)
    kvb_ref[:, B_KV * B_HD:] = kv[:, B_KV * B_HD:].astype(BF16)
    _store_slabs(gate_ref, 0, _dot(u, wg_ref[...]))


def _ab_proj(x, hb, g, wqkv, wz, wg, wq, wkv, bqg, bkg):
    r = x.shape[0] + hb.shape[0]
    row = lambda n: pl.BlockSpec((ROW_TILE, n), lambda i: (i, 0))
    full = lambda a: pl.BlockSpec(a.shape, lambda i: (0, 0))
    return pl.pallas_call(
        functools.partial(_ab_proj_kernel, ntt=x.shape[0] // ROW_TILE),
        grid=(r // ROW_TILE,),
        in_specs=_pair_specs(x, hb) + [full(g), full(wqkv), full(wz), full(wg), full(wq), full(wkv), full(bqg),
                                       full(bkg)],
        out_specs=[_slab_spec(3 * A_HEADS, lambda i: i), _slab_spec(A_HEADS, lambda i: i),
                   _slab_spec(A_HEADS, lambda i: i), row(512), row(256)],
        out_shape=[jax.ShapeDtypeStruct((3 * A_HEADS, r, BLK), F32), jax.ShapeDtypeStruct((A_HEADS, r, BLK), F32),
                   jax.ShapeDtypeStruct((A_HEADS, r, BLK), F32), jax.ShapeDtypeStruct((r, 512), BF16),
                   jax.ShapeDtypeStruct((r, 256), BF16)],
        compiler_params=_cparams("parallel"),
        name="ab_proj",
    )(x, hb, g, wqkv, wz, wg, wq, wkv, bqg, bkg)


GATE_ROWS = 8


def _split3(x):
    hi = x.astype(BF16)
    r1 = x - hi.astype(F32)
    mid = r1.astype(BF16)
    lo = (r1 - mid.astype(F32)).astype(BF16)
    return hi, mid, lo


def _seq_block(tok_ref, head_ref, n):
    tok = tok_ref[pl.ds(pl.multiple_of(jnp.maximum(n - 1, 0) * BLK, BLK), BLK), :]
    return jnp.where(n == 0, head_ref[...], tok)


def _seq_views(n_tok, seq, off):
    head0 = n_tok // BLK
    return [pl.BlockSpec((None, seq, BLK), lambda i, j: (j + off, i, 0)),
            pl.BlockSpec((None, BLK, BLK), lambda i, j: (j + off, head0 + i, 0))]


def _gate_block(pre, n, neg_a, dtb):
    nr = GATE_ROWS
    t = pre.T[0:nr, :]
    ri = lax.broadcasted_iota(jnp.int32, (BLK, BLK), 0)
    ci = lax.broadcasted_iota(jnp.int32, (BLK, BLK), 1)
    role = ri[0:nr, :]
    live = (ci[0:nr, :] + n * BLK) >= FRONT
    beta = jnp.where(live, _sigmoid(t), 0.0)
    g = jnp.where(live, neg_a * _softplus(t + dtb), 0.0)
    parts = [p.astype(F32) for p in _split3(g)]
    parts = jnp.concatenate(parts + [jnp.zeros((BLK - 3 * nr, BLK), F32)], axis=0).astype(BF16)
    tri = jnp.concatenate([(ri <= ci).astype(BF16), (ri >= ci).astype(BF16)], axis=1)
    sums = _dot(parts, tri)
    both = sums[0:nr] + sums[nr:2 * nr] + sums[2 * nr:3 * nr]
    pre_sum, suf_sum = both[:, :BLK], both[:, BLK:]
    tot = pre_sum + suf_sum - g
    row = jnp.where(role < 2, beta, jnp.where(role == 2, pre_sum, jnp.where(role == 3, suf_sum, tot)))
    col = jnp.concatenate([row, jnp.zeros((BLK - nr, BLK), F32)], axis=0).T
    return col, row


def _delta_kernel(q_ref, qh_ref, k_ref, kh_ref, v_ref, vh_ref, z_ref, zh_ref, gp_ref, gph_ref, alog_ref, dtb_ref,
                  cwq_ref, cwk_ref, cwv_ref, og_ref, y_ref, yh_ref, sadd_s, smul_s, o_s, omul_s, gl_s):
    seq = q_ref.shape[0]
    nblk = seq // BLK + 1
    grp = _chunk_group(nblk)
    neg_a = -jnp.exp(alog_ref[...])
    dtb = dtb_ref[...]
    ri = lax.broadcasted_iota(jnp.int32, (BLK, BLK), 0)
    ci = lax.broadcasted_iota(jnp.int32, (BLK, BLK), 1)
    eye = (ri == ci).astype(F32)
    incl = (ri >= ci, ri <= ci)
    strict = (ri > ci, ri < ci)

    hw = A_CONV // 2

    def conv_silu(ref, head_ref, w_ref, n, maybe_edge):
        base = jnp.clip((n - 1) * BLK, hw, seq - BLK - hw)
        acc = ref[pl.ds(base - hw, BLK), :] * w_ref[0:1, :]
        for j in range(1, A_CONV):
            acc = acc + ref[pl.ds(base - hw + j, BLK), :] * w_ref[j:j + 1, :]
        if maybe_edge:
            cur = _seq_block(ref, head_ref, n)
            tok_prev = ref[pl.ds(pl.multiple_of(jnp.maximum((n - 1) * BLK - 8, 0), 8), 8), :]
            prev = jnp.where(n == 0, 0.0, jnp.where(n == 1, head_ref[BLK - 8:, :], tok_prev))
            nxt = ref[pl.ds(pl.multiple_of(jnp.minimum(n * BLK, seq - 8), 8), 8), :]
            nxt = jnp.where(n < nblk - 1, nxt, 0.0)
            win = jnp.concatenate([prev, cur, nxt], axis=0)
            edge = win[8 - hw:8 - hw + BLK, :] * w_ref[0:1, :]
            for j in range(1, A_CONV):
                edge = edge + win[8 - hw + j:8 - hw + j + BLK, :] * w_ref[j:j + 1, :]
            acc = jnp.where((n <= 1) | (n == nblk - 1), edge, acc)
        return _silu(acc)

    def l2n(x):
        return x * lax.rsqrt(jnp.sum(x * x, axis=-1, keepdims=True) + EPS)

    edge_slots = {0 % grp, 1 % grp, (nblk - 1) % grp}

    def chunk_inputs(n, slot):
        rows = pl.ds(pl.multiple_of(n * BLK, BLK), BLK)
        live = (ri[:, 0:1] + n * BLK) >= FRONT
        edge = slot in edge_slots
        qn = jnp.where(live, l2n(conv_silu(q_ref, qh_ref, cwq_ref, n, edge)) * (A_DK ** -0.5), 0.0)
        kn = jnp.where(live, l2n(conv_silu(k_ref, kh_ref, cwk_ref, n, edge)), 0.0)
        vv = jnp.where(live, conv_silu(v_ref, vh_ref, cwv_ref, n, edge), 0.0)
        kn16 = kn.astype(BF16)
        kq = _dot_nt(jnp.concatenate([kn16, qn.astype(BF16)], axis=0), kn16)
        bg, gt = _gate_block(_seq_block(gp_ref, gph_ref, n), n, neg_a, dtb)
        return dict(n=n, rows=rows, qn=qn, kn=kn, vv=vv, kk=kq[:BLK], qk=kq[BLK:], bg=bg, gt=gt)

    def chain_setup(c, d):
        bg, gt = c["bg"], c["gt"]
        beta, ccol, tot = bg[:, d:d + 1], bg[:, 2 + d:3 + d], bg[:, 4 + d:5 + d]
        crow = gt[2 + d:3 + d, :]
        dec = jnp.exp(jnp.where(incl[d], ccol - crow, NEG))
        a = jnp.where(strict[d], beta * c["kk"] * dec, 0.0)
        return dict(c=c, d=d, beta=beta, ccol=ccol, tot=tot, dec=dec, a=a, t=eye - a, x=a.astype(BF16))

    def prep(g, carry):
        chunks = [chunk_inputs(g * grp + j, j) for j in range(grp)]
        chains = [chain_setup(c, d) for c in chunks for d in range(2)]
        zero = jnp.zeros((BLK, BLK), BF16)

        def blockdiag(xp):
            return jnp.concatenate([jnp.concatenate([xp[:, :BLK], zero], axis=1),
                                    jnp.concatenate([zero, xp[:, BLK:]], axis=1)], axis=0)

        pairs = [(chains[2 * j], chains[2 * j + 1]) for j in range(grp)]
        xps = [jnp.concatenate([f["x"], b["x"]], axis=1) for f, b in pairs]
        tps = [jnp.concatenate([f["t"], b["t"]], axis=1) for f, b in pairs]
        for _ in range(INV_SQUARINGS):
            xps = [_dot(xp, blockdiag(xp)).astype(BF16) for xp in xps]
            txs = [_dot(tp.astype(BF16), blockdiag(xp)) for tp, xp in zip(tps, xps)]
            tps = [tp + tx for tp, tx in zip(tps, txs)]
        for (f, b), tp in zip(pairs, tps):
            f["t"], b["t"] = tp[:, :BLK], tp[:, BLK:]
        for ch in chains:
            c = ch["c"]
            ch["ec"] = jnp.exp(ch["ccol"])
            ch["rhs"] = jnp.concatenate([ch["beta"] * c["vv"], ch["beta"] * c["kn"] * ch["ec"]], axis=1)
            ch["t16"] = ch["t"].astype(BF16)
        x0s = [_dot(ch["t16"], ch["rhs"].astype(BF16)) for ch in chains]
        res = []
        for ch, x0 in zip(chains, x0s):
            ax = _dot(ch["a"].astype(BF16), x0.astype(BF16))
            res.append((ch["rhs"] - x0 - ax).astype(BF16))
        uws = [(x0 + _dot(ch["t16"], e)).astype(BF16) for ch, x0, e in zip(chains, x0s, res)]
        kuws = [_dot((ch["c"]["kn"] * jnp.exp(ch["tot"] - ch["ccol"])).T.astype(BF16), uw)
                for ch, uw in zip(chains, uws)]
        quws = [_dot((ch["c"]["qk"] * ch["dec"]).astype(BF16), uw) for ch, uw in zip(chains, uws)]
        for ch, kuw, quw in zip(chains, kuws, quws):
            c, d = ch["c"], ch["d"]
            rows = c["rows"]
            sadd_s[d, rows, :] = kuw[:, :BLK]
            smul_s[d, rows, :] = (-kuw[:, BLK:]).astype(BF16)
            o_s[d, rows, :] = quw[:, :BLK]
            omul_s[d, rows, :] = (c["qn"] * ch["ec"] - quw[:, BLK:]).astype(BF16)
            gl_s[d * nblk + c["n"]] = jnp.broadcast_to(jnp.exp(ch["tot"]), (BLK, BLK))[0:8, :]
        return carry

    lax.fori_loop(0, nblk // grp, prep, 0)

    def scan_step(d, n, s):
        rows = pl.ds(pl.multiple_of(n * BLK, BLK), BLK)
        both = _dot(jnp.concatenate([smul_s[d, rows, :], omul_s[d, rows, :]], axis=0), s.astype(BF16))
        o_s[d, rows, :] = o_s[d, rows, :] + both[BLK:]
        return s * gl_s[d * nblk + n][0:1, :] + both[:BLK] + sadd_s[d, rows, :]

    def scan(i, carry):
        sf, sb = carry
        sf = scan_step(0, i, sf)
        sb = scan_step(1, nblk - 1 - i, sb)
        return sf, sb

    s0 = jnp.zeros((BLK, BLK), F32)
    lax.fori_loop(0, nblk, scan, (s0, s0))

    def gated(rows, z):
        o = o_s[0, rows, :] + o_s[1, rows, :]
        return (_rms(o, og_ref[...]) * _silu(z)).astype(y_ref.dtype)

    yh_ref[...] = gated(slice(0, BLK), zh_ref[...])

    def finish(n, carry):
        tok_rows = pl.ds(pl.multiple_of((n - 1) * BLK, BLK), BLK)
        y_ref[tok_rows, :] = gated(pl.ds(pl.multiple_of(n * BLK, BLK), BLK), z_ref[tok_rows, :])
        return carry

    lax.fori_loop(1, nblk, finish, 0, unroll=4 if (nblk - 1) % 4 == 0 else 1)


def _delta(qkv, z, gate_pre, b, n_tok, alog_rows, dtb_rows, conv_w, o_gain):
    seq = n_tok // b
    lp = seq + BLK
    nblk = lp // BLK
    cw = lambda off: pl.BlockSpec((A_CONV, BLK), lambda i, j: (0, j + off))
    views = lambda off: _seq_views(n_tok, seq, off)
    per_head = pl.BlockSpec((None, GATE_ROWS, BLK), lambda i, j: (j, 0, 0))
    return pl.pallas_call(
        _delta_kernel,
        grid=(b, A_HEADS),
        in_specs=views(0) + views(A_HEADS) + views(2 * A_HEADS) + views(0) + views(0)
                 + [per_head, per_head, cw(0), cw(A_HEADS), cw(2 * A_HEADS),
                    pl.BlockSpec((1, BLK), lambda i, j: (0, 0))],
        out_specs=[pl.BlockSpec((None, seq, BLK), lambda i, j: (j, i, 0)),
                   pl.BlockSpec((None, BLK, BLK), lambda i, j: (j, i, 0))],
        out_shape=[jax.ShapeDtypeStruct((A_HEADS, n_tok, BLK), BF16),
                   jax.ShapeDtypeStruct((A_HEADS, b * BLK, BLK), BF16)],
        scratch_shapes=[pltpu.VMEM((2, lp, BLK), F32), pltpu.VMEM((2, lp, BLK), BF16),
                        pltpu.VMEM((2, lp, BLK), F32), pltpu.VMEM((2, lp, BLK), BF16),
                        pltpu.VMEM((2 * nblk, 8, BLK), F32)],
        compiler_params=_cparams("parallel", "parallel"),
        name="delta_mixer",
    )(qkv, qkv, qkv, qkv, qkv, qkv, z, z, gate_pre, gate_pre, alog_rows, dtb_rows, conv_w, conv_w, conv_w, o_gain)


def _window_kernel(q_ref, kp_ref, kc_ref, kn_ref, km_ref, bias_ref, sink_ref, y_ref):
    i = pl.program_id(1)
    nblk = pl.num_programs(1)
    grp = B_HEADS // B_KV
    nk = 4 * BLK
    c = lax.broadcasted_iota(jnp.int32, (1, nk), 1)
    kblk = i - 1 + (c >> 7)
    edge = jnp.where((c >= 3 * BLK) | ((kblk >= 1) & (kblk < nblk)), 0.0, NEG)
    q = q_ref[...]
    kvs = (kp_ref[...], kc_ref[...], kn_ref[...], km_ref[...])
    ones = jnp.ones((nk, 2 * B_HD), BF16)
    lane = lax.broadcasted_iota(jnp.int32, (BLK, 2 * B_HD), 1)
    s4s, vexts = [], []
    for kvh in range(B_KV):
        ks = jnp.concatenate([t[:, kvh * B_HD:(kvh + 1) * B_HD] for t in kvs], axis=0)
        vs = jnp.concatenate([t[:, (B_KV + kvh) * B_HD:(B_KV + kvh + 1) * B_HD] for t in kvs], axis=0)
        q4 = jnp.concatenate([q[:, hh * B_HD:(hh + 1) * B_HD] for hh in range(kvh * grp, (kvh + 1) * grp)],
                             axis=0)
        s4s.append(_dot_nt(q4, ks))
        vexts.append(jnp.concatenate([vs, vs, ones], axis=1))
    ms = []
    pvs = []
    for kvh in range(B_KV):
        ps = []
        for gi in range(grp):
            hh = kvh * grp + gi
            s = s4s[kvh][gi * BLK:(gi + 1) * BLK] + bias_ref[hh] + edge
            m = jnp.maximum(jnp.max(s, axis=-1, keepdims=True), sink_ref[hh:hh + 1, 0:1])
            ps.append(jnp.exp2(s - m).astype(BF16))
            ms.append(m)
        pvs.append(_dot(jnp.concatenate(ps, axis=0), vexts[kvh]))
    outs = []
    for hh in range(B_HEADS):
        kvh, gi = divmod(hh, grp)
        o = pvs[kvh][gi * BLK:(gi + 1) * BLK]
        den = o[:, 2 * B_HD:] + jnp.exp2(sink_ref[hh:hh + 1, 0:1] - ms[hh])
        outs.append(o[:, :2 * B_HD] / den)
    for j in range(B_HEADS // 2):
        pair = jnp.where(lane < B_HD, outs[2 * j], outs[2 * j + 1])
        y_ref[:, 2 * j * B_HD:(2 * j + 2) * B_HD] = pair.astype(y_ref.dtype)

    @pl.when(i == 0)
    def _():
        rr = lax.broadcasted_iota(jnp.int32, y_ref.shape, 0)
        y_ref[...] = jnp.where(rr >= FRONT, y_ref[...], 0).astype(y_ref.dtype)


def _window_bias():
    r = jnp.arange(BLK)[:, None]
    c = jnp.arange(4 * BLK)[None, :]
    dist = jnp.abs(BLK + r - c)
    slopes = jnp.exp2(-8.0 * (jnp.arange(B_HEADS, dtype=F32) + 1.0) / B_HEADS)
    band = (c < 3 * BLK) & (dist <= B_WIN)
    alibi = -slopes[:, None, None] * dist.astype(F32)[None] * LOG2E
    rest = jnp.where(c >= 3 * BLK + FRONT, 0.0, NEG)
    return jnp.where(band[None], alibi, rest[None]).astype(F32)


def _window(qb, kvb, b, n_tok, sink_rows):
    nblk = n_tok // b // BLK + 1
    bias = _window_bias()
    head0 = n_tok // BLK

    def blk(i, j):
        return jnp.where(j == 0, head0 + i, i * (nblk - 1) + j - 1)

    kv = lambda f: pl.BlockSpec((BLK, 2 * B_KV * B_HD), f)
    return pl.pallas_call(
        _window_kernel,
        grid=(b, nblk),
        in_specs=[pl.BlockSpec((BLK, B_HEADS * B_HD), lambda i, j: (blk(i, j), 0)),
                  kv(lambda i, j: (blk(i, jnp.maximum(j - 1, 0)), 0)),
                  kv(lambda i, j: (blk(i, j), 0)),
                  kv(lambda i, j: (blk(i, jnp.minimum(j + 1, nblk - 1)), 0)),
                  kv(lambda i, j: (head0 + i, 0)),
                  pl.BlockSpec(bias.shape, lambda i, j: (0, 0, 0)),
                  pl.BlockSpec((B_HEADS, BLK), lambda i, j: (0, 0))],
        out_specs=pl.BlockSpec((BLK, B_HEADS * B_HD), lambda i, j: (blk(i, j), 0)),
        out_shape=jax.ShapeDtypeStruct((qb.shape[0], B_HEADS * B_HD), BF16),
        compiler_params=_cparams("parallel", "parallel"),
        name="window_mixer",
    )(qb, kvb, kvb, kvb, kvb, bias, sink_rows)


def _out_mlp_kernel(*refs, arity, ntt):
    vals, pos = [], 0
    for a in arity:
        vals.append(refs[pos][...] if a == 1 else _pick(refs[pos], refs[pos + 1], ntt))
        pos += a
    wo_ref, g_ref, w1_ref, w2_ref, o_ref = refs[pos:]
    mix = jnp.concatenate(vals[1:], axis=1)
    h = vals[0] + _dot(mix, wo_ref[...])
    u = _rms(h, g_ref[...]).astype(BF16)
    dff = w1_ref.shape[1]
    acc = h
    for c in range(dff // FF_CHUNK):
        sl = slice(c * FF_CHUNK, (c + 1) * FF_CHUNK)
        a = jnp.maximum(_dot(u, w1_ref[:, sl]), 0.0)
        acc = acc + _dot((a * a).astype(BF16), w2_ref[sl, :])
    o_ref[...] = acc


def _out_mlp(rows_out, ntt, tensors, wo, g, w1, w2):
    d = wo.shape[1]
    row = lambda n: pl.BlockSpec((ROW_TILE, n), lambda i: (i, 0))
    full = lambda a: pl.BlockSpec(a.shape, lambda i: (0, 0))
    specs, args, arity = [], [], []
    for t in tensors:
        if isinstance(t, tuple):
            specs += _pair_specs(t[0], t[1])
            args += list(t)
            arity.append(2)
        else:
            specs.append(row(t.shape[1]))
            args.append(t)
            arity.append(1)
    return pl.pallas_call(
        functools.partial(_out_mlp_kernel, arity=tuple(arity), ntt=ntt),
        grid=(rows_out // ROW_TILE,),
        in_specs=specs + [full(wo), full(g), full(w1), full(w2)],
        out_specs=row(d),
        out_shape=jax.ShapeDtypeStruct((rows_out, d), F32),
        compiler_params=_cparams("parallel"),
        name="out_mlp",
    )(*args, wo, g, w1, w2)


def _c_proj_kernel(h_ref, g_ref, wq_ref, wk_ref, wv_ref, qg_ref, kg_ref, cos_ref, sin_ref,
                   q_ref, k_ref, v_ref):
    u = _rms(h_ref[...], g_ref[...]).astype(BF16)
    cosf = cos_ref[...]
    sinf = sin_ref[...]
    half = C_HD // 2

    def norm_rope(x, gain):
        x = _rms(x, gain)
        swapped = jnp.concatenate([x[:, half:], x[:, :half]], axis=1)
        return x * cosf + swapped * sinf

    k = _dot(u, wk_ref[...])
    half_w = C_HEADS * C_HD // 2
    q_lo = _dot(u, wq_ref[:, :half_w])
    for hh in range(C_KV):
        sl = slice(hh * C_HD, (hh + 1) * C_HD)
        k_ref[:, sl] = norm_rope(k[:, sl], kg_ref[...]).astype(BF16)
    q_hi = _dot(u, wq_ref[:, half_w:])
    for hh in range(C_HEADS // 2):
        sl = slice(hh * C_HD, (hh + 1) * C_HD)
        q_ref[:, sl] = (norm_rope(q_lo[:, sl], qg_ref[...]) * (C_HD ** -0.5 * LOG2E)).astype(BF16)
    v_ref[...] = _dot(u, wv_ref[...]).astype(BF16)
    for hh in range(C_HEADS // 2):
        sl = slice(hh * C_HD, (hh + 1) * C_HD)
        q_ref[:, half_w + hh * C_HD:half_w + (hh + 1) * C_HD] = (
            norm_rope(q_hi[:, sl], qg_ref[...]) * (C_HD ** -0.5 * LOG2E)).astype(BF16)


def _c_proj(h, ntt, seq, g, wq, wk, wv, qg, kg, cosf, sinf):
    r, d = h.shape
    tm = C_ROW_TILE
    per_seq = seq // tm
    n_tok_tiles = ntt * (ROW_TILE // tm)
    row = lambda n: pl.BlockSpec((tm, n), lambda i: (i, 0))
    full = lambda a: pl.BlockSpec(a.shape, lambda i: (0, 0))
    pos = pl.BlockSpec((tm, C_HD), lambda i: (jnp.where(i < n_tok_tiles, i % per_seq, per_seq), 0))
    return pl.pallas_call(
        _c_proj_kernel,
        grid=(r // tm,),
        in_specs=[row(d), full(g), full(wq), full(wk), full(wv), full(qg), full(kg), pos, pos],
        out_specs=[row(C_HEADS * C_HD), row(C_KV * C_HD), row(C_KV * C_HD)],
        out_shape=[jax.ShapeDtypeStruct((r, C_HEADS * C_HD), BF16),
                   jax.ShapeDtypeStruct((r, C_KV * C_HD), BF16),
                   jax.ShapeDtypeStruct((r, C_KV * C_HD), BF16)],
        compiler_params=_cparams("parallel"),
        name="c_proj",
    )(h, g, wq, wk, wv, qg, kg, cosf, sinf)


ATT_TK = 2048
ATT_QB = 4


def _dense_kernel(q_ref, k_ref, kh_ref, v_ref, vh_ref, y_ref, *scratch):
    grp = C_HEADS // C_KV
    nkb = k_ref.shape[0] // ATT_TK
    nq = ATT_QB
    sa_s, sb_s, acc_s = scratch[:nq], scratch[nq:2 * nq], scratch[2 * nq:]
    qs = [jnp.concatenate([q_ref[c * BLK:(c + 1) * BLK, g * C_HD:(g + 1) * C_HD] for g in range(grp)], axis=0)
          for c in range(nq)]
    m_rows = grp * BLK

    def keys(t):
        return pl.ds(pl.multiple_of(t * ATT_TK, ATT_TK), ATT_TK)

    def v_ones(v):
        return jnp.concatenate([v, jnp.ones(v.shape, BF16)], axis=1)

    def scores(t, s_refs):
        kt = k_ref[keys(t), :]
        for c in range(nq):
            s_refs[c][...] = _dot_nt(qs[c], kt)

    def step(t, ms, s_refs):
        vt = v_ones(v_ref[keys(t), :])
        out = []
        for c in range(nq):
            s = s_refs[c][...]
            m_new = jnp.maximum(ms[c], jnp.max(s, axis=-1, keepdims=True))
            p = jnp.exp2(s - m_new).astype(BF16)
            acc_s[c][...] = jnp.exp2(ms[c] - m_new) * acc_s[c][...] + _dot(p, vt)
            out.append(m_new)
        return out

    scores(0, sa_s)
    k0 = kh_ref[...]
    v0 = v_ones(vh_ref[...])
    kc = lax.broadcasted_iota(jnp.int32, (m_rows, BLK), 1)
    ms = []
    for c in range(nq):
        s0 = jnp.where(kc >= FRONT, _dot_nt(qs[c], k0), NEG)
        m = jnp.max(s0, axis=-1, keepdims=True)
        acc_s[c][...] = _dot(jnp.exp2(s0 - m).astype(BF16), v0)
        ms.append(m)

    def body(j, ms):
        scores(2 * j + 1, sb_s)
        ms = step(2 * j, ms, sa_s)
        scores(2 * j + 2, sa_s)
        return step(2 * j + 1, ms, sb_s)

    ms = lax.fori_loop(0, nkb // 2 - 1, body, ms)
    scores(nkb - 1, sb_s)
    ms = step(nkb - 2, ms, sa_s)
    ms = step(nkb - 1, ms, sb_s)
    for c in range(nq):
        acc = acc_s[c][...]
        o = acc[:, :C_HD] / acc[:, C_HD:C_HD + 1]
        for g in range(grp):
            y_ref[c * BLK:(c + 1) * BLK, g * C_HD:(g + 1) * C_HD] = o[g * BLK:(g + 1) * BLK, :].astype(y_ref.dtype)


def _dense(q, k, v, b, n_tok):
    seq = n_tok // b
    grp = C_HEADS // C_KV
    tq = ATT_QB * BLK
    assert seq % tq == 0 and seq % (2 * ATT_TK) == 0
    per_seq = seq // tq
    head0 = n_tok // BLK
    score = pltpu.VMEM((grp * BLK, ATT_TK), F32)
    tok = pl.BlockSpec((seq, C_HD), lambda i, j, t: (i, j))
    head = pl.BlockSpec((BLK, C_HD), lambda i, j, t: (head0 + i, j))
    return pl.pallas_call(
        _dense_kernel,
        grid=(b, C_KV, per_seq),
        in_specs=[pl.BlockSpec((tq, grp * C_HD), lambda i, j, t: (i * per_seq + t, j)), tok, head, tok, head],
        out_specs=pl.BlockSpec((tq, grp * C_HD), lambda i, j, t: (i * per_seq + t, j)),
        out_shape=jax.ShapeDtypeStruct((n_tok, C_HEADS * C_HD), BF16),
        scratch_shapes=[score] * (2 * ATT_QB) + [pltpu.VMEM((grp * BLK, 2 * C_HD), F32)] * ATT_QB,
        compiler_params=_cparams("parallel", "parallel", "arbitrary"),
        name="dense_mixer",
    )(q, k, k, v, v)


def _rope_tables(seq):
    rows = seq // GRID_W
    row = jnp.repeat(jnp.arange(rows), GRID_W)
    col = jnp.tile(jnp.arange(GRID_W), rows)
    head = jnp.tile(jnp.concatenate([jnp.zeros((FRONT,), jnp.int32), jnp.arange(N_META) - N_META]), ROW_TILE // BLK)
    row = jnp.concatenate([row, head]).astype(F32)
    col = jnp.concatenate([col, head]).astype(F32)
    axis_dim = C_HD // 2
    freqs = ROPE_THETA ** (-jnp.arange(0, axis_dim, 2, dtype=F32) / axis_dim)
    ang = jnp.concatenate([row[:, None] * freqs, col[:, None] * freqs], axis=-1)
    cos, sin = jnp.cos(ang), jnp.sin(ang)
    return jnp.concatenate([cos, cos], axis=-1), jnp.concatenate([-sin, sin], axis=-1)


def _gate_weight(w_b, w_a):
    d = w_b.shape[0]
    w_b = w_b.reshape(d, 2, A_HEADS)
    w_a = w_a.reshape(d, 2, A_HEADS)
    per_head = jnp.concatenate([w_b, w_a, w_a], axis=1)
    per_head = jnp.transpose(per_head, (0, 2, 1))
    per_head = jnp.pad(per_head, ((0, 0), (0, 0), (0, BLK - 6)))
    return per_head.reshape(d, A_HEADS * BLK)


def _gate_rows(p):
    t = jnp.transpose(p.astype(F32), (1, 0))
    rows = jnp.concatenate([jnp.zeros_like(t), t, t], axis=1)
    rows = jnp.pad(rows, ((0, 0), (0, GATE_ROWS - 6)))
    return jnp.broadcast_to(rows[:, :, None], (A_HEADS, GATE_ROWS, BLK))


def kernel(x, meta_tokens, attn_norm_g, mlp_norm_g, w_in_ab, conv_w_a, a_log, dt_bias, a_out_norm_g,
           b_q_norm_g, b_k_norm_g, b_sink, w_out_ab, w_qkv_c, c_q_norm_g, c_k_norm_g, w_out_c, w_ff1, w_ff2):
    bsz, seq, d = x.shape
    n_tok = bsz * seq
    n_rows = n_tok + bsz * BLK
    ntt = n_tok // ROW_TILE
    assert attn_norm_g.shape[0] == 2 and seq % ROW_TILE == 0 and (bsz * BLK) % ROW_TILE == 0
    x2 = x.reshape(n_tok, d)
    meta = jnp.broadcast_to(meta_tokens.astype(x.dtype)[None], (bsz, N_META, d))
    head = jnp.concatenate([jnp.zeros((bsz, FRONT, d), x.dtype), meta], axis=1).reshape(bsz * BLK, d)
    row2 = lambda v: v.astype(F32).reshape(1, -1)

    w = w_in_ab[0]
    qkv_w = w[:, :1536].astype(BF16)
    z_w = w[:, 1536:2048].astype(BF16)
    gate_w = _gate_weight(w[:, 2048:2056], w[:, 2056:2064]).astype(BF16)
    bq_w = w[:, 2064:2576].astype(BF16)
    bkv_w = w[:, 2576:2832].astype(BF16)
    qkv, z, gate_pre, qb, kvb = _ab_proj(x2, head, row2(attn_norm_g[0]), qkv_w, z_w, gate_w, bq_w, bkv_w,
                                         row2(b_q_norm_g[0]), row2(b_k_norm_g[0]))
    ya = _delta(qkv, z, gate_pre, bsz, n_tok, _gate_rows(a_log[0]), _gate_rows(dt_bias[0]),
                conv_w_a[0].astype(F32), row2(a_out_norm_g[0]))
    sink_rows = jnp.broadcast_to(b_sink[0].astype(F32)[:, None] * LOG2E, (B_HEADS, BLK))
    yb = _window(qb, kvb, bsz, n_tok, sink_rows)
    h = _out_mlp(n_rows, ntt, [(x2, head), tuple(ya), yb], w_out_ab[0].astype(BF16), row2(mlp_norm_g[0]),
                 w_ff1[0].astype(BF16), w_ff2[0].astype(BF16))

    w = w_qkv_c[0]
    deint = jnp.concatenate([jnp.arange(0, C_HD, 2), jnp.arange(1, C_HD, 2)])
    perm = lambda wc, nh: wc.reshape(d, nh, C_HD)[:, :, deint].reshape(d, nh * C_HD)
    wq = perm(w[:, :C_HEADS * C_HD], C_HEADS).astype(BF16)
    wk = perm(w[:, C_HEADS * C_HD:(C_HEADS + C_KV) * C_HD], C_KV).astype(BF16)
    wv = w[:, (C_HEADS + C_KV) * C_HD:].astype(BF16)
    cosf, sinf = _rope_tables(seq)
    q, k, v = _c_proj(h, ntt, seq, row2(attn_norm_g[1]), wq, wk, wv,
                      row2(c_q_norm_g[0][deint]), row2(c_k_norm_g[0][deint]), cosf, sinf)
    att = _dense(q, k, v, bsz, n_tok)
    out = _out_mlp(n_tok, ntt, [h, att], w_out_c[0].astype(BF16), row2(mlp_norm_g[1]),
                   w_ff1[1].astype(BF16), w_ff2[1].astype(BF16))
    return out.reshape(bsz, seq, d)
```

```python
import functools
import math

import jax
import jax.numpy as jnp
from jax import lax
from jax.experimental import pallas as pl
from jax.experimental.pallas import tpu as pltpu

F32 = jnp.float32
BF16 = jnp.bfloat16

EPS = 1e-6
N_META = 16
BLK = 128
FRONT = BLK - N_META
GRID_W = 64
ROPE_THETA = 10000.0
A_HEADS, A_DK, A_CONV = 4, 128, 5
B_HEADS, B_KV, B_HD, B_WIN = 8, 2, 64, 128
C_HEADS, C_KV, C_HD = 8, 2, 128
NEG = -1e30
LOG2E = math.log2(math.e)

VMEM_LIMIT = 56 * 1024 * 1024
ROW_TILE = 512
C_ROW_TILE = 256
FF_CHUNK = 512
INV_SQUARINGS = BLK.bit_length() - 2
PREP_UNROLL = 11


def _chunk_group(nblk):
    return max(g for g in range(1, PREP_UNROLL + 1) if nblk % g == 0)


def _cparams(*sem):
    return pltpu.CompilerParams(dimension_semantics=sem, vmem_limit_bytes=VMEM_LIMIT)


def _sigmoid(x):
    return 1.0 / (1.0 + jnp.exp(-x))


def _silu(x):
    return x * _sigmoid(x)


def _softplus(x):
    return jnp.maximum(x, 0.0) + jnp.log1p(jnp.exp(-jnp.abs(x)))


def _rms(x, g):
    return x * lax.rsqrt(jnp.mean(x * x, axis=-1, keepdims=True) + EPS) * g


def _dot(a, b):
    return jnp.dot(a, b, preferred_element_type=F32)


def _dot_nt(a, b):
    return lax.dot_general(a, b, (((1,), (1,)), ((), ())), preferred_element_type=F32)


def _slab_spec(n_slabs, index):
    return pl.BlockSpec((n_slabs, ROW_TILE, BLK), lambda i: (0, index(i), 0))


def _pair_specs(tok, head):
    ntt = tok.shape[-2] // ROW_TILE
    index = (lambda i: jnp.minimum(i, ntt - 1), lambda i: jnp.maximum(i - ntt, 0))
    if tok.ndim == 3:
        return [_slab_spec(tok.shape[0], ix) for ix in index]
    flat = lambda ix: pl.BlockSpec((ROW_TILE, tok.shape[1]), lambda i: (ix(i), 0))
    return [flat(ix) for ix in index]


def _rows(v):
    return jnp.concatenate([v[c] for c in range(v.shape[0])], axis=1) if v.ndim == 3 else v


def _pick(tok_ref, head_ref, ntt):
    return _rows(jnp.where(pl.program_id(0) < ntt, tok_ref[...], head_ref[...]))


def _store_slabs(ref, first, value):
    for c in range(value.shape[1] // BLK):
        ref[first + c] = value[:, c * BLK:(c + 1) * BLK]


def _ab_proj_kernel(x_ref, hb_ref, g_ref, wqkv_ref, wz_ref, wg_ref, wq_ref, wkv_ref, bqg_ref, bkg_ref,
                    qkv_ref, z_ref, gate_ref, qb_ref, kvb_ref, *, ntt):
    u = _rms(_pick(x_ref, hb_ref, ntt), g_ref[...]).astype(BF16)
    qb = _dot(u, wq_ref[...])
    kv = _dot(u, wkv_ref[...])
    scale = B_HD ** -0.5 * LOG2E

    def q_heads(lo, hi):
        for hh in range(lo, hi):
            sl = slice(hh * B_HD, (hh + 1) * B_HD)
            qb_ref[:, sl] = (_rms(qb[:, sl], bqg_ref[...]) * scale).astype(BF16)

    _store_slabs(qkv_ref, 0, _dot(u, wqkv_ref[:, 0:512]))
    q_heads(0, 3)
    _store_slabs(qkv_ref, A_HEADS, _dot(u, wqkv_ref[:, 512:1024]))
    q_heads(3, 6)
    _store_slabs(qkv_ref, 2 * A_HEADS, _dot(u, wqkv_ref[:, 1024:1536]))
    q_heads(6, B_HEADS)
    _store_slabs(z_ref, 0, _dot(u, wz_ref[...]))
    for hh in range(B_KV):
        sl = slice(hh * B_HD, (hh + 1) * B_HD)
        kvb_ref[:, sl] = _rms(kv[:, sl], bkg_ref[...]).astype(BF16)
    kvb_ref[:, B_KV * B_HD:] = kv[:, B_KV * B_HD:].astype(BF16)
    _store_slabs(gate_ref, 0, _dot(u, wg_ref[...]))


def _ab_proj(x, hb, g, wqkv, wz, wg, wq, wkv, bqg, bkg):
    r = x.shape[0] + hb.shape[0]
    row = lambda n: pl.BlockSpec((ROW_TILE, n), lambda i: (i, 0))
    full = lambda a: pl.BlockSpec(a.shape, lambda i: (0, 0))
    return pl.pallas_call(
        functools.partial(_ab_proj_kernel, ntt=x.shape[0] // ROW_TILE),
        grid=(r // ROW_TILE,),
        in_specs=_pair_specs(x, hb) + [full(g), full(wqkv), full(wz), full(wg), full(wq), full(wkv), full(bqg),
                                       full(bkg)],
        out_specs=[_slab_spec(3 * A_HEADS, lambda i: i), _slab_spec(A_HEADS, lambda i: i),
                   _slab_spec(A_HEADS, lambda i: i), row(512), row(256)],
        out_shape=[jax.ShapeDtypeStruct((3 * A_HEADS, r, BLK), F32), jax.ShapeDtypeStruct((A_HEADS, r, BLK), F32),
                   jax.ShapeDtypeStruct((A_HEADS, r, BLK), F32), jax.ShapeDtypeStruct((r, 512), BF16),
                   jax.ShapeDtypeStruct((r, 256), BF16)],
        compiler_params=_cparams("parallel"),
        name="ab_proj",
    )(x, hb, g, wqkv, wz, wg, wq, wkv, bqg, bkg)


GATE_ROWS = 8


def _split3(x):
    hi = x.astype(BF16)
    r1 = x - hi.astype(F32)
    mid = r1.astype(BF16)
    lo = (r1 - mid.astype(F32)).astype(BF16)
    return hi, mid, lo


def _seq_block(tok_ref, head_ref, n):
    tok = tok_ref[pl.ds(pl.multiple_of(jnp.maximum(n - 1, 0) * BLK, BLK), BLK), :]
    return jnp.where(n == 0, head_ref[...], tok)


def _seq_views(n_tok, seq, off):
    head0 = n_tok // BLK
    return [pl.BlockSpec((None, seq, BLK), lambda i, j: (j + off, i, 0)),
            pl.BlockSpec((None, BLK, BLK), lambda i, j: (j + off, head0 + i, 0))]


def _gate_block(pre, n, neg_a, dtb):
    nr = GATE_ROWS
    t = pre.T[0:nr, :]
    ri = lax.broadcasted_iota(jnp.int32, (BLK, BLK), 0)
    ci = lax.broadcasted_iota(jnp.int32, (BLK, BLK), 1)
    role = ri[0:nr, :]
    live = (ci[0:nr, :] + n * BLK) >= FRONT
    beta = jnp.where(live, _sigmoid(t), 0.0)
    g = jnp.where(live, neg_a * _softplus(t + dtb), 0.0)
    parts = [p.astype(F32) for p in _split3(g)]
    parts = jnp.concatenate(parts + [jnp.zeros((BLK - 3 * nr, BLK), F32)], axis=0).astype(BF16)
    tri = jnp.concatenate([(ri <= ci).astype(BF16), (ri >= ci).astype(BF16)], axis=1)
    sums = _dot(parts, tri)
    both = sums[0:nr] + sums[nr:2 * nr] + sums[2 * nr:3 * nr]
    pre_sum, suf_sum = both[:, :BLK], both[:, BLK:]
    tot = pre_sum + suf_sum - g
    row = jnp.where(role < 2, beta, jnp.where(role == 2, pre_sum, jnp.where(role == 3, suf_sum, tot)))
    col = jnp.concatenate([row, jnp.zeros((BLK - nr, BLK), F32)], axis=0).T
    return col, row


def _delta_kernel(q_ref, qh_ref, k_ref, kh_ref, v_ref, vh_ref, z_ref, zh_ref, gp_ref, gph_ref, alog_ref, dtb_ref,
                  cwq_ref, cwk_ref, cwv_ref, og_ref, y_ref, yh_ref, sadd_s, smul_s, o_s, omul_s, gl_s):
    seq = q_ref.shape[0]
    nblk = seq // BLK + 1
    grp = _chunk_group(nblk)
    neg_a = -jnp.exp(alog_ref[...])
    dtb = dtb_ref[...]
    ri = lax.broadcasted_iota(jnp.int32, (BLK, BLK), 0)
    ci = lax.broadcasted_iota(jnp.int32, (BLK, BLK), 1)
    eye = (ri == ci).astype(F32)
    incl = (ri >= ci, ri <= ci)
    strict = (ri > ci, ri < ci)

    hw = A_CONV // 2

    def conv_silu(ref, head_ref, w_ref, n, maybe_edge):
        base = jnp.clip((n - 1) * BLK, hw, seq - BLK - hw)
        acc = ref[pl.ds(base - hw, BLK), :] * w_ref[0:1, :]
        for j in range(1, A_CONV):
            acc = acc + ref[pl.ds(base - hw + j, BLK), :] * w_ref[j:j + 1, :]
        if maybe_edge:
            cur = _seq_block(ref, head_ref, n)
            tok_prev = ref[pl.ds(pl.multiple_of(jnp.maximum((n - 1) * BLK - 8, 0), 8), 8), :]
            prev = jnp.where(n == 0, 0.0, jnp.where(n == 1, head_ref[BLK - 8:, :], tok_prev))
            nxt = ref[pl.ds(pl.multiple_of(jnp.minimum(n * BLK, seq - 8), 8), 8), :]
            nxt = jnp.where(n < nblk - 1, nxt, 0.0)
            win = jnp.concatenate([prev, cur, nxt], axis=0)
            edge = win[8 - hw:8 - hw + BLK, :] * w_ref[0:1, :]
            for j in range(1, A_CONV):
                edge = edge + win[8 - hw + j:8 - hw + j + BLK, :] * w_ref[j:j + 1, :]
            acc = jnp.where((n <= 1) | (n == nblk - 1), edge, acc)
        return _silu(acc)

    def l2n(x):
        return x * lax.rsqrt(jnp.sum(x * x, axis=-1, keepdims=True) + EPS)

    edge_slots = {0 % grp, 1 % grp, (nblk - 1) % grp}

    def chunk_inputs(n, slot):
        rows = pl.ds(pl.multiple_of(n * BLK, BLK), BLK)
        live = (ri[:, 0:1] + n * BLK) >= FRONT
        edge = slot in edge_slots
        qn = jnp.where(live, l2n(conv_silu(q_ref, qh_ref, cwq_ref, n, edge)) * (A_DK ** -0.5), 0.0)
        kn = jnp.where(live, l2n(conv_silu(k_ref, kh_ref, cwk_ref, n, edge)), 0.0)
        vv = jnp.where(live, conv_silu(v_ref, vh_ref, cwv_ref, n, edge), 0.0)
        kn16 = kn.astype(BF16)
        kq = _dot_nt(jnp.concatenate([kn16, qn.astype(BF16)], axis=0), kn16)
        bg, gt = _gate_block(_seq_block(gp_ref, gph_ref, n), n, neg_a, dtb)
        return dict(n=n, rows=rows, qn=qn, kn=kn, vv=vv, kk=kq[:BLK], qk=kq[BLK:], bg=bg, gt=gt)

    def chain_setup(c, d):
        bg, gt = c["bg"], c["gt"]
        beta, ccol, tot = bg[:, d:d + 1], bg[:, 2 + d:3 + d], bg[:, 4 + d:5 + d]
        crow = gt[2 + d:3 + d, :]
        dec = jnp.exp(jnp.where(incl[d], ccol - crow, NEG))
        a = jnp.where(strict[d], beta * c["kk"] * dec, 0.0)
        return dict(c=c, d=d, beta=beta, ccol=ccol, tot=tot, dec=dec, a=a, t=eye - a, x=a.astype(BF16))

    def prep(g, carry):
        chunks = [chunk_inputs(g * grp + j, j) for j in range(grp)]
        chains = [chain_setup(c, d) for c in chunks for d in range(2)]
        zero = jnp.zeros((BLK, BLK), BF16)

        def blockdiag(xp):
            return jnp.concatenate([jnp.concatenate([xp[:, :BLK], zero], axis=1),
                                    jnp.concatenate([zero, xp[:, BLK:]], axis=1)], axis=0)

        pairs = [(chains[2 * j], chains[2 * j + 1]) for j in range(grp)]
        xps = [jnp.concatenate([f["x"], b["x"]], axis=1) for f, b in pairs]
        tps = [jnp.concatenate([f["t"], b["t"]], axis=1) for f, b in pairs]
        for _ in range(INV_SQUARINGS):
            xps = [_dot(xp, blockdiag(xp)).astype(BF16) for xp in xps]
            txs = [_dot(tp.astype(BF16), blockdiag(xp)) for tp, xp in zip(tps, xps)]
            tps = [tp + tx for tp, tx in zip(tps, txs)]
        for (f, b), tp in zip(pairs, tps):
            f["t"], b["t"] = tp[:, :BLK], tp[:, BLK:]
        for ch in chains:
            c = ch["c"]
            ch["ec"] = jnp.exp(ch["ccol"])
            ch["rhs"] = jnp.concatenate([ch["beta"] * c["vv"], ch["beta"] * c["kn"] * ch["ec"]], axis=1)
            ch["t16"] = ch["t"].astype(BF16)
        x0s = [_dot(ch["t16"], ch["rhs"].astype(BF16)) for ch in chains]
        res = []
        for ch, x0 in zip(chains, x0s):
            ax = _dot(ch["a"].astype(BF16), x0.astype(BF16))
            res.append((ch["rhs"] - x0 - ax).astype(BF16))
        uws = [(x0 + _dot(ch["t16"], e)).astype(BF16) for ch, x0, e in zip(chains, x0s, res)]
        kuws = [_dot((ch["c"]["kn"] * jnp.exp(ch["tot"] - ch["ccol"])).T.astype(BF16), uw)
                for ch, uw in zip(chains, uws)]
        quws = [_dot((ch["c"]["qk"] * ch["dec"]).astype(BF16), uw) for ch, uw in zip(chains, uws)]
        for ch, kuw, quw in zip(chains, kuws, quws):
            c, d = ch["c"], ch["d"]
            rows = c["rows"]
            sadd_s[d, rows, :] = kuw[:, :BLK]
            smul_s[d, rows, :] = (-kuw[:, BLK:]).astype(BF16)
            o_s[d, rows, :] = quw[:, :BLK]
            omul_s[d, rows, :] = (c["qn"] * ch["ec"] - quw[:, BLK:]).astype(BF16)
            gl_s[d * nblk + c["n"]] = jnp.broadcast_to(jnp.exp(ch["tot"]), (BLK, BLK))[0:8, :]
        return carry

    lax.fori_loop(0, nblk // grp, prep, 0)

    def scan_step(d, n, s):
        rows = pl.ds(pl.multiple_of(n * BLK, BLK), BLK)
        both = _dot(jnp.concatenate([smul_s[d, rows, :], omul_s[d, rows, :]], axis=0), s.astype(BF16))
        o_s[d, rows, :] = o_s[d, rows, :] + both[BLK:]
        return s * gl_s[d * nblk + n][0:1, :] + both[:BLK] + sadd_s[d, rows, :]

    def gated(rows, z):
        o = o_s[0, rows, :] + o_s[1, rows, :]
        return (_rms(o, og_ref[...]) * _silu(z)).astype(y_ref.dtype)

    def finish(n):
        tok_rows = pl.ds(pl.multiple_of((n - 1) * BLK, BLK), BLK)
        y_ref[tok_rows, :] = gated(pl.ds(pl.multiple_of(n * BLK, BLK), BLK), z_ref[tok_rows, :])

    def scan(i, carry):
        sf, sb = carry
        sf = scan_step(0, i, sf)
        sb = scan_step(1, nblk - 1 - i, sb)
        return sf, sb

    def scan_and_finish(i, carry):
        finish(i - 1)
        finish(nblk - i)
        return scan(i, carry)

    s0 = jnp.zeros((BLK, BLK), F32)
    first_done = nblk // 2 + 1
    carry = lax.fori_loop(0, first_done, scan, (s0, s0))
    lax.fori_loop(first_done, nblk, scan_and_finish, carry)
    finish(nblk - 1)
    yh_ref[...] = gated(slice(0, BLK), zh_ref[...])


def _delta(qkv, z, gate_pre, b, n_tok, alog_rows, dtb_rows, conv_w, o_gain):
    seq = n_tok // b
    lp = seq + BLK
    nblk = lp // BLK
    cw = lambda off: pl.BlockSpec((A_CONV, BLK), lambda i, j: (0, j + off))
    views = lambda off: _seq_views(n_tok, seq, off)
    per_head = pl.BlockSpec((None, GATE_ROWS, BLK), lambda i, j: (j, 0, 0))
    return pl.pallas_call(
        _delta_kernel,
        grid=(b, A_HEADS),
        in_specs=views(0) + views(A_HEADS) + views(2 * A_HEADS) + views(0) + views(0)
                 + [per_head, per_head, cw(0), cw(A_HEADS), cw(2 * A_HEADS),
                    pl.BlockSpec((1, BLK), lambda i, j: (0, 0))],
        out_specs=[pl.BlockSpec((None, seq, BLK), lambda i, j: (j, i, 0)),
                   pl.BlockSpec((None, BLK, BLK), lambda i, j: (j, i, 0))],
        out_shape=[jax.ShapeDtypeStruct((A_HEADS, n_tok, BLK), BF16),
                   jax.ShapeDtypeStruct((A_HEADS, b * BLK, BLK), BF16)],
        scratch_shapes=[pltpu.VMEM((2, lp, BLK), F32), pltpu.VMEM((2, lp, BLK), BF16),
                        pltpu.VMEM((2, lp, BLK), F32), pltpu.VMEM((2, lp, BLK), BF16),
                        pltpu.VMEM((2 * nblk, 8, BLK), F32)],
        compiler_params=_cparams("parallel", "parallel"),
        name="delta_mixer",
    )(qkv, qkv, qkv, qkv, qkv, qkv, z, z, gate_pre, gate_pre, alog_rows, dtb_rows, conv_w, conv_w, conv_w, o_gain)


def _window_kernel(q_ref, kp_ref, kc_ref, kn_ref, km_ref, bias_ref, sink_ref, y_ref):
    i = pl.program_id(1)
    nblk = pl.num_programs(1)
    grp = B_HEADS // B_KV
    nk = 4 * BLK
    c = lax.broadcasted_iota(jnp.int32, (1, nk), 1)
    kblk = i - 1 + (c >> 7)
    edge = jnp.where((c >= 3 * BLK) | ((kblk >= 1) & (kblk < nblk)), 0.0, NEG)
    q = q_ref[...]
    kvs = (kp_ref[...], kc_ref[...], kn_ref[...], km_ref[...])
    ones = jnp.ones((nk, 2 * B_HD), BF16)
    lane = lax.broadcasted_iota(jnp.int32, (BLK, 2 * B_HD), 1)
    s4s, vexts = [], []
    for kvh in range(B_KV):
        ks = jnp.concatenate([t[:, kvh * B_HD:(kvh + 1) * B_HD] for t in kvs], axis=0)
        vs = jnp.concatenate([t[:, (B_KV + kvh) * B_HD:(B_KV + kvh + 1) * B_HD] for t in kvs], axis=0)
        q4 = jnp.concatenate([q[:, hh * B_HD:(hh + 1) * B_HD] for hh in range(kvh * grp, (kvh + 1) * grp)],
                             axis=0)
        s4s.append(_dot_nt(q4, ks))
        vexts.append(jnp.concatenate([vs, vs, ones], axis=1))
    ms = []
    pvs = []
    for kvh in range(B_KV):
        ps = []
        for gi in range(grp):
            hh = kvh * grp + gi
            s = s4s[kvh][gi * BLK:(gi + 1) * BLK] + bias_ref[hh] + edge
            m = jnp.maximum(jnp.max(s, axis=-1, keepdims=True), sink_ref[hh:hh + 1, 0:1])
            ps.append(jnp.exp2(s - m).astype(BF16))
            ms.append(m)
        pvs.append(_dot(jnp.concatenate(ps, axis=0), vexts[kvh]))
    outs = []
    for hh in range(B_HEADS):
        kvh, gi = divmod(hh, grp)
        o = pvs[kvh][gi * BLK:(gi + 1) * BLK]
        den = o[:, 2 * B_HD:] + jnp.exp2(sink_ref[hh:hh + 1, 0:1] - ms[hh])
        outs.append(o[:, :2 * B_HD] / den)
    for j in range(B_HEADS // 2):
        pair = jnp.where(lane < B_HD, outs[2 * j], outs[2 * j + 1])
        y_ref[:, 2 * j * B_HD:(2 * j + 2) * B_HD] = pair.astype(y_ref.dtype)

    @pl.when(i == 0)
    def _():
        rr = lax.broadcasted_iota(jnp.int32, y_ref.shape, 0)
        y_ref[...] = jnp.where(rr >= FRONT, y_ref[...], 0).astype(y_ref.dtype)


def _window_bias():
    r = jnp.arange(BLK)[:, None]
    c = jnp.arange(4 * BLK)[None, :]
    dist = jnp.abs(BLK + r - c)
    slopes = jnp.exp2(-8.0 * (jnp.arange(B_HEADS, dtype=F32) + 1.0) / B_HEADS)
    band = (c < 3 * BLK) & (dist <= B_WIN)
    alibi = -slopes[:, None, None] * dist.astype(F32)[None] * LOG2E
    rest = jnp.where(c >= 3 * BLK + FRONT, 0.0, NEG)
    return jnp.where(band[None], alibi, rest[None]).astype(F32)


def _window(qb, kvb, b, n_tok, sink_rows):
    nblk = n_tok // b // BLK + 1
    bias = _window_bias()
    head0 = n_tok // BLK

    def blk(i, j):
        return jnp.where(j == 0, head0 + i, i * (nblk - 1) + j - 1)

    kv = lambda f: pl.BlockSpec((BLK, 2 * B_KV * B_HD), f)
    return pl.pallas_call(
        _window_kernel,
        grid=(b, nblk),
        in_specs=[pl.BlockSpec((BLK, B_HEADS * B_HD), lambda i, j: (blk(i, j), 0)),
                  kv(lambda i, j: (blk(i, jnp.maximum(j - 1, 0)), 0)),
                  kv(lambda i, j: (blk(i, j), 0)),
                  kv(lambda i, j: (blk(i, jnp.minimum(j + 1, nblk - 1)), 0)),
                  kv(lambda i, j: (head0 + i, 0)),
                  pl.BlockSpec(bias.shape, lambda i, j: (0, 0, 0)),
                  pl.BlockSpec((B_HEADS, BLK), lambda i, j: (0, 0))],
        out_specs=pl.BlockSpec((BLK, B_HEADS * B_HD), lambda i, j: (blk(i, j), 0)),
        out_shape=jax.ShapeDtypeStruct((qb.shape[0], B_HEADS * B_HD), BF16),
        compiler_params=_cparams("parallel", "parallel"),
        name="window_mixer",
    )(qb, kvb, kvb, kvb, kvb, bias, sink_rows)


def _out_mlp_kernel(*refs, arity, ntt):
    vals, pos = [], 0
    for a in arity:
        vals.append(refs[pos][...] if a == 1 else _pick(refs[pos], refs[pos + 1], ntt))
        pos += a
    wo_ref, g_ref, w1_ref, w2_ref, o_ref = refs[pos:]
    mix = jnp.concatenate(vals[1:], axis=1)
    h = vals[0] + _dot(mix, wo_ref[...])
    u = _rms(h, g_ref[...]).astype(BF16)
    dff = w1_ref.shape[1]
    acc = h
    for c in range(dff // FF_CHUNK):
        sl = slice(c * FF_CHUNK, (c + 1) * FF_CHUNK)
        a = jnp.maximum(_dot(u, w1_ref[:, sl]), 0.0)
        acc = acc + _dot((a * a).astype(BF16), w2_ref[sl, :])
    o_ref[...] = acc


def _out_mlp(rows_out, ntt, tensors, wo, g, w1, w2):
    d = wo.shape[1]
    row = lambda n: pl.BlockSpec((ROW_TILE, n), lambda i: (i, 0))
    full = lambda a: pl.BlockSpec(a.shape, lambda i: (0, 0))
    specs, args, arity = [], [], []
    for t in tensors:
        if isinstance(t, tuple):
            specs += _pair_specs(t[0], t[1])
            args += list(t)
            arity.append(2)
        else:
            specs.append(row(t.shape[1]))
            args.append(t)
            arity.append(1)
    return pl.pallas_call(
        functools.partial(_out_mlp_kernel, arity=tuple(arity), ntt=ntt),
        grid=(rows_out // ROW_TILE,),
        in_specs=specs + [full(wo), full(g), full(w1), full(w2)],
        out_specs=row(d),
        out_shape=jax.ShapeDtypeStruct((rows_out, d), F32),
        compiler_params=_cparams("parallel"),
        name="out_mlp",
    )(*args, wo, g, w1, w2)


def _c_proj_kernel(h_ref, g_ref, wq_ref, wk_ref, wv_ref, qg_ref, kg_ref, cos_ref, sin_ref,
                   q_ref, k_ref, v_ref):
    u = _rms(h_ref[...], g_ref[...]).astype(BF16)
    cosf = cos_ref[...]
    sinf = sin_ref[...]
    half = C_HD // 2

    def norm_rope(x, gain):
        x = _rms(x, gain)
        swapped = jnp.concatenate([x[:, half:], x[:, :half]], axis=1)
        return x * cosf + swapped * sinf

    k = _dot(u, wk_ref[...])
    half_w = C_HEADS * C_HD // 2
    q_lo = _dot(u, wq_ref[:, :half_w])
    for hh in range(C_KV):
        sl = slice(hh * C_HD, (hh + 1) * C_HD)
        k_ref[:, sl] = norm_rope(k[:, sl], kg_ref[...]).astype(BF16)
    q_hi = _dot(u, wq_ref[:, half_w:])
    for hh in range(C_HEADS // 2):
        sl = slice(hh * C_HD, (hh + 1) * C_HD)
        q_ref[:, sl] = (norm_rope(q_lo[:, sl], qg_ref[...]) * (C_HD ** -0.5 * LOG2E)).astype(BF16)
    v_ref[...] = _dot(u, wv_ref[...]).astype(BF16)
    for hh in range(C_HEADS // 2):
        sl = slice(hh * C_HD, (hh + 1) * C_HD)
        q_ref[:, half_w + hh * C_HD:half_w + (hh + 1) * C_HD] = (
            norm_rope(q_hi[:, sl], qg_ref[...]) * (C_HD ** -0.5 * LOG2E)).astype(BF16)


def _c_proj(h, ntt, seq, g, wq, wk, wv, qg, kg, cosf, sinf):
    r, d = h.shape
    tm = C_ROW_TILE
    per_seq = seq // tm
    n_tok_tiles = ntt * (ROW_TILE // tm)
    row = lambda n: pl.BlockSpec((tm, n), lambda i: (i, 0))
    full = lambda a: pl.BlockSpec(a.shape, lambda i: (0, 0))
    pos = pl.BlockSpec((tm, C_HD), lambda i: (jnp.where(i < n_tok_tiles, i % per_seq, per_seq), 0))
    return pl.pallas_call(
        _c_proj_kernel,
        grid=(r // tm,),
        in_specs=[row(d), full(g), full(wq), full(wk), full(wv), full(qg), full(kg), pos, pos],
        out_specs=[row(C_HEADS * C_HD), row(C_KV * C_HD), row(C_KV * C_HD)],
        out_shape=[jax.ShapeDtypeStruct((r, C_HEADS * C_HD), BF16),
                   jax.ShapeDtypeStruct((r, C_KV * C_HD), BF16),
                   jax.ShapeDtypeStruct((r, C_KV * C_HD), BF16)],
        compiler_params=_cparams("parallel"),
        name="c_proj",
    )(h, g, wq, wk, wv, qg, kg, cosf, sinf)


ATT_TK = 2048
ATT_QB = 4


def _dense_kernel(q_ref, k_ref, kh_ref, v_ref, vh_ref, y_ref, *scratch):
    grp = C_HEADS // C_KV
    nkb = k_ref.shape[0] // ATT_TK
    nq = ATT_QB
    sa_s, sb_s, acc_s = scratch[:nq], scratch[nq:2 * nq], scratch[2 * nq:]
    qs = [jnp.concatenate([q_ref[c * BLK:(c + 1) * BLK, g * C_HD:(g + 1) * C_HD] for g in range(grp)], axis=0)
          for c in range(nq)]
    m_rows = grp * BLK

    def keys(t):
        return pl.ds(pl.multiple_of(t * ATT_TK, ATT_TK), ATT_TK)

    def v_ones(v):
        return jnp.concatenate([v, jnp.ones(v.shape, BF16)], axis=1)

    def scores(t, s_refs):
        kt = k_ref[keys(t), :]
        for c in range(nq):
            s_refs[c][...] = _dot_nt(qs[c], kt)

    def step(t, ms, s_refs):
        vt = v_ones(v_ref[keys(t), :])
        out = []
        for c in range(nq):
            s = s_refs[c][...]
            m_new = jnp.maximum(ms[c], jnp.max(s, axis=-1, keepdims=True))
            p = jnp.exp2(s - m_new).astype(BF16)
            acc_s[c][...] = jnp.exp2(ms[c] - m_new) * acc_s[c][...] + _dot(p, vt)
            out.append(m_new)
        return out

    scores(0, sa_s)
    k0 = kh_ref[...]
    v0 = v_ones(vh_ref[...])
    kc = lax.broadcasted_iota(jnp.int32, (m_rows, BLK), 1)
    ms = []
    for c in range(nq):
        s0 = jnp.where(kc >= FRONT, _dot_nt(qs[c], k0), NEG)
        m = jnp.max(s0, axis=-1, keepdims=True)
        acc_s[c][...] = _dot(jnp.exp2(s0 - m).astype(BF16), v0)
        ms.append(m)

    def body(j, ms):
        scores(2 * j + 1, sb_s)
        ms = step(2 * j, ms, sa_s)
        scores(2 * j + 2, sa_s)
        return step(2 * j + 1, ms, sb_s)

    ms = lax.fori_loop(0, nkb // 2 - 1, body, ms)
    scores(nkb - 1, sb_s)
    ms = step(nkb - 2, ms, sa_s)
    ms = step(nkb - 1, ms, sb_s)
    for c in range(nq):
        acc = acc_s[c][...]
        o = acc[:, :C_HD] / acc[:, C_HD:C_HD + 1]
        for g in range(grp):
            y_ref[c * BLK:(c + 1) * BLK, g * C_HD:(g + 1) * C_HD] = o[g * BLK:(g + 1) * BLK, :].astype(y_ref.dtype)


def _dense(q, k, v, b, n_tok):
    seq = n_tok // b
    grp = C_HEADS // C_KV
    tq = ATT_QB * BLK
    assert seq % tq == 0 and seq % (2 * ATT_TK) == 0
    per_seq = seq // tq
    head0 = n_tok // BLK
    score = pltpu.VMEM((grp * BLK, ATT_TK), F32)
    tok = pl.BlockSpec((seq, C_HD), lambda i, j, t: (i, j))
    head = pl.BlockSpec((BLK, C_HD), lambda i, j, t: (head0 + i, j))
    return pl.pallas_call(
        _dense_kernel,
        grid=(b, C_KV, per_seq),
        in_specs=[pl.BlockSpec((tq, grp * C_HD), lambda i, j, t: (i * per_seq + t, j)), tok, head, tok, head],
        out_specs=pl.BlockSpec((tq, grp * C_HD), lambda i, j, t: (i * per_seq + t, j)),
        out_shape=jax.ShapeDtypeStruct((n_tok, C_HEADS * C_HD), BF16),
        scratch_shapes=[score] * (2 * ATT_QB) + [pltpu.VMEM((grp * BLK, 2 * C_HD), F32)] * ATT_QB,
        compiler_params=_cparams("parallel", "parallel", "arbitrary"),
        name="dense_mixer",
    )(q, k, k, v, v)


def _rope_tables(seq):
    rows = seq // GRID_W
    row = jnp.repeat(jnp.arange(rows), GRID_W)
    col = jnp.tile(jnp.arange(GRID_W), rows)
    head = jnp.tile(jnp.concatenate([jnp.zeros((FRONT,), jnp.int32), jnp.arange(N_META) - N_META]), ROW_TILE // BLK)
    row = jnp.concatenate([row, head]).astype(F32)
    col = jnp.concatenate([col, head]).astype(F32)
    axis_dim = C_HD // 2
    freqs = ROPE_THETA ** (-jnp.arange(0, axis_dim, 2, dtype=F32) / axis_dim)
    ang = jnp.concatenate([row[:, None] * freqs, col[:, None] * freqs], axis=-1)
    cos, sin = jnp.cos(ang), jnp.sin(ang)
    return jnp.concatenate([cos, cos], axis=-1), jnp.concatenate([-sin, sin], axis=-1)


def _gate_weight(w_b, w_a):
    d = w_b.shape[0]
    w_b = w_b.reshape(d, 2, A_HEADS)
    w_a = w_a.reshape(d, 2, A_HEADS)
    per_head = jnp.concatenate([w_b, w_a, w_a], axis=1)
    per_head = jnp.transpose(per_head, (0, 2, 1))
    per_head = jnp.pad(per_head, ((0, 0), (0, 0), (0, BLK - 6)))
    return per_head.reshape(d, A_HEADS * BLK)


def _gate_rows(p):
    t = jnp.transpose(p.astype(F32), (1, 0))
    rows = jnp.concatenate([jnp.zeros_like(t), t, t], axis=1)
    rows = jnp.pad(rows, ((0, 0), (0, GATE_ROWS - 6)))
    return jnp.broadcast_to(rows[:, :, None], (A_HEADS, GATE_ROWS, BLK))


def kernel(x, meta_tokens, attn_norm_g, mlp_norm_g, w_in_ab, conv_w_a, a_log, dt_bias, a_out_norm_g,
           b_q_norm_g, b_k_norm_g, b_sink, w_out_ab, w_qkv_c, c_q_norm_g, c_k_norm_g, w_out_c, w_ff1, w_ff2):
    bsz, seq, d = x.shape
    n_tok = bsz * seq
    n_rows = n_tok + bsz * BLK
    ntt = n_tok // ROW_TILE
    assert attn_norm_g.shape[0] == 2 and seq % ROW_TILE == 0 and (bsz * BLK) % ROW_TILE == 0
    x2 = x.reshape(n_tok, d)
    meta = jnp.broadcast_to(meta_tokens.astype(x.dtype)[None], (bsz, N_META, d))
    head = jnp.concatenate([jnp.zeros((bsz, FRONT, d), x.dtype), meta], axis=1).reshape(bsz * BLK, d)
    row2 = lambda v: v.astype(F32).reshape(1, -1)

    w = w_in_ab[0]
    qkv_w = w[:, :1536].astype(BF16)
    z_w = w[:, 1536:2048].astype(BF16)
    gate_w = _gate_weight(w[:, 2048:2056], w[:, 2056:2064]).astype(BF16)
    bq_w = w[:, 2064:2576].astype(BF16)
    bkv_w = w[:, 2576:2832].astype(BF16)
    qkv, z, gate_pre, qb, kvb = _ab_proj(x2, head, row2(attn_norm_g[0]), qkv_w, z_w, gate_w, bq_w, bkv_w,
                                         row2(b_q_norm_g[0]), row2(b_k_norm_g[0]))
    ya = _delta(qkv, z, gate_pre, bsz, n_tok, _gate_rows(a_log[0]), _gate_rows(dt_bias[0]),
                conv_w_a[0].astype(F32), row2(a_out_norm_g[0]))
    sink_rows = jnp.broadcast_to(b_sink[0].astype(F32)[:, None] * LOG2E, (B_HEADS, BLK))
    yb = _window(qb, kvb, bsz, n_tok, sink_rows)
    h = _out_mlp(n_rows, ntt, [(x2, head), tuple(ya), yb], w_out_ab[0].astype(BF16), row2(mlp_norm_g[0]),
                 w_ff1[0].astype(BF16), w_ff2[0].astype(BF16))

    w = w_qkv_c[0]
    deint = jnp.concatenate([jnp.arange(0, C_HD, 2), jnp.arange(1, C_HD, 2)])
    perm = lambda wc, nh: wc.reshape(d, nh, C_HD)[:, :, deint].reshape(d, nh * C_HD)
    wq = perm(w[:, :C_HEADS * C_HD], C_HEADS).astype(BF16)
    wk = perm(w[:, C_HEADS * C_HD:(C_HEADS + C_KV) * C_HD], C_KV).astype(BF16)
    wv = w[:, (C_HEADS + C_KV) * C_HD:].astype(BF16)
    cosf, sinf = _rope_tables(seq)
    q, k, v = _c_proj(h, ntt, seq, row2(attn_norm_g[1]), wq, wk, wv,
                      row2(c_q_norm_g[0][deint]), row2(c_k_norm_g[0][deint]), cosf, sinf)
    att = _dense(q, k, v, bsz, n_tok)
    out = _out_mlp(n_tok, ntt, [h, att], w_out_c[0].astype(BF16), row2(mlp_norm_g[1]),
                   w_ff1[1].astype(BF16), w_ff2[1].astype(BF16))
    return out.reshape(bsz, seq, d)
```

```python
import functools
import math

import jax
import jax.numpy as jnp
from jax import lax
from jax.experimental import pallas as pl
from jax.experimental.pallas import tpu as pltpu

F32 = jnp.float32
BF16 = jnp.bfloat16

EPS = 1e-6
N_META = 16
BLK = 128
FRONT = BLK - N_META
GRID_W = 64
ROPE_THETA = 10000.0
A_HEADS, A_DK, A_CONV = 4, 128, 5
B_HEADS, B_KV, B_HD, B_WIN = 8, 2, 64, 128
C_HEADS, C_KV, C_HD = 8, 2, 128
NEG = -1e30
LOG2E = math.log2(math.e)

VMEM_LIMIT = 56 * 1024 * 1024
ROW_TILE = 512
C_ROW_TILE = 256
FF_CHUNK = 512
INV_SQUARINGS = BLK.bit_length() - 2
PREP_UNROLL = 11


def _chunk_group(nblk):
    return max(g for g in range(1, PREP_UNROLL + 1) if nblk % g == 0)


def _cparams(*sem):
    return pltpu.CompilerParams(dimension_semantics=sem, vmem_limit_bytes=VMEM_LIMIT)


def _sigmoid(x):
    return 1.0 / (1.0 + jnp.exp(-x))


def _silu(x):
    return x * _sigmoid(x)


def _softplus(x):
    return jnp.maximum(x, 0.0) + jnp.log1p(jnp.exp(-jnp.abs(x)))


def _rms(x, g):
    return x * lax.rsqrt(jnp.mean(x * x, axis=-1, keepdims=True) + EPS) * g


def _dot(a, b):
    return jnp.dot(a, b, preferred_element_type=F32)


def _dot_nt(a, b):
    return lax.dot_general(a, b, (((1,), (1,)), ((), ())), preferred_element_type=F32)


def _slab_spec(n_slabs, index):
    return pl.BlockSpec((n_slabs, ROW_TILE, BLK), lambda i: (0, index(i), 0))


def _pair_specs(tok, head):
    ntt = tok.shape[-2] // ROW_TILE
    index = (lambda i: jnp.minimum(i, ntt - 1), lambda i: jnp.maximum(i - ntt, 0))
    if tok.ndim == 3:
        return [_slab_spec(tok.shape[0], ix) for ix in index]
    flat = lambda ix: pl.BlockSpec((ROW_TILE, tok.shape[1]), lambda i: (ix(i), 0))
    return [flat(ix) for ix in index]


def _rows(v):
    return jnp.concatenate([v[c] for c in range(v.shape[0])], axis=1) if v.ndim == 3 else v


def _pick(tok_ref, head_ref, ntt):
    return _rows(jnp.where(pl.program_id(0) < ntt, tok_ref[...], head_ref[...]))


def _store_slabs(ref, first, value):
    for c in range(value.shape[1] // BLK):
        ref[first + c] = value[:, c * BLK:(c + 1) * BLK]


def _ab_proj_kernel(x_ref, hb_ref, g_ref, wqkv_ref, wz_ref, wg_ref, wq_ref, wkv_ref, bqg_ref, bkg_ref,
                    qkv_ref, z_ref, gate_ref, qb_ref, kvb_ref, *, ntt):
    u = _rms(_pick(x_ref, hb_ref, ntt), g_ref[...]).astype(BF16)
    qb = _dot(u, wq_ref[...])
    kv = _dot(u, wkv_ref[...])
    scale = B_HD ** -0.5 * LOG2E

    def q_heads(lo, hi):
        for hh in range(lo, hi):
            sl = slice(hh * B_HD, (hh + 1) * B_HD)
            qb_ref[:, sl] = (_rms(qb[:, sl], bqg_ref[...]) * scale).astype(BF16)

    _store_slabs(qkv_ref, 0, _dot(u, wqkv_ref[:, 0:512]))
    q_heads(0, 3)
    _store_slabs(qkv_ref, A_HEADS, _dot(u, wqkv_ref[:, 512:1024]))
    q_heads(3, 6)
    _store_slabs(qkv_ref, 2 * A_HEADS, _dot(u, wqkv_ref[:, 1024:1536]))
    q_heads(6, B_HEADS)
    _store_slabs(z_ref, 0, _dot(u, wz_ref[...]))
    for hh in range(B_KV):
        sl = slice(hh * B_HD, (hh + 1) * B_HD)
        kvb_ref[:, sl] = _rms(kv[:, sl], bkg_ref[...]).astype(BF16)
    kvb_ref[:, B_KV * B_HD:] = kv[:, B_KV * B_HD:].astype(BF16)
    _store_slabs(gate_ref, 0, _dot(u, wg_ref[...]))


def _ab_proj(x, hb, g, wqkv, wz, wg, wq, wkv, bqg, bkg):
    r = x.shape[0] + hb.shape[0]
    row = lambda n: pl.BlockSpec((ROW_TILE, n), lambda i: (i, 0))
    full = lambda a: pl.BlockSpec(a.shape, lambda i: (0, 0))
    return pl.pallas_call(
        functools.partial(_ab_proj_kernel, ntt=x.shape[0] // ROW_TILE),
        grid=(r // ROW_TILE,),
        in_specs=_pair_specs(x, hb) + [full(g), full(wqkv), full(wz), full(wg), full(wq), full(wkv), full(bqg),
                                       full(bkg)],
        out_specs=[_slab_spec(3 * A_HEADS, lambda i: i), _slab_spec(A_HEADS, lambda i: i),
                   _slab_spec(A_HEADS, lambda i: i), row(512), row(256)],
        out_shape=[jax.ShapeDtypeStruct((3 * A_HEADS, r, BLK), F32), jax.ShapeDtypeStruct((A_HEADS, r, BLK), F32),
                   jax.ShapeDtypeStruct((A_HEADS, r, BLK), F32), jax.ShapeDtypeStruct((r, 512), BF16),
                   jax.ShapeDtypeStruct((r, 256), BF16)],
        compiler_params=_cparams("parallel"),
        name="ab_proj",
    )(x, hb, g, wqkv, wz, wg, wq, wkv, bqg, bkg)


GATE_ROWS = 8


def _split3(x):
    hi = x.astype(BF16)
    r1 = x - hi.astype(F32)
    mid = r1.astype(BF16)
    lo = (r1 - mid.astype(F32)).astype(BF16)
    return hi, mid, lo


def _seq_block(tok_ref, head_ref, n):
    tok = tok_ref[pl.ds(pl.multiple_of(jnp.maximum(n - 1, 0) * BLK, BLK), BLK), :]
    return jnp.where(n == 0, head_ref[...], tok)


def _seq_views(n_tok, seq, off):
    head0 = n_tok // BLK
    return [pl.BlockSpec((None, seq, BLK), lambda i, j: (j + off, i, 0)),
            pl.BlockSpec((None, BLK, BLK), lambda i, j: (j + off, head0 + i, 0))]


def _gate_block(pre, n, neg_a, dtb):
    nr = GATE_ROWS
    t = pre.T[0:nr, :]
    ri = lax.broadcasted_iota(jnp.int32, (BLK, BLK), 0)
    ci = lax.broadcasted_iota(jnp.int32, (BLK, BLK), 1)
    role = ri[0:nr, :]
    live = (ci[0:nr, :] + n * BLK) >= FRONT
    beta = jnp.where(live, _sigmoid(t), 0.0)
    g = jnp.where(live, neg_a * _softplus(t + dtb), 0.0)
    parts = [p.astype(F32) for p in _split3(g)]
    parts = jnp.concatenate(parts + [jnp.zeros((BLK - 3 * nr, BLK), F32)], axis=0).astype(BF16)
    tri = jnp.concatenate([(ri <= ci).astype(BF16), (ri >= ci).astype(BF16)], axis=1)
    sums = _dot(parts, tri)
    both = sums[0:nr] + sums[nr:2 * nr] + sums[2 * nr:3 * nr]
    pre_sum, suf_sum = both[:, :BLK], both[:, BLK:]
    tot = pre_sum + suf_sum - g
    row = jnp.where(role < 2, beta, jnp.where(role == 2, pre_sum, jnp.where(role == 3, suf_sum, tot)))
    col = jnp.concatenate([row, jnp.zeros((BLK - nr, BLK), F32)], axis=0).T
    return col, row


def _delta_kernel(q_ref, qh_ref, k_ref, kh_ref, v_ref, vh_ref, z_ref, zh_ref, gp_ref, gph_ref, alog_ref, dtb_ref,
                  cwq_ref, cwk_ref, cwv_ref, og_ref, y_ref, yh_ref, sadd_s, smul_s, o_s, omul_s, gl_s):
    seq = q_ref.shape[0]
    nblk = seq // BLK + 1
    grp = _chunk_group(nblk)
    neg_a = -jnp.exp(alog_ref[...])
    dtb = dtb_ref[...]
    ri = lax.broadcasted_iota(jnp.int32, (BLK, BLK), 0)
    ci = lax.broadcasted_iota(jnp.int32, (BLK, BLK), 1)
    eye = (ri == ci).astype(F32)
    incl = (ri >= ci, ri <= ci)
    strict = (ri > ci, ri < ci)

    hw = A_CONV // 2

    def conv_silu(ref, head_ref, w_ref, n, maybe_edge):
        base = jnp.clip((n - 1) * BLK, hw, seq - BLK - hw)
        acc = ref[pl.ds(base - hw, BLK), :] * w_ref[0:1, :]
        for j in range(1, A_CONV):
            acc = acc + ref[pl.ds(base - hw + j, BLK), :] * w_ref[j:j + 1, :]
        if maybe_edge:
            cur = _seq_block(ref, head_ref, n)
            tok_prev = ref[pl.ds(pl.multiple_of(jnp.maximum((n - 1) * BLK - 8, 0), 8), 8), :]
            prev = jnp.where(n == 0, 0.0, jnp.where(n == 1, head_ref[BLK - 8:, :], tok_prev))
            nxt = ref[pl.ds(pl.multiple_of(jnp.minimum(n * BLK, seq - 8), 8), 8), :]
            nxt = jnp.where(n < nblk - 1, nxt, 0.0)
            win = jnp.concatenate([prev, cur, nxt], axis=0)
            edge = win[8 - hw:8 - hw + BLK, :] * w_ref[0:1, :]
            for j in range(1, A_CONV):
                edge = edge + win[8 - hw + j:8 - hw + j + BLK, :] * w_ref[j:j + 1, :]
            acc = jnp.where((n <= 1) | (n == nblk - 1), edge, acc)
        return _silu(acc)

    def l2n(x):
        return x * lax.rsqrt(jnp.sum(x * x, axis=-1, keepdims=True) + EPS)

    edge_slots = {0 % grp, 1 % grp, (nblk - 1) % grp}

    def chunk_inputs(n, slot):
        rows = pl.ds(pl.multiple_of(n * BLK, BLK), BLK)
        live = (ri[:, 0:1] + n * BLK) >= FRONT
        edge = slot in edge_slots
        qn = jnp.where(live, l2n(conv_silu(q_ref, qh_ref, cwq_ref, n, edge)) * (A_DK ** -0.5), 0.0)
        kn = jnp.where(live, l2n(conv_silu(k_ref, kh_ref, cwk_ref, n, edge)), 0.0)
        vv = jnp.where(live, conv_silu(v_ref, vh_ref, cwv_ref, n, edge), 0.0)
        kn16 = kn.astype(BF16)
        kq = _dot_nt(jnp.concatenate([kn16, qn.astype(BF16)], axis=0), kn16)
        bg, gt = _gate_block(_seq_block(gp_ref, gph_ref, n), n, neg_a, dtb)
        return dict(n=n, rows=rows, qn=qn, kn=kn, vv=vv, kk=kq[:BLK], qk=kq[BLK:], bg=bg, gt=gt)

    def chain_setup(c, d):
        bg, gt = c["bg"], c["gt"]
        beta, ccol, tot = bg[:, d:d + 1], bg[:, 2 + d:3 + d], bg[:, 4 + d:5 + d]
        crow = gt[2 + d:3 + d, :]
        dec = jnp.exp(jnp.where(incl[d], ccol - crow, NEG))
        a = jnp.where(strict[d], beta * c["kk"] * dec, 0.0)
        return dict(c=c, d=d, beta=beta, ccol=ccol, tot=tot, dec=dec, a=a, t=eye - a, x=a.astype(BF16))

    def prep(g, carry):
        chunks = [chunk_inputs(g * grp + j, j) for j in range(grp)]
        chains = [chain_setup(c, d) for c in chunks for d in range(2)]
        zero = jnp.zeros((BLK, BLK), BF16)

        def blockdiag(xp):
            return jnp.concatenate([jnp.concatenate([xp[:, :BLK], zero], axis=1),
                                    jnp.concatenate([zero, xp[:, BLK:]], axis=1)], axis=0)

        pairs = [(chains[2 * j], chains[2 * j + 1]) for j in range(grp)]
        xps = [jnp.concatenate([f["x"], b["x"]], axis=1) for f, b in pairs]
        tps = [jnp.concatenate([f["t"], b["t"]], axis=1) for f, b in pairs]
        for _ in range(INV_SQUARINGS):
            xps = [_dot(xp, blockdiag(xp)).astype(BF16) for xp in xps]
            txs = [_dot(tp.astype(BF16), blockdiag(xp)) for tp, xp in zip(tps, xps)]
            tps = [tp + tx for tp, tx in zip(tps, txs)]
        for (f, b), tp in zip(pairs, tps):
            f["t"], b["t"] = tp[:, :BLK], tp[:, BLK:]
        for ch in chains:
            c = ch["c"]
            ch["ec"] = jnp.exp(ch["ccol"])
            ch["rhs"] = jnp.concatenate([ch["beta"] * c["vv"], ch["beta"] * c["kn"] * ch["ec"]], axis=1)
            ch["t16"] = ch["t"].astype(BF16)
        x0s = [_dot(ch["t16"], ch["rhs"].astype(BF16)) for ch in chains]
        res = []
        for ch, x0 in zip(chains, x0s):
            ax = _dot(ch["a"].astype(BF16), x0.astype(BF16))
            res.append((ch["rhs"] - x0 - ax).astype(BF16))
        uws = [(x0 + _dot(ch["t16"], e)).astype(BF16) for ch, x0, e in zip(chains, x0s, res)]
        kuws = [_dot((ch["c"]["kn"] * jnp.exp(ch["tot"] - ch["ccol"])).T.astype(BF16), uw)
                for ch, uw in zip(chains, uws)]
        quws = [_dot((ch["c"]["qk"] * ch["dec"]).astype(BF16), uw) for ch, uw in zip(chains, uws)]
        for ch, kuw, quw in zip(chains, kuws, quws):
            c, d = ch["c"], ch["d"]
            rows = c["rows"]
            sadd_s[d, rows, :] = kuw[:, :BLK]
            smul_s[d, rows, :] = (-kuw[:, BLK:]).astype(BF16)
            o_s[d, rows, :] = quw[:, :BLK]
            omul_s[d, rows, :] = (c["qn"] * ch["ec"] - quw[:, BLK:]).astype(BF16)
            gl_s[d * nblk + c["n"]] = jnp.broadcast_to(jnp.exp(ch["tot"]), (BLK, BLK))[0:8, :]
        return carry

    lax.fori_loop(0, nblk // grp, prep, 0)

    def scan_step(d, n, s):
        rows = pl.ds(pl.multiple_of(n * BLK, BLK), BLK)
        both = _dot(jnp.concatenate([smul_s[d, rows, :], omul_s[d, rows, :]], axis=0), s.astype(BF16))
        o_s[d, rows, :] = o_s[d, rows, :] + both[BLK:]
        return s * gl_s[d * nblk + n][0:1, :] + both[:BLK] + sadd_s[d, rows, :]

    def gated(rows, z):
        o = o_s[0, rows, :] + o_s[1, rows, :]
        return (_rms(o, og_ref[...]) * _silu(z)).astype(y_ref.dtype)

    def finish(n):
        tok_rows = pl.ds(pl.multiple_of((n - 1) * BLK, BLK), BLK)
        y_ref[tok_rows, :] = gated(pl.ds(pl.multiple_of(n * BLK, BLK), BLK), z_ref[tok_rows, :])

    def scan(i, carry):
        sf, sb = carry
        sf = scan_step(0, i, sf)
        sb = scan_step(1, nblk - 1 - i, sb)
        return sf, sb

    def scan_and_finish(i, carry):
        finish(i - 1)
        finish(nblk - i)
        return scan(i, carry)

    s0 = jnp.zeros((BLK, BLK), F32)
    first_done = nblk // 2 + 1
    carry = lax.fori_loop(0, first_done, scan, (s0, s0))
    lax.fori_loop(first_done, nblk, scan_and_finish, carry)
    finish(nblk - 1)
    yh_ref[...] = gated(slice(0, BLK), zh_ref[...])


def _delta(qkv, z, gate_pre, b, n_tok, alog_rows, dtb_rows, conv_w, o_gain):
    seq = n_tok // b
    lp = seq + BLK
    nblk = lp // BLK
    cw = lambda off: pl.BlockSpec((A_CONV, BLK), lambda i, j: (0, j + off))
    views = lambda off: _seq_views(n_tok, seq, off)
    per_head = pl.BlockSpec((None, GATE_ROWS, BLK), lambda i, j: (j, 0, 0))
    return pl.pallas_call(
        _delta_kernel,
        grid=(b, A_HEADS),
        in_specs=views(0) + views(A_HEADS) + views(2 * A_HEADS) + views(0) + views(0)
                 + [per_head, per_head, cw(0), cw(A_HEADS), cw(2 * A_HEADS),
                    pl.BlockSpec((1, BLK), lambda i, j: (0, 0))],
        out_specs=[pl.BlockSpec((None, seq, BLK), lambda i, j: (j, i, 0)),
                   pl.BlockSpec((None, BLK, BLK), lambda i, j: (j, i, 0))],
        out_shape=[jax.ShapeDtypeStruct((A_HEADS, n_tok, BLK), BF16),
                   jax.ShapeDtypeStruct((A_HEADS, b * BLK, BLK), BF16)],
        scratch_shapes=[pltpu.VMEM((2, lp, BLK), F32), pltpu.VMEM((2, lp, BLK), BF16),
                        pltpu.VMEM((2, lp, BLK), F32), pltpu.VMEM((2, lp, BLK), BF16),
                        pltpu.VMEM((2 * nblk, 8, BLK), F32)],
        compiler_params=_cparams("parallel", "parallel"),
        name="delta_mixer",
    )(qkv, qkv, qkv, qkv, qkv, qkv, z, z, gate_pre, gate_pre, alog_rows, dtb_rows, conv_w, conv_w, conv_w, o_gain)


def _window_kernel(q_ref, kp_ref, kc_ref, kn_ref, km_ref, bias_ref, y_ref):
    i = pl.program_id(1)
    nblk = pl.num_programs(1)
    grp = B_HEADS // B_KV
    nk = 4 * BLK
    c = lax.broadcasted_iota(jnp.int32, (1, nk), 1)
    kblk = i - 1 + (c >> 7)
    edge = jnp.where((c >= 3 * BLK) | ((kblk >= 1) & (kblk < nblk)), 0.0, NEG)
    q = q_ref[...]
    kvs = (kp_ref[...], kc_ref[...], kn_ref[...], km_ref[...])
    ones = jnp.ones((nk, 2 * B_HD), BF16)
    lane = lax.broadcasted_iota(jnp.int32, (BLK, 2 * B_HD), 1)
    s4s, vexts = [], []
    for kvh in range(B_KV):
        ks = jnp.concatenate([t[:, kvh * B_HD:(kvh + 1) * B_HD] for t in kvs], axis=0)
        vs = jnp.concatenate([t[:, (B_KV + kvh) * B_HD:(B_KV + kvh + 1) * B_HD] for t in kvs], axis=0)
        q4 = jnp.concatenate([q[:, hh * B_HD:(hh + 1) * B_HD] for hh in range(kvh * grp, (kvh + 1) * grp)],
                             axis=0)
        s4s.append(_dot_nt(q4, ks))
        vexts.append(jnp.concatenate([vs, vs, ones], axis=1))
    pvs = []
    for kvh in range(B_KV):
        ps = []
        for gi in range(grp):
            hh = kvh * grp + gi
            s = s4s[kvh][gi * BLK:(gi + 1) * BLK] + bias_ref[hh] + edge
            ps.append(jnp.exp2(s - jnp.max(s, axis=-1, keepdims=True)).astype(BF16))
        pvs.append(_dot(jnp.concatenate(ps, axis=0), vexts[kvh]))
    outs = []
    for hh in range(B_HEADS):
        kvh, gi = divmod(hh, grp)
        o = pvs[kvh][gi * BLK:(gi + 1) * BLK]
        outs.append(o[:, :2 * B_HD] / o[:, 2 * B_HD:])
    for j in range(B_HEADS // 2):
        pair = jnp.where(lane < B_HD, outs[2 * j], outs[2 * j + 1])
        y_ref[:, 2 * j * B_HD:(2 * j + 2) * B_HD] = pair.astype(y_ref.dtype)

    @pl.when(i == 0)
    def _():
        rr = lax.broadcasted_iota(jnp.int32, y_ref.shape, 0)
        y_ref[...] = jnp.where(rr >= FRONT, y_ref[...], 0).astype(y_ref.dtype)


def _window_bias(sink):
    r = jnp.arange(BLK)[:, None]
    c = jnp.arange(4 * BLK)[None, :]
    dist = jnp.abs(BLK + r - c)
    slopes = jnp.exp2(-8.0 * (jnp.arange(B_HEADS, dtype=F32) + 1.0) / B_HEADS)
    band = (c < 3 * BLK) & (dist <= B_WIN)
    alibi = -slopes[:, None, None] * dist.astype(F32)[None] * LOG2E
    rest = jnp.where(c >= 3 * BLK + FRONT, 0.0, NEG)
    bias = jnp.where(band[None], alibi, rest[None])
    sink_col = (c == 3 * BLK)[None]
    return jnp.where(sink_col, sink.astype(F32)[:, None, None] * LOG2E, bias).astype(F32)


def _window(qb, kvb, b, n_tok, sink):
    nblk = n_tok // b // BLK + 1
    bias = _window_bias(sink)
    head0 = n_tok // BLK

    def blk(i, j):
        return jnp.where(j == 0, head0 + i, i * (nblk - 1) + j - 1)

    kv = lambda f: pl.BlockSpec((BLK, 2 * B_KV * B_HD), f)
    return pl.pallas_call(
        _window_kernel,
        grid=(b, nblk),
        in_specs=[pl.BlockSpec((BLK, B_HEADS * B_HD), lambda i, j: (blk(i, j), 0)),
                  kv(lambda i, j: (blk(i, jnp.maximum(j - 1, 0)), 0)),
                  kv(lambda i, j: (blk(i, j), 0)),
                  kv(lambda i, j: (blk(i, jnp.minimum(j + 1, nblk - 1)), 0)),
                  kv(lambda i, j: (head0 + i, 0)),
                  pl.BlockSpec(bias.shape, lambda i, j: (0, 0, 0))],
        out_specs=pl.BlockSpec((BLK, B_HEADS * B_HD), lambda i, j: (blk(i, j), 0)),
        out_shape=jax.ShapeDtypeStruct((qb.shape[0], B_HEADS * B_HD), BF16),
        compiler_params=_cparams("parallel", "parallel"),
        name="window_mixer",
    )(qb, kvb, kvb, kvb, kvb, bias)


def _out_mlp_kernel(*refs, arity, ntt):
    vals, pos = [], 0
    for a in arity:
        vals.append(refs[pos][...] if a == 1 else _pick(refs[pos], refs[pos + 1], ntt))
        pos += a
    wo_ref, g_ref, w1_ref, w2_ref, o_ref = refs[pos:]
    mix = jnp.concatenate(vals[1:], axis=1)
    h = vals[0] + _dot(mix, wo_ref[...])
    u = _rms(h, g_ref[...]).astype(BF16)
    dff = w1_ref.shape[1]
    acc = h
    for c in range(dff // FF_CHUNK):
        sl = slice(c * FF_CHUNK, (c + 1) * FF_CHUNK)
        a = jnp.maximum(_dot(u, w1_ref[:, sl]), 0.0)
        acc = acc + _dot((a * a).astype(BF16), w2_ref[sl, :])
    o_ref[...] = acc


def _out_mlp(rows_out, ntt, tensors, wo, g, w1, w2):
    d = wo.shape[1]
    row = lambda n: pl.BlockSpec((ROW_TILE, n), lambda i: (i, 0))
    full = lambda a: pl.BlockSpec(a.shape, lambda i: (0, 0))
    specs, args, arity = [], [], []
    for t in tensors:
        if isinstance(t, tuple):
            specs += _pair_specs(t[0], t[1])
            args += list(t)
            arity.append(2)
        else:
            specs.append(row(t.shape[1]))
            args.append(t)
            arity.append(1)
    return pl.pallas_call(
        functools.partial(_out_mlp_kernel, arity=tuple(arity), ntt=ntt),
        grid=(rows_out // ROW_TILE,),
        in_specs=specs + [full(wo), full(g), full(w1), full(w2)],
        out_specs=row(d),
        out_shape=jax.ShapeDtypeStruct((rows_out, d), F32),
        compiler_params=_cparams("parallel"),
        name="out_mlp",
    )(*args, wo, g, w1, w2)


def _c_proj_kernel(h_ref, g_ref, wq_ref, wk_ref, wv_ref, qg_ref, kg_ref, cos_ref, sin_ref,
                   q_ref, k_ref, v_ref):
    u = _rms(h_ref[...], g_ref[...]).astype(BF16)
    cosf = cos_ref[...]
    sinf = sin_ref[...]
    half = C_HD // 2

    def norm_rope(x, gain):
        x = _rms(x, gain)
        swapped = jnp.concatenate([x[:, half:], x[:, :half]], axis=1)
        return x * cosf + swapped * sinf

    k = _dot(u, wk_ref[...])
    half_w = C_HEADS * C_HD // 2
    q_lo = _dot(u, wq_ref[:, :half_w])
    for hh in range(C_KV):
        sl = slice(hh * C_HD, (hh + 1) * C_HD)
        k_ref[:, sl] = norm_rope(k[:, sl], kg_ref[...]).astype(BF16)
    q_hi = _dot(u, wq_ref[:, half_w:])
    for hh in range(C_HEADS // 2):
        sl = slice(hh * C_HD, (hh + 1) * C_HD)
        q_ref[:, sl] = (norm_rope(q_lo[:, sl], qg_ref[...]) * (C_HD ** -0.5 * LOG2E)).astype(BF16)
    v_ref[...] = _dot(u, wv_ref[...]).astype(BF16)
    for hh in range(C_HEADS // 2):
        sl = slice(hh * C_HD, (hh + 1) * C_HD)
        q_ref[:, half_w + hh * C_HD:half_w + (hh + 1) * C_HD] = (
            norm_rope(q_hi[:, sl], qg_ref[...]) * (C_HD ** -0.5 * LOG2E)).astype(BF16)


def _c_proj(h, ntt, seq, g, wq, wk, wv, qg, kg, cosf, sinf):
    r, d = h.shape
    tm = C_ROW_TILE
    per_seq = seq // tm
    n_tok_tiles = ntt * (ROW_TILE // tm)
    row = lambda n: pl.BlockSpec((tm, n), lambda i: (i, 0))
    full = lambda a: pl.BlockSpec(a.shape, lambda i: (0, 0))
    pos = pl.BlockSpec((tm, C_HD), lambda i: (jnp.where(i < n_tok_tiles, i % per_seq, per_seq), 0))
    return pl.pallas_call(
        _c_proj_kernel,
        grid=(r // tm,),
        in_specs=[row(d), full(g), full(wq), full(wk), full(wv), full(qg), full(kg), pos, pos],
        out_specs=[row(C_HEADS * C_HD), row(C_KV * C_HD), row(C_KV * C_HD)],
        out_shape=[jax.ShapeDtypeStruct((r, C_HEADS * C_HD), BF16),
                   jax.ShapeDtypeStruct((r, C_KV * C_HD), BF16),
                   jax.ShapeDtypeStruct((r, C_KV * C_HD), BF16)],
        compiler_params=_cparams("parallel"),
        name="c_proj",
    )(h, g, wq, wk, wv, qg, kg, cosf, sinf)


ATT_TK = 2048
ATT_QB = 4


def _dense_kernel(q_ref, k_ref, kh_ref, v_ref, vh_ref, y_ref, *scratch):
    grp = C_HEADS // C_KV
    nkb = k_ref.shape[0] // ATT_TK
    nq = ATT_QB
    sa_s, sb_s, acc_s = scratch[:nq], scratch[nq:2 * nq], scratch[2 * nq:]
    qs = [jnp.concatenate([q_ref[c * BLK:(c + 1) * BLK, g * C_HD:(g + 1) * C_HD] for g in range(grp)], axis=0)
          for c in range(nq)]
    m_rows = grp * BLK

    def keys(t):
        return pl.ds(pl.multiple_of(t * ATT_TK, ATT_TK), ATT_TK)

    def v_ones(v):
        return jnp.concatenate([v, jnp.ones(v.shape, BF16)], axis=1)

    def scores(t, s_refs):
        kt = k_ref[keys(t), :]
        for c in range(nq):
            s_refs[c][...] = _dot_nt(qs[c], kt)

    def step(t, ms, s_refs):
        vt = v_ones(v_ref[keys(t), :])
        out = []
        for c in range(nq):
            s = s_refs[c][...]
            m_new = jnp.maximum(ms[c], jnp.max(s, axis=-1, keepdims=True))
            p = jnp.exp2(s - m_new).astype(BF16)
            acc_s[c][...] = jnp.exp2(ms[c] - m_new) * acc_s[c][...] + _dot(p, vt)
            out.append(m_new)
        return out

    scores(0, sa_s)
    k0 = kh_ref[...]
    v0 = v_ones(vh_ref[...])
    kc = lax.broadcasted_iota(jnp.int32, (m_rows, BLK), 1)
    ms = []
    for c in range(nq):
        s0 = jnp.where(kc >= FRONT, _dot_nt(qs[c], k0), NEG)
        m = jnp.max(s0, axis=-1, keepdims=True)
        acc_s[c][...] = _dot(jnp.exp2(s0 - m).astype(BF16), v0)
        ms.append(m)

    def body(j, ms):
        scores(2 * j + 1, sb_s)
        ms = step(2 * j, ms, sa_s)
        scores(2 * j + 2, sa_s)
        return step(2 * j + 1, ms, sb_s)

    ms = lax.fori_loop(0, nkb // 2 - 1, body, ms)
    scores(nkb - 1, sb_s)
    ms = step(nkb - 2, ms, sa_s)
    ms = step(nkb - 1, ms, sb_s)
    for c in range(nq):
        acc = acc_s[c][...]
        o = acc[:, :C_HD] / acc[:, C_HD:C_HD + 1]
        for g in range(grp):
            y_ref[c * BLK:(c + 1) * BLK, g * C_HD:(g + 1) * C_HD] = o[g * BLK:(g + 1) * BLK, :].astype(y_ref.dtype)


def _dense(q, k, v, b, n_tok):
    seq = n_tok // b
    grp = C_HEADS // C_KV
    tq = ATT_QB * BLK
    assert seq % tq == 0 and seq % (2 * ATT_TK) == 0
    per_seq = seq // tq
    head0 = n_tok // BLK
    score = pltpu.VMEM((grp * BLK, ATT_TK), F32)
    tok = pl.BlockSpec((seq, C_HD), lambda i, j, t: (i, j))
    head = pl.BlockSpec((BLK, C_HD), lambda i, j, t: (head0 + i, j))
    return pl.pallas_call(
        _dense_kernel,
        grid=(b, C_KV, per_seq),
        in_specs=[pl.BlockSpec((tq, grp * C_HD), lambda i, j, t: (i * per_seq + t, j)), tok, head, tok, head],
        out_specs=pl.BlockSpec((tq, grp * C_HD), lambda i, j, t: (i * per_seq + t, j)),
        out_shape=jax.ShapeDtypeStruct((n_tok, C_HEADS * C_HD), BF16),
        scratch_shapes=[score] * (2 * ATT_QB) + [pltpu.VMEM((grp * BLK, 2 * C_HD), F32)] * ATT_QB,
        compiler_params=_cparams("parallel", "parallel", "arbitrary"),
        name="dense_mixer",
    )(q, k, k, v, v)


def _rope_tables(seq):
    rows = seq // GRID_W
    row = jnp.repeat(jnp.arange(rows), GRID_W)
    col = jnp.tile(jnp.arange(GRID_W), rows)
    head = jnp.tile(jnp.concatenate([jnp.zeros((FRONT,), jnp.int32), jnp.arange(N_META) - N_META]), ROW_TILE // BLK)
    row = jnp.concatenate([row, head]).astype(F32)
    col = jnp.concatenate([col, head]).astype(F32)
    axis_dim = C_HD // 2
    freqs = ROPE_THETA ** (-jnp.arange(0, axis_dim, 2, dtype=F32) / axis_dim)
    ang = jnp.concatenate([row[:, None] * freqs, col[:, None] * freqs], axis=-1)
    cos, sin = jnp.cos(ang), jnp.sin(ang)
    return jnp.concatenate([cos, cos], axis=-1), jnp.concatenate([-sin, sin], axis=-1)


def _gate_weight(w_b, w_a):
    d = w_b.shape[0]
    w_b = w_b.reshape(d, 2, A_HEADS)
    w_a = w_a.reshape(d, 2, A_HEADS)
    per_head = jnp.concatenate([w_b, w_a, w_a], axis=1)
    per_head = jnp.transpose(per_head, (0, 2, 1))
    per_head = jnp.pad(per_head, ((0, 0), (0, 0), (0, BLK - 6)))
    return per_head.reshape(d, A_HEADS * BLK)


def _gate_rows(p):
    t = jnp.transpose(p.astype(F32), (1, 0))
    rows = jnp.concatenate([jnp.zeros_like(t), t, t], axis=1)
    rows = jnp.pad(rows, ((0, 0), (0, GATE_ROWS - 6)))
    return jnp.broadcast_to(rows[:, :, None], (A_HEADS, GATE_ROWS, BLK))


def kernel(x, meta_tokens, attn_norm_g, mlp_norm_g, w_in_ab, conv_w_a, a_log, dt_bias, a_out_norm_g,
           b_q_norm_g, b_k_norm_g, b_sink, w_out_ab, w_qkv_c, c_q_norm_g, c_k_norm_g, w_out_c, w_ff1, w_ff2):
    bsz, seq, d = x.shape
    n_tok = bsz * seq
    n_rows = n_tok + bsz * BLK
    ntt = n_tok // ROW_TILE
    assert attn_norm_g.shape[0] == 2 and seq % ROW_TILE == 0 and (bsz * BLK) % ROW_TILE == 0
    x2 = x.reshape(n_tok, d)
    meta = jnp.broadcast_to(meta_tokens.astype(x.dtype)[None], (bsz, N_META, d))
    head = jnp.concatenate([jnp.zeros((bsz, FRONT, d), x.dtype), meta], axis=1).reshape(bsz * BLK, d)
    row2 = lambda v: v.astype(F32).reshape(1, -1)

    w = w_in_ab[0]
    qkv_w = w[:, :1536].astype(BF16)
    z_w = w[:, 1536:2048].astype(BF16)
    gate_w = _gate_weight(w[:, 2048:2056], w[:, 2056:2064]).astype(BF16)
    bq_w = w[:, 2064:2576].astype(BF16)
    bkv_w = w[:, 2576:2832].astype(BF16)
    qkv, z, gate_pre, qb, kvb = _ab_proj(x2, head, row2(attn_norm_g[0]), qkv_w, z_w, gate_w, bq_w, bkv_w,
                                         row2(b_q_norm_g[0]), row2(b_k_norm_g[0]))
    ya = _delta(qkv, z, gate_pre, bsz, n_tok, _gate_rows(a_log[0]), _gate_rows(dt_bias[0]),
                conv_w_a[0].astype(F32), row2(a_out_norm_g[0]))
    yb = _window(qb, kvb, bsz, n_tok, b_sink[0])
    h = _out_mlp(n_rows, ntt, [(x2, head), tuple(ya), yb], w_out_ab[0].astype(BF16), row2(mlp_norm_g[0]),
                 w_ff1[0].astype(BF16), w_ff2[0].astype(BF16))

    w = w_qkv_c[0]
    deint = jnp.concatenate([jnp.arange(0, C_HD, 2), jnp.arange(1, C_HD, 2)])
    perm = lambda wc, nh: wc.reshape(d, nh, C_HD)[:, :, deint].reshape(d, nh * C_HD)
    wq = perm(w[:, :C_HEADS * C_HD], C_HEADS).astype(BF16)
    wk = perm(w[:, C_HEADS * C_HD:(C_HEADS + C_KV) * C_HD], C_KV).astype(BF16)
    wv = w[:, (C_HEADS + C_KV) * C_HD:].astype(BF16)
    cosf, sinf = _rope_tables(seq)
    q, k, v = _c_proj(h, ntt, seq, row2(attn_norm_g[1]), wq, wk, wv,
                      row2(c_q_norm_g[0][deint]), row2(c_k_norm_g[0][deint]), cosf, sinf)
    att = _dense(q, k, v, bsz, n_tok)
    out = _out_mlp(n_tok, ntt, [h, att], w_out_c[0].astype(BF16), row2(mlp_norm_g[1]),
                   w_ff1[1].astype(BF16), w_ff2[1].astype(BF16))
    return out.reshape(bsz, seq, d)
```

```python
import functools
import math

import jax
import jax.numpy as jnp
from jax import lax
from jax.experimental import pallas as pl
from jax.experimental.pallas import tpu as pltpu

F32 = jnp.float32
BF16 = jnp.bfloat16

EPS = 1e-6
N_META = 16
BLK = 128
FRONT = BLK - N_META
GRID_W = 64
ROPE_THETA = 10000.0
A_HEADS, A_DK, A_CONV = 4, 128, 5
B_HEADS, B_KV, B_HD, B_WIN = 8, 2, 64, 128
C_HEADS, C_KV, C_HD = 8, 2, 128
NEG = -1e30
LOG2E = math.log2(math.e)

VMEM_LIMIT = 62 * 1024 * 1024
ROW_TILE = 512
C_ROW_TILE = 256
FF_CHUNK = 512
INV_SQUARINGS = BLK.bit_length() - 2
PREP_UNROLL = 11


def _chunk_group(nblk):
    return max(g for g in range(1, PREP_UNROLL + 1) if nblk % g == 0)


def _cparams(*sem):
    return pltpu.CompilerParams(dimension_semantics=sem, vmem_limit_bytes=VMEM_LIMIT)


def _sigmoid(x):
    return 1.0 / (1.0 + jnp.exp(-x))


def _silu(x):
    return x * _sigmoid(x)


def _softplus(x):
    return jnp.maximum(x, 0.0) + jnp.log1p(jnp.exp(-jnp.abs(x)))


def _rms(x, g):
    return x * lax.rsqrt(jnp.mean(x * x, axis=-1, keepdims=True) + EPS) * g


def _dot(a, b):
    return jnp.dot(a, b, preferred_element_type=F32)


def _dot_nt(a, b):
    return lax.dot_general(a, b, (((1,), (1,)), ((), ())), preferred_element_type=F32)


def _slab_spec(n_slabs, index):
    return pl.BlockSpec((n_slabs, ROW_TILE, BLK), lambda i: (0, index(i), 0))


def _pair_specs(tok, head):
    ntt = tok.shape[-2] // ROW_TILE
    index = (lambda i: jnp.minimum(i, ntt - 1), lambda i: jnp.maximum(i - ntt, 0))
    if tok.ndim == 3:
        return [_slab_spec(tok.shape[0], ix) for ix in index]
    flat = lambda ix: pl.BlockSpec((ROW_TILE, tok.shape[1]), lambda i: (ix(i), 0))
    return [flat(ix) for ix in index]


def _rows(v):
    return jnp.concatenate([v[c] for c in range(v.shape[0])], axis=1) if v.ndim == 3 else v


def _pick(tok_ref, head_ref, ntt):
    return _rows(jnp.where(pl.program_id(0) < ntt, tok_ref[...], head_ref[...]))


def _store_slabs(ref, first, value):
    for c in range(value.shape[1] // BLK):
        ref[first + c] = value[:, c * BLK:(c + 1) * BLK]


def _ab_proj_kernel(x_ref, hb_ref, g_ref, wqkv_ref, wz_ref, wg_ref, wq_ref, wkv_ref, bqg_ref, bkg_ref,
                    qkv_ref, z_ref, gate_ref, qb_ref, kvb_ref, *, ntt):
    u = _rms(_pick(x_ref, hb_ref, ntt), g_ref[...]).astype(BF16)
    qb = _dot(u, wq_ref[...])
    kv = _dot(u, wkv_ref[...])
    scale = B_HD ** -0.5 * LOG2E

    def q_heads(lo, hi):
        for hh in range(lo, hi):
            sl = slice(hh * B_HD, (hh + 1) * B_HD)
            qb_ref[:, sl] = (_rms(qb[:, sl], bqg_ref[...]) * scale).astype(BF16)

    _store_slabs(qkv_ref, 0, _dot(u, wqkv_ref[:, 0:512]))
    q_heads(0, 3)
    _store_slabs(qkv_ref, A_HEADS, _dot(u, wqkv_ref[:, 512:1024]))
    q_heads(3, 6)
    _store_slabs(qkv_ref, 2 * A_HEADS, _dot(u, wqkv_ref[:, 1024:1536]))
    q_heads(6, B_HEADS)
    _store_slabs(z_ref, 0, _dot(u, wz_ref[...]))
    for hh in range(B_KV):
        sl = slice(hh * B_HD, (hh + 1) * B_HD)
        kvb_ref[:, sl] = _rms(kv[:, sl], bkg_ref[...]).astype(BF16)
    kvb_ref[:, B_KV * B_HD:] = kv[:, B_KV * B_HD:].astype(BF16)
    _store_slabs(gate_ref, 0, _dot(u, wg_ref[...]))


def _ab_proj(x, hb, g, wqkv, wz, wg, wq, wkv, bqg, bkg):
    r = x.shape[0] + hb.shape[0]
    row = lambda n: pl.BlockSpec((ROW_TILE, n), lambda i: (i, 0))
    full = lambda a: pl.BlockSpec(a.shape, lambda i: (0, 0))
    return pl.pallas_call(
        functools.partial(_ab_proj_kernel, ntt=x.shape[0] // ROW_TILE),
        grid=(r // ROW_TILE,),
        in_specs=_pair_specs(x, hb) + [full(g), full(wqkv), full(wz), full(wg), full(wq), full(wkv), full(bqg),
                                       full(bkg)],
        out_specs=[_slab_spec(3 * A_HEADS, lambda i: i), _slab_spec(A_HEADS, lambda i: i),
                   _slab_spec(A_HEADS, lambda i: i), row(512), row(256)],
        out_shape=[jax.ShapeDtypeStruct((3 * A_HEADS, r, BLK), F32), jax.ShapeDtypeStruct((A_HEADS, r, BLK), F32),
                   jax.ShapeDtypeStruct((A_HEADS, r, BLK), F32), jax.ShapeDtypeStruct((r, 512), BF16),
                   jax.ShapeDtypeStruct((r, 256), BF16)],
        compiler_params=_cparams("parallel"),
        name="ab_proj",
    )(x, hb, g, wqkv, wz, wg, wq, wkv, bqg, bkg)


GATE_ROWS = 8


def _split3(x):
    hi = x.astype(BF16)
    r1 = x - hi.astype(F32)
    mid = r1.astype(BF16)
    lo = (r1 - mid.astype(F32)).astype(BF16)
    return hi, mid, lo


def _seq_block(tok_ref, head_ref, n):
    tok = tok_ref[pl.ds(pl.multiple_of(jnp.maximum(n - 1, 0) * BLK, BLK), BLK), :]
    return jnp.where(n == 0, head_ref[...], tok)


def _seq_views(n_tok, seq, off):
    head0 = n_tok // BLK
    return [pl.BlockSpec((None, seq, BLK), lambda i, j: (j + off, i, 0)),
            pl.BlockSpec((None, BLK, BLK), lambda i, j: (j + off, head0 + i, 0))]


def _gate_block(pre, n, neg_a, dtb):
    nr = GATE_ROWS
    t = pre.T[0:nr, :]
    ri = lax.broadcasted_iota(jnp.int32, (BLK, BLK), 0)
    ci = lax.broadcasted_iota(jnp.int32, (BLK, BLK), 1)
    role = ri[0:nr, :]
    live = (ci[0:nr, :] + n * BLK) >= FRONT
    beta = jnp.where(live, _sigmoid(t), 0.0)
    g = jnp.where(live, neg_a * _softplus(t + dtb), 0.0)
    parts = [p.astype(F32) for p in _split3(g)]
    parts = jnp.concatenate(parts + [jnp.zeros((BLK - 3 * nr, BLK), F32)], axis=0).astype(BF16)
    tri = jnp.concatenate([(ri <= ci).astype(BF16), (ri >= ci).astype(BF16)], axis=1)
    sums = _dot(parts, tri)
    both = sums[0:nr] + sums[nr:2 * nr] + sums[2 * nr:3 * nr]
    pre_sum, suf_sum = both[:, :BLK], both[:, BLK:]
    tot = pre_sum + suf_sum - g
    row = jnp.where(role < 2, beta, jnp.where(role == 2, pre_sum, jnp.where(role == 3, suf_sum, tot)))
    col = jnp.concatenate([row, jnp.zeros((BLK - nr, BLK), F32)], axis=0).T
    return col, row


def _delta_kernel(q_ref, qh_ref, k_ref, kh_ref, v_ref, vh_ref, z_ref, zh_ref, gp_ref, gph_ref, alog_ref, dtb_ref,
                  cwq_ref, cwk_ref, cwv_ref, og_ref, y_ref, yh_ref, sadd_s, smul_s, o_s, omul_s, gl_s):
    seq = q_ref.shape[0]
    nblk = seq // BLK + 1
    grp = _chunk_group(nblk)
    neg_a = -jnp.exp(alog_ref[...])
    dtb = dtb_ref[...]
    ri = lax.broadcasted_iota(jnp.int32, (BLK, BLK), 0)
    ci = lax.broadcasted_iota(jnp.int32, (BLK, BLK), 1)
    eye = (ri == ci).astype(F32)
    incl = (ri >= ci, ri <= ci)
    strict = (ri > ci, ri < ci)

    hw = A_CONV // 2

    def conv_silu(ref, head_ref, w_ref, n, maybe_edge):
        base = jnp.clip((n - 1) * BLK, hw, seq - BLK - hw)
        acc = ref[pl.ds(base - hw, BLK), :] * w_ref[0:1, :]
        for j in range(1, A_CONV):
            acc = acc + ref[pl.ds(base - hw + j, BLK), :] * w_ref[j:j + 1, :]
        if maybe_edge:
            cur = _seq_block(ref, head_ref, n)
            tok_prev = ref[pl.ds(pl.multiple_of(jnp.maximum((n - 1) * BLK - 8, 0), 8), 8), :]
            prev = jnp.where(n == 0, 0.0, jnp.where(n == 1, head_ref[BLK - 8:, :], tok_prev))
            nxt = ref[pl.ds(pl.multiple_of(jnp.minimum(n * BLK, seq - 8), 8), 8), :]
            nxt = jnp.where(n < nblk - 1, nxt, 0.0)
            win = jnp.concatenate([prev, cur, nxt], axis=0)
            edge = win[8 - hw:8 - hw + BLK, :] * w_ref[0:1, :]
            for j in range(1, A_CONV):
                edge = edge + win[8 - hw + j:8 - hw + j + BLK, :] * w_ref[j:j + 1, :]
            acc = jnp.where((n <= 1) | (n == nblk - 1), edge, acc)
        return _silu(acc)

    def l2n(x):
        return x * lax.rsqrt(jnp.sum(x * x, axis=-1, keepdims=True) + EPS)

    edge_slots = {0 % grp, 1 % grp, (nblk - 1) % grp}

    def chunk_inputs(n, slot):
        rows = pl.ds(pl.multiple_of(n * BLK, BLK), BLK)
        live = (ri[:, 0:1] + n * BLK) >= FRONT
        edge = slot in edge_slots
        qn = jnp.where(live, l2n(conv_silu(q_ref, qh_ref, cwq_ref, n, edge)) * (A_DK ** -0.5), 0.0)
        kn = jnp.where(live, l2n(conv_silu(k_ref, kh_ref, cwk_ref, n, edge)), 0.0)
        vv = jnp.where(live, conv_silu(v_ref, vh_ref, cwv_ref, n, edge), 0.0)
        kn16 = kn.astype(BF16)
        kq = _dot_nt(jnp.concatenate([kn16, qn.astype(BF16)], axis=0), kn16)
        bg, gt = _gate_block(_seq_block(gp_ref, gph_ref, n), n, neg_a, dtb)
        return dict(n=n, rows=rows, qn=qn, kn=kn, vv=vv, kk=kq[:BLK], qk=kq[BLK:], bg=bg, gt=gt)

    def chain_setup(c, d):
        bg, gt = c["bg"], c["gt"]
        beta, ccol, tot = bg[:, d:d + 1], bg[:, 2 + d:3 + d], bg[:, 4 + d:5 + d]
        crow = gt[2 + d:3 + d, :]
        dec = jnp.exp(jnp.where(incl[d], ccol - crow, NEG))
        a = jnp.where(strict[d], beta * c["kk"] * dec, 0.0)
        return dict(c=c, d=d, beta=beta, ccol=ccol, tot=tot, dec=dec, a=a, t=eye - a, x=a.astype(BF16))

    def prep(first, tick=lambda: None):
        chunks = [chunk_inputs(first + j, j) for j in range(grp)]
        tick()
        chains = [chain_setup(c, d) for c in chunks for d in range(2)]
        tick()
        zero = jnp.zeros((BLK, BLK), BF16)

        def blockdiag(xp):
            return jnp.concatenate([jnp.concatenate([xp[:, :BLK], zero], axis=1),
                                    jnp.concatenate([zero, xp[:, BLK:]], axis=1)], axis=0)

        pairs = [(chains[2 * j], chains[2 * j + 1]) for j in range(grp)]
        xps = [jnp.concatenate([f["x"], b["x"]], axis=1) for f, b in pairs]
        tps = [jnp.concatenate([f["t"], b["t"]], axis=1) for f, b in pairs]
        for _ in range(INV_SQUARINGS):
            xps = [_dot(xp, blockdiag(xp)).astype(BF16) for xp in xps]
            txs = [_dot(tp.astype(BF16), blockdiag(xp)) for tp, xp in zip(tps, xps)]
            tps = [tp + tx for tp, tx in zip(tps, txs)]
            tick()
        for (f, b), tp in zip(pairs, tps):
            f["t"], b["t"] = tp[:, :BLK], tp[:, BLK:]
        for ch in chains:
            c = ch["c"]
            ch["ec"] = jnp.exp(ch["ccol"])
            ch["rhs"] = jnp.concatenate([ch["beta"] * c["vv"], ch["beta"] * c["kn"] * ch["ec"]], axis=1)
            ch["t16"] = ch["t"].astype(BF16)
        x0s = [_dot(ch["t16"], ch["rhs"].astype(BF16)) for ch in chains]
        tick()
        res = []
        for ch, x0 in zip(chains, x0s):
            ax = _dot(ch["a"].astype(BF16), x0.astype(BF16))
            res.append((ch["rhs"] - x0 - ax).astype(BF16))
        tick()
        uws = [(x0 + _dot(ch["t16"], e)).astype(BF16) for ch, x0, e in zip(chains, x0s, res)]
        tick()
        kuws = [_dot((ch["c"]["kn"] * jnp.exp(ch["tot"] - ch["ccol"])).T.astype(BF16), uw)
                for ch, uw in zip(chains, uws)]
        quws = [_dot((ch["c"]["qk"] * ch["dec"]).astype(BF16), uw) for ch, uw in zip(chains, uws)]
        for ch, kuw, quw in zip(chains, kuws, quws):
            c, d = ch["c"], ch["d"]
            rows = c["rows"]
            sadd_s[d, rows, :] = kuw[:, :BLK]
            smul_s[d, rows, :] = (-kuw[:, BLK:]).astype(BF16)
            o_s[d, rows, :] = quw[:, :BLK]
            omul_s[d, rows, :] = (c["qn"] * ch["ec"] - quw[:, BLK:]).astype(BF16)
            gl_s[d * nblk + c["n"]] = jnp.broadcast_to(jnp.exp(ch["tot"]), (BLK, BLK))[0:8, :]

    def scan_step(d, n, s):
        rows = pl.ds(pl.multiple_of(n * BLK, BLK), BLK)
        both = _dot(jnp.concatenate([smul_s[d, rows, :], omul_s[d, rows, :]], axis=0), s.astype(BF16))
        o_s[d, rows, :] = o_s[d, rows, :] + both[BLK:]
        return s * gl_s[d * nblk + n][0:1, :] + both[:BLK] + sadd_s[d, rows, :]

    def gated(rows, z):
        o = o_s[0, rows, :] + o_s[1, rows, :]
        return (_rms(o, og_ref[...]) * _silu(z)).astype(y_ref.dtype)

    def finish(n):
        tok_rows = pl.ds(pl.multiple_of((n - 1) * BLK, BLK), BLK)
        y_ref[tok_rows, :] = gated(pl.ds(pl.multiple_of(n * BLK, BLK), BLK), z_ref[tok_rows, :])

    def scan(i, carry):
        sf, sb = carry
        sf = scan_step(0, i, sf)
        sb = scan_step(1, nblk - 1 - i, sb)
        return sf, sb

    def scan_and_finish(i, carry):
        finish(i - 1)
        finish(nblk - i)
        return scan(i, carry)

    s0 = jnp.zeros((BLK, BLK), F32)
    first_done = nblk // 2 + 1
    ngroups = nblk // grp
    if ngroups == 3 and grp < first_done:
        def outer(g, c):
            prep(g * (2 * grp))
            return c

        lax.fori_loop(0, 2, outer, 0)
        state = [(s0, s0)]
        todo = list(range(grp))

        def tick():
            if todo:
                state[0] = scan(jnp.int32(todo.pop(0)), state[0])

        prep(jnp.int32(grp), tick)
        while todo:
            tick()
        carry, start = state[0], grp
    else:
        def ordered(g, c):
            prep(g * grp)
            return c

        lax.fori_loop(0, ngroups, ordered, 0)
        carry, start = (s0, s0), 0
    carry = lax.fori_loop(start, first_done, scan, carry)
    lax.fori_loop(first_done, nblk, scan_and_finish, carry)
    finish(nblk - 1)
    yh_ref[...] = gated(slice(0, BLK), zh_ref[...])


def _delta(qkv, z, gate_pre, b, n_tok, alog_rows, dtb_rows, conv_w, o_gain):
    seq = n_tok // b
    lp = seq + BLK
    nblk = lp // BLK
    cw = lambda off: pl.BlockSpec((A_CONV, BLK), lambda i, j: (0, j + off))
    views = lambda off: _seq_views(n_tok, seq, off)
    per_head = pl.BlockSpec((None, GATE_ROWS, BLK), lambda i, j: (j, 0, 0))
    return pl.pallas_call(
        _delta_kernel,
        grid=(b, A_HEADS),
        in_specs=views(0) + views(A_HEADS) + views(2 * A_HEADS) + views(0) + views(0)
                 + [per_head, per_head, cw(0), cw(A_HEADS), cw(2 * A_HEADS),
                    pl.BlockSpec((1, BLK), lambda i, j: (0, 0))],
        out_specs=[pl.BlockSpec((None, seq, BLK), lambda i, j: (j, i, 0)),
                   pl.BlockSpec((None, BLK, BLK), lambda i, j: (j, i, 0))],
        out_shape=[jax.ShapeDtypeStruct((A_HEADS, n_tok, BLK), BF16),
                   jax.ShapeDtypeStruct((A_HEADS, b * BLK, BLK), BF16)],
        scratch_shapes=[pltpu.VMEM((2, lp, BLK), F32), pltpu.VMEM((2, lp, BLK), BF16),
                        pltpu.VMEM((2, lp, BLK), F32), pltpu.VMEM((2, lp, BLK), BF16),
                        pltpu.VMEM((2 * nblk, 8, BLK), F32)],
        compiler_params=_cparams("parallel", "parallel"),
        name="delta_mixer",
    )(qkv, qkv, qkv, qkv, qkv, qkv, z, z, gate_pre, gate_pre, alog_rows, dtb_rows, conv_w, conv_w, conv_w, o_gain)


def _window_kernel(q_ref, kp_ref, kc_ref, kn_ref, km_ref, bias_ref, y_ref):
    i = pl.program_id(1)
    nblk = pl.num_programs(1)
    grp = B_HEADS // B_KV
    nk = 4 * BLK
    c = lax.broadcasted_iota(jnp.int32, (1, nk), 1)
    kblk = i - 1 + (c >> 7)
    edge = jnp.where((c >= 3 * BLK) | ((kblk >= 1) & (kblk < nblk)), 0.0, NEG)
    q = q_ref[...]
    kvs = (kp_ref[...], kc_ref[...], kn_ref[...], km_ref[...])
    ones = jnp.ones((nk, 2 * B_HD), BF16)
    lane = lax.broadcasted_iota(jnp.int32, (BLK, 2 * B_HD), 1)
    s4s, vexts = [], []
    for kvh in range(B_KV):
        ks = jnp.concatenate([t[:, kvh * B_HD:(kvh + 1) * B_HD] for t in kvs], axis=0)
        vs = jnp.concatenate([t[:, (B_KV + kvh) * B_HD:(B_KV + kvh + 1) * B_HD] for t in kvs], axis=0)
        q4 = jnp.concatenate([q[:, hh * B_HD:(hh + 1) * B_HD] for hh in range(kvh * grp, (kvh + 1) * grp)],
                             axis=0)
        s4s.append(_dot_nt(q4, ks))
        vexts.append(jnp.concatenate([vs, vs, ones], axis=1))
    pvs = []
    for kvh in range(B_KV):
        ps = []
        for gi in range(grp):
            hh = kvh * grp + gi
            s = s4s[kvh][gi * BLK:(gi + 1) * BLK] + bias_ref[hh] + edge
            ps.append(jnp.exp2(s - jnp.max(s, axis=-1, keepdims=True)).astype(BF16))
        pvs.append(_dot(jnp.concatenate(ps, axis=0), vexts[kvh]))
    outs = []
    for hh in range(B_HEADS):
        kvh, gi = divmod(hh, grp)
        o = pvs[kvh][gi * BLK:(gi + 1) * BLK]
        outs.append(o[:, :2 * B_HD] / o[:, 2 * B_HD:])
    for j in range(B_HEADS // 2):
        pair = jnp.where(lane < B_HD, outs[2 * j], outs[2 * j + 1])
        y_ref[:, 2 * j * B_HD:(2 * j + 2) * B_HD] = pair.astype(y_ref.dtype)

    @pl.when(i == 0)
    def _():
        rr = lax.broadcasted_iota(jnp.int32, y_ref.shape, 0)
        y_ref[...] = jnp.where(rr >= FRONT, y_ref[...], 0).astype(y_ref.dtype)


def _window_bias(sink):
    r = jnp.arange(BLK)[:, None]
    c = jnp.arange(4 * BLK)[None, :]
    dist = jnp.abs(BLK + r - c)
    slopes = jnp.exp2(-8.0 * (jnp.arange(B_HEADS, dtype=F32) + 1.0) / B_HEADS)
    band = (c < 3 * BLK) & (dist <= B_WIN)
    alibi = -slopes[:, None, None] * dist.astype(F32)[None] * LOG2E
    rest = jnp.where(c >= 3 * BLK + FRONT, 0.0, NEG)
    bias = jnp.where(band[None], alibi, rest[None])
    sink_col = (c == 3 * BLK)[None]
    return jnp.where(sink_col, sink.astype(F32)[:, None, None] * LOG2E, bias).astype(F32)


def _window(qb, kvb, b, n_tok, sink):
    nblk = n_tok // b // BLK + 1
    bias = _window_bias(sink)
    head0 = n_tok // BLK

    def blk(i, j):
        return jnp.where(j == 0, head0 + i, i * (nblk - 1) + j - 1)

    kv = lambda f: pl.BlockSpec((BLK, 2 * B_KV * B_HD), f)
    return pl.pallas_call(
        _window_kernel,
        grid=(b, nblk),
        in_specs=[pl.BlockSpec((BLK, B_HEADS * B_HD), lambda i, j: (blk(i, j), 0)),
                  kv(lambda i, j: (blk(i, jnp.maximum(j - 1, 0)), 0)),
                  kv(lambda i, j: (blk(i, j), 0)),
                  kv(lambda i, j: (blk(i, jnp.minimum(j + 1, nblk - 1)), 0)),
                  kv(lambda i, j: (head0 + i, 0)),
                  pl.BlockSpec(bias.shape, lambda i, j: (0, 0, 0))],
        out_specs=pl.BlockSpec((BLK, B_HEADS * B_HD), lambda i, j: (blk(i, j), 0)),
        out_shape=jax.ShapeDtypeStruct((qb.shape[0], B_HEADS * B_HD), BF16),
        compiler_params=_cparams("parallel", "parallel"),
        name="window_mixer",
    )(qb, kvb, kvb, kvb, kvb, bias)


def _out_mlp_kernel(*refs, arity, ntt):
    vals, pos = [], 0
    for a in arity:
        vals.append(refs[pos][...] if a == 1 else _pick(refs[pos], refs[pos + 1], ntt))
        pos += a
    wo_ref, g_ref, w1_ref, w2_ref, o_ref = refs[pos:]
    mix = jnp.concatenate(vals[1:], axis=1)
    h = vals[0] + _dot(mix, wo_ref[...])
    u = _rms(h, g_ref[...]).astype(BF16)
    dff = w1_ref.shape[1]
    acc = h
    for c in range(dff // FF_CHUNK):
        sl = slice(c * FF_CHUNK, (c + 1) * FF_CHUNK)
        a = jnp.maximum(_dot(u, w1_ref[:, sl]), 0.0)
        acc = acc + _dot((a * a).astype(BF16), w2_ref[sl, :])
    o_ref[...] = acc


def _out_mlp(rows_out, ntt, tensors, wo, g, w1, w2):
    d = wo.shape[1]
    row = lambda n: pl.BlockSpec((ROW_TILE, n), lambda i: (i, 0))
    full = lambda a: pl.BlockSpec(a.shape, lambda i: (0, 0))
    specs, args, arity = [], [], []
    for t in tensors:
        if isinstance(t, tuple):
            specs += _pair_specs(t[0], t[1])
            args += list(t)
            arity.append(2)
        else:
            specs.append(row(t.shape[1]))
            args.append(t)
            arity.append(1)
    return pl.pallas_call(
        functools.partial(_out_mlp_kernel, arity=tuple(arity), ntt=ntt),
        grid=(rows_out // ROW_TILE,),
        in_specs=specs + [full(wo), full(g), full(w1), full(w2)],
        out_specs=row(d),
        out_shape=jax.ShapeDtypeStruct((rows_out, d), F32),
        compiler_params=_cparams("parallel"),
        name="out_mlp",
    )(*args, wo, g, w1, w2)


def _c_proj_kernel(h_ref, g_ref, wq_ref, wk_ref, wv_ref, qg_ref, kg_ref, cos_ref, sin_ref,
                   q_ref, k_ref, v_ref):
    u = _rms(h_ref[...], g_ref[...]).astype(BF16)
    cosf = cos_ref[...]
    sinf = sin_ref[...]
    half = C_HD // 2

    def norm_rope(x, gain):
        x = _rms(x, gain)
        swapped = jnp.concatenate([x[:, half:], x[:, :half]], axis=1)
        return x * cosf + swapped * sinf

    k = _dot(u, wk_ref[...])
    half_w = C_HEADS * C_HD // 2
    q_lo = _dot(u, wq_ref[:, :half_w])
    for hh in range(C_KV):
        sl = slice(hh * C_HD, (hh + 1) * C_HD)
        k_ref[:, sl] = norm_rope(k[:, sl], kg_ref[...]).astype(BF16)
    q_hi = _dot(u, wq_ref[:, half_w:])
    for hh in range(C_HEADS // 2):
        sl = slice(hh * C_HD, (hh + 1) * C_HD)
        q_ref[:, sl] = (norm_rope(q_lo[:, sl], qg_ref[...]) * (C_HD ** -0.5 * LOG2E)).astype(BF16)
    v_ref[...] = _dot(u, wv_ref[...]).astype(BF16)
    for hh in range(C_HEADS // 2):
        sl = slice(hh * C_HD, (hh + 1) * C_HD)
        q_ref[:, half_w + hh * C_HD:half_w + (hh + 1) * C_HD] = (
            norm_rope(q_hi[:, sl], qg_ref[...]) * (C_HD ** -0.5 * LOG2E)).astype(BF16)


def _c_proj(h, ntt, seq, g, wq, wk, wv, qg, kg, cosf, sinf):
    r, d = h.shape
    tm = C_ROW_TILE
    per_seq = seq // tm
    n_tok_tiles = ntt * (ROW_TILE // tm)
    row = lambda n: pl.BlockSpec((tm, n), lambda i: (i, 0))
    full = lambda a: pl.BlockSpec(a.shape, lambda i: (0, 0))
    pos = pl.BlockSpec((tm, C_HD), lambda i: (jnp.where(i < n_tok_tiles, i % per_seq, per_seq), 0))
    return pl.pallas_call(
        _c_proj_kernel,
        grid=(r // tm,),
        in_specs=[row(d), full(g), full(wq), full(wk), full(wv), full(qg), full(kg), pos, pos],
        out_specs=[row(C_HEADS * C_HD), row(C_KV * C_HD), row(C_KV * C_HD)],
        out_shape=[jax.ShapeDtypeStruct((r, C_HEADS * C_HD), BF16),
                   jax.ShapeDtypeStruct((r, C_KV * C_HD), BF16),
                   jax.ShapeDtypeStruct((r, C_KV * C_HD), BF16)],
        compiler_params=_cparams("parallel"),
        name="c_proj",
    )(h, g, wq, wk, wv, qg, kg, cosf, sinf)


ATT_TK = 2048
ATT_QB = 4


def _dense_kernel(q_ref, k_ref, kh_ref, v_ref, vh_ref, y_ref, *scratch):
    grp = C_HEADS // C_KV
    nkb = k_ref.shape[0] // ATT_TK
    nq = ATT_QB
    sa_s, sb_s, acc_s = scratch[:nq], scratch[nq:2 * nq], scratch[2 * nq:]
    qs = [jnp.concatenate([q_ref[c * BLK:(c + 1) * BLK, g * C_HD:(g + 1) * C_HD] for g in range(grp)], axis=0)
          for c in range(nq)]
    m_rows = grp * BLK

    def keys(t):
        return pl.ds(pl.multiple_of(t * ATT_TK, ATT_TK), ATT_TK)

    def v_ones(v):
        return jnp.concatenate([v, jnp.ones(v.shape, BF16)], axis=1)

    def scores(t, s_refs):
        kt = k_ref[keys(t), :]
        for c in range(nq):
            s_refs[c][...] = _dot_nt(qs[c], kt)

    def step(t, ms, s_refs):
        vt = v_ones(v_ref[keys(t), :])
        out = []
        for c in range(nq):
            s = s_refs[c][...]
            m_new = jnp.maximum(ms[c], jnp.max(s, axis=-1, keepdims=True))
            p = jnp.exp2(s - m_new).astype(BF16)
            acc_s[c][...] = jnp.exp2(ms[c] - m_new) * acc_s[c][...] + _dot(p, vt)
            out.append(m_new)
        return out

    scores(0, sa_s)
    k0 = kh_ref[...]
    v0 = v_ones(vh_ref[...])
    kc = lax.broadcasted_iota(jnp.int32, (m_rows, BLK), 1)
    ms = []
    for c in range(nq):
        s0 = jnp.where(kc >= FRONT, _dot_nt(qs[c], k0), NEG)
        m = jnp.max(s0, axis=-1, keepdims=True)
        acc_s[c][...] = _dot(jnp.exp2(s0 - m).astype(BF16), v0)
        ms.append(m)

    def body(j, ms):
        scores(2 * j + 1, sb_s)
        ms = step(2 * j, ms, sa_s)
        scores(2 * j + 2, sa_s)
        return step(2 * j + 1, ms, sb_s)

    ms = lax.fori_loop(0, nkb // 2 - 1, body, ms)
    scores(nkb - 1, sb_s)
    ms = step(nkb - 2, ms, sa_s)
    ms = step(nkb - 1, ms, sb_s)
    for c in range(nq):
        acc = acc_s[c][...]
        o = acc[:, :C_HD] / acc[:, C_HD:C_HD + 1]
        for g in range(grp):
            y_ref[c * BLK:(c + 1) * BLK, g * C_HD:(g + 1) * C_HD] = o[g * BLK:(g + 1) * BLK, :].astype(y_ref.dtype)


def _dense(q, k, v, b, n_tok):
    seq = n_tok // b
    grp = C_HEADS // C_KV
    tq = ATT_QB * BLK
    assert seq % tq == 0 and seq % (2 * ATT_TK) == 0
    per_seq = seq // tq
    head0 = n_tok // BLK
    score = pltpu.VMEM((grp * BLK, ATT_TK), F32)
    tok = pl.BlockSpec((seq, C_HD), lambda i, j, t: (i, j))
    head = pl.BlockSpec((BLK, C_HD), lambda i, j, t: (head0 + i, j))
    return pl.pallas_call(
        _dense_kernel,
        grid=(b, C_KV, per_seq),
        in_specs=[pl.BlockSpec((tq, grp * C_HD), lambda i, j, t: (i * per_seq + t, j)), tok, head, tok, head],
        out_specs=pl.BlockSpec((tq, grp * C_HD), lambda i, j, t: (i * per_seq + t, j)),
        out_shape=jax.ShapeDtypeStruct((n_tok, C_HEADS * C_HD), BF16),
        scratch_shapes=[score] * (2 * ATT_QB) + [pltpu.VMEM((grp * BLK, 2 * C_HD), F32)] * ATT_QB,
        compiler_params=_cparams("parallel", "parallel", "arbitrary"),
        name="dense_mixer",
    )(q, k, k, v, v)


def _rope_tables(seq):
    rows = seq // GRID_W
    row = jnp.repeat(jnp.arange(rows), GRID_W)
    col = jnp.tile(jnp.arange(GRID_W), rows)
    head = jnp.tile(jnp.concatenate([jnp.zeros((FRONT,), jnp.int32), jnp.arange(N_META) - N_META]), ROW_TILE // BLK)
    row = jnp.concatenate([row, head]).astype(F32)
    col = jnp.concatenate([col, head]).astype(F32)
    axis_dim = C_HD // 2
    freqs = ROPE_THETA ** (-jnp.arange(0, axis_dim, 2, dtype=F32) / axis_dim)
    ang = jnp.concatenate([row[:, None] * freqs, col[:, None] * freqs], axis=-1)
    cos, sin = jnp.cos(ang), jnp.sin(ang)
    return jnp.concatenate([cos, cos], axis=-1), jnp.concatenate([-sin, sin], axis=-1)


def _gate_weight(w_b, w_a):
    d = w_b.shape[0]
    w_b = w_b.reshape(d, 2, A_HEADS)
    w_a = w_a.reshape(d, 2, A_HEADS)
    per_head = jnp.concatenate([w_b, w_a, w_a], axis=1)
    per_head = jnp.transpose(per_head, (0, 2, 1))
    per_head = jnp.pad(per_head, ((0, 0), (0, 0), (0, BLK - 6)))
    return per_head.reshape(d, A_HEADS * BLK)


def _gate_rows(p):
    t = jnp.transpose(p.astype(F32), (1, 0))
    rows = jnp.concatenate([jnp.zeros_like(t), t, t], axis=1)
    rows = jnp.pad(rows, ((0, 0), (0, GATE_ROWS - 6)))
    return jnp.broadcast_to(rows[:, :, None], (A_HEADS, GATE_ROWS, BLK))


def kernel(x, meta_tokens, attn_norm_g, mlp_norm_g, w_in_ab, conv_w_a, a_log, dt_bias, a_out_norm_g,
           b_q_norm_g, b_k_norm_g, b_sink, w_out_ab, w_qkv_c, c_q_norm_g, c_k_norm_g, w_out_c, w_ff1, w_ff2):
    bsz, seq, d = x.shape
    n_tok = bsz * seq
    n_rows = n_tok + bsz * BLK
    ntt = n_tok // ROW_TILE
    assert attn_norm_g.shape[0] == 2 and seq % ROW_TILE == 0 and (bsz * BLK) % ROW_TILE == 0
    x2 = x.reshape(n_tok, d)
    meta = jnp.broadcast_to(meta_tokens.astype(x.dtype)[None], (bsz, N_META, d))
    head = jnp.concatenate([jnp.zeros((bsz, FRONT, d), x.dtype), meta], axis=1).reshape(bsz * BLK, d)
    row2 = lambda v: v.astype(F32).reshape(1, -1)

    w = w_in_ab[0]
    qkv_w = w[:, :1536].astype(BF16)
    z_w = w[:, 1536:2048].astype(BF16)
    gate_w = _gate_weight(w[:, 2048:2056], w[:, 2056:2064]).astype(BF16)
    bq_w = w[:, 2064:2576].astype(BF16)
    bkv_w = w[:, 2576:2832].astype(BF16)
    qkv, z, gate_pre, qb, kvb = _ab_proj(x2, head, row2(attn_norm_g[0]), qkv_w, z_w, gate_w, bq_w, bkv_w,
                                         row2(b_q_norm_g[0]), row2(b_k_norm_g[0]))
    ya = _delta(qkv, z, gate_pre, bsz, n_tok, _gate_rows(a_log[0]), _gate_rows(dt_bias[0]),
                conv_w_a[0].astype(F32), row2(a_out_norm_g[0]))
    yb = _window(qb, kvb, bsz, n_tok, b_sink[0])
    h = _out_mlp(n_rows, ntt, [(x2, head), tuple(ya), yb], w_out_ab[0].astype(BF16), row2(mlp_norm_g[0]),
                 w_ff1[0].astype(BF16), w_ff2[0].astype(BF16))

    w = w_qkv_c[0]
    deint = jnp.concatenate([jnp.arange(0, C_HD, 2), jnp.arange(1, C_HD, 2)])
    perm = lambda wc, nh: wc.reshape(d, nh, C_HD)[:, :, deint].reshape(d, nh * C_HD)
    wq = perm(w[:, :C_HEADS * C_HD], C_HEADS).astype(BF16)
    wk = perm(w[:, C_HEADS * C_HD:(C_HEADS + C_KV) * C_HD], C_KV).astype(BF16)
    wv = w[:, (C_HEADS + C_KV) * C_HD:].astype(BF16)
    cosf, sinf = _rope_tables(seq)
    q, k, v = _c_proj(h, ntt, seq, row2(attn_norm_g[1]), wq, wk, wv,
                      row2(c_q_norm_g[0][deint]), row2(c_k_norm_g[0][deint]), cosf, sinf)
    att = _dense(q, k, v, bsz, n_tok)
    out = _out_mlp(n_tok, ntt, [h, att], w_out_c[0].astype(BF16), row2(mlp_norm_g[1]),
                   w_ff1[1].astype(BF16), w_ff2[1].astype(BF16))
    return out.reshape(bsz, seq, d)
```

```python
import functools
import math

import jax
import jax.numpy as jnp
from jax import lax
from jax.experimental import pallas as pl
from jax.experimental.pallas import tpu as pltpu

F32 = jnp.float32
BF16 = jnp.bfloat16

EPS = 1e-6
N_META = 16
BLK = 128
FRONT = BLK - N_META
GRID_W = 64
ROPE_THETA = 10000.0
A_HEADS, A_DK, A_CONV = 4, 128, 5
B_HEADS, B_KV, B_HD, B_WIN = 8, 2, 64, 128
C_HEADS, C_KV, C_HD = 8, 2, 128
NEG = -1e30
LOG2E = math.log2(math.e)

VMEM_LIMIT = 62 * 1024 * 1024
ROW_TILE = 1024
C_ROW_TILE = 256
FF_CHUNK = 512
INV_SQUARINGS = BLK.bit_length() - 2
PREP_UNROLL = 11


def _chunk_group(nblk):
    return max(g for g in range(1, PREP_UNROLL + 1) if nblk % g == 0)


def _cparams(*sem):
    return pltpu.CompilerParams(dimension_semantics=sem, vmem_limit_bytes=VMEM_LIMIT)


def _sigmoid(x):
    return 1.0 / (1.0 + jnp.exp(-x))


def _silu(x):
    return x * _sigmoid(x)


def _softplus(x):
    return jnp.maximum(x, 0.0) + jnp.log1p(jnp.exp(-jnp.abs(x)))


def _rms(x, g):
    return x * lax.rsqrt(jnp.mean(x * x, axis=-1, keepdims=True) + EPS) * g


def _dot(a, b):
    return jnp.dot(a, b, preferred_element_type=F32)


def _dot_nt(a, b):
    return lax.dot_general(a, b, (((1,), (1,)), ((), ())), preferred_element_type=F32)


def _slab_spec(n_slabs, index):
    return pl.BlockSpec((n_slabs, ROW_TILE, BLK), lambda i: (0, index(i), 0))


def _pair_specs(tok, head):
    ntt = tok.shape[-2] // ROW_TILE
    index = (lambda i: jnp.minimum(i, ntt - 1), lambda i: jnp.maximum(i - ntt, 0))
    if tok.ndim == 3:
        return [_slab_spec(tok.shape[0], ix) for ix in index]
    flat = lambda ix: pl.BlockSpec((ROW_TILE, tok.shape[1]), lambda i: (ix(i), 0))
    return [flat(ix) for ix in index]


def _rows(v):
    return jnp.concatenate([v[c] for c in range(v.shape[0])], axis=1) if v.ndim == 3 else v


def _pick(tok_ref, head_ref, ntt):
    return _rows(jnp.where(pl.program_id(0) < ntt, tok_ref[...], head_ref[...]))


def _store_slabs(ref, first, value):
    for c in range(value.shape[1] // BLK):
        ref[first + c] = value[:, c * BLK:(c + 1) * BLK]


def _ab_proj_kernel(x_ref, hb_ref, g_ref, wqkv_ref, wz_ref, wg_ref, wq_ref, wkv_ref, bqg_ref, bkg_ref,
                    qkv_ref, z_ref, gate_ref, qb_ref, kvb_ref, *, ntt):
    u = _rms(_pick(x_ref, hb_ref, ntt), g_ref[...]).astype(BF16)
    qb = _dot(u, wq_ref[...])
    kv = _dot(u, wkv_ref[...])
    scale = B_HD ** -0.5 * LOG2E

    def q_heads(lo, hi):
        for hh in range(lo, hi):
            sl = slice(hh * B_HD, (hh + 1) * B_HD)
            qb_ref[:, sl] = (_rms(qb[:, sl], bqg_ref[...]) * scale).astype(BF16)

    _store_slabs(qkv_ref, 0, _dot(u, wqkv_ref[:, 0:512]))
    q_heads(0, 3)
    _store_slabs(qkv_ref, A_HEADS, _dot(u, wqkv_ref[:, 512:1024]))
    q_heads(3, 6)
    _store_slabs(qkv_ref, 2 * A_HEADS, _dot(u, wqkv_ref[:, 1024:1536]))
    q_heads(6, B_HEADS)
    _store_slabs(z_ref, 0, _dot(u, wz_ref[...]))
    for hh in range(B_KV):
        sl = slice(hh * B_HD, (hh + 1) * B_HD)
        kvb_ref[:, sl] = _rms(kv[:, sl], bkg_ref[...]).astype(BF16)
    kvb_ref[:, B_KV * B_HD:] = kv[:, B_KV * B_HD:].astype(BF16)
    _store_slabs(gate_ref, 0, _dot(u, wg_ref[...]))


def _ab_proj(x, hb, g, wqkv, wz, wg, wq, wkv, bqg, bkg):
    r = x.shape[0] + hb.shape[0]
    row = lambda n: pl.BlockSpec((ROW_TILE, n), lambda i: (i, 0))
    full = lambda a: pl.BlockSpec(a.shape, lambda i: (0, 0))
    return pl.pallas_call(
        functools.partial(_ab_proj_kernel, ntt=x.shape[0] // ROW_TILE),
        grid=(r // ROW_TILE,),
        in_specs=_pair_specs(x, hb) + [full(g), full(wqkv), full(wz), full(wg), full(wq), full(wkv), full(bqg),
                                       full(bkg)],
        out_specs=[_slab_spec(3 * A_HEADS, lambda i: i), _slab_spec(A_HEADS, lambda i: i),
                   _slab_spec(A_HEADS, lambda i: i), row(512), row(256)],
        out_shape=[jax.ShapeDtypeStruct((3 * A_HEADS, r, BLK), F32), jax.ShapeDtypeStruct((A_HEADS, r, BLK), F32),
                   jax.ShapeDtypeStruct((A_HEADS, r, BLK), F32), jax.ShapeDtypeStruct((r, 512), BF16),
                   jax.ShapeDtypeStruct((r, 256), BF16)],
        compiler_params=_cparams("parallel"),
        name="ab_proj",
    )(x, hb, g, wqkv, wz, wg, wq, wkv, bqg, bkg)


GATE_ROWS = 8


def _split3(x):
    hi = x.astype(BF16)
    r1 = x - hi.astype(F32)
    mid = r1.astype(BF16)
    lo = (r1 - mid.astype(F32)).astype(BF16)
    return hi, mid, lo


def _seq_block(tok_ref, head_ref, n):
    tok = tok_ref[pl.ds(pl.multiple_of(jnp.maximum(n - 1, 0) * BLK, BLK), BLK), :]
    return jnp.where(n == 0, head_ref[...], tok)


def _seq_views(n_tok, seq, off):
    head0 = n_tok // BLK
    return [pl.BlockSpec((None, seq, BLK), lambda i, j: (j + off, i, 0)),
            pl.BlockSpec((None, BLK, BLK), lambda i, j: (j + off, head0 + i, 0))]


def _gate_block(pre, n, neg_a, dtb):
    nr = GATE_ROWS
    t = pre.T[0:nr, :]
    ri = lax.broadcasted_iota(jnp.int32, (BLK, BLK), 0)
    ci = lax.broadcasted_iota(jnp.int32, (BLK, BLK), 1)
    role = ri[0:nr, :]
    live = (ci[0:nr, :] + n * BLK) >= FRONT
    beta = jnp.where(live, _sigmoid(t), 0.0)
    g = jnp.where(live, neg_a * _softplus(t + dtb), 0.0)
    parts = [p.astype(F32) for p in _split3(g)]
    parts = jnp.concatenate(parts + [jnp.zeros((BLK - 3 * nr, BLK), F32)], axis=0).astype(BF16)
    tri = jnp.concatenate([(ri <= ci).astype(BF16), (ri >= ci).astype(BF16)], axis=1)
    sums = _dot(parts, tri)
    both = sums[0:nr] + sums[nr:2 * nr] + sums[2 * nr:3 * nr]
    pre_sum, suf_sum = both[:, :BLK], both[:, BLK:]
    tot = pre_sum + suf_sum - g
    row = jnp.where(role < 2, beta, jnp.where(role == 2, pre_sum, jnp.where(role == 3, suf_sum, tot)))
    col = jnp.concatenate([row, jnp.zeros((BLK - nr, BLK), F32)], axis=0).T
    return col, row


def _delta_kernel(q_ref, qh_ref, k_ref, kh_ref, v_ref, vh_ref, z_ref, zh_ref, gp_ref, gph_ref, alog_ref, dtb_ref,
                  cwq_ref, cwk_ref, cwv_ref, og_ref, y_ref, yh_ref, sadd_s, smul_s, o_s, omul_s, gl_s):
    seq = q_ref.shape[0]
    nblk = seq // BLK + 1
    grp = _chunk_group(nblk)
    neg_a = -jnp.exp(alog_ref[...])
    dtb = dtb_ref[...]
    ri = lax.broadcasted_iota(jnp.int32, (BLK, BLK), 0)
    ci = lax.broadcasted_iota(jnp.int32, (BLK, BLK), 1)
    eye = (ri == ci).astype(F32)
    incl = (ri >= ci, ri <= ci)
    strict = (ri > ci, ri < ci)

    hw = A_CONV // 2

    def conv_silu(ref, head_ref, w_ref, n, maybe_edge):
        base = jnp.clip((n - 1) * BLK, hw, seq - BLK - hw)
        acc = ref[pl.ds(base - hw, BLK), :] * w_ref[0:1, :]
        for j in range(1, A_CONV):
            acc = acc + ref[pl.ds(base - hw + j, BLK), :] * w_ref[j:j + 1, :]
        if maybe_edge:
            cur = _seq_block(ref, head_ref, n)
            tok_prev = ref[pl.ds(pl.multiple_of(jnp.maximum((n - 1) * BLK - 8, 0), 8), 8), :]
            prev = jnp.where(n == 0, 0.0, jnp.where(n == 1, head_ref[BLK - 8:, :], tok_prev))
            nxt = ref[pl.ds(pl.multiple_of(jnp.minimum(n * BLK, seq - 8), 8), 8), :]
            nxt = jnp.where(n < nblk - 1, nxt, 0.0)
            win = jnp.concatenate([prev, cur, nxt], axis=0)
            edge = win[8 - hw:8 - hw + BLK, :] * w_ref[0:1, :]
            for j in range(1, A_CONV):
                edge = edge + win[8 - hw + j:8 - hw + j + BLK, :] * w_ref[j:j + 1, :]
            acc = jnp.where((n <= 1) | (n == nblk - 1), edge, acc)
        return _silu(acc)

    def l2n(x):
        return x * lax.rsqrt(jnp.sum(x * x, axis=-1, keepdims=True) + EPS)

    edge_slots = {0 % grp, 1 % grp, (nblk - 1) % grp}

    def chunk_inputs(n, slot):
        rows = pl.ds(pl.multiple_of(n * BLK, BLK), BLK)
        live = (ri[:, 0:1] + n * BLK) >= FRONT
        edge = slot in edge_slots
        qn = jnp.where(live, l2n(conv_silu(q_ref, qh_ref, cwq_ref, n, edge)) * (A_DK ** -0.5), 0.0)
        kn = jnp.where(live, l2n(conv_silu(k_ref, kh_ref, cwk_ref, n, edge)), 0.0)
        vv = jnp.where(live, conv_silu(v_ref, vh_ref, cwv_ref, n, edge), 0.0)
        kn16 = kn.astype(BF16)
        kq = _dot_nt(jnp.concatenate([kn16, qn.astype(BF16)], axis=0), kn16)
        bg, gt = _gate_block(_seq_block(gp_ref, gph_ref, n), n, neg_a, dtb)
        return dict(n=n, rows=rows, qn=qn, kn=kn, vv=vv, kk=kq[:BLK], qk=kq[BLK:], bg=bg, gt=gt)

    def chain_setup(c, d):
        bg, gt = c["bg"], c["gt"]
        beta, ccol, tot = bg[:, d:d + 1], bg[:, 2 + d:3 + d], bg[:, 4 + d:5 + d]
        crow = gt[2 + d:3 + d, :]
        dec = jnp.exp(jnp.where(incl[d], ccol - crow, NEG))
        a = jnp.where(strict[d], beta * c["kk"] * dec, 0.0)
        return dict(c=c, d=d, beta=beta, ccol=ccol, tot=tot, dec=dec, a=a, t=eye - a, x=a.astype(BF16))

    def prep(first, tick=lambda: None):
        chunks = [chunk_inputs(first + j, j) for j in range(grp)]
        tick()
        chains = [chain_setup(c, d) for c in chunks for d in range(2)]
        tick()
        zero = jnp.zeros((BLK, BLK), BF16)

        def blockdiag(xp):
            return jnp.concatenate([jnp.concatenate([xp[:, :BLK], zero], axis=1),
                                    jnp.concatenate([zero, xp[:, BLK:]], axis=1)], axis=0)

        pairs = [(chains[2 * j], chains[2 * j + 1]) for j in range(grp)]
        xps = [jnp.concatenate([f["x"], b["x"]], axis=1) for f, b in pairs]
        tps = [jnp.concatenate([f["t"], b["t"]], axis=1) for f, b in pairs]
        for _ in range(INV_SQUARINGS):
            xps = [_dot(xp, blockdiag(xp)).astype(BF16) for xp in xps]
            txs = [_dot(tp.astype(BF16), blockdiag(xp)) for tp, xp in zip(tps, xps)]
            tps = [tp + tx for tp, tx in zip(tps, txs)]
            tick()
        for (f, b), tp in zip(pairs, tps):
            f["t"], b["t"] = tp[:, :BLK], tp[:, BLK:]
        for ch in chains:
            c = ch["c"]
            ch["ec"] = jnp.exp(ch["ccol"])
            ch["rhs"] = jnp.concatenate([ch["beta"] * c["vv"], ch["beta"] * c["kn"] * ch["ec"]], axis=1)
            ch["t16"] = ch["t"].astype(BF16)
        x0s = [_dot(ch["t16"], ch["rhs"].astype(BF16)) for ch in chains]
        tick()
        res = []
        for ch, x0 in zip(chains, x0s):
            ax = _dot(ch["a"].astype(BF16), x0.astype(BF16))
            res.append((ch["rhs"] - x0 - ax).astype(BF16))
        tick()
        uws = [(x0 + _dot(ch["t16"], e)).astype(BF16) for ch, x0, e in zip(chains, x0s, res)]
        tick()
        kuws = [_dot((ch["c"]["kn"] * jnp.exp(ch["tot"] - ch["ccol"])).T.astype(BF16), uw)
                for ch, uw in zip(chains, uws)]
        quws = [_dot((ch["c"]["qk"] * ch["dec"]).astype(BF16), uw) for ch, uw in zip(chains, uws)]
        for ch, kuw, quw in zip(chains, kuws, quws):
            c, d = ch["c"], ch["d"]
            rows = c["rows"]
            sadd_s[d, rows, :] = kuw[:, :BLK]
            smul_s[d, rows, :] = (-kuw[:, BLK:]).astype(BF16)
            o_s[d, rows, :] = quw[:, :BLK]
            omul_s[d, rows, :] = (c["qn"] * ch["ec"] - quw[:, BLK:]).astype(BF16)
            gl_s[d * nblk + c["n"]] = jnp.broadcast_to(jnp.exp(ch["tot"]), (BLK, BLK))[0:8, :]

    def scan_step(d, n, s):
        rows = pl.ds(pl.multiple_of(n * BLK, BLK), BLK)
        both = _dot(jnp.concatenate([smul_s[d, rows, :], omul_s[d, rows, :]], axis=0), s.astype(BF16))
        o_s[d, rows, :] = o_s[d, rows, :] + both[BLK:]
        return s * gl_s[d * nblk + n][0:1, :] + both[:BLK] + sadd_s[d, rows, :]

    def gated(rows, z):
        o = o_s[0, rows, :] + o_s[1, rows, :]
        return (_rms(o, og_ref[...]) * _silu(z)).astype(y_ref.dtype)

    def finish(n):
        tok_rows = pl.ds(pl.multiple_of((n - 1) * BLK, BLK), BLK)
        y_ref[tok_rows, :] = gated(pl.ds(pl.multiple_of(n * BLK, BLK), BLK), z_ref[tok_rows, :])

    def scan(i, carry):
        sf, sb = carry
        sf = scan_step(0, i, sf)
        sb = scan_step(1, nblk - 1 - i, sb)
        return sf, sb

    def scan_and_finish(i, carry):
        finish(i - 1)
        finish(nblk - i)
        return scan(i, carry)

    s0 = jnp.zeros((BLK, BLK), F32)
    first_done = nblk // 2 + 1
    ngroups = nblk // grp
    if ngroups == 3 and grp < first_done:
        def outer(g, c):
            prep(g * (2 * grp))
            return c

        lax.fori_loop(0, 2, outer, 0)
        state = [(s0, s0)]
        todo = list(range(grp))

        def tick():
            if todo:
                state[0] = scan(jnp.int32(todo.pop(0)), state[0])

        prep(jnp.int32(grp), tick)
        while todo:
            tick()
        carry, start = state[0], grp
    else:
        def ordered(g, c):
            prep(g * grp)
            return c

        lax.fori_loop(0, ngroups, ordered, 0)
        carry, start = (s0, s0), 0
    carry = lax.fori_loop(start, first_done, scan, carry)
    lax.fori_loop(first_done, nblk, scan_and_finish, carry)
    finish(nblk - 1)
    yh_ref[...] = gated(slice(0, BLK), zh_ref[...])


def _delta(qkv, z, gate_pre, b, n_tok, alog_rows, dtb_rows, conv_w, o_gain):
    seq = n_tok // b
    lp = seq + BLK
    nblk = lp // BLK
    cw = lambda off: pl.BlockSpec((A_CONV, BLK), lambda i, j: (0, j + off))
    views = lambda off: _seq_views(n_tok, seq, off)
    per_head = pl.BlockSpec((None, GATE_ROWS, BLK), lambda i, j: (j, 0, 0))
    return pl.pallas_call(
        _delta_kernel,
        grid=(b, A_HEADS),
        in_specs=views(0) + views(A_HEADS) + views(2 * A_HEADS) + views(0) + views(0)
                 + [per_head, per_head, cw(0), cw(A_HEADS), cw(2 * A_HEADS),
                    pl.BlockSpec((1, BLK), lambda i, j: (0, 0))],
        out_specs=[pl.BlockSpec((None, seq, BLK), lambda i, j: (j, i, 0)),
                   pl.BlockSpec((None, BLK, BLK), lambda i, j: (j, i, 0))],
        out_shape=[jax.ShapeDtypeStruct((A_HEADS, n_tok, BLK), BF16),
                   jax.ShapeDtypeStruct((A_HEADS, b * BLK, BLK), BF16)],
        scratch_shapes=[pltpu.VMEM((2, lp, BLK), F32), pltpu.VMEM((2, lp, BLK), BF16),
                        pltpu.VMEM((2, lp, BLK), F32), pltpu.VMEM((2, lp, BLK), BF16),
                        pltpu.VMEM((2 * nblk, 8, BLK), F32)],
        compiler_params=_cparams("parallel", "parallel"),
        name="delta_mixer",
    )(qkv, qkv, qkv, qkv, qkv, qkv, z, z, gate_pre, gate_pre, alog_rows, dtb_rows, conv_w, conv_w, conv_w, o_gain)


def _window_kernel(q_ref, kp_ref, kc_ref, kn_ref, km_ref, bias_ref, y_ref):
    i = pl.program_id(1)
    nblk = pl.num_programs(1)
    grp = B_HEADS // B_KV
    nk = 4 * BLK
    c = lax.broadcasted_iota(jnp.int32, (1, nk), 1)
    kblk = i - 1 + (c >> 7)
    edge = jnp.where((c >= 3 * BLK) | ((kblk >= 1) & (kblk < nblk)), 0.0, NEG)
    q = q_ref[...]
    kvs = (kp_ref[...], kc_ref[...], kn_ref[...], km_ref[...])
    ones = jnp.ones((nk, 2 * B_HD), BF16)
    lane = lax.broadcasted_iota(jnp.int32, (BLK, 2 * B_HD), 1)
    s4s, vexts = [], []
    for kvh in range(B_KV):
        ks = jnp.concatenate([t[:, kvh * B_HD:(kvh + 1) * B_HD] for t in kvs], axis=0)
        vs = jnp.concatenate([t[:, (B_KV + kvh) * B_HD:(B_KV + kvh + 1) * B_HD] for t in kvs], axis=0)
        q4 = jnp.concatenate([q[:, hh * B_HD:(hh + 1) * B_HD] for hh in range(kvh * grp, (kvh + 1) * grp)],
                             axis=0)
        s4s.append(_dot_nt(q4, ks))
        vexts.append(jnp.concatenate([vs, vs, ones], axis=1))
    pvs = []
    for kvh in range(B_KV):
        ps = []
        for gi in range(grp):
            hh = kvh * grp + gi
            s = s4s[kvh][gi * BLK:(gi + 1) * BLK] + bias_ref[hh] + edge
            ps.append(jnp.exp2(s - jnp.max(s, axis=-1, keepdims=True)).astype(BF16))
        pvs.append(_dot(jnp.concatenate(ps, axis=0), vexts[kvh]))
    outs = []
    for hh in range(B_HEADS):
        kvh, gi = divmod(hh, grp)
        o = pvs[kvh][gi * BLK:(gi + 1) * BLK]
        outs.append(o[:, :2 * B_HD] / o[:, 2 * B_HD:])
    for j in range(B_HEADS // 2):
        pair = jnp.where(lane < B_HD, outs[2 * j], outs[2 * j + 1])
        y_ref[:, 2 * j * B_HD:(2 * j + 2) * B_HD] = pair.astype(y_ref.dtype)

    @pl.when(i == 0)
    def _():
        rr = lax.broadcasted_iota(jnp.int32, y_ref.shape, 0)
        y_ref[...] = jnp.where(rr >= FRONT, y_ref[...], 0).astype(y_ref.dtype)


def _window_bias(sink):
    r = jnp.arange(BLK)[:, None]
    c = jnp.arange(4 * BLK)[None, :]
    dist = jnp.abs(BLK + r - c)
    slopes = jnp.exp2(-8.0 * (jnp.arange(B_HEADS, dtype=F32) + 1.0) / B_HEADS)
    band = (c < 3 * BLK) & (dist <= B_WIN)
    alibi = -slopes[:, None, None] * dist.astype(F32)[None] * LOG2E
    rest = jnp.where(c >= 3 * BLK + FRONT, 0.0, NEG)
    bias = jnp.where(band[None], alibi, rest[None])
    sink_col = (c == 3 * BLK)[None]
    return jnp.where(sink_col, sink.astype(F32)[:, None, None] * LOG2E, bias).astype(F32)


def _window(qb, kvb, b, n_tok, sink):
    nblk = n_tok // b // BLK + 1
    bias = _window_bias(sink)
    head0 = n_tok // BLK

    def blk(i, j):
        return jnp.where(j == 0, head0 + i, i * (nblk - 1) + j - 1)

    kv = lambda f: pl.BlockSpec((BLK, 2 * B_KV * B_HD), f)
    return pl.pallas_call(
        _window_kernel,
        grid=(b, nblk),
        in_specs=[pl.BlockSpec((BLK, B_HEADS * B_HD), lambda i, j: (blk(i, j), 0)),
                  kv(lambda i, j: (blk(i, jnp.maximum(j - 1, 0)), 0)),
                  kv(lambda i, j: (blk(i, j), 0)),
                  kv(lambda i, j: (blk(i, jnp.minimum(j + 1, nblk - 1)), 0)),
                  kv(lambda i, j: (head0 + i, 0)),
                  pl.BlockSpec(bias.shape, lambda i, j: (0, 0, 0))],
        out_specs=pl.BlockSpec((BLK, B_HEADS * B_HD), lambda i, j: (blk(i, j), 0)),
        out_shape=jax.ShapeDtypeStruct((qb.shape[0], B_HEADS * B_HD), BF16),
        compiler_params=_cparams("parallel", "parallel"),
        name="window_mixer",
    )(qb, kvb, kvb, kvb, kvb, bias)


def _out_mlp_kernel(*refs, arity, ntt):
    vals, pos = [], 0
    for a in arity:
        vals.append(refs[pos][...] if a == 1 else _pick(refs[pos], refs[pos + 1], ntt))
        pos += a
    wo_ref, g_ref, w1_ref, w2_ref, o_ref = refs[pos:]
    mix = jnp.concatenate(vals[1:], axis=1)
    h = vals[0] + _dot(mix, wo_ref[...])
    u = _rms(h, g_ref[...]).astype(BF16)
    dff = w1_ref.shape[1]
    acc = h
    for c in range(dff // FF_CHUNK):
        sl = slice(c * FF_CHUNK, (c + 1) * FF_CHUNK)
        a = jnp.maximum(_dot(u, w1_ref[:, sl]), 0.0)
        acc = acc + _dot((a * a).astype(BF16), w2_ref[sl, :])
    o_ref[...] = acc


def _out_mlp(rows_out, ntt, tensors, wo, g, w1, w2):
    d = wo.shape[1]
    row = lambda n: pl.BlockSpec((ROW_TILE, n), lambda i: (i, 0))
    full = lambda a: pl.BlockSpec(a.shape, lambda i: (0, 0))
    specs, args, arity = [], [], []
    for t in tensors:
        if isinstance(t, tuple):
            specs += _pair_specs(t[0], t[1])
            args += list(t)
            arity.append(2)
        else:
            specs.append(row(t.shape[1]))
            args.append(t)
            arity.append(1)
    return pl.pallas_call(
        functools.partial(_out_mlp_kernel, arity=tuple(arity), ntt=ntt),
        grid=(rows_out // ROW_TILE,),
        in_specs=specs + [full(wo), full(g), full(w1), full(w2)],
        out_specs=row(d),
        out_shape=jax.ShapeDtypeStruct((rows_out, d), F32),
        compiler_params=_cparams("parallel"),
        name="out_mlp",
    )(*args, wo, g, w1, w2)


def _c_proj_kernel(h_ref, g_ref, wq_ref, wk_ref, wv_ref, qg_ref, kg_ref, cos_ref, sin_ref,
                   q_ref, k_ref, v_ref):
    u = _rms(h_ref[...], g_ref[...]).astype(BF16)
    cosf = cos_ref[...]
    sinf = sin_ref[...]
    half = C_HD // 2

    def norm_rope(x, gain):
        x = _rms(x, gain)
        swapped = jnp.concatenate([x[:, half:], x[:, :half]], axis=1)
        return x * cosf + swapped * sinf

    k = _dot(u, wk_ref[...])
    half_w = C_HEADS * C_HD // 2
    q_lo = _dot(u, wq_ref[:, :half_w])
    for hh in range(C_KV):
        sl = slice(hh * C_HD, (hh + 1) * C_HD)
        k_ref[:, sl] = norm_rope(k[:, sl], kg_ref[...]).astype(BF16)
    q_hi = _dot(u, wq_ref[:, half_w:])
    for hh in range(C_HEADS // 2):
        sl = slice(hh * C_HD, (hh + 1) * C_HD)
        q_ref[:, sl] = (norm_rope(q_lo[:, sl], qg_ref[...]) * (C_HD ** -0.5 * LOG2E)).astype(BF16)
    v_ref[...] = _dot(u, wv_ref[...]).astype(BF16)
    for hh in range(C_HEADS // 2):
        sl = slice(hh * C_HD, (hh + 1) * C_HD)
        q_ref[:, half_w + hh * C_HD:half_w + (hh + 1) * C_HD] = (
            norm_rope(q_hi[:, sl], qg_ref[...]) * (C_HD ** -0.5 * LOG2E)).astype(BF16)


def _c_proj(h, ntt, seq, g, wq, wk, wv, qg, kg, cosf, sinf):
    r, d = h.shape
    tm = C_ROW_TILE
    per_seq = seq // tm
    n_tok_tiles = ntt * (ROW_TILE // tm)
    row = lambda n: pl.BlockSpec((tm, n), lambda i: (i, 0))
    full = lambda a: pl.BlockSpec(a.shape, lambda i: (0, 0))
    pos = pl.BlockSpec((tm, C_HD), lambda i: (jnp.where(i < n_tok_tiles, i % per_seq, per_seq), 0))
    return pl.pallas_call(
        _c_proj_kernel,
        grid=(r // tm,),
        in_specs=[row(d), full(g), full(wq), full(wk), full(wv), full(qg), full(kg), pos, pos],
        out_specs=[row(C_HEADS * C_HD), row(C_KV * C_HD), row(C_KV * C_HD)],
        out_shape=[jax.ShapeDtypeStruct((r, C_HEADS * C_HD), BF16),
                   jax.ShapeDtypeStruct((r, C_KV * C_HD), BF16),
                   jax.ShapeDtypeStruct((r, C_KV * C_HD), BF16)],
        compiler_params=_cparams("parallel"),
        name="c_proj",
    )(h, g, wq, wk, wv, qg, kg, cosf, sinf)


ATT_TK = 2048
ATT_QB = 4


def _dense_kernel(q_ref, k_ref, kh_ref, v_ref, vh_ref, y_ref, *scratch):
    grp = C_HEADS // C_KV
    nkb = k_ref.shape[0] // ATT_TK
    nq = ATT_QB
    sa_s, sb_s, acc_s = scratch[:nq], scratch[nq:2 * nq], scratch[2 * nq:]
    qs = [jnp.concatenate([q_ref[c * BLK:(c + 1) * BLK, g * C_HD:(g + 1) * C_HD] for g in range(grp)], axis=0)
          for c in range(nq)]
    m_rows = grp * BLK

    def keys(t):
        return pl.ds(pl.multiple_of(t * ATT_TK, ATT_TK), ATT_TK)

    def v_ones(v):
        return jnp.concatenate([v, jnp.ones(v.shape, BF16)], axis=1)

    def scores(t, s_refs):
        kt = k_ref[keys(t), :]
        for c in range(nq):
            s_refs[c][...] = _dot_nt(qs[c], kt)

    def step(t, ms, s_refs):
        vt = v_ones(v_ref[keys(t), :])
        out = []
        for c in range(nq):
            s = s_refs[c][...]
            m_new = jnp.maximum(ms[c], jnp.max(s, axis=-1, keepdims=True))
            p = jnp.exp2(s - m_new).astype(BF16)
            acc_s[c][...] = jnp.exp2(ms[c] - m_new) * acc_s[c][...] + _dot(p, vt)
            out.append(m_new)
        return out

    scores(0, sa_s)
    k0 = kh_ref[...]
    v0 = v_ones(vh_ref[...])
    kc = lax.broadcasted_iota(jnp.int32, (m_rows, BLK), 1)
    ms = []
    for c in range(nq):
        s0 = jnp.where(kc >= FRONT, _dot_nt(qs[c], k0), NEG)
        m = jnp.max(s0, axis=-1, keepdims=True)
        acc_s[c][...] = _dot(jnp.exp2(s0 - m).astype(BF16), v0)
        ms.append(m)

    def body(j, ms):
        scores(2 * j + 1, sb_s)
        ms = step(2 * j, ms, sa_s)
        scores(2 * j + 2, sa_s)
        return step(2 * j + 1, ms, sb_s)

    ms = lax.fori_loop(0, nkb // 2 - 1, body, ms)
    scores(nkb - 1, sb_s)
    ms = step(nkb - 2, ms, sa_s)
    ms = step(nkb - 1, ms, sb_s)
    for c in range(nq):
        acc = acc_s[c][...]
        o = acc[:, :C_HD] / acc[:, C_HD:C_HD + 1]
        for g in range(grp):
            y_ref[c * BLK:(c + 1) * BLK, g * C_HD:(g + 1) * C_HD] = o[g * BLK:(g + 1) * BLK, :].astype(y_ref.dtype)


def _dense(q, k, v, b, n_tok):
    seq = n_tok // b
    grp = C_HEADS // C_KV
    tq = ATT_QB * BLK
    assert seq % tq == 0 and seq % (2 * ATT_TK) == 0
    per_seq = seq // tq
    head0 = n_tok // BLK
    score = pltpu.VMEM((grp * BLK, ATT_TK), F32)
    tok = pl.BlockSpec((seq, C_HD), lambda i, j, t: (i, j))
    head = pl.BlockSpec((BLK, C_HD), lambda i, j, t: (head0 + i, j))
    return pl.pallas_call(
        _dense_kernel,
        grid=(b, C_KV, per_seq),
        in_specs=[pl.BlockSpec((tq, grp * C_HD), lambda i, j, t: (i * per_seq + t, j)), tok, head, tok, head],
        out_specs=pl.BlockSpec((tq, grp * C_HD), lambda i, j, t: (i * per_seq + t, j)),
        out_shape=jax.ShapeDtypeStruct((n_tok, C_HEADS * C_HD), BF16),
        scratch_shapes=[score] * (2 * ATT_QB) + [pltpu.VMEM((grp * BLK, 2 * C_HD), F32)] * ATT_QB,
        compiler_params=_cparams("parallel", "parallel", "arbitrary"),
        name="dense_mixer",
    )(q, k, k, v, v)


def _rope_tables(seq):
    rows = seq // GRID_W
    row = jnp.repeat(jnp.arange(rows), GRID_W)
    col = jnp.tile(jnp.arange(GRID_W), rows)
    head = jnp.tile(jnp.concatenate([jnp.zeros((FRONT,), jnp.int32), jnp.arange(N_META) - N_META]), ROW_TILE // BLK)
    row = jnp.concatenate([row, head]).astype(F32)
    col = jnp.concatenate([col, head]).astype(F32)
    axis_dim = C_HD // 2
    freqs = ROPE_THETA ** (-jnp.arange(0, axis_dim, 2, dtype=F32) / axis_dim)
    ang = jnp.concatenate([row[:, None] * freqs, col[:, None] * freqs], axis=-1)
    cos, sin = jnp.cos(ang), jnp.sin(ang)
    return jnp.concatenate([cos, cos], axis=-1), jnp.concatenate([-sin, sin], axis=-1)


def _gate_weight(w_b, w_a):
    d = w_b.shape[0]
    w_b = w_b.reshape(d, 2, A_HEADS)
    w_a = w_a.reshape(d, 2, A_HEADS)
    per_head = jnp.concatenate([w_b, w_a, w_a], axis=1)
    per_head = jnp.transpose(per_head, (0, 2, 1))
    per_head = jnp.pad(per_head, ((0, 0), (0, 0), (0, BLK - 6)))
    return per_head.reshape(d, A_HEADS * BLK)


def _gate_rows(p):
    t = jnp.transpose(p.astype(F32), (1, 0))
    rows = jnp.concatenate([jnp.zeros_like(t), t, t], axis=1)
    rows = jnp.pad(rows, ((0, 0), (0, GATE_ROWS - 6)))
    return jnp.broadcast_to(rows[:, :, None], (A_HEADS, GATE_ROWS, BLK))


def kernel(x, meta_tokens, attn_norm_g, mlp_norm_g, w_in_ab, conv_w_a, a_log, dt_bias, a_out_norm_g,
           b_q_norm_g, b_k_norm_g, b_sink, w_out_ab, w_qkv_c, c_q_norm_g, c_k_norm_g, w_out_c, w_ff1, w_ff2):
    bsz, seq, d = x.shape
    n_tok = bsz * seq
    n_rows = n_tok + bsz * BLK
    ntt = n_tok // ROW_TILE
    assert attn_norm_g.shape[0] == 2 and seq % ROW_TILE == 0 and (bsz * BLK) % ROW_TILE == 0
    x2 = x.reshape(n_tok, d)
    meta = jnp.broadcast_to(meta_tokens.astype(x.dtype)[None], (bsz, N_META, d))
    head = jnp.concatenate([jnp.zeros((bsz, FRONT, d), x.dtype), meta], axis=1).reshape(bsz * BLK, d)
    row2 = lambda v: v.astype(F32).reshape(1, -1)

    w = w_in_ab[0]
    qkv_w = w[:, :1536].astype(BF16)
    z_w = w[:, 1536:2048].astype(BF16)
    gate_w = _gate_weight(w[:, 2048:2056], w[:, 2056:2064]).astype(BF16)
    bq_w = w[:, 2064:2576].astype(BF16)
    bkv_w = w[:, 2576:2832].astype(BF16)
    qkv, z, gate_pre, qb, kvb = _ab_proj(x2, head, row2(attn_norm_g[0]), qkv_w, z_w, gate_w, bq_w, bkv_w,
                                         row2(b_q_norm_g[0]), row2(b_k_norm_g[0]))
    ya = _delta(qkv, z, gate_pre, bsz, n_tok, _gate_rows(a_log[0]), _gate_rows(dt_bias[0]),
                conv_w_a[0].astype(F32), row2(a_out_norm_g[0]))
    yb = _window(qb, kvb, bsz, n_tok, b_sink[0])
    h = _out_mlp(n_rows, ntt, [(x2, head), tuple(ya), yb], w_out_ab[0].astype(BF16), row2(mlp_norm_g[0]),
                 w_ff1[0].astype(BF16), w_ff2[0].astype(BF16))

    w = w_qkv_c[0]
    deint = jnp.concatenate([jnp.arange(0, C_HD, 2), jnp.arange(1, C_HD, 2)])
    perm = lambda wc, nh: wc.reshape(d, nh, C_HD)[:, :, deint].reshape(d, nh * C_HD)
    wq = perm(w[:, :C_HEADS * C_HD], C_HEADS).astype(BF16)
    wk = perm(w[:, C_HEADS * C_HD:(C_HEADS + C_KV) * C_HD], C_KV).astype(BF16)
    wv = w[:, (C_HEADS + C_KV) * C_HD:].astype(BF16)
    cosf, sinf = _rope_tables(seq)
    q, k, v = _c_proj(h, ntt, seq, row2(attn_norm_g[1]), wq, wk, wv,
                      row2(c_q_norm_g[0][deint]), row2(c_k_norm_g[0][deint]), cosf, sinf)
    att = _dense(q, k, v, bsz, n_tok)
    out = _out_mlp(n_tok, ntt, [h, att], w_out_c[0].astype(BF16), row2(mlp_norm_g[1]),
                   w_ff1[1].astype(BF16), w_ff2[1].astype(BF16))
    return out.reshape(bsz, seq, d)
```

```python
import functools
import math

import jax
import jax.numpy as jnp
from jax import lax
from jax.experimental import pallas as pl
from jax.experimental.pallas import tpu as pltpu

F32 = jnp.float32
BF16 = jnp.bfloat16

EPS = 1e-6
N_META = 16
BLK = 128
FRONT = BLK - N_META
GRID_W = 64
ROPE_THETA = 10000.0
A_HEADS, A_DK, A_CONV = 4, 128, 5
B_HEADS, B_KV, B_HD, B_WIN = 8, 2, 64, 128
C_HEADS, C_KV, C_HD = 8, 2, 128
NEG = -1e30
LOG2E = math.log2(math.e)

VMEM_LIMIT = 62 * 1024 * 1024
ROW_TILE = 1024
C_ROW_TILE = 256
FF_CHUNK = 512
INV_SQUARINGS = BLK.bit_length() - 2
PREP_UNROLL = 11


def _chunk_group(nblk):
    return max(g for g in range(1, PREP_UNROLL + 1) if nblk % g == 0)


def _cparams(*sem):
    return pltpu.CompilerParams(dimension_semantics=sem, vmem_limit_bytes=VMEM_LIMIT)


def _sigmoid(x):
    return 1.0 / (1.0 + jnp.exp(-x))


def _silu(x):
    return x * _sigmoid(x)


def _softplus(x):
    return jnp.maximum(x, 0.0) + jnp.log1p(jnp.exp(-jnp.abs(x)))


def _rms(x, g):
    return x * lax.rsqrt(jnp.mean(x * x, axis=-1, keepdims=True) + EPS) * g


def _dot(a, b):
    return jnp.dot(a, b, preferred_element_type=F32)


def _dot_nt(a, b):
    return lax.dot_general(a, b, (((1,), (1,)), ((), ())), preferred_element_type=F32)


def _slab_spec(n_slabs, index):
    return pl.BlockSpec((n_slabs, ROW_TILE, BLK), lambda i: (0, index(i), 0))


def _pair_specs(tok, head):
    ntt = tok.shape[-2] // ROW_TILE
    index = (lambda i: jnp.minimum(i, ntt - 1), lambda i: jnp.maximum(i - ntt, 0))
    if tok.ndim == 3:
        return [_slab_spec(tok.shape[0], ix) for ix in index]
    flat = lambda ix: pl.BlockSpec((ROW_TILE, tok.shape[1]), lambda i: (ix(i), 0))
    return [flat(ix) for ix in index]


def _rows(v):
    return jnp.concatenate([v[c] for c in range(v.shape[0])], axis=1) if v.ndim == 3 else v


def _pick(tok_ref, head_ref, ntt):
    return _rows(jnp.where(pl.program_id(0) < ntt, tok_ref[...], head_ref[...]))


def _store_slabs(ref, first, value):
    for c in range(value.shape[1] // BLK):
        ref[first + c] = value[:, c * BLK:(c + 1) * BLK]


def _ab_proj_kernel(x_ref, hb_ref, g_ref, wqkv_ref, wz_ref, wg_ref, wq_ref, wkv_ref, bqg_ref, bkg_ref,
                    qkv_ref, z_ref, gate_ref, qb_ref, kvb_ref, *, ntt):
    u = _rms(_pick(x_ref, hb_ref, ntt), g_ref[...]).astype(BF16)
    qb = _dot(u, wq_ref[...])
    kv = _dot(u, wkv_ref[...])
    scale = B_HD ** -0.5 * LOG2E

    def q_heads(lo, hi):
        for hh in range(lo, hi):
            sl = slice(hh * B_HD, (hh + 1) * B_HD)
            qb_ref[:, sl] = (_rms(qb[:, sl], bqg_ref[...]) * scale).astype(BF16)

    _store_slabs(qkv_ref, 0, _dot(u, wqkv_ref[:, 0:512]))
    q_heads(0, 3)
    _store_slabs(qkv_ref, A_HEADS, _dot(u, wqkv_ref[:, 512:1024]))
    q_heads(3, 6)
    _store_slabs(qkv_ref, 2 * A_HEADS, _dot(u, wqkv_ref[:, 1024:1536]))
    q_heads(6, B_HEADS)
    _store_slabs(z_ref, 0, _dot(u, wz_ref[...]))
    for hh in range(B_KV):
        sl = slice(hh * B_HD, (hh + 1) * B_HD)
        kvb_ref[:, sl] = _rms(kv[:, sl], bkg_ref[...]).astype(BF16)
    kvb_ref[:, B_KV * B_HD:] = kv[:, B_KV * B_HD:].astype(BF16)
    _store_slabs(gate_ref, 0, _dot(u, wg_ref[...]))


def _ab_proj(x, hb, g, wqkv, wz, wg, wq, wkv, bqg, bkg):
    r = x.shape[0] + hb.shape[0]
    row = lambda n: pl.BlockSpec((ROW_TILE, n), lambda i: (i, 0))
    full = lambda a: pl.BlockSpec(a.shape, lambda i: (0, 0))
    return pl.pallas_call(
        functools.partial(_ab_proj_kernel, ntt=x.shape[0] // ROW_TILE),
        grid=(r // ROW_TILE,),
        in_specs=_pair_specs(x, hb) + [full(g), full(wqkv), full(wz), full(wg), full(wq), full(wkv), full(bqg),
                                       full(bkg)],
        out_specs=[_slab_spec(3 * A_HEADS, lambda i: i), _slab_spec(A_HEADS, lambda i: i),
                   _slab_spec(A_HEADS, lambda i: i), row(512), row(256)],
        out_shape=[jax.ShapeDtypeStruct((3 * A_HEADS, r, BLK), F32), jax.ShapeDtypeStruct((A_HEADS, r, BLK), F32),
                   jax.ShapeDtypeStruct((A_HEADS, r, BLK), F32), jax.ShapeDtypeStruct((r, 512), BF16),
                   jax.ShapeDtypeStruct((r, 256), BF16)],
        compiler_params=_cparams("parallel"),
        name="ab_proj",
    )(x, hb, g, wqkv, wz, wg, wq, wkv, bqg, bkg)


GATE_ROWS = 8


def _split3(x):
    hi = x.astype(BF16)
    r1 = x - hi.astype(F32)
    mid = r1.astype(BF16)
    lo = (r1 - mid.astype(F32)).astype(BF16)
    return hi, mid, lo


def _seq_block(tok_ref, head_ref, n):
    tok = tok_ref[pl.ds(pl.multiple_of(jnp.maximum(n - 1, 0) * BLK, BLK), BLK), :]
    return jnp.where(n == 0, head_ref[...], tok)


def _seq_views(n_tok, seq, off):
    head0 = n_tok // BLK
    return [pl.BlockSpec((None, seq, BLK), lambda i, j: (j + off, i, 0)),
            pl.BlockSpec((None, BLK, BLK), lambda i, j: (j + off, head0 + i, 0))]


def _gate_block(pre, n, neg_a, dtb):
    nr = GATE_ROWS
    t = pre.T[0:nr, :]
    ri = lax.broadcasted_iota(jnp.int32, (BLK, BLK), 0)
    ci = lax.broadcasted_iota(jnp.int32, (BLK, BLK), 1)
    role = ri[0:nr, :]
    live = (ci[0:nr, :] + n * BLK) >= FRONT
    beta = jnp.where(live, _sigmoid(t), 0.0)
    g = jnp.where(live, neg_a * _softplus(t + dtb), 0.0)
    parts = [p.astype(F32) for p in _split3(g)]
    parts = jnp.concatenate(parts + [jnp.zeros((BLK - 3 * nr, BLK), F32)], axis=0).astype(BF16)
    tri = jnp.concatenate([(ri <= ci).astype(BF16), (ri >= ci).astype(BF16)], axis=1)
    sums = _dot(parts, tri)
    both = sums[0:nr] + sums[nr:2 * nr] + sums[2 * nr:3 * nr]
    pre_sum, suf_sum = both[:, :BLK], both[:, BLK:]
    tot = pre_sum + suf_sum - g
    row = jnp.where(role < 2, beta, jnp.where(role == 2, pre_sum, jnp.where(role == 3, suf_sum, tot)))
    col = jnp.concatenate([row, jnp.zeros((BLK - nr, BLK), F32)], axis=0).T
    return col, row


def _delta_kernel(q_ref, qh_ref, k_ref, kh_ref, v_ref, vh_ref, z_ref, zh_ref, gp_ref, gph_ref, alog_ref, dtb_ref,
                  cwq_ref, cwk_ref, cwv_ref, og_ref, y_ref, yh_ref, sadd_s, smul_s, o_s, omul_s, gl_s):
    seq = q_ref.shape[0]
    nblk = seq // BLK + 1
    grp = _chunk_group(nblk)
    neg_a = -jnp.exp(alog_ref[...])
    dtb = dtb_ref[...]
    ri = lax.broadcasted_iota(jnp.int32, (BLK, BLK), 0)
    ci = lax.broadcasted_iota(jnp.int32, (BLK, BLK), 1)
    eye = (ri == ci).astype(F32)
    incl = (ri >= ci, ri <= ci)
    strict = (ri > ci, ri < ci)

    hw = A_CONV // 2

    def conv_silu(ref, head_ref, w_ref, n, maybe_edge):
        base = jnp.clip((n - 1) * BLK, hw, seq - BLK - hw)
        acc = ref[pl.ds(base - hw, BLK), :] * w_ref[0:1, :]
        for j in range(1, A_CONV):
            acc = acc + ref[pl.ds(base - hw + j, BLK), :] * w_ref[j:j + 1, :]
        if maybe_edge:
            cur = _seq_block(ref, head_ref, n)
            tok_prev = ref[pl.ds(pl.multiple_of(jnp.maximum((n - 1) * BLK - 8, 0), 8), 8), :]
            prev = jnp.where(n == 0, 0.0, jnp.where(n == 1, head_ref[BLK - 8:, :], tok_prev))
            nxt = ref[pl.ds(pl.multiple_of(jnp.minimum(n * BLK, seq - 8), 8), 8), :]
            nxt = jnp.where(n < nblk - 1, nxt, 0.0)
            win = jnp.concatenate([prev, cur, nxt], axis=0)
            edge = win[8 - hw:8 - hw + BLK, :] * w_ref[0:1, :]
            for j in range(1, A_CONV):
                edge = edge + win[8 - hw + j:8 - hw + j + BLK, :] * w_ref[j:j + 1, :]
            acc = jnp.where((n <= 1) | (n == nblk - 1), edge, acc)
        return _silu(acc)

    def l2n(x):
        return x * lax.rsqrt(jnp.sum(x * x, axis=-1, keepdims=True) + EPS)

    edge_slots = {0 % grp, 1 % grp, (nblk - 1) % grp}

    def chunk_inputs(n, slot):
        rows = pl.ds(pl.multiple_of(n * BLK, BLK), BLK)
        live = (ri[:, 0:1] + n * BLK) >= FRONT
        edge = slot in edge_slots
        qn = jnp.where(live, l2n(conv_silu(q_ref, qh_ref, cwq_ref, n, edge)) * (A_DK ** -0.5), 0.0)
        kn = jnp.where(live, l2n(conv_silu(k_ref, kh_ref, cwk_ref, n, edge)), 0.0)
        vv = jnp.where(live, conv_silu(v_ref, vh_ref, cwv_ref, n, edge), 0.0)
        kn16 = kn.astype(BF16)
        kq = _dot_nt(jnp.concatenate([kn16, qn.astype(BF16)], axis=0), kn16)
        bg, gt = _gate_block(_seq_block(gp_ref, gph_ref, n), n, neg_a, dtb)
        return dict(n=n, rows=rows, qn=qn, kn=kn, vv=vv, kk=kq[:BLK], qk=kq[BLK:], bg=bg, gt=gt)

    def chain_setup(c, d):
        bg, gt = c["bg"], c["gt"]
        beta, ccol, tot = bg[:, d:d + 1], bg[:, 2 + d:3 + d], bg[:, 4 + d:5 + d]
        crow = gt[2 + d:3 + d, :]
        dec = jnp.exp(jnp.where(incl[d], ccol - crow, NEG))
        a = jnp.where(strict[d], beta * c["kk"] * dec, 0.0)
        return dict(c=c, d=d, beta=beta, ccol=ccol, tot=tot, dec=dec, a=a, t=eye - a, x=a.astype(BF16))

    def prep(first, tick=lambda: None):
        chunks = [chunk_inputs(first + j, j) for j in range(grp)]
        tick()
        chains = [chain_setup(c, d) for c in chunks for d in range(2)]
        tick()
        zero = jnp.zeros((BLK, BLK), BF16)

        def blockdiag(xp):
            return jnp.concatenate([jnp.concatenate([xp[:, :BLK], zero], axis=1),
                                    jnp.concatenate([zero, xp[:, BLK:]], axis=1)], axis=0)

        pairs = [(chains[2 * j], chains[2 * j + 1]) for j in range(grp)]
        xps = [jnp.concatenate([f["x"], b["x"]], axis=1) for f, b in pairs]
        tps = [jnp.concatenate([f["t"], b["t"]], axis=1) for f, b in pairs]
        for _ in range(INV_SQUARINGS):
            xps = [_dot(xp, blockdiag(xp)).astype(BF16) for xp in xps]
            txs = [_dot(tp.astype(BF16), blockdiag(xp)) for tp, xp in zip(tps, xps)]
            tps = [tp + tx for tp, tx in zip(tps, txs)]
            tick()
        for (f, b), tp in zip(pairs, tps):
            f["t"], b["t"] = tp[:, :BLK], tp[:, BLK:]
        for ch in chains:
            c = ch["c"]
            ch["ec"] = jnp.exp(ch["ccol"])
            ch["rhs"] = jnp.concatenate([ch["beta"] * c["vv"], ch["beta"] * c["kn"] * ch["ec"]], axis=1)
            ch["t16"] = ch["t"].astype(BF16)
        x0s = [_dot(ch["t16"], ch["rhs"].astype(BF16)) for ch in chains]
        tick()
        res = []
        for ch, x0 in zip(chains, x0s):
            ax = _dot(ch["a"].astype(BF16), x0.astype(BF16))
            res.append((ch["rhs"] - x0 - ax).astype(BF16))
        tick()
        uws = [(x0 + _dot(ch["t16"], e)).astype(BF16) for ch, x0, e in zip(chains, x0s, res)]
        tick()
        kuws = [_dot((ch["c"]["kn"] * jnp.exp(ch["tot"] - ch["ccol"])).T.astype(BF16), uw)
                for ch, uw in zip(chains, uws)]
        quws = [_dot((ch["c"]["qk"] * ch["dec"]).astype(BF16), uw) for ch, uw in zip(chains, uws)]
        for ch, kuw, quw in zip(chains, kuws, quws):
            c, d = ch["c"], ch["d"]
            rows = c["rows"]
            sadd_s[d, rows, :] = kuw[:, :BLK]
            smul_s[d, rows, :] = (-kuw[:, BLK:]).astype(BF16)
            o_s[d, rows, :] = quw[:, :BLK]
            omul_s[d, rows, :] = (c["qn"] * ch["ec"] - quw[:, BLK:]).astype(BF16)
            gl_s[d * nblk + c["n"]] = jnp.broadcast_to(jnp.exp(ch["tot"]), (BLK, BLK))[0:8, :]

    def scan_step(d, n, s):
        rows = pl.ds(pl.multiple_of(n * BLK, BLK), BLK)
        both = _dot(jnp.concatenate([smul_s[d, rows, :], omul_s[d, rows, :]], axis=0), s.astype(BF16))
        o_s[d, rows, :] = o_s[d, rows, :] + both[BLK:]
        return s * gl_s[d * nblk + n][0:1, :] + both[:BLK] + sadd_s[d, rows, :]

    def gated(rows, z):
        o = o_s[0, rows, :] + o_s[1, rows, :]
        return (_rms(o, og_ref[...]) * _silu(z)).astype(y_ref.dtype)

    def finish(n):
        tok_rows = pl.ds(pl.multiple_of((n - 1) * BLK, BLK), BLK)
        y_ref[tok_rows, :] = gated(pl.ds(pl.multiple_of(n * BLK, BLK), BLK), z_ref[tok_rows, :])

    def scan(i, carry):
        sf, sb = carry
        sf = scan_step(0, i, sf)
        sb = scan_step(1, nblk - 1 - i, sb)
        return sf, sb

    def scan_and_finish(i, carry):
        finish(i - 1)
        finish(nblk - i)
        return scan(i, carry)

    s0 = jnp.zeros((BLK, BLK), F32)
    first_done = nblk // 2 + 1
    ngroups = nblk // grp
    if ngroups == 3 and grp < first_done:
        def outer(g, c):
            prep(g * (2 * grp))
            return c

        lax.fori_loop(0, 2, outer, 0)
        state = [(s0, s0)]
        todo = list(range(grp))

        def tick():
            if todo:
                state[0] = scan(jnp.int32(todo.pop(0)), state[0])

        prep(jnp.int32(grp), tick)
        while todo:
            tick()
        carry, start = state[0], grp
    else:
        def ordered(g, c):
            prep(g * grp)
            return c

        lax.fori_loop(0, ngroups, ordered, 0)
        carry, start = (s0, s0), 0
    carry = lax.fori_loop(start, first_done, scan, carry)
    lax.fori_loop(first_done, nblk, scan_and_finish, carry)
    finish(nblk - 1)
    yh_ref[...] = gated(slice(0, BLK), zh_ref[...])


def _delta(qkv, z, gate_pre, b, n_tok, alog_rows, dtb_rows, conv_w, o_gain):
    seq = n_tok // b
    lp = seq + BLK
    nblk = lp // BLK
    cw = lambda off: pl.BlockSpec((A_CONV, BLK), lambda i, j: (0, j + off))
    views = lambda off: _seq_views(n_tok, seq, off)
    per_head = pl.BlockSpec((None, GATE_ROWS, BLK), lambda i, j: (j, 0, 0))
    return pl.pallas_call(
        _delta_kernel,
        grid=(b, A_HEADS),
        in_specs=views(0) + views(A_HEADS) + views(2 * A_HEADS) + views(0) + views(0)
                 + [per_head, per_head, cw(0), cw(A_HEADS), cw(2 * A_HEADS),
                    pl.BlockSpec((1, BLK), lambda i, j: (0, 0))],
        out_specs=[pl.BlockSpec((None, seq, BLK), lambda i, j: (j, i, 0)),
                   pl.BlockSpec((None, BLK, BLK), lambda i, j: (j, i, 0))],
        out_shape=[jax.ShapeDtypeStruct((A_HEADS, n_tok, BLK), BF16),
                   jax.ShapeDtypeStruct((A_HEADS, b * BLK, BLK), BF16)],
        scratch_shapes=[pltpu.VMEM((2, lp, BLK), F32), pltpu.VMEM((2, lp, BLK), BF16),
                        pltpu.VMEM((2, lp, BLK), F32), pltpu.VMEM((2, lp, BLK), BF16),
                        pltpu.VMEM((2 * nblk, 8, BLK), F32)],
        compiler_params=_cparams("parallel", "parallel"),
        name="delta_mixer",
    )(qkv, qkv, qkv, qkv, qkv, qkv, z, z, gate_pre, gate_pre, alog_rows, dtb_rows, conv_w, conv_w, conv_w, o_gain)


def _window_kernel(q_ref, kp_ref, kc_ref, kn_ref, km_ref, bias_ref, y_ref):
    i = pl.program_id(1)
    nblk = pl.num_programs(1)
    grp = B_HEADS // B_KV
    nk = 4 * BLK
    c = lax.broadcasted_iota(jnp.int32, (1, nk), 1)
    kblk = i - 1 + (c >> 7)
    edge = jnp.where((c >= 3 * BLK) | ((kblk >= 1) & (kblk < nblk)), 0.0, NEG)
    q = q_ref[...]
    kvs = (kp_ref[...], kc_ref[...], kn_ref[...], km_ref[...])
    ones = jnp.ones((nk, 2 * B_HD), BF16)
    lane = lax.broadcasted_iota(jnp.int32, (BLK, 2 * B_HD), 1)
    s4s, vexts = [], []
    for kvh in range(B_KV):
        ks = jnp.concatenate([t[:, kvh * B_HD:(kvh + 1) * B_HD] for t in kvs], axis=0)
        vs = jnp.concatenate([t[:, (B_KV + kvh) * B_HD:(B_KV + kvh + 1) * B_HD] for t in kvs], axis=0)
        q4 = jnp.concatenate([q[:, hh * B_HD:(hh + 1) * B_HD] for hh in range(kvh * grp, (kvh + 1) * grp)],
                             axis=0)
        s4s.append(_dot_nt(q4, ks))
        vexts.append(jnp.concatenate([vs, vs, ones], axis=1))
    pvs = []
    for kvh in range(B_KV):
        ps = []
        for gi in range(grp):
            hh = kvh * grp + gi
            s = s4s[kvh][gi * BLK:(gi + 1) * BLK] + bias_ref[hh] + edge
            ps.append(jnp.exp2(s - jnp.max(s, axis=-1, keepdims=True)).astype(BF16))
        pvs.append(_dot(jnp.concatenate(ps, axis=0), vexts[kvh]))
    outs = []
    for hh in range(B_HEADS):
        kvh, gi = divmod(hh, grp)
        o = pvs[kvh][gi * BLK:(gi + 1) * BLK]
        outs.append(o[:, :2 * B_HD] / o[:, 2 * B_HD:])
    for j in range(B_HEADS // 2):
        pair = jnp.where(lane < B_HD, outs[2 * j], outs[2 * j + 1])
        y_ref[:, 2 * j * B_HD:(2 * j + 2) * B_HD] = pair.astype(y_ref.dtype)

    @pl.when(i == 0)
    def _():
        rr = lax.broadcasted_iota(jnp.int32, y_ref.shape, 0)
        y_ref[...] = jnp.where(rr >= FRONT, y_ref[...], 0).astype(y_ref.dtype)


def _window_bias(sink):
    r = jnp.arange(BLK)[:, None]
    c = jnp.arange(4 * BLK)[None, :]
    dist = jnp.abs(BLK + r - c)
    slopes = jnp.exp2(-8.0 * (jnp.arange(B_HEADS, dtype=F32) + 1.0) / B_HEADS)
    band = (c < 3 * BLK) & (dist <= B_WIN)
    alibi = -slopes[:, None, None] * dist.astype(F32)[None] * LOG2E
    rest = jnp.where(c >= 3 * BLK + FRONT, 0.0, NEG)
    bias = jnp.where(band[None], alibi, rest[None])
    sink_col = (c == 3 * BLK)[None]
    return jnp.where(sink_col, sink.astype(F32)[:, None, None] * LOG2E, bias).astype(F32)


def _window(qb, kvb, b, n_tok, sink):
    nblk = n_tok // b // BLK + 1
    bias = _window_bias(sink)
    head0 = n_tok // BLK

    def blk(i, j):
        return jnp.where(j == 0, head0 + i, i * (nblk - 1) + j - 1)

    kv = lambda f: pl.BlockSpec((BLK, 2 * B_KV * B_HD), f)
    return pl.pallas_call(
        _window_kernel,
        grid=(b, nblk),
        in_specs=[pl.BlockSpec((BLK, B_HEADS * B_HD), lambda i, j: (blk(i, j), 0)),
                  kv(lambda i, j: (blk(i, jnp.maximum(j - 1, 0)), 0)),
                  kv(lambda i, j: (blk(i, j), 0)),
                  kv(lambda i, j: (blk(i, jnp.minimum(j + 1, nblk - 1)), 0)),
                  kv(lambda i, j: (head0 + i, 0)),
                  pl.BlockSpec(bias.shape, lambda i, j: (0, 0, 0))],
        out_specs=pl.BlockSpec((BLK, B_HEADS * B_HD), lambda i, j: (blk(i, j), 0)),
        out_shape=jax.ShapeDtypeStruct((qb.shape[0], B_HEADS * B_HD), BF16),
        compiler_params=_cparams("parallel", "parallel"),
        name="window_mixer",
    )(qb, kvb, kvb, kvb, kvb, bias)


def _out_mlp_kernel(*refs, arity, ntt):
    vals, pos = [], 0
    for a in arity:
        vals.append(refs[pos][...] if a == 1 else _pick(refs[pos], refs[pos + 1], ntt))
        pos += a
    wo_ref, g_ref, w1_ref, w2_ref, o_ref = refs[pos:]
    mix = jnp.concatenate(vals[1:], axis=1)
    h = vals[0] + _dot(mix, wo_ref[...])
    u = _rms(h, g_ref[...]).astype(BF16)
    dff = w1_ref.shape[1]
    acc = h
    for c in range(dff // FF_CHUNK):
        sl = slice(c * FF_CHUNK, (c + 1) * FF_CHUNK)
        a = jnp.maximum(_dot(u, w1_ref[:, sl]), 0.0)
        acc = acc + _dot((a * a).astype(BF16), w2_ref[sl, :])
    o_ref[...] = acc


def _out_mlp(rows_out, ntt, tensors, wo, g, w1, w2):
    d = wo.shape[1]
    row = lambda n: pl.BlockSpec((ROW_TILE, n), lambda i: (i, 0))
    full = lambda a: pl.BlockSpec(a.shape, lambda i: (0, 0))
    specs, args, arity = [], [], []
    for t in tensors:
        if isinstance(t, tuple):
            specs += _pair_specs(t[0], t[1])
            args += list(t)
            arity.append(2)
        else:
            specs.append(row(t.shape[1]))
            args.append(t)
            arity.append(1)
    return pl.pallas_call(
        functools.partial(_out_mlp_kernel, arity=tuple(arity), ntt=ntt),
        grid=(rows_out // ROW_TILE,),
        in_specs=specs + [full(wo), full(g), full(w1), full(w2)],
        out_specs=row(d),
        out_shape=jax.ShapeDtypeStruct((rows_out, d), F32),
        compiler_params=_cparams("parallel"),
        name="out_mlp",
    )(*args, wo, g, w1, w2)


def _c_proj_kernel(h_ref, g_ref, wq_ref, wk_ref, wv_ref, qg_ref, kg_ref, cos_ref, sin_ref,
                   q_ref, k_ref, v_ref):
    u = _rms(h_ref[...], g_ref[...]).astype(BF16)
    cosf = cos_ref[...]
    sinf = sin_ref[...]
    half = C_HD // 2

    def norm_rope(x, gain):
        out = []
        for r in (slice(0, x.shape[0] // 2), slice(x.shape[0] // 2, x.shape[0])):
            xr = _rms(x[r], gain)
            swapped = jnp.concatenate([xr[:, half:], xr[:, :half]], axis=1)
            out.append(xr * cosf[r] + swapped * sinf[r])
        return jnp.concatenate(out, axis=0)

    k = _dot(u, wk_ref[...])
    half_w = C_HEADS * C_HD // 2
    q_lo = _dot(u, wq_ref[:, :half_w])
    for hh in range(C_KV):
        sl = slice(hh * C_HD, (hh + 1) * C_HD)
        k_ref[:, sl] = norm_rope(k[:, sl], kg_ref[...]).astype(BF16)
    q_hi = _dot(u, wq_ref[:, half_w:])
    for hh in range(C_HEADS // 2):
        sl = slice(hh * C_HD, (hh + 1) * C_HD)
        q_ref[:, sl] = (norm_rope(q_lo[:, sl], qg_ref[...]) * (C_HD ** -0.5 * LOG2E)).astype(BF16)
    v_ref[...] = _dot(u, wv_ref[...]).astype(BF16)
    for hh in range(C_HEADS // 2):
        sl = slice(hh * C_HD, (hh + 1) * C_HD)
        q_ref[:, half_w + hh * C_HD:half_w + (hh + 1) * C_HD] = (
            norm_rope(q_hi[:, sl], qg_ref[...]) * (C_HD ** -0.5 * LOG2E)).astype(BF16)


def _c_proj(h, ntt, seq, g, wq, wk, wv, qg, kg, cosf, sinf):
    r, d = h.shape
    tm = C_ROW_TILE
    per_seq = seq // tm
    n_tok_tiles = ntt * (ROW_TILE // tm)
    row = lambda n: pl.BlockSpec((tm, n), lambda i: (i, 0))
    full = lambda a: pl.BlockSpec(a.shape, lambda i: (0, 0))
    pos = pl.BlockSpec((tm, C_HD), lambda i: (jnp.where(i < n_tok_tiles, i % per_seq, per_seq), 0))
    return pl.pallas_call(
        _c_proj_kernel,
        grid=(r // tm,),
        in_specs=[row(d), full(g), full(wq), full(wk), full(wv), full(qg), full(kg), pos, pos],
        out_specs=[row(C_HEADS * C_HD), row(C_KV * C_HD), row(C_KV * C_HD)],
        out_shape=[jax.ShapeDtypeStruct((r, C_HEADS * C_HD), BF16),
                   jax.ShapeDtypeStruct((r, C_KV * C_HD), BF16),
                   jax.ShapeDtypeStruct((r, C_KV * C_HD), BF16)],
        compiler_params=_cparams("parallel"),
        name="c_proj",
    )(h, g, wq, wk, wv, qg, kg, cosf, sinf)


ATT_TK = 2048
ATT_QB = 4


def _dense_kernel(q_ref, k_ref, kh_ref, v_ref, vh_ref, y_ref, *scratch):
    grp = C_HEADS // C_KV
    nkb = k_ref.shape[0] // ATT_TK
    nq = ATT_QB
    sa_s, sb_s, acc_s = scratch[:nq], scratch[nq:2 * nq], scratch[2 * nq:]
    qs = [jnp.concatenate([q_ref[c * BLK:(c + 1) * BLK, g * C_HD:(g + 1) * C_HD] for g in range(grp)], axis=0)
          for c in range(nq)]
    m_rows = grp * BLK

    def keys(t):
        return pl.ds(pl.multiple_of(t * ATT_TK, ATT_TK), ATT_TK)

    def v_ones(v):
        return jnp.concatenate([v, jnp.ones(v.shape, BF16)], axis=1)

    def scores(t, s_refs):
        kt = k_ref[keys(t), :]
        for c in range(nq):
            s_refs[c][...] = _dot_nt(qs[c], kt)

    def step(t, ms, s_refs):
        vt = v_ones(v_ref[keys(t), :])
        out = []
        for c in range(nq):
            s = s_refs[c][...]
            m_new = jnp.maximum(ms[c], jnp.max(s, axis=-1, keepdims=True))
            p = jnp.exp2(s - m_new).astype(BF16)
            acc_s[c][...] = jnp.exp2(ms[c] - m_new) * acc_s[c][...] + _dot(p, vt)
            out.append(m_new)
        return out

    scores(0, sa_s)
    k0 = kh_ref[...]
    v0 = v_ones(vh_ref[...])
    kc = lax.broadcasted_iota(jnp.int32, (m_rows, BLK), 1)
    ms = []
    for c in range(nq):
        s0 = jnp.where(kc >= FRONT, _dot_nt(qs[c], k0), NEG)
        m = jnp.max(s0, axis=-1, keepdims=True)
        acc_s[c][...] = _dot(jnp.exp2(s0 - m).astype(BF16), v0)
        ms.append(m)

    def body(j, ms):
        scores(2 * j + 1, sb_s)
        ms = step(2 * j, ms, sa_s)
        scores(2 * j + 2, sa_s)
        return step(2 * j + 1, ms, sb_s)

    ms = lax.fori_loop(0, nkb // 2 - 1, body, ms)
    scores(nkb - 1, sb_s)
    ms = step(nkb - 2, ms, sa_s)
    ms = step(nkb - 1, ms, sb_s)
    for c in range(nq):
        acc = acc_s[c][...]
        o = acc[:, :C_HD] / acc[:, C_HD:C_HD + 1]
        for g in range(grp):
            y_ref[c * BLK:(c + 1) * BLK, g * C_HD:(g + 1) * C_HD] = o[g * BLK:(g + 1) * BLK, :].astype(y_ref.dtype)


def _dense(q, k, v, b, n_tok):
    seq = n_tok // b
    grp = C_HEADS // C_KV
    tq = ATT_QB * BLK
    assert seq % tq == 0 and seq % (2 * ATT_TK) == 0
    per_seq = seq // tq
    head0 = n_tok // BLK
    score = pltpu.VMEM((grp * BLK, ATT_TK), F32)
    tok = pl.BlockSpec((seq, C_HD), lambda i, j, t: (i, j))
    head = pl.BlockSpec((BLK, C_HD), lambda i, j, t: (head0 + i, j))
    return pl.pallas_call(
        _dense_kernel,
        grid=(b, C_KV, per_seq),
        in_specs=[pl.BlockSpec((tq, grp * C_HD), lambda i, j, t: (i * per_seq + t, j)), tok, head, tok, head],
        out_specs=pl.BlockSpec((tq, grp * C_HD), lambda i, j, t: (i * per_seq + t, j)),
        out_shape=jax.ShapeDtypeStruct((n_tok, C_HEADS * C_HD), BF16),
        scratch_shapes=[score] * (2 * ATT_QB) + [pltpu.VMEM((grp * BLK, 2 * C_HD), F32)] * ATT_QB,
        compiler_params=_cparams("parallel", "parallel", "arbitrary"),
        name="dense_mixer",
    )(q, k, k, v, v)


def _rope_tables(seq):
    rows = seq // GRID_W
    row = jnp.repeat(jnp.arange(rows), GRID_W)
    col = jnp.tile(jnp.arange(GRID_W), rows)
    head = jnp.tile(jnp.concatenate([jnp.zeros((FRONT,), jnp.int32), jnp.arange(N_META) - N_META]), ROW_TILE // BLK)
    row = jnp.concatenate([row, head]).astype(F32)
    col = jnp.concatenate([col, head]).astype(F32)
    axis_dim = C_HD // 2
    freqs = ROPE_THETA ** (-jnp.arange(0, axis_dim, 2, dtype=F32) / axis_dim)
    ang = jnp.concatenate([row[:, None] * freqs, col[:, None] * freqs], axis=-1)
    cos, sin = jnp.cos(ang), jnp.sin(ang)
    return jnp.concatenate([cos, cos], axis=-1), jnp.concatenate([-sin, sin], axis=-1)


def _gate_weight(w_b, w_a):
    d = w_b.shape[0]
    w_b = w_b.reshape(d, 2, A_HEADS)
    w_a = w_a.reshape(d, 2, A_HEADS)
    per_head = jnp.concatenate([w_b, w_a, w_a], axis=1)
    per_head = jnp.transpose(per_head, (0, 2, 1))
    per_head = jnp.pad(per_head, ((0, 0), (0, 0), (0, BLK - 6)))
    return per_head.reshape(d, A_HEADS * BLK)


def _gate_rows(p):
    t = jnp.transpose(p.astype(F32), (1, 0))
    rows = jnp.concatenate([jnp.zeros_like(t), t, t], axis=1)
    rows = jnp.pad(rows, ((0, 0), (0, GATE_ROWS - 6)))
    return jnp.broadcast_to(rows[:, :, None], (A_HEADS, GATE_ROWS, BLK))


def kernel(x, meta_tokens, attn_norm_g, mlp_norm_g, w_in_ab, conv_w_a, a_log, dt_bias, a_out_norm_g,
           b_q_norm_g, b_k_norm_g, b_sink, w_out_ab, w_qkv_c, c_q_norm_g, c_k_norm_g, w_out_c, w_ff1, w_ff2):
    bsz, seq, d = x.shape
    n_tok = bsz * seq
    n_rows = n_tok + bsz * BLK
    ntt = n_tok // ROW_TILE
    assert attn_norm_g.shape[0] == 2 and seq % ROW_TILE == 0 and (bsz * BLK) % ROW_TILE == 0
    x2 = x.reshape(n_tok, d)
    meta = jnp.broadcast_to(meta_tokens.astype(x.dtype)[None], (bsz, N_META, d))
    head = jnp.concatenate([jnp.zeros((bsz, FRONT, d), x.dtype), meta], axis=1).reshape(bsz * BLK, d)
    row2 = lambda v: v.astype(F32).reshape(1, -1)

    w = w_in_ab[0]
    qkv_w = w[:, :1536].astype(BF16)
    z_w = w[:, 1536:2048].astype(BF16)
    gate_w = _gate_weight(w[:, 2048:2056], w[:, 2056:2064]).astype(BF16)
    bq_w = w[:, 2064:2576].astype(BF16)
    bkv_w = w[:, 2576:2832].astype(BF16)
    qkv, z, gate_pre, qb, kvb = _ab_proj(x2, head, row2(attn_norm_g[0]), qkv_w, z_w, gate_w, bq_w, bkv_w,
                                         row2(b_q_norm_g[0]), row2(b_k_norm_g[0]))
    ya = _delta(qkv, z, gate_pre, bsz, n_tok, _gate_rows(a_log[0]), _gate_rows(dt_bias[0]),
                conv_w_a[0].astype(F32), row2(a_out_norm_g[0]))
    yb = _window(qb, kvb, bsz, n_tok, b_sink[0])
    h = _out_mlp(n_rows, ntt, [(x2, head), tuple(ya), yb], w_out_ab[0].astype(BF16), row2(mlp_norm_g[0]),
                 w_ff1[0].astype(BF16), w_ff2[0].astype(BF16))

    w = w_qkv_c[0]
    deint = jnp.concatenate([jnp.arange(0, C_HD, 2), jnp.arange(1, C_HD, 2)])
    perm = lambda wc, nh: wc.reshape(d, nh, C_HD)[:, :, deint].reshape(d, nh * C_HD)
    wq = perm(w[:, :C_HEADS * C_HD], C_HEADS).astype(BF16)
    wk = perm(w[:, C_HEADS * C_HD:(C_HEADS + C_KV) * C_HD], C_KV).astype(BF16)
    wv = w[:, (C_HEADS + C_KV) * C_HD:].astype(BF16)
    cosf, sinf = _rope_tables(seq)
    q, k, v = _c_proj(h, ntt, seq, row2(attn_norm_g[1]), wq, wk, wv,
                      row2(c_q_norm_g[0][deint]), row2(c_k_norm_g[0][deint]), cosf, sinf)
    att = _dense(q, k, v, bsz, n_tok)
    out = _out_mlp(n_tok, ntt, [h, att], w_out_c[0].astype(BF16), row2(mlp_norm_g[1]),
                   w_ff1[1].astype(BF16), w_ff2[1].astype(BF16))
    return out.reshape(bsz, seq, d)
```
